```python
import math
import jax, jax.numpy as jnp
from jax import lax
import numpy as np

D_MODEL = 1024
BATCH = 8
SEQ = 2048
DEPTH = 2
DEC_BATCH = 128
DEC_SEQ = 8
PAST_LEN = 2048
PAGE_SIZE = 128

HEAD_DIM = 64
H_A = 6
KVH_A = 2
H_IDX = 8
D_IDX = 64
K_TOP_MAX = 256
IDX_W_SCALE = (H_IDX ** -0.5) * (D_IDX ** -0.5)
H_B = 4
KVH_B = 2
H_C = 6
KVH_C = 2
N_BUCKETS = 32
T5_MAX_EXACT = 16
T5_MAX_DIST = 128
D_FF = 4 * D_MODEL
Q_BLOCK = 128
EPS = 1e-6
W_A = H_A * HEAD_DIM
W_B = H_B * 2 * HEAD_DIM
W_C = H_C * HEAD_DIM
IN_WIDTHS = (H_A * HEAD_DIM, KVH_A * HEAD_DIM, KVH_A * HEAD_DIM, H_IDX * D_IDX, D_IDX, H_IDX,
             H_B * 2 * HEAD_DIM, KVH_B * 2 * HEAD_DIM, KVH_B * 2 * HEAD_DIM,
             H_C * HEAD_DIM, KVH_C * HEAD_DIM, KVH_C * HEAD_DIM, H_C)
IN_SPLITS = tuple(int(s) for s in np.cumsum(IN_WIDTHS)[:-1])
N_IN = int(sum(IN_WIDTHS))

kernel_name = "hybrid_dsa_diff_fox_adaln_decode_step"


def rms_norm(x, g):
    xf = x.astype(jnp.float32)
    y = xf * lax.rsqrt(jnp.mean(xf * xf, axis=-1, keepdims=True) + EPS)
    return (y * g.astype(jnp.float32)).astype(x.dtype)


def t5_bucket(rel):
    n = jnp.maximum(rel, 0)
    nf = jnp.maximum(n, 1).astype(jnp.float32)
    large = T5_MAX_EXACT + (jnp.log(nf / T5_MAX_EXACT) / math.log(T5_MAX_DIST / T5_MAX_EXACT)
                            * (N_BUCKETS - T5_MAX_EXACT)).astype(jnp.int32)
    return jnp.where(n < T5_MAX_EXACT, n, jnp.minimum(large, N_BUCKETS - 1))


def sweep_query_blocks(fn, q_pos, *q_arrays):
    t = q_pos.shape[0]
    blk = min(Q_BLOCK, t)
    nb = t // blk

    def to_blocks(a):
        return jnp.moveaxis(a.reshape(a.shape[0], nb, blk, *a.shape[2:]), 1, 0)

    out = lax.map(lambda args: fn(*args),
                  (q_pos.reshape(nb, blk),) + tuple(to_blocks(a) for a in q_arrays))
    out = jnp.moveaxis(out, 0, 1)
    return out.reshape(out.shape[0], t, *out.shape[3:])


def dsa_attention(q, qi, wi, k, v, ki, q_pos, bias_tab):
    b = q.shape[0]
    n_keys = k.shape[1]
    k_top = min(K_TOP_MAX, n_keys // 4)
    g = H_A // KVH_A
    key_pos = jnp.arange(n_keys, dtype=jnp.int32)

    def block(qp, qb, qib, wib):
        tb = qp.shape[0]
        s = jnp.einsum('bthd,bsd->bths', qib, ki).astype(jnp.float32)
        score = jnp.einsum('bths,bth->bts', jax.nn.relu(s), wib.astype(jnp.float32))
        score = jnp.where(key_pos[None, None, :] <= qp[None, :, None], score, -jnp.inf)
        _, idx = lax.top_k(score, k_top)
        k_sel = jax.vmap(lambda kb, ib: kb[ib])(k, idx)
        v_sel = jax.vmap(lambda vb, ib: vb[ib])(v, idx)
        qg = qb.reshape(b, tb, KVH_A, g, HEAD_DIM)
        logits = jnp.einsum('btkgd,btjkd->btkgj', qg, k_sel).astype(jnp.float32) * HEAD_DIM ** -0.5
        rel = qp[None, :, None] - idx
        bias = jnp.moveaxis(bias_tab[t5_bucket(rel)], -1, 2).reshape(b, tb, KVH_A, g, k_top)
        logits = jnp.where((rel >= 0)[:, :, None, None, :],
                           logits + bias.astype(jnp.float32), -jnp.inf)
        p = jax.nn.softmax(logits, axis=-1).astype(v.dtype)
        o = jnp.einsum('btkgj,btjkd->btkgd', p, v_sel)
        return o.reshape(b, tb, H_A * HEAD_DIM)

    return sweep_query_blocks(block, q_pos, q, qi, wi)


def diff_attention(q, k, v, lam, q_pos, bias_tab):
    b = q.shape[0]
    n_keys = k.shape[1]
    g = H_B // KVH_B
    key_pos = jnp.arange(n_keys, dtype=jnp.int32)

    def block(qp, qb):
        tb = qp.shape[0]
        qg = qb.reshape(b, tb, KVH_B, g, 2, HEAD_DIM)
        logits = jnp.einsum('btkgcd,bskcd->bckgts', qg, k).astype(jnp.float32) * HEAD_DIM ** -0.5
        rel = qp[:, None] - key_pos[None, :]
        bias = jnp.moveaxis(bias_tab[t5_bucket(rel)], -1, 0).reshape(KVH_B, g, tb, n_keys)
        logits = jnp.where(rel >= 0, logits + bias.astype(jnp.float32), -jnp.inf)
        p = jax.nn.softmax(logits, axis=-1)
        attn = (p[:, 0] - lam * p[:, 1]).astype(v.dtype)
        o = jnp.einsum('bkgts,bske->btkge', attn, v)
        return o.reshape(b, tb, H_B, 2 * HEAD_DIM)

    return sweep_query_blocks(block, q_pos, q)


def forgetting_attention(q, k, v, cum_q, cum_k, q_pos):
    b = q.shape[0]
    n_keys = k.shape[1]
    g = H_C // KVH_C
    key_pos = jnp.arange(n_keys, dtype=jnp.int32)
    ck_t = jnp.swapaxes(cum_k, 1, 2)

    def block(qp, qb, cqb):
        tb = qp.shape[0]
        qg = qb.reshape(b, tb, KVH_C, g, HEAD_DIM)
        logits = jnp.einsum('btkgd,bskd->bkgts', qg, k).astype(jnp.float32) * HEAD_DIM ** -0.5
        decay = jnp.swapaxes(cqb, 1, 2)[:, :, :, None] - ck_t[:, :, None, :]
        decay = decay.reshape(b, KVH_C, g, tb, n_keys)
        causal = key_pos[None, :] <= qp[:, None]
        logits = jnp.where(causal, logits + decay, -jnp.inf)
        p = jax.nn.softmax(logits, axis=-1).astype(v.dtype)
        o = jnp.einsum('bkgts,bskd->btkgd', p, v)
        return o.reshape(b, tb, H_C * HEAD_DIM)

    return sweep_query_blocks(block, q_pos, q, cum_q)


def gather_pages(cache_l, page_table):
    g = cache_l[page_table]
    return g.reshape(g.shape[0], g.shape[1] * g.shape[2], *g.shape[3:])


def decoder_layer(x, c, q_pos, past, lam, lam_init, t5_table, w_ada, b_ada, g_mix, g_ffn, w_in,
                  b_forget, g_subln, w_gate, w_pa, w_pb, w_pc, w_out, w_ff1, w_ff2):
    b, t, _ = x.shape
    mod = jax.nn.silu(c) @ w_ada + b_ada
    sh1, sc1, gt1, sh2, sc2, gt2 = jnp.split(mod[:, None, :], 6, axis=-1)

    h = rms_norm(x, g_mix) * (1 + sc1) + sh1
    (aq, ak, av, aqi, aki, awi, bq, bk, bv, cq, ck, cv, cf) = jnp.split(h @ w_in, IN_SPLITS, axis=-1)
    aq = aq.reshape(b, t, H_A, HEAD_DIM)
    ak = ak.reshape(b, t, KVH_A, HEAD_DIM)
    av = av.reshape(b, t, KVH_A, HEAD_DIM)
    aqi = aqi.reshape(b, t, H_IDX, D_IDX)
    awi = awi * IDX_W_SCALE
    bq = bq.reshape(b, t, H_B, 2, HEAD_DIM)
    bk = bk.reshape(b, t, KVH_B, 2, HEAD_DIM)
    bv = bv.reshape(b, t, KVH_B, 2 * HEAD_DIM)
    cq = cq.reshape(b, t, H_C, HEAD_DIM)
    ck = ck.reshape(b, t, KVH_C, HEAD_DIM)
    cv = cv.reshape(b, t, KVH_C, HEAD_DIM)
    c_logf = jax.nn.log_sigmoid((cf + b_forget).astype(jnp.float32)).astype(x.dtype)

    new_rows = (ak, av, aki, bk, bv, ck, cv, c_logf)
    if past is None:
        full = new_rows
    else:
        full = tuple(jnp.concatenate([p_, n_], axis=1) for p_, n_ in zip(past, new_rows))
    fak, fav, faki, fbk, fbv, fck, fcv, fclogf = full

    o_a = dsa_attention(aq, aqi, awi, fak, fav, faki, q_pos, t5_table[:, :H_A])
    o_b = diff_attention(bq, fbk, fbv, lam, q_pos, t5_table[:, H_A:])
    o_b = (rms_norm(o_b, g_subln) * (1.0 - lam_init)).reshape(b, t, W_B)
    cum = jnp.cumsum(fclogf.astype(jnp.float32), axis=1)
    o_c = forgetting_attention(cq, fck, fcv, cum[:, -t:], cum, q_pos)

    gates = jax.nn.sigmoid((h @ w_gate).astype(jnp.float32)).astype(x.dtype)
    ga, gb, gc = jnp.split(gates, 3, axis=-1)
    merged = ga * (o_a @ w_pa) + gb * (o_b @ w_pb) + gc * (o_c @ w_pc)
    x = x + gt1 * (merged @ w_out)

    h2 = rms_norm(x, g_ffn) * (1 + sc2) + sh2
    x = x + gt2 * (jnp.square(jax.nn.relu(h2 @ w_ff1)) @ w_ff2)
    return x, new_rows


def stack_rows(rows):
    return tuple(jnp.stack([r[i] for r in rows]) for i in range(len(rows[0])))


def setup_inputs(seed: int = 0) -> dict:
    key = jax.random.key(seed)
    ks = iter(jax.random.split(key, 48))

    def nrm(shape, scale):
        return jax.random.normal(next(ks), shape, jnp.float32) * scale

    n_pages = PAST_LEN // PAGE_SIZE
    n_used = DEC_BATCH * n_pages
    n_pool = n_used + max(n_used // 4, 1)

    def pool(*tail):
        return (DEPTH, n_pool, PAGE_SIZE) + tail

    inputs = {}
    inputs["x_prompt"] = nrm((BATCH, SEQ, D_MODEL), 1.0)
    inputs["x_sample"] = nrm((DEC_BATCH, DEC_SEQ, D_MODEL), 1.0)
    inputs["c_prompt"] = nrm((BATCH, D_MODEL), 1.0)
    inputs["c_sample"] = nrm((DEC_BATCH, D_MODEL), 1.0)
    inputs["cache_a_k"] = nrm(pool(KVH_A, HEAD_DIM), 1.0)
    inputs["cache_a_v"] = nrm(pool(KVH_A, HEAD_DIM), 1.0)
    inputs["cache_a_kidx"] = nrm(pool(D_IDX), 1.0)
    inputs["cache_b_k"] = nrm(pool(KVH_B, 2, HEAD_DIM), 1.0)
    inputs["cache_b_v"] = nrm(pool(KVH_B, 2 * HEAD_DIM), 1.0)
    inputs["cache_c_k"] = nrm(pool(KVH_C, HEAD_DIM), 1.0)
    inputs["cache_c_v"] = nrm(pool(KVH_C, HEAD_DIM), 1.0)
    inputs["cache_c_logf"] = jax.nn.log_sigmoid(3.0 + nrm(pool(H_C), 1.0))
    inputs["page_table"] = jax.random.permutation(next(ks), n_pool)[:n_used].reshape(
        DEC_BATCH, n_pages).astype(jnp.int32)
    inputs["t5_table"] = nrm((N_BUCKETS, H_A + H_B), 0.5)
    inputs["w_ada"] = nrm((DEPTH, D_MODEL, 6 * D_MODEL), 0.5 * D_MODEL ** -0.5)
    inputs["b_ada"] = nrm((DEPTH, 6 * D_MODEL), 0.02)
    inputs["g_mix"] = 1.0 + nrm((DEPTH, D_MODEL), 0.02)
    inputs["g_ffn"] = 1.0 + nrm((DEPTH, D_MODEL), 0.02)
    inputs["w_in"] = nrm((DEPTH, D_MODEL, N_IN), D_MODEL ** -0.5)
    inputs["b_forget"] = 3.0 + nrm((DEPTH, H_C), 0.5)
    inputs["lam_q1"] = nrm((DEPTH, HEAD_DIM), 0.1)
    inputs["lam_k1"] = nrm((DEPTH, HEAD_DIM), 0.1)
    inputs["lam_q2"] = nrm((DEPTH, HEAD_DIM), 0.1)
    inputs["lam_k2"] = nrm((DEPTH, HEAD_DIM), 0.1)
    inputs["g_subln"] = 1.0 + nrm((DEPTH, 2 * HEAD_DIM), 0.02)
    inputs["w_gate"] = nrm((DEPTH, D_MODEL, 3 * D_MODEL), D_MODEL ** -0.5)
    inputs["w_pa"] = nrm((DEPTH, W_A, D_MODEL), W_A ** -0.5)
    inputs["w_pb"] = nrm((DEPTH, W_B, D_MODEL), W_B ** -0.5)
    inputs["w_pc"] = nrm((DEPTH, W_C, D_MODEL), W_C ** -0.5)
    inputs["w_out"] = nrm((DEPTH, D_MODEL, D_MODEL), D_MODEL ** -0.5)
    inputs["w_ff1"] = nrm((DEPTH, D_MODEL, D_FF), D_MODEL ** -0.5)
    inputs["w_ff2"] = nrm((DEPTH, D_FF, D_MODEL), D_FF ** -0.5)
    inputs["g_final"] = 1.0 + nrm((D_MODEL,), 0.02)
    return inputs


def reference(x_prompt, x_sample, c_prompt, c_sample, cache_a_k, cache_a_v, cache_a_kidx,
              cache_b_k, cache_b_v, cache_c_k, cache_c_v, cache_c_logf, page_table, t5_table,
              w_ada, b_ada, g_mix, g_ffn, w_in, b_forget, lam_q1, lam_k1, lam_q2, lam_k2, g_subln,
              w_gate, w_pa, w_pb, w_pc, w_out, w_ff1, w_ff2, g_final):
    seq = x_prompt.shape[1]
    dec_seq = x_sample.shape[1]
    past_len = page_table.shape[1] * cache_a_k.shape[2]
    pos_prompt = jnp.arange(seq, dtype=jnp.int32)
    pos_sample = past_len + jnp.arange(dec_seq, dtype=jnp.int32)
    caches = (cache_a_k, cache_a_v, cache_a_kidx, cache_b_k, cache_b_v,
              cache_c_k, cache_c_v, cache_c_logf)

    xp, xs = x_prompt, x_sample
    rows_p, rows_s = [], []
    for l in range(DEPTH):
        lam_init = 0.8 - 0.6 * math.exp(-0.3 * l)
        lam = (jnp.exp(jnp.sum(lam_q1[l].astype(jnp.float32) * lam_k1[l].astype(jnp.float32)))
               - jnp.exp(jnp.sum(lam_q2[l].astype(jnp.float32) * lam_k2[l].astype(jnp.float32)))
               + lam_init)
        layer_w = (t5_table, w_ada[l], b_ada[l], g_mix[l], g_ffn[l], w_in[l], b_forget[l],
                   g_subln[l], w_gate[l], w_pa[l], w_pb[l], w_pc[l], w_out[l], w_ff1[l], w_ff2[l])
        xp, new_p = decoder_layer(xp, c_prompt, pos_prompt, None, lam, lam_init, *layer_w)
        past = tuple(gather_pages(cache[l], page_table) for cache in caches)
        xs, new_s = decoder_layer(xs, c_sample, pos_sample, past, lam, lam_init, *layer_w)
        rows_p.append(new_p)
        rows_s.append(new_s)

    y_prompt = rms_norm(xp, g_final)
    y_sample = rms_norm(xs, g_final)
    p_ak, p_av, p_akidx, p_bk, p_bv, p_ck, p_cv, p_clogf = stack_rows(rows_p)
    s_ak, s_av, s_akidx, s_bk, s_bv, s_ck, s_cv, s_clogf = stack_rows(rows_s)
    return (y_prompt, y_sample,
            p_ak, p_av, p_akidx, p_bk, p_bv, p_ck, p_cv, p_clogf,
            s_ak, s_av, s_akidx, s_bk, s_bv, s_ck, s_cv, s_clogf)
```

```python
import functools
import math

import numpy as np
import jax
import jax.numpy as jnp
from jax import lax
from jax.experimental import pallas as pl
from jax.experimental.pallas import tpu as pltpu

F32 = jnp.float32
BF16 = jnp.bfloat16
I32 = jnp.int32

D = 1024
HD = 64
H_A, KVH_A = 6, 2
H_IDX, D_IDX = 8, 64
K_TOP_MAX = 256
IDX_W_SCALE = (H_IDX ** -0.5) * (D_IDX ** -0.5)
H_B, KVH_B = 4, 2
H_C, KVH_C = 6, 2
N_BUCKETS, T5_MAX_EXACT, T5_MAX_DIST = 32, 16, 128
D_FF = 4 * D
EPS = 1e-6
N_HEADS_T5 = H_A + H_B
PAGE = 128

LANES = 128
TQ = 128
TKC = 128
NEG = -1e30
INT_MIN = -2 ** 31
VMEM_LIMIT = 56 * 1024 * 1024

_W = dict(aq=H_A * HD, ak=KVH_A * HD, av=KVH_A * HD, aqi=H_IDX * D_IDX, aki=D_IDX, awi=H_IDX,
          bq=H_B * 2 * HD, bk=KVH_B * 2 * HD, bv=KVH_B * 2 * HD,
          cq=H_C * HD, ck=KVH_C * HD, cv=KVH_C * HD, cf=H_C)
_OFF = {}
_o = 0
for _k, _v in _W.items():
    _OFF[_k] = _o
    _o += _v

_R_AK, _R_AV, _R_AKI, _R_BK, _R_BV, _R_CK, _R_CV, _R_CF = 0, 128, 256, 384, 640, 896, 1024, 1152
N_ROWS = 1280
_T_AQ, _T_AQI, _T_AWI, _T_BQ, _T_CQ, _T_AV, _T_BV, _T_CV = 0, 384, 896, 912, 1424, 1808, 1936, 2192
N_T = 2320
_S_AQ, _S_AQI, _S_AWI, _S_BQ, _S_CQ = 0, 768, 1792, 2816, 4864
N_SQ = 5632


def _cparams(*sem):
    return pltpu.CompilerParams(dimension_semantics=sem, vmem_limit_bytes=VMEM_LIMIT)


def _dot(a, b):
    return jnp.dot(a, b, preferred_element_type=F32)


def _dot_nt(a, b):
    return lax.dot_general(a, b, (((1,), (1,)), ((), ())), preferred_element_type=F32)


def _split3(x):
    hi = x.astype(BF16)
    r1 = x - hi.astype(F32)
    mid = r1.astype(BF16)
    lo = (r1 - mid.astype(F32)).astype(BF16)
    return hi, mid, lo


def _dot3_rhs(a_bf16, x):
    hi, mid, lo = _split3(x)
    return _dot(a_bf16, hi) + _dot(a_bf16, mid) + _dot(a_bf16, lo)


def _dot3_lhs(x, b_bf16):
    hi, mid, lo = _split3(x)
    return _dot(hi, b_bf16) + _dot(mid, b_bf16) + _dot(lo, b_bf16)


def _norm_mod(x, g, sc, sh):
    ms = jnp.mean(x * x, axis=-1, keepdims=True)
    return (x * lax.rsqrt(ms + EPS) * g) * (1.0 + sc) + sh


def _log_sigmoid(z):
    return jnp.minimum(z, 0.0) - jnp.log1p(jnp.exp(-jnp.abs(z)))


def _sigmoid(z):
    return 1.0 / (1.0 + jnp.exp(-z))


def _sortable_key(x):
    bits = lax.bitcast_convert_type(x, I32)
    return bits ^ ((bits >> 31) & 0x7FFFFFFF)


def _ada_kernel(c_ref, w_ref, b_ref, o_ref):
    c = c_ref[...]
    s = (c * _sigmoid(c)).astype(BF16)
    o_ref[0] = _dot(s, w_ref[0].astype(BF16)) + b_ref[0]


def _ada(c_all, w_ada, b_ada):
    depth = w_ada.shape[0]
    nc = c_all.shape[0]
    nt = 6
    return pl.pallas_call(
        _ada_kernel,
        grid=(depth, nt),
        in_specs=[pl.BlockSpec((nc, D), lambda l, j: (0, 0)),
                  pl.BlockSpec((1, D, D), lambda l, j: (l, 0, j)),
                  pl.BlockSpec((1, 1, D), lambda l, j: (l, 0, j))],
        out_specs=pl.BlockSpec((1, nc, D), lambda l, j: (l, 0, j)),
        out_shape=jax.ShapeDtypeStruct((depth, nc, 6 * D), F32),
        compiler_params=_cparams("arbitrary", "arbitrary"),
        name="ada",
    )(c_all, w_ada, b_ada.reshape(depth, 1, 6 * D))


def _t5_bucket_np(rel):
    n = np.maximum(rel, 0)
    nf = np.maximum(n, 1).astype(np.float32)
    large = T5_MAX_EXACT + (np.log(nf / np.float32(T5_MAX_EXACT)) / np.float32(math.log(T5_MAX_DIST / T5_MAX_EXACT))
                            * np.float32(N_BUCKETS - T5_MAX_EXACT)).astype(np.int32)
    return np.where(n < T5_MAX_EXACT, n, np.minimum(large, N_BUCKETS - 1)).astype(np.int32)


def _bias_bucket_tables(past_len, dec_seq):
    s = np.arange(TKC)[:, None]
    t = np.arange(TQ)[None, :]
    tiles = []
    for d in (0, 1):
        rel = d * TKC + t - s
        tiles.append(np.where(rel >= 0, _t5_bucket_np(rel), -1))
    prompt = np.stack(tiles).astype(np.int32)
    i = np.arange(dec_seq)[:, None]
    lane = np.arange(PAGE)[None, :]
    rel_last = (past_len + i) - (past_len - PAGE + lane)
    rel_new = i - lane
    dl = _t5_bucket_np(rel_last)
    dn = np.where((rel_new >= 0) & (lane < dec_seq), _t5_bucket_np(rel_new), -1)
    dec = np.stack([dl, dn]).astype(np.int32)
    return prompt, dec


def _bias_kernel(tab_ref, pb_ref, db_ref, pt_ref, dt_ref):
    pb = pb_ref[...]
    db = db_ref[...]
    for h in range(N_HEADS_T5):
        def lut(bk):
            acc = jnp.zeros(bk.shape, F32)
            for b in range(N_BUCKETS):
                acc = jnp.where(bk == b, tab_ref[b, h], acc)
            return jnp.where(bk < 0, NEG, acc - tab_ref[N_BUCKETS - 1, h])
        pt_ref[h] = lut(pb)
        dt_ref[h] = lut(db)


def _bias_tiles(t5_table, past_len, dec_seq):
    pb, db = _bias_bucket_tables(past_len, dec_seq)
    return pl.pallas_call(
        _bias_kernel,
        in_specs=[pl.BlockSpec(memory_space=pltpu.SMEM),
                  pl.BlockSpec(memory_space=pltpu.VMEM),
                  pl.BlockSpec(memory_space=pltpu.VMEM)],
        out_specs=[pl.BlockSpec(memory_space=pltpu.VMEM), pl.BlockSpec(memory_space=pltpu.VMEM)],
        out_shape=[jax.ShapeDtypeStruct((N_HEADS_T5,) + pb.shape, F32),
                   jax.ShapeDtypeStruct((N_HEADS_T5,) + db.shape, F32)],
        name="t5_bias_tiles",
    )(t5_table, jnp.asarray(pb), jnp.asarray(db))


def _seg(w, name):
    return w[:, _OFF[name]:_OFF[name] + _W[name]]


def _cf_block(cf):
    n = cf.shape[0]
    z2 = jnp.zeros((n, 2), cf.dtype)
    return jnp.concatenate([cf, z2, cf, z2, cf, z2, jnp.zeros((n, LANES - 24), cf.dtype)], axis=1)


def _rows_block(w):
    z64 = jnp.zeros((D, 64), w.dtype)
    return jnp.concatenate([_seg(w, "ak"), _seg(w, "av"), _seg(w, "aki"), z64, _seg(w, "bk"), _seg(w, "bv"),
                            _seg(w, "ck"), _seg(w, "cv"), _cf_block(_seg(w, "cf"))], axis=1)


def _prep_in_prompt(w):
    scale = HD ** -0.5
    z8 = jnp.zeros((D, 8), w.dtype)
    w_t = jnp.concatenate([_seg(w, "aq") * scale, _seg(w, "aqi"), _seg(w, "awi"), z8, _seg(w, "bq") * scale,
                           _seg(w, "cq") * scale, _seg(w, "av"), _seg(w, "bv"), _seg(w, "cv")], axis=1)
    return _rows_block(w).astype(BF16), w_t.T.astype(BF16)


def _pad_heads(wq, n_heads, kv_of, width, n_slots):
    blocks = []
    for h in range(n_heads):
        blk = jnp.zeros((D, n_slots * HD), wq.dtype)
        s = kv_of(h)
        blk = blk.at[:, s * HD:(s + 1) * HD].set(wq[:, h * width:(h + 1) * width])
        blocks.append(blk)
    return jnp.concatenate(blocks, axis=1)


def _prep_in_sample(w):
    scale = HD ** -0.5
    aq = _pad_heads(_seg(w, "aq") * scale, H_A, lambda h: h // (H_A // KVH_A), HD, 2)
    aqi = _pad_heads(_seg(w, "aqi"), H_IDX, lambda h: 0, D_IDX, 2)
    awi = jnp.repeat(_seg(w, "awi"), LANES, axis=1)
    bq = _pad_heads(_seg(w, "bq") * scale, 2 * H_B, lambda hc: (hc // 2 // (H_B // KVH_B)) * 2 + hc % 2, HD, 4)
    cq = _pad_heads(_seg(w, "cq") * scale, H_C, lambda h: h // (H_C // KVH_C), HD, 2)
    return jnp.concatenate([aq, aqi, awi, bq, cq, _rows_block(w)], axis=1).astype(BF16)


def _pad_proj_rows(wp, n_heads, kv_of):
    blocks = []
    for h in range(n_heads):
        blk = jnp.zeros((2 * HD, D), wp.dtype)
        s = kv_of(h)
        blk = blk.at[s * HD:(s + 1) * HD].set(wp[h * HD:(h + 1) * HD])
        blocks.append(blk)
    return jnp.concatenate(blocks, axis=0)


def _store_rows(pr, bf, ak_o, av_o, aki_o, bk_o, bv_o, ck_o, cv_o, lf_o, idx):
    ak_o[idx] = pr[:, _R_AK:_R_AK + 128]
    av_o[idx] = pr[:, _R_AV:_R_AV + 128]
    aki_o[idx] = pr[:, _R_AKI:_R_AKI + 64]
    bk_o[idx] = pr[:, _R_BK:_R_BK + 256]
    bv_o[idx] = pr[:, _R_BV:_R_BV + 256]
    ck_o[idx] = pr[:, _R_CK:_R_CK + 128]
    cv_o[idx] = pr[:, _R_CV:_R_CV + 128]
    lf = _log_sigmoid(pr[:, _R_CF:_R_CF + 128] + bf)
    lf_o[idx] = lf[:, 0:H_C]
    return lf


def _inproj_p_kernel(x_ref, mod_ref, g_ref, wr_ref, wt_ref, bf_ref,
                     ak_o, av_o, aki_o, bk_o, bv_o, ck_o, cv_o, lf_o,
                     akb_o, akib_o, bkb_o, ckb_o, lfrep_o,
                     aqT_o, aqiT_o, awiT_o, bqT_o, cqT_o, avT_o, bvT_o, cvT_o):
    x = x_ref[0]
    mod = mod_ref[0]
    h = _norm_mod(x, g_ref[...], mod[:, D:2 * D], mod[:, 0:D])
    pr = _dot(h.astype(BF16), wr_ref[...])
    lf = _store_rows(pr, bf_ref[...], ak_o, av_o, aki_o, bk_o, bv_o, ck_o, cv_o, lf_o, 0)
    akb_o[0] = pr[:, _R_AK:_R_AK + 128].astype(BF16)
    akib_o[0] = pr[:, _R_AKI:_R_AKI + 64].astype(BF16)
    bkb_o[0] = pr[:, _R_BK:_R_BK + 256].astype(BF16)
    ckb_o[0] = pr[:, _R_CK:_R_CK + 128].astype(BF16)
    lfrep_o[0] = lf
    pt = _dot(wt_ref[...], h.T.astype(BF16))
    aqT_o[0] = pt[_T_AQ:_T_AQ + 384].astype(BF16)
    aqiT_o[0] = pt[_T_AQI:_T_AQI + 512].astype(BF16)
    awiT_o[0] = pt[_T_AWI:_T_AWI + 16]
    bqT_o[0] = pt[_T_BQ:_T_BQ + 512].astype(BF16)
    cqT_o[0] = pt[_T_CQ:_T_CQ + 384].astype(BF16)
    avT_o[0] = pt[_T_AV:_T_AV + 128].astype(BF16)
    bvT_o[0] = pt[_T_BV:_T_BV + 256].astype(BF16)
    cvT_o[0] = pt[_T_CV:_T_CV + 128].astype(BF16)


def _inproj_prompt(x, mod_a, g, wr, wt, bfb, tm):
    b, l, _ = x.shape
    row = lambda w, dt: jax.ShapeDtypeStruct((b, l, w), dt)
    tr = lambda r, dt: jax.ShapeDtypeStruct((b, r, l), dt)
    row_spec = lambda w: pl.BlockSpec((1, tm, w), lambda i, j: (i, j, 0))
    tr_spec = lambda r: pl.BlockSpec((1, r, tm), lambda i, j: (i, 0, j))
    row_w = [128, 128, 64, 256, 256, 128, 128, H_C]
    rowb_w = [128, 64, 256, 128]
    tr_r = [(384, BF16), (512, BF16), (16, F32), (512, BF16), (384, BF16), (128, BF16), (256, BF16), (128, BF16)]
    out_shape = ([row(w, F32) for w in row_w] + [row(w, BF16) for w in rowb_w] + [row(128, F32)]
                 + [tr(r, dt) for r, dt in tr_r])
    out_specs = ([row_spec(w) for w in row_w] + [row_spec(w) for w in rowb_w] + [row_spec(128)]
                 + [tr_spec(r) for r, _ in tr_r])
    return pl.pallas_call(
        _inproj_p_kernel,
        grid=(b, l // tm),
        in_specs=[pl.BlockSpec((1, tm, D), lambda i, j: (i, j, 0)),
                  pl.BlockSpec((1, 1, 3 * D), lambda i, j: (i, 0, 0)),
                  pl.BlockSpec((1, D), lambda i, j: (0, 0)),
                  pl.BlockSpec((D, N_ROWS), lambda i, j: (0, 0)),
                  pl.BlockSpec((N_T, D), lambda i, j: (0, 0)),
                  pl.BlockSpec((1, LANES), lambda i, j: (0, 0))],
        out_specs=out_specs,
        out_shape=out_shape,
        compiler_params=_cparams("arbitrary", "arbitrary"),
        name="inproj_prompt",
    )(x, mod_a, g, wr, wt, bfb)


def _inproj_s_kernel(x_ref, mod_ref, g_ref, w_ref, bf_ref,
                     ak_o, av_o, aki_o, bk_o, bv_o, ck_o, cv_o, lf_o,
                     qa_o, qi_o, wi_o, qb_o, qc_o, lfrep_o):
    x = x_ref[...]
    gsz, r, _ = x.shape
    mod = mod_ref[...]
    h = _norm_mod(x, g_ref[...], mod[:, :, D:2 * D], mod[:, :, 0:D]).reshape(gsz * r, D)
    pr = _dot(h.astype(BF16), w_ref[...])
    lf = _store_rows(pr[:, N_SQ:], bf_ref[...], ak_o, av_o, aki_o, bk_o, bv_o, ck_o, cv_o, lf_o,
                     (slice(None), slice(None)))
    lfrep_o[...] = lf
    qa_o[...] = pr[:, _S_AQ:_S_AQ + 768].astype(BF16)
    qi_o[...] = pr[:, _S_AQI:_S_AQI + 1024].astype(BF16)
    wi_o[...] = pr[:, _S_AWI:_S_AWI + 1024]
    qb_o[...] = pr[:, _S_BQ:_S_BQ + 2048].astype(BF16)
    qc_o[...] = pr[:, _S_CQ:_S_CQ + 768].astype(BF16)


def _inproj_sample(x, mod_a, g, w, bfb, gsz):
    nb, r, _ = x.shape
    n = nb * r
    tm = gsz * r
    row_w = [128, 128, 64, 256, 256, 128, 128, H_C]
    outs = [(w_, F32) for w_ in row_w] + [(768, BF16), (1024, BF16), (1024, F32), (2048, BF16), (768, BF16), (128, F32)]
    return pl.pallas_call(
        _inproj_s_kernel,
        grid=(nb // gsz,),
        in_specs=[pl.BlockSpec((gsz, r, D), lambda i: (i, 0, 0)),
                  pl.BlockSpec((gsz, 1, 3 * D), lambda i: (i, 0, 0)),
                  pl.BlockSpec((1, D), lambda i: (0, 0)),
                  pl.BlockSpec((D, N_SQ + N_ROWS), lambda i: (0, 0)),
                  pl.BlockSpec((1, LANES), lambda i: (0, 0))],
        out_specs=[pl.BlockSpec((tm, w_), lambda i: (i, 0)) for w_, _ in outs],
        out_shape=[jax.ShapeDtypeStruct((n, w_), dt) for w_, dt in outs],
        compiler_params=_cparams("arbitrary"),
        name="inproj_sample",
    )(x, mod_a, g, w, bfb)


_CUM_T = 256


def _cum_kernel(lf_ref, o_ref):
    l = lf_ref.shape[1]
    r = lax.broadcasted_iota(I32, (_CUM_T, _CUM_T), 0)
    c = lax.broadcasted_iota(I32, (_CUM_T, _CUM_T), 1)
    tri = jnp.where(c <= r, 1.0, 0.0).astype(BF16)
    lane = lax.broadcasted_iota(I32, (_CUM_T, LANES), 1)
    carry = jnp.zeros((1, LANES), F32)
    for i in range(l // _CUM_T):
        x = lf_ref[0, i * _CUM_T:(i + 1) * _CUM_T, :]
        cum = _dot3_rhs(tri, x) + carry
        carry = cum[_CUM_T - 1:_CUM_T, :]
        hi, mid, lo = _split3(cum)
        piece = jnp.where(lane < 8, hi, jnp.where(lane < 16, mid, lo))
        o_ref[0, i * _CUM_T:(i + 1) * _CUM_T, :] = -piece


def _cum_prompt(lfrep):
    b, l, _ = lfrep.shape
    return pl.pallas_call(
        _cum_kernel,
        grid=(b,),
        in_specs=[pl.BlockSpec((1, l, LANES), lambda i: (i, 0, 0))],
        out_specs=pl.BlockSpec((1, l, LANES), lambda i: (i, 0, 0)),
        out_shape=jax.ShapeDtypeStruct((b, l, LANES), BF16),
        compiler_params=_cparams("arbitrary"),
        name="forget_cumsum",
    )(lfrep)


def _online_update(s, m_scr, acc_scr, vta):
    m_prev = m_scr[...]
    m_new = jnp.maximum(m_prev, jnp.max(s, axis=0, keepdims=True))
    alpha = jnp.exp(m_prev - m_new)
    p = jnp.exp(s - m_new).astype(BF16)
    m_scr[...] = m_new
    n_g = len(vta)
    w = s.shape[1] // n_g
    for g in range(n_g):
        pv = _dot(vta[g], p[:, g * w:(g + 1) * w])
        acc_scr[g] = acc_scr[g] * alpha[:, g * w:(g + 1) * w] + pv


def _v_aug(vt_ref, j, g, rows):
    v = vt_ref[0, g * rows:(g + 1) * rows, pl.ds(pl.multiple_of(j * TKC, TKC), TKC)]
    return jnp.concatenate([v, jnp.ones((16, TKC), BF16)], axis=0)


def _chunk_loop(qi, chunk):
    def far(j, carry):
        chunk(j, None)
        return carry
    lax.fori_loop(0, qi - 1, far, 0)

    @pl.when(qi >= 1)
    def _():
        chunk(qi - 1, 1)
    chunk(qi, 0)


def _attn_a_kernel(qT_ref, qiT_ref, wT_ref, k_ref, ki_ref, vT_ref, bias_ref, oT_ref,
                   qpad, qipad, key_scr, am_scr, j_scr, m_scr, acc_scr, *, k_top):
    b = pl.program_id(0)
    qi = pl.program_id(1)
    n_chunk = qi + 1
    grp = H_A // KVH_A

    @pl.when((b == 0) & (qi == 0))
    def _():
        qpad[...] = jnp.zeros_like(qpad)
    for h in range(H_A):
        g = h // grp
        qpad[g * HD:(g + 1) * HD, h * TQ:(h + 1) * TQ] = qT_ref[0, h * HD:(h + 1) * HD, :]
    for h in range(H_IDX):
        qipad[:, h * TQ:(h + 1) * TQ] = qiT_ref[0, h * D_IDX:(h + 1) * D_IDX, :]
    w = wT_ref[0, 0:H_IDX, :] * IDX_W_SCALE
    row = lax.broadcasted_iota(I32, (TKC, TQ), 0)
    col = lax.broadcasted_iota(I32, (TKC, TQ), 1)

    def score_chunk(j, diag):
        sl = pl.ds(pl.multiple_of(j * TKC, TKC), TKC)
        s = _dot(ki_ref[0, sl, :], qipad[...])
        sc = jnp.zeros((TKC, TQ), F32)
        for h in range(H_IDX):
            sc = sc + jnp.maximum(s[:, h * TQ:(h + 1) * TQ], 0.0) * w[h:h + 1, :]
        sc = jnp.where(sc == 0.0, 0.0, sc)
        if diag:
            sc = jnp.where(row > col, -jnp.inf, sc)
        key_scr[sl, :] = _sortable_key(sc)

    def sc_body(j, carry):
        score_chunk(j, False)
        return carry
    lax.fori_loop(0, qi, sc_body, 0)
    score_chunk(qi, True)

    def count(pred):
        def body(j, cnt):
            sl = pl.ds(pl.multiple_of(j * TKC, TKC), TKC)
            hit = jnp.where(pred(key_scr[sl, :], j), 1, 0).astype(I32)
            return cnt + jnp.sum(hit.reshape(TKC // 8, 8, TQ), axis=0)
        cnt8 = lax.fori_loop(0, n_chunk, body, jnp.zeros((8, TQ), I32))
        return jnp.sum(cnt8, axis=0, keepdims=True)

    needs_search = n_chunk * TQ > k_top

    @pl.when(jnp.logical_not(needs_search))
    def _():
        def body(j, carry):
            am_scr[pl.ds(pl.multiple_of(j * TKC, TKC), TKC), :] = jnp.zeros((TKC, TQ), F32)
            return carry
        lax.fori_loop(0, n_chunk, body, 0)

    @pl.when(needs_search)
    def _():
        def it_body(it, t):
            cand = t ^ lax.shift_left(jnp.int32(1), 31 - it)
            cnt = count(lambda k, j: k >= cand)
            return jnp.where(cnt >= k_top, cand, t)
        thr = lax.fori_loop(0, 32, it_body, jnp.full((1, TQ), INT_MIN, I32))
        cnt_gt = count(lambda k, j: k > thr)
        cnt_eq = count(lambda k, j: k == thr)
        need = k_top - cnt_gt
        n_bits = max(1, (key_scr.shape[0] - 1).bit_length())
        j_scr[...] = jnp.full((1, TQ), key_scr.shape[0], I32)

        @pl.when(jnp.max(jnp.where(cnt_eq > need, 1, 0)) > 0)
        def _():
            def tie_body(it, jv):
                cand = jv | lax.shift_left(jnp.int32(1), n_bits - 1 - it)
                cnt = count(lambda k, j: (k == thr) & ((row + j * TKC) < cand))
                return jnp.where(cnt < need, cand, jv)
            j_scr[...] = lax.fori_loop(0, n_bits, tie_body, jnp.zeros((1, TQ), I32))
        jv = j_scr[...]

        def body(j, carry):
            sl = pl.ds(pl.multiple_of(j * TKC, TKC), TKC)
            k = key_scr[sl, :]
            sel = (k > thr) | ((k == thr) & ((row + j * TKC) <= jv))
            am_scr[sl, :] = jnp.where(sel, 0.0, NEG)
            return carry
        lax.fori_loop(0, n_chunk, body, 0)

    m_scr[...] = jnp.full(m_scr.shape, NEG, F32)
    acc_scr[...] = jnp.zeros_like(acc_scr)

    def chunk(j, kind):
        sl = pl.ds(pl.multiple_of(j * TKC, TKC), TKC)
        s = _dot(k_ref[0, sl, :], qpad[...])
        am = am_scr[sl, :]
        parts = []
        for h in range(H_A):
            sh = s[:, h * TQ:(h + 1) * TQ] + am
            if kind is not None:
                sh = sh + bias_ref[h, kind]
            parts.append(sh)
        s = jnp.concatenate(parts, axis=1)
        _online_update(s, m_scr, acc_scr, [_v_aug(vT_ref, j, g, HD) for g in range(KVH_A)])

    _chunk_loop(qi, chunk)

    for h in range(H_A):
        g, hh = h // grp, h % grp
        a = acc_scr[g][:, hh * TQ:(hh + 1) * TQ]
        oT_ref[0, h * HD:(h + 1) * HD, :] = a[0:HD] / a[HD:HD + 1]


def _attn_a(aqT, aqiT, awiT, akb, akib, avT, bias_pt):
    b, _, l = aqT.shape
    k_top = min(K_TOP_MAX, l // 4)
    blk_t = lambda r: pl.BlockSpec((1, r, TQ), lambda i, j: (i, 0, j))
    full = lambda s1, s2: pl.BlockSpec((1, s1, s2), lambda i, j: (i, 0, 0))
    return pl.pallas_call(
        functools.partial(_attn_a_kernel, k_top=k_top),
        grid=(b, l // TQ),
        in_specs=[blk_t(384), blk_t(512), blk_t(16), full(l, 128), full(l, 64), full(128, l),
                  pl.BlockSpec((H_A, 2, TKC, TQ), lambda i, j: (0, 0, 0, 0))],
        out_specs=blk_t(384),
        out_shape=jax.ShapeDtypeStruct((b, 384, l), F32),
        scratch_shapes=[pltpu.VMEM((128, H_A * TQ), BF16),
                        pltpu.VMEM((D_IDX, H_IDX * TQ), BF16),
                        pltpu.VMEM((l, TQ), I32),
                        pltpu.VMEM((l, TQ), F32),
                        pltpu.VMEM((1, TQ), I32),
                        pltpu.VMEM((1, H_A * TQ), F32),
                        pltpu.VMEM((KVH_A, HD + 16, (H_A // KVH_A) * TQ), F32)],
        compiler_params=_cparams("arbitrary", "arbitrary"),
        name="attn_dsa_prompt",
    )(aqT, aqiT, awiT, akb, akib, avT, bias_pt)


def _lambda_value(lam_ref, lam_init):
    lv = lam_ref[...]
    s1 = jnp.sum(lv[0:1] * lv[1:2], axis=1, keepdims=True)
    s2 = jnp.sum(lv[2:3] * lv[3:4], axis=1, keepdims=True)
    return jnp.exp(s1) - jnp.exp(s2) + lam_init


def _attn_b_kernel(qT_ref, k_ref, vT_ref, bias_ref, lam_ref, gsub_ref, oT_ref,
                   qpad, m_scr, acc_scr, *, lam_init):
    b = pl.program_id(0)
    qi = pl.program_id(1)
    grp = H_B // KVH_B

    @pl.when((b == 0) & (qi == 0))
    def _():
        qpad[...] = jnp.zeros_like(qpad)
    for h in range(H_B):
        for c in range(2):
            slot = (h // grp) * 2 + c
            hc = h * 2 + c
            qpad[slot * HD:(slot + 1) * HD, hc * TQ:(hc + 1) * TQ] = qT_ref[0, hc * HD:(hc + 1) * HD, :]

    m_scr[...] = jnp.full(m_scr.shape, NEG, F32)
    acc_scr[...] = jnp.zeros_like(acc_scr)

    def chunk(j, kind):
        sl = pl.ds(pl.multiple_of(j * TKC, TKC), TKC)
        s = _dot(k_ref[0, sl, :], qpad[...])
        if kind is not None:
            parts = []
            for hc in range(2 * H_B):
                parts.append(s[:, hc * TQ:(hc + 1) * TQ] + bias_ref[hc // 2, kind])
            s = jnp.concatenate(parts, axis=1)
        _online_update(s, m_scr, acc_scr, [_v_aug(vT_ref, j, g, 2 * HD) for g in range(KVH_B)])

    _chunk_loop(qi, chunk)

    lam = _lambda_value(lam_ref, lam_init)
    gs = gsub_ref[...]
    for h in range(H_B):
        g, hh = h // grp, h % grp
        a1 = acc_scr[g][:, (hh * 2) * TQ:(hh * 2 + 1) * TQ]
        a2 = acc_scr[g][:, (hh * 2 + 1) * TQ:(hh * 2 + 2) * TQ]
        o = a1[0:2 * HD] / a1[2 * HD:2 * HD + 1] - lam * (a2[0:2 * HD] / a2[2 * HD:2 * HD + 1])
        ms = jnp.mean(o * o, axis=0, keepdims=True)
        oT_ref[0, h * 2 * HD:(h + 1) * 2 * HD, :] = (o * lax.rsqrt(ms + EPS) * gs) * (1.0 - lam_init)


def _attn_b(bqT, bkb, bvT, bias_pt, lamvec, gsub_col, lam_init):
    b, _, l = bqT.shape
    blk_t = lambda r: pl.BlockSpec((1, r, TQ), lambda i, j: (i, 0, j))
    full = lambda s1, s2: pl.BlockSpec((1, s1, s2), lambda i, j: (i, 0, 0))
    return pl.pallas_call(
        functools.partial(_attn_b_kernel, lam_init=lam_init),
        grid=(b, l // TQ),
        in_specs=[blk_t(512), full(l, 256), full(256, l),
                  pl.BlockSpec((H_B, 2, TKC, TQ), lambda i, j: (0, 0, 0, 0)),
                  pl.BlockSpec((4, HD), lambda i, j: (0, 0)),
                  pl.BlockSpec((2 * HD, 1), lambda i, j: (0, 0))],
        out_specs=blk_t(512),
        out_shape=jax.ShapeDtypeStruct((b, 512, l), F32),
        scratch_shapes=[pltpu.VMEM((256, 2 * H_B * TQ), BF16),
                        pltpu.VMEM((1, 2 * H_B * TQ), F32),
                        pltpu.VMEM((KVH_B, 2 * HD + 16, 2 * (H_B // KVH_B) * TQ), F32)],
        compiler_params=_cparams("arbitrary", "arbitrary"),
        name="attn_diff_prompt",
    )(bqT, bkb, bvT, bias_pt, lamvec, gsub_col)


def _attn_c_kernel(qT_ref, k_ref, kaug_ref, vT_ref, oT_ref, qpad, m_scr, acc_scr):
    b = pl.program_id(0)
    qi = pl.program_id(1)
    grp = H_C // KVH_C

    @pl.when((b == 0) & (qi == 0))
    def _():
        r = lax.broadcasted_iota(I32, (256, H_C * TQ), 0) - 128
        cblk = lax.broadcasted_iota(I32, (256, H_C * TQ), 1) // TQ
        ones = (r >= 0) & (r < 24) & ((r % 8) == cblk)
        qpad[...] = jnp.where(ones, 1.0, 0.0).astype(BF16)
    for h in range(H_C):
        g = h // grp
        qpad[g * HD:(g + 1) * HD, h * TQ:(h + 1) * TQ] = qT_ref[0, h * HD:(h + 1) * HD, :]

    m_scr[...] = jnp.full(m_scr.shape, NEG, F32)
    acc_scr[...] = jnp.zeros_like(acc_scr)
    row = lax.broadcasted_iota(I32, (TKC, TQ), 0)
    col = lax.broadcasted_iota(I32, (TKC, TQ), 1)
    causal = jnp.where(row > col, NEG, 0.0)

    def chunk(j, kind):
        sl = pl.ds(pl.multiple_of(j * TKC, TKC), TKC)
        s = _dot(k_ref[0, sl, :], qpad[0:128, :]) + _dot(kaug_ref[0, sl, :], qpad[128:256, :])
        if kind == 0:
            s = s + jnp.concatenate([causal] * H_C, axis=1)
        _online_update(s, m_scr, acc_scr, [_v_aug(vT_ref, j, g, HD) for g in range(KVH_C)])

    def far(j, carry):
        chunk(j, None)
        return carry
    lax.fori_loop(0, qi, far, 0)
    chunk(qi, 0)

    for h in range(H_C):
        g, hh = h // grp, h % grp
        a = acc_scr[g][:, hh * TQ:(hh + 1) * TQ]
        oT_ref[0, h * HD:(h + 1) * HD, :] = a[0:HD] / a[HD:HD + 1]


def _attn_c(cqT, ckb, kaug, cvT):
    b, _, l = cqT.shape
    blk_t = lambda r: pl.BlockSpec((1, r, TQ), lambda i, j: (i, 0, j))
    full = lambda s1, s2: pl.BlockSpec((1, s1, s2), lambda i, j: (i, 0, 0))
    return pl.pallas_call(
        _attn_c_kernel,
        grid=(b, l // TQ),
        in_specs=[blk_t(384), full(l, 128), full(l, 128), full(128, l)],
        out_specs=blk_t(384),
        out_shape=jax.ShapeDtypeStruct((b, 384, l), F32),
        scratch_shapes=[pltpu.VMEM((256, H_C * TQ), BF16),
                        pltpu.VMEM((1, H_C * TQ), F32),
                        pltpu.VMEM((KVH_C, HD + 16, (H_C // KVH_C) * TQ), F32)],
        compiler_params=_cparams("arbitrary", "arbitrary"),
        name="attn_forget_prompt",
    )(cqT, ckb, kaug, cvT)


_N_CACHE = 8


def _stack_heads(ref, n, width):
    return jnp.concatenate([ref[:, h * width:(h + 1) * width] for h in range(n)], axis=0)


def _row_softmax_pv(s_pages, v_pages):
    mx = s_pages[0]
    for s in s_pages[1:]:
        mx = jnp.maximum(mx, s)
    m = jnp.max(mx, axis=1, keepdims=True)
    acc = None
    lsum = None
    for s, v in zip(s_pages, v_pages):
        p = jnp.exp(s - m)
        lsum = p if lsum is None else lsum + p
        pv = _dot(p.astype(BF16), v)
        acc = pv if acc is None else acc + pv
    return acc, jnp.sum(lsum, axis=1, keepdims=True)


def _decode_kernel(pt_ref, qa_ref, qi_ref, wi_ref, qb_ref, qc_ref,
                   ak_n, av_n, aki_n, bk_n, bv_n, ck_n, cv_n, lfT_n,
                   bias_ref, lam_ref, gsub_ref, *rest, n_pages, k_top, lam_init):
    pages = rest[:_N_CACHE * n_pages]
    oa_ref, ob_ref, oc_ref = rest[_N_CACHE * n_pages:]
    r = qa_ref.shape[0]
    n_all = n_pages + 1

    def page(c, p):
        return pages[c * n_pages + p][0, 0]

    def pad_new(ref):
        x = ref[...]
        return jnp.concatenate([x, jnp.zeros((PAGE - r, x.shape[1]), x.dtype)], axis=0)

    def keys(c, new_ref):
        return [page(c, p).astype(BF16) for p in range(n_pages)] + [pad_new(new_ref).astype(BF16)]

    lane = lax.broadcasted_iota(I32, (r, PAGE), 1)
    qrow = lax.broadcasted_iota(I32, (r, PAGE), 0)
    new_visible = lane <= qrow

    qi2 = _stack_heads(qi_ref, H_IDX, 2 * D_IDX)[:, 0:D_IDX]
    wcol = _stack_heads(wi_ref, H_IDX, LANES) * IDX_W_SCALE
    key_pages = []
    for p, kip in enumerate(keys(2, aki_n)):
        z = jnp.maximum(_dot_nt(qi2, kip), 0.0) * wcol
        sc = z[0:r]
        for h in range(1, H_IDX):
            sc = sc + z[h * r:(h + 1) * r]
        sc = jnp.where(sc == 0.0, 0.0, sc)
        if p == n_pages:
            sc = jnp.where(new_visible, sc, -jnp.inf)
        key_pages.append(_sortable_key(sc))

    def count(pred):
        tot = None
        for p, k in enumerate(key_pages):
            hit = jnp.where(pred(k, p), 1.0, 0.0)
            tot = hit if tot is None else tot + hit
        return jnp.sum(tot, axis=1, keepdims=True)

    thr = jnp.full((r, 1), INT_MIN, I32)
    for it in range(32):
        cand = thr ^ jnp.int32(-2 ** 31 if it == 0 else 1 << (31 - it))
        cnt = count(lambda k, p: k >= cand)
        thr = jnp.where(cnt >= k_top, cand, thr)
    cnt_gt = count(lambda k, p: k > thr)
    cnt_eq = count(lambda k, p: k == thr)
    need = k_top - cnt_gt
    n_bits = max(1, (n_all * PAGE - 1).bit_length())

    def tie_search():
        jv = jnp.zeros((r, 1), I32)
        for it in range(n_bits):
            cand = jv | jnp.int32(1 << (n_bits - 1 - it))
            cnt = count(lambda k, p: (k == thr) & ((lane + p * PAGE) < cand))
            jv = jnp.where(cnt < need, cand, jv)
        return jv

    any_excess = jnp.max(jnp.where(cnt_eq > need, 1, 0)) > 0
    jv = lax.cond(any_excess, tie_search, lambda: jnp.full((r, 1), n_all * PAGE, I32))
    am_pages = []
    for p, k in enumerate(key_pages):
        sel = (k > thr) | ((k == thr) & ((lane + p * PAGE) <= jv))
        am_pages.append(jnp.where(sel, 0.0, NEG))

    def bias_rows(h0, n_h, rep, kind):
        return jnp.concatenate([bias_ref[h0 + h, kind] for h in range(n_h) for _ in range(rep)], axis=0)

    qa2 = _stack_heads(qa_ref, H_A, 2 * HD)
    s_pages = []
    for p, kp in enumerate(keys(0, ak_n)):
        s = _dot_nt(qa2, kp) + jnp.concatenate([am_pages[p]] * H_A, axis=0)
        if p == n_pages - 1:
            s = s + bias_rows(0, H_A, 1, 0)
        if p == n_pages:
            s = s + bias_rows(0, H_A, 1, 1)
        s_pages.append(s)
    acc, lsum = _row_softmax_pv(s_pages, keys(1, av_n))
    o = acc / lsum
    oa_ref[...] = jnp.concatenate([o[h * r:(h + 1) * r] for h in range(H_A)], axis=1)

    qb2 = _stack_heads(qb_ref, 2 * H_B, 4 * HD)
    s_pages = []
    for p, kp in enumerate(keys(3, bk_n)):
        s = _dot_nt(qb2, kp)
        if p == n_pages - 1:
            s = s + bias_rows(H_A, H_B, 2, 0)
        if p == n_pages:
            s = s + bias_rows(H_A, H_B, 2, 1)
        s_pages.append(s)
    acc, lsum = _row_softmax_pv(s_pages, keys(4, bv_n))
    o = acc / lsum
    lam = _lambda_value(lam_ref, lam_init)
    gs = gsub_ref[...]
    outs = []
    for h in range(H_B):
        g = h // (H_B // KVH_B)
        o1 = o[(2 * h) * r:(2 * h + 1) * r, g * 2 * HD:(g + 1) * 2 * HD]
        o2 = o[(2 * h + 1) * r:(2 * h + 2) * r, g * 2 * HD:(g + 1) * 2 * HD]
        od = o1 - lam * o2
        ms = jnp.mean(od * od, axis=1, keepdims=True)
        outs.append((od * lax.rsqrt(ms + EPS) * gs) * (1.0 - lam_init))
    ob_ref[...] = jnp.concatenate(outs, axis=1)

    x = jnp.concatenate([page(7, p) for p in range(n_pages)] + [lfT_n[0]], axis=0)
    ri = lax.broadcasted_iota(I32, (PAGE, PAGE), 0)
    ci = lax.broadcasted_iota(I32, (PAGE, PAGE), 1)
    upper = jnp.where(ri <= ci, 1.0, 0.0).astype(BF16)
    ones = jnp.ones((PAGE, PAGE), BF16)
    nr = 8 * n_all
    rr = lax.broadcasted_iota(I32, (nr, nr), 0)
    cc = lax.broadcasted_iota(I32, (nr, nr), 1)
    prev_pages = jnp.where(((rr % 8) == (cc % 8)) & ((cc // 8) < (rr // 8)), 1.0, 0.0).astype(BF16)
    cum = _dot3_lhs(x, upper) + _dot3_rhs(prev_pages, _dot3_lhs(x, ones))

    qc2 = _stack_heads(qc_ref, H_C, 2 * HD)
    causal_new = jnp.concatenate([jnp.where(new_visible, 0.0, NEG)] * H_C, axis=0)
    s_pages = []
    for p, kp in enumerate(keys(5, ck_n)):
        decay = jnp.concatenate(
            [jnp.broadcast_to(cum[p * 8 + h:p * 8 + h + 1, :], (r, PAGE)) for h in range(H_C)], axis=0)
        s = _dot_nt(qc2, kp) - decay
        if p == n_pages:
            s = s + causal_new
        s_pages.append(s)
    acc, lsum = _row_softmax_pv(s_pages, keys(6, cv_n))
    o = acc / lsum
    oc_ref[...] = jnp.concatenate([o[h * r:(h + 1) * r] for h in range(H_C)], axis=1)


def _decode(page_table, layer, qa, qi, wi, qb, qc, new_rows, lfT_new, caches, bias_dec, lamvec, gsub_row,
            lam_init, r):
    nb, n_pages = page_table.shape
    k_top = min(K_TOP_MAX, (n_pages * PAGE + r) // 4)
    tok = lambda w: pl.BlockSpec((r, w), lambda b, pt: (b, 0))
    in_specs = [tok(768), tok(1024), tok(1024), tok(2048), tok(768)]
    in_specs += [tok(w) for w in (128, 128, 64, 256, 256, 128, 128)]
    in_specs += [pl.BlockSpec((1, 8, LANES), lambda b, pt: (b, 0, 0)),
                 pl.BlockSpec((N_HEADS_T5, 2, r, LANES), lambda b, pt: (0, 0, 0, 0)),
                 pl.BlockSpec((4, HD), lambda b, pt: (0, 0)),
                 pl.BlockSpec((1, 2 * HD), lambda b, pt: (0, 0))]
    page_args = []
    for c in caches:
        w = c.shape[-1]
        rows = c.shape[-2]
        for p in range(n_pages):
            in_specs.append(pl.BlockSpec((1, 1, rows, w),
                                         functools.partial(lambda b, pt, p_: (layer, pt[b, p_], 0, 0), p_=p)))
            page_args.append(c)
    outs = [(768, F32), (512, F32), (768, F32)]
    grid_spec = pltpu.PrefetchScalarGridSpec(
        num_scalar_prefetch=1,
        grid=(nb,),
        in_specs=in_specs,
        out_specs=[pl.BlockSpec((r, w), lambda b, pt: (b, 0)) for w, _ in outs])
    return pl.pallas_call(
        functools.partial(_decode_kernel, n_pages=n_pages, k_top=k_top, lam_init=lam_init),
        grid_spec=grid_spec,
        out_shape=[jax.ShapeDtypeStruct((nb * r, w), dt) for w, dt in outs],
        compiler_params=_cparams("arbitrary"),
        name="decode_attention",
    )(page_table, qa, qi, wi, qb, qc, *new_rows, lfT_new, bias_dec, lamvec, gsub_row, *page_args)


def _merge_kernel(x_ref, mod_ref, g_ref, oa_ref, ob_ref, oc_ref, wg_ref, wpa_ref, wpb_ref, wpc_ref, wo_ref,
                  o_ref, *, transposed):
    x = x_ref[...]
    gsz, r, _ = x.shape
    mod = mod_ref[...]
    h = _norm_mod(x, g_ref[...], mod[:, :, D:2 * D], mod[:, :, 0:D]).reshape(gsz * r, D).astype(BF16)
    gates = _sigmoid(_dot(h, wg_ref[...]))
    if transposed:
        oa, ob, oc = oa_ref[0].T, ob_ref[0].T, oc_ref[0].T
    else:
        oa, ob, oc = oa_ref[...], ob_ref[...], oc_ref[...]
    merged = (gates[:, 0:D] * _dot(oa.astype(BF16), wpa_ref[...])
              + gates[:, D:2 * D] * _dot(ob.astype(BF16), wpb_ref[...])
              + gates[:, 2 * D:3 * D] * _dot(oc.astype(BF16), wpc_ref[...]))
    y = _dot(merged.astype(BF16), wo_ref[...]).reshape(gsz, r, D)
    o_ref[...] = x + mod[:, :, 2 * D:3 * D] * y


def _merge(x, mod_a, g, oa, ob, oc, wg, wpa, wpb, wpc, wo, gsz, r, transposed):
    nb, rr, _ = x.shape
    const = lambda a: pl.BlockSpec(a.shape, lambda i, j: (0,) * a.ndim)
    if transposed:
        grid = (nb, rr // r)
        x_spec = pl.BlockSpec((1, r, D), lambda i, j: (i, j, 0))
        mod_spec = pl.BlockSpec((1, 1, 3 * D), lambda i, j: (i, 0, 0))
        o_spec = lambda a: pl.BlockSpec((1, a.shape[1], r), lambda i, j: (i, 0, j))
    else:
        grid = (nb // gsz, 1)
        x_spec = pl.BlockSpec((gsz, rr, D), lambda i, j: (i, 0, 0))
        mod_spec = pl.BlockSpec((gsz, 1, 3 * D), lambda i, j: (i, 0, 0))
        o_spec = lambda a: pl.BlockSpec((gsz * rr, a.shape[1]), lambda i, j: (i, 0))
    return pl.pallas_call(
        functools.partial(_merge_kernel, transposed=transposed),
        grid=grid,
        in_specs=[x_spec, mod_spec, const(g), o_spec(oa), o_spec(ob), o_spec(oc),
                  const(wg), const(wpa), const(wpb), const(wpc), const(wo)],
        out_specs=x_spec,
        out_shape=jax.ShapeDtypeStruct(x.shape, F32),
        compiler_params=_cparams("arbitrary", "arbitrary"),
        name="merge_prompt" if transposed else "merge_sample",
    )(x, mod_a, g, oa, ob, oc, wg, wpa, wpb, wpc, wo)


def _ffn_kernel(x_ref, mod_ref, g_ref, w1_ref, w2_ref, gf_ref, o_ref, *, final):
    x = x_ref[...]
    gsz, r, _ = x.shape
    mod = mod_ref[...]
    h = _norm_mod(x, g_ref[...], mod[:, :, D:2 * D], mod[:, :, 0:D]).reshape(gsz * r, D).astype(BF16)
    u = jnp.maximum(_dot(h, w1_ref[...]), 0.0)
    y = _dot((u * u).astype(BF16), w2_ref[...]).reshape(gsz, r, D)
    x2 = x + mod[:, :, 2 * D:3 * D] * y
    if final:
        ms = jnp.mean(x2 * x2, axis=-1, keepdims=True)
        x2 = x2 * lax.rsqrt(ms + EPS) * gf_ref[...]
    o_ref[...] = x2


def _ffn(x, mod_b, g, w1, w2, g_final, gsz, r, final, name):
    nb, rr, _ = x.shape
    const = lambda a: pl.BlockSpec(a.shape, lambda i, j: (0,) * a.ndim)
    if gsz == 1:
        grid = (nb, rr // r)
        x_spec = pl.BlockSpec((1, r, D), lambda i, j: (i, j, 0))
        mod_spec = pl.BlockSpec((1, 1, 3 * D), lambda i, j: (i, 0, 0))
    else:
        grid = (nb // gsz, 1)
        x_spec = pl.BlockSpec((gsz, rr, D), lambda i, j: (i, 0, 0))
        mod_spec = pl.BlockSpec((gsz, 1, 3 * D), lambda i, j: (i, 0, 0))
    return pl.pallas_call(
        functools.partial(_ffn_kernel, final=final),
        grid=grid,
        in_specs=[x_spec, mod_spec, const(g), const(w1), const(w2), const(g_final)],
        out_specs=x_spec,
        out_shape=jax.ShapeDtypeStruct(x.shape, F32),
        compiler_params=_cparams("arbitrary", "arbitrary"),
        name=name,
    )(x, mod_b, g, w1, w2, g_final)


TM_PROMPT = 512
G_SAMPLE = 16


def kernel(x_prompt, x_sample, c_prompt, c_sample, cache_a_k, cache_a_v, cache_a_kidx, cache_b_k, cache_b_v, cache_c_k, cache_c_v, cache_c_logf, page_table, t5_table, w_ada, b_ada, g_mix, g_ffn, w_in, b_forget, lam_q1, lam_k1, lam_q2, lam_k2, g_subln, w_gate, w_pa, w_pb, w_pc, w_out, w_ff1, w_ff2, g_final):
    depth = w_in.shape[0]
    nbp, seq, _ = x_prompt.shape
    nbs, dec_seq, _ = x_sample.shape
    n_pool, page = cache_a_k.shape[1], cache_a_k.shape[2]
    n_pages = page_table.shape[1]
    past_len = n_pages * page
    assert page == PAGE and seq % max(TM_PROMPT, _CUM_T) == 0 and dec_seq == 8 and nbs % G_SAMPLE == 0
    tm = min(TM_PROMPT, seq)

    nc = nbp + nbs
    mod = _ada(jnp.concatenate([c_prompt, c_sample], axis=0), w_ada, b_ada)
    mod = mod.reshape(depth, nc, 1, 6 * D)
    bias_pt, bias_dec = _bias_tiles(t5_table, past_len, dec_seq)

    flat = lambda c: c.reshape(depth, n_pool, page, -1)
    lf_t = jnp.swapaxes(cache_c_logf, 2, 3)
    lf_t = jnp.concatenate([lf_t, jnp.zeros((depth, n_pool, 8 - H_C, page), F32)], axis=2)
    caches = [flat(cache_a_k), flat(cache_a_v), flat(cache_a_kidx), flat(cache_b_k), flat(cache_b_v),
              flat(cache_c_k), flat(cache_c_v), lf_t]

    xp, xs = x_prompt, x_sample
    rows_p, rows_s = [], []
    g_final2 = g_final.reshape(1, D)
    for l in range(depth):
        lam_init = 0.8 - 0.6 * math.exp(-0.3 * l)
        wr, wt = _prep_in_prompt(w_in[l])
        ws = _prep_in_sample(w_in[l])
        bfb = _cf_block(b_forget[l].reshape(1, H_C))
        wg = w_gate[l].astype(BF16)
        wpa, wpb, wpc = w_pa[l].astype(BF16), w_pb[l].astype(BF16), w_pc[l].astype(BF16)
        wpa_s = _pad_proj_rows(w_pa[l], H_A, lambda h: h // (H_A // KVH_A)).astype(BF16)
        wpc_s = _pad_proj_rows(w_pc[l], H_C, lambda h: h // (H_C // KVH_C)).astype(BF16)
        wo = w_out[l].astype(BF16)
        w1, w2 = w_ff1[l].astype(BF16), w_ff2[l].astype(BF16)
        gm, gf = g_mix[l].reshape(1, D), g_ffn[l].reshape(1, D)
        lamvec = jnp.stack([lam_q1[l], lam_k1[l], lam_q2[l], lam_k2[l]])
        mod_pa, mod_pb = mod[l, :nbp, :, :3 * D], mod[l, :nbp, :, 3 * D:]
        mod_sa, mod_sb = mod[l, nbp:, :, :3 * D], mod[l, nbp:, :, 3 * D:]
        last = l == depth - 1

        (ak, av, aki, bk, bv, ck, cv, lf, akb, akib, bkb, ckb, lfrep,
         aqT, aqiT, awiT, bqT, cqT, avT, bvT, cvT) = _inproj_prompt(xp, mod_pa, gm, wr, wt, bfb, tm)
        rows_p.append((ak, av, aki, bk, bv, ck, cv, lf))
        kaug = _cum_prompt(lfrep)
        oaT = _attn_a(aqT, aqiT, awiT, akb, akib, avT, bias_pt[:H_A])
        obT = _attn_b(bqT, bkb, bvT, bias_pt[H_A:], lamvec, g_subln[l].reshape(2 * HD, 1), lam_init)
        ocT = _attn_c(cqT, ckb, kaug, cvT)
        x1 = _merge(xp, mod_pa, gm, oaT, obT, ocT, wg, wpa, wpb, wpc, wo, 1, tm, True)
        xp = _ffn(x1, mod_pb, gf, w1, w2, g_final2, 1, tm, last, "ffn_prompt")

        (sak, sav, saki, sbk, sbv, sck, scv, slf, qa, qi, wi, qb, qc, slfrep) = _inproj_sample(
            xs, mod_sa, gm, ws, bfb, G_SAMPLE)
        rows_s.append((sak, sav, saki, sbk, sbv, sck, scv, slf))
        lft_new = jnp.swapaxes(slfrep.reshape(nbs, dec_seq, LANES)[:, :, 0:8], 1, 2)
        lft_new = jnp.concatenate([lft_new, jnp.zeros((nbs, 8, LANES - dec_seq), F32)], axis=2)
        lane_head = jnp.arange(8)[None, :, None] < H_C
        lft_new = jnp.where(lane_head, lft_new, 0.0)
        oa, ob, oc = _decode(page_table, l, qa, qi, wi, qb, qc, (sak, sav, saki, sbk, sbv, sck, scv), lft_new,
                             caches, bias_dec, lamvec, g_subln[l].reshape(1, 2 * HD), lam_init, dec_seq)
        x1s = _merge(xs, mod_sa, gm, oa, ob, oc, wg, wpa_s, wpb, wpc_s, wo, G_SAMPLE, dec_seq, False)
        xs = _ffn(x1s, mod_sb, gf, w1, w2, g_final2, G_SAMPLE, dec_seq, last, "ffn_sample")

    def stack(rows, nb, t):
        shapes = [(KVH_A, HD), (KVH_A, HD), (D_IDX,), (KVH_B, 2, HD), (KVH_B, 2 * HD), (KVH_C, HD), (KVH_C, HD),
                  (H_C,)]
        return tuple(jnp.stack([r[i] for r in rows]).reshape((depth, nb, t) + shapes[i]) for i in range(8))

    return (xp, xs) + stack(rows_p, nbp, seq) + stack(rows_s, nbs, dec_seq)
```

```python
import functools
import math

import numpy as np
import jax
import jax.numpy as jnp
from jax import lax
from jax.experimental import pallas as pl
from jax.experimental.pallas import tpu as pltpu

F32 = jnp.float32
BF16 = jnp.bfloat16
I32 = jnp.int32

D = 1024
HD = 64
H_A, KVH_A = 6, 2
H_IDX, D_IDX = 8, 64
K_TOP_MAX = 256
IDX_W_SCALE = (H_IDX ** -0.5) * (D_IDX ** -0.5)
H_B, KVH_B = 4, 2
H_C, KVH_C = 6, 2
N_BUCKETS, T5_MAX_EXACT, T5_MAX_DIST = 32, 16, 128
D_FF = 4 * D
EPS = 1e-6
N_HEADS_T5 = H_A + H_B
PAGE = 128

LANES = 128
TQ = 128
TKC = 128
NEG = -1e30
INT_MIN = -2 ** 31
VMEM_LIMIT = 56 * 1024 * 1024

_W = dict(aq=H_A * HD, ak=KVH_A * HD, av=KVH_A * HD, aqi=H_IDX * D_IDX, aki=D_IDX, awi=H_IDX,
          bq=H_B * 2 * HD, bk=KVH_B * 2 * HD, bv=KVH_B * 2 * HD,
          cq=H_C * HD, ck=KVH_C * HD, cv=KVH_C * HD, cf=H_C)
_OFF = {}
_o = 0
for _k, _v in _W.items():
    _OFF[_k] = _o
    _o += _v

_R_AK, _R_AV, _R_AKI, _R_BK, _R_BV, _R_CK, _R_CV, _R_CF = 0, 128, 256, 384, 640, 896, 1024, 1152
N_ROWS = 1280
_P_AK, _P_AKI, _P_BK, _P_CK, _P_CF, _P_BV = 0, 128, 256, 512, 640, 768
N_PROWS = 1024
_T_AQ, _T_AQI, _T_AWI, _T_BQ, _T_CQ, _T_AV, _T_BV, _T_CV = 0, 384, 896, 912, 1424, 1808, 1936, 2192
_T_AK, _T_AKI, _T_BK, _T_CK = 2320, 2448, 2512, 2768
N_T = 2896
_S_AQ, _S_AQI, _S_AWI, _S_BQ, _S_CQ = 0, 768, 1792, 2816, 4864
N_SQ = 5632


def _cparams(*sem):
    return pltpu.CompilerParams(dimension_semantics=sem, vmem_limit_bytes=VMEM_LIMIT)


def _dot(a, b):
    return jnp.dot(a, b, preferred_element_type=F32)


def _dot_nt(a, b):
    return lax.dot_general(a, b, (((1,), (1,)), ((), ())), preferred_element_type=F32)


def _split3(x):
    hi = x.astype(BF16)
    r1 = x - hi.astype(F32)
    mid = r1.astype(BF16)
    lo = (r1 - mid.astype(F32)).astype(BF16)
    return hi, mid, lo


def _dot3_rhs(a_bf16, x):
    hi, mid, lo = _split3(x)
    return _dot(a_bf16, hi) + _dot(a_bf16, mid) + _dot(a_bf16, lo)


def _dot3_lhs(x, b_bf16):
    hi, mid, lo = _split3(x)
    return _dot(hi, b_bf16) + _dot(mid, b_bf16) + _dot(lo, b_bf16)


def _norm_mod(x, g, sc, sh):
    ms = jnp.mean(x * x, axis=-1, keepdims=True)
    return (x * lax.rsqrt(ms + EPS) * g) * (1.0 + sc) + sh


def _log_sigmoid(z):
    return jnp.minimum(z, 0.0) - jnp.log1p(jnp.exp(-jnp.abs(z)))


def _sigmoid(z):
    return 1.0 / (1.0 + jnp.exp(-z))


def _sortable_key(x):
    bits = lax.bitcast_convert_type(x, I32)
    return bits ^ ((bits >> 31) & 0x7FFFFFFF)


def _ada_kernel(c_ref, w_ref, b_ref, o_ref):
    c = c_ref[...]
    s = (c * _sigmoid(c)).astype(BF16)
    o_ref[0] = _dot(s, w_ref[0].astype(BF16)) + b_ref[0]


def _ada(c_all, w_ada, b_ada):
    depth = w_ada.shape[0]
    nc = c_all.shape[0]
    nt = 6
    return pl.pallas_call(
        _ada_kernel,
        grid=(depth, nt),
        in_specs=[pl.BlockSpec((nc, D), lambda l, j: (0, 0)),
                  pl.BlockSpec((1, D, D), lambda l, j: (l, 0, j)),
                  pl.BlockSpec((1, 1, D), lambda l, j: (l, 0, j))],
        out_specs=pl.BlockSpec((1, nc, D), lambda l, j: (l, 0, j)),
        out_shape=jax.ShapeDtypeStruct((depth, nc, 6 * D), F32),
        compiler_params=_cparams("arbitrary", "arbitrary"),
        name="ada",
    )(c_all, w_ada, b_ada.reshape(depth, 1, 6 * D))


def _t5_bucket_np(rel):
    n = np.maximum(rel, 0)
    nf = np.maximum(n, 1).astype(np.float32)
    large = T5_MAX_EXACT + (np.log(nf / np.float32(T5_MAX_EXACT)) / np.float32(math.log(T5_MAX_DIST / T5_MAX_EXACT))
                            * np.float32(N_BUCKETS - T5_MAX_EXACT)).astype(np.int32)
    return np.where(n < T5_MAX_EXACT, n, np.minimum(large, N_BUCKETS - 1)).astype(np.int32)


def _bias_bucket_tables(past_len, dec_seq):
    s = np.arange(TKC)[:, None]
    t = np.arange(TQ)[None, :]
    tiles = []
    for d in (0, 1):
        rel = d * TKC + t - s
        tiles.append(np.where(rel >= 0, _t5_bucket_np(rel), -1))
    prompt = np.stack(tiles).astype(np.int32)
    i = np.arange(dec_seq)[:, None]
    lane = np.arange(PAGE)[None, :]
    rel_last = (past_len + i) - (past_len - PAGE + lane)
    rel_new = i - lane
    dl = _t5_bucket_np(rel_last)
    dn = np.where((rel_new >= 0) & (lane < dec_seq), _t5_bucket_np(rel_new), -1)
    dec = np.stack([dl, dn]).astype(np.int32)
    return prompt, dec


def _bias_kernel(tab_ref, pb_ref, db_ref, pt_ref, dt_ref):
    pb = pb_ref[...]
    db = db_ref[...]
    for h in range(N_HEADS_T5):
        def lut(bk):
            acc = jnp.zeros(bk.shape, F32)
            for b in range(N_BUCKETS):
                acc = jnp.where(bk == b, tab_ref[b, h], acc)
            return jnp.where(bk < 0, NEG, acc - tab_ref[N_BUCKETS - 1, h])
        pt_ref[h] = lut(pb)
        dt_ref[h] = lut(db)


def _bias_tiles(t5_table, past_len, dec_seq):
    pb, db = _bias_bucket_tables(past_len, dec_seq)
    return pl.pallas_call(
        _bias_kernel,
        in_specs=[pl.BlockSpec(memory_space=pltpu.SMEM),
                  pl.BlockSpec(memory_space=pltpu.VMEM),
                  pl.BlockSpec(memory_space=pltpu.VMEM)],
        out_specs=[pl.BlockSpec(memory_space=pltpu.VMEM), pl.BlockSpec(memory_space=pltpu.VMEM)],
        out_shape=[jax.ShapeDtypeStruct((N_HEADS_T5,) + pb.shape, F32),
                   jax.ShapeDtypeStruct((N_HEADS_T5,) + db.shape, F32)],
        name="t5_bias_tiles",
    )(t5_table, jnp.asarray(pb), jnp.asarray(db))


def _seg(w, name):
    return w[:, _OFF[name]:_OFF[name] + _W[name]]


def _cf_block(cf):
    n = cf.shape[0]
    z2 = jnp.zeros((n, 2), cf.dtype)
    return jnp.concatenate([cf, z2, cf, z2, cf, z2, jnp.zeros((n, LANES - 24), cf.dtype)], axis=1)


def _rows_block(w):
    z64 = jnp.zeros((D, 64), w.dtype)
    return jnp.concatenate([_seg(w, "ak"), _seg(w, "av"), _seg(w, "aki"), z64, _seg(w, "bk"), _seg(w, "bv"),
                            _seg(w, "ck"), _seg(w, "cv"), _cf_block(_seg(w, "cf"))], axis=1)


def _prep_in_prompt(w):
    scale = HD ** -0.5
    z8 = jnp.zeros((D, 8), w.dtype)
    z64 = jnp.zeros((D, 64), w.dtype)
    w_t = jnp.concatenate([_seg(w, "aq") * scale, _seg(w, "aqi"), _seg(w, "awi"), z8, _seg(w, "bq") * scale,
                           _seg(w, "cq") * scale, _seg(w, "av"), _seg(w, "bv"), _seg(w, "cv"),
                           _seg(w, "ak"), _seg(w, "aki"), _seg(w, "bk"), _seg(w, "ck")], axis=1)
    w_r = jnp.concatenate([_seg(w, "ak"), _seg(w, "aki"), z64, _seg(w, "bk"), _seg(w, "ck"),
                           _cf_block(_seg(w, "cf")), _seg(w, "bv")], axis=1)
    return w_r.astype(BF16), w_t.T.astype(BF16)


def _pad_heads(wq, n_heads, kv_of, width, n_slots):
    blocks = []
    for h in range(n_heads):
        blk = jnp.zeros((D, n_slots * HD), wq.dtype)
        s = kv_of(h)
        blk = blk.at[:, s * HD:(s + 1) * HD].set(wq[:, h * width:(h + 1) * width])
        blocks.append(blk)
    return jnp.concatenate(blocks, axis=1)


def _prep_in_sample(w):
    scale = HD ** -0.5
    aq = _pad_heads(_seg(w, "aq") * scale, H_A, lambda h: h // (H_A // KVH_A), HD, 2)
    aqi = _pad_heads(_seg(w, "aqi"), H_IDX, lambda h: 0, D_IDX, 2)
    awi = jnp.repeat(_seg(w, "awi"), LANES, axis=1)
    bq = _pad_heads(_seg(w, "bq") * scale, 2 * H_B, lambda hc: (hc // 2 // (H_B // KVH_B)) * 2 + hc % 2, HD, 4)
    cq = _pad_heads(_seg(w, "cq") * scale, H_C, lambda h: h // (H_C // KVH_C), HD, 2)
    return jnp.concatenate([aq, aqi, awi, bq, cq, _rows_block(w)], axis=1).astype(BF16)


def _pad_proj_rows(wp, n_heads, kv_of):
    blocks = []
    for h in range(n_heads):
        blk = jnp.zeros((2 * HD, D), wp.dtype)
        s = kv_of(h)
        blk = blk.at[s * HD:(s + 1) * HD].set(wp[h * HD:(h + 1) * HD])
        blocks.append(blk)
    return jnp.concatenate(blocks, axis=0)


def _store_rows(pr, bf, ak_o, av_o, aki_o, bk_o, bv_o, ck_o, cv_o, lf_o, idx):
    ak_o[idx] = pr[:, _R_AK:_R_AK + 128]
    av_o[idx] = pr[:, _R_AV:_R_AV + 128]
    aki_o[idx] = pr[:, _R_AKI:_R_AKI + 64]
    bk_o[idx] = pr[:, _R_BK:_R_BK + 256]
    bv_o[idx] = pr[:, _R_BV:_R_BV + 256]
    ck_o[idx] = pr[:, _R_CK:_R_CK + 128]
    cv_o[idx] = pr[:, _R_CV:_R_CV + 128]
    lf = _log_sigmoid(pr[:, _R_CF:_R_CF + 128] + bf)
    lf_o[idx] = lf[:, 0:H_C]
    return lf


def _inproj_p_kernel(x_ref, mod_ref, g_ref, wr_ref, wt_ref, bf_ref,
                     bv_o, lf_o, akb_o, akib_o, bkb_o, ckb_o, lfrep_o,
                     akT_o, avT_o, akiT_o, bkT_o, ckT_o, cvT_o, awiT_o,
                     aqT_o, aqiT_o, bqT_o, cqT_o, avTb_o, bvTb_o, cvTb_o):
    x = x_ref[0]
    mod = mod_ref[0]
    h = _norm_mod(x, g_ref[...], mod[:, D:2 * D], mod[:, 0:D])
    pr = _dot(h.astype(BF16), wr_ref[...])
    bv_o[0] = pr[:, _P_BV:_P_BV + 256]
    lf = _log_sigmoid(pr[:, _P_CF:_P_CF + 128] + bf_ref[...])
    lf_o[0] = lf[:, 0:H_C]
    lfrep_o[0] = lf
    akb_o[0] = pr[:, _P_AK:_P_AK + 128].astype(BF16)
    akib_o[0] = pr[:, _P_AKI:_P_AKI + 64].astype(BF16)
    bkb_o[0] = pr[:, _P_BK:_P_BK + 256].astype(BF16)
    ckb_o[0] = pr[:, _P_CK:_P_CK + 128].astype(BF16)
    pt = _dot(wt_ref[...], h.T.astype(BF16))
    akT_o[0] = pt[_T_AK:_T_AK + 128]
    avT_o[0] = pt[_T_AV:_T_AV + 128]
    akiT_o[0] = pt[_T_AKI:_T_AKI + 64]
    bkT_o[0] = pt[_T_BK:_T_BK + 256]
    ckT_o[0] = pt[_T_CK:_T_CK + 128]
    cvT_o[0] = pt[_T_CV:_T_CV + 128]
    awiT_o[0] = pt[_T_AWI:_T_AWI + 16]
    aqT_o[0] = pt[_T_AQ:_T_AQ + 384].astype(BF16)
    aqiT_o[0] = pt[_T_AQI:_T_AQI + 512].astype(BF16)
    bqT_o[0] = pt[_T_BQ:_T_BQ + 512].astype(BF16)
    cqT_o[0] = pt[_T_CQ:_T_CQ + 384].astype(BF16)
    avTb_o[0] = pt[_T_AV:_T_AV + 128].astype(BF16)
    bvTb_o[0] = pt[_T_BV:_T_BV + 256].astype(BF16)
    cvTb_o[0] = pt[_T_CV:_T_CV + 128].astype(BF16)


def _inproj_prompt(x, mod_a, g, wr, wt, bfb, tm):
    b, l, _ = x.shape
    row = lambda w, dt: jax.ShapeDtypeStruct((b, l, w), dt)
    tr = lambda r, dt: jax.ShapeDtypeStruct((b, r, l), dt)
    row_spec = lambda w: pl.BlockSpec((1, tm, w), lambda i, j: (i, j, 0))
    tr_spec = lambda r: pl.BlockSpec((1, r, tm), lambda i, j: (i, 0, j))
    rows = [(256, F32), (H_C, F32), (128, BF16), (64, BF16), (256, BF16), (128, BF16), (128, F32)]
    trs = [(128, F32), (128, F32), (64, F32), (256, F32), (128, F32), (128, F32), (16, F32),
           (384, BF16), (512, BF16), (512, BF16), (384, BF16), (128, BF16), (256, BF16), (128, BF16)]
    return pl.pallas_call(
        _inproj_p_kernel,
        grid=(b, l // tm),
        in_specs=[pl.BlockSpec((1, tm, D), lambda i, j: (i, j, 0)),
                  pl.BlockSpec((1, 1, 3 * D), lambda i, j: (i, 0, 0)),
                  pl.BlockSpec((1, D), lambda i, j: (0, 0)),
                  pl.BlockSpec((D, N_PROWS), lambda i, j: (0, 0)),
                  pl.BlockSpec((N_T, D), lambda i, j: (0, 0)),
                  pl.BlockSpec((1, LANES), lambda i, j: (0, 0))],
        out_specs=[row_spec(w) for w, _ in rows] + [tr_spec(r) for r, _ in trs],
        out_shape=[row(w, dt) for w, dt in rows] + [tr(r, dt) for r, dt in trs],
        compiler_params=_cparams("arbitrary", "arbitrary"),
        name="inproj_prompt",
    )(x, mod_a, g, wr, wt, bfb)


def _inproj_s_kernel(x_ref, mod_ref, g_ref, w_ref, bf_ref,
                     ak_o, av_o, aki_o, bk_o, bv_o, ck_o, cv_o, lf_o,
                     qa_o, qi_o, wi_o, qb_o, qc_o, lfrep_o):
    x = x_ref[...]
    gsz, r, _ = x.shape
    mod = mod_ref[...]
    h = _norm_mod(x, g_ref[...], mod[:, :, D:2 * D], mod[:, :, 0:D]).reshape(gsz * r, D)
    pr = _dot(h.astype(BF16), w_ref[...])
    lf = _store_rows(pr[:, N_SQ:], bf_ref[...], ak_o, av_o, aki_o, bk_o, bv_o, ck_o, cv_o, lf_o,
                     (slice(None), slice(None)))
    lfrep_o[...] = lf
    qa_o[...] = pr[:, _S_AQ:_S_AQ + 768].astype(BF16)
    qi_o[...] = pr[:, _S_AQI:_S_AQI + 1024].astype(BF16)
    wi_o[...] = pr[:, _S_AWI:_S_AWI + 1024]
    qb_o[...] = pr[:, _S_BQ:_S_BQ + 2048].astype(BF16)
    qc_o[...] = pr[:, _S_CQ:_S_CQ + 768].astype(BF16)


def _inproj_sample(x, mod_a, g, w, bfb, gsz):
    nb, r, _ = x.shape
    n = nb * r
    tm = gsz * r
    row_w = [128, 128, 64, 256, 256, 128, 128, H_C]
    outs = [(w_, F32) for w_ in row_w] + [(768, BF16), (1024, BF16), (1024, F32), (2048, BF16), (768, BF16), (128, F32)]
    return pl.pallas_call(
        _inproj_s_kernel,
        grid=(nb // gsz,),
        in_specs=[pl.BlockSpec((gsz, r, D), lambda i: (i, 0, 0)),
                  pl.BlockSpec((gsz, 1, 3 * D), lambda i: (i, 0, 0)),
                  pl.BlockSpec((1, D), lambda i: (0, 0)),
                  pl.BlockSpec((D, N_SQ + N_ROWS), lambda i: (0, 0)),
                  pl.BlockSpec((1, LANES), lambda i: (0, 0))],
        out_specs=[pl.BlockSpec((tm, w_), lambda i: (i, 0)) for w_, _ in outs],
        out_shape=[jax.ShapeDtypeStruct((n, w_), dt) for w_, dt in outs],
        compiler_params=_cparams("arbitrary"),
        name="inproj_sample",
    )(x, mod_a, g, w, bfb)


_CUM_T = 256


def _cum_kernel(lf_ref, o_ref):
    l = lf_ref.shape[1]
    r = lax.broadcasted_iota(I32, (_CUM_T, _CUM_T), 0)
    c = lax.broadcasted_iota(I32, (_CUM_T, _CUM_T), 1)
    tri = jnp.where(c <= r, 1.0, 0.0).astype(BF16)
    lane = lax.broadcasted_iota(I32, (_CUM_T, LANES), 1)
    carry = jnp.zeros((1, LANES), F32)
    for i in range(l // _CUM_T):
        x = lf_ref[0, i * _CUM_T:(i + 1) * _CUM_T, :]
        cum = _dot3_rhs(tri, x) + carry
        carry = cum[_CUM_T - 1:_CUM_T, :]
        hi, mid, lo = _split3(cum)
        piece = jnp.where(lane < 8, hi, jnp.where(lane < 16, mid, lo))
        o_ref[0, i * _CUM_T:(i + 1) * _CUM_T, :] = -piece


def _cum_prompt(lfrep):
    b, l, _ = lfrep.shape
    return pl.pallas_call(
        _cum_kernel,
        grid=(b,),
        in_specs=[pl.BlockSpec((1, l, LANES), lambda i: (i, 0, 0))],
        out_specs=pl.BlockSpec((1, l, LANES), lambda i: (i, 0, 0)),
        out_shape=jax.ShapeDtypeStruct((b, l, LANES), BF16),
        compiler_params=_cparams("arbitrary"),
        name="forget_cumsum",
    )(lfrep)


def _online_update(s, m_scr, acc_scr, vta):
    m_prev = m_scr[...]
    m_new = jnp.maximum(m_prev, jnp.max(s, axis=0, keepdims=True))
    alpha = jnp.exp(m_prev - m_new)
    p = jnp.exp(s - m_new).astype(BF16)
    m_scr[...] = m_new
    n_g = len(vta)
    w = s.shape[1] // n_g
    for g in range(n_g):
        pv = _dot(vta[g], p[:, g * w:(g + 1) * w])
        acc_scr[g] = acc_scr[g] * alpha[:, g * w:(g + 1) * w] + pv


def _v_aug(vt_ref, j, g, rows):
    v = vt_ref[0, g * rows:(g + 1) * rows, pl.ds(pl.multiple_of(j * TKC, TKC), TKC)]
    return jnp.concatenate([v, jnp.ones((16, TKC), BF16)], axis=0)


def _chunk_loop(qi, chunk):
    def far(j, carry):
        chunk(j, None)
        return carry
    lax.fori_loop(0, qi - 1, far, 0)

    @pl.when(qi >= 1)
    def _():
        chunk(qi - 1, 1)
    chunk(qi, 0)


def _attn_a_kernel(qT_ref, qiT_ref, wT_ref, k_ref, ki_ref, vT_ref, bias_ref, oT_ref,
                   qpad, qipad, key_scr, am_scr, j_scr, m_scr, acc_scr, *, k_top):
    b = pl.program_id(0)
    qi = pl.program_id(1)
    n_chunk = qi + 1
    grp = H_A // KVH_A

    @pl.when((b == 0) & (qi == 0))
    def _():
        qpad[...] = jnp.zeros_like(qpad)
    for h in range(H_A):
        g = h // grp
        qpad[g * HD:(g + 1) * HD, h * TQ:(h + 1) * TQ] = qT_ref[0, h * HD:(h + 1) * HD, :]
    for h in range(H_IDX):
        qipad[:, h * TQ:(h + 1) * TQ] = qiT_ref[0, h * D_IDX:(h + 1) * D_IDX, :]
    w = wT_ref[0, 0:H_IDX, :] * IDX_W_SCALE
    row = lax.broadcasted_iota(I32, (TKC, TQ), 0)
    col = lax.broadcasted_iota(I32, (TKC, TQ), 1)

    def score_chunk(j, diag):
        sl = pl.ds(pl.multiple_of(j * TKC, TKC), TKC)
        s = _dot(ki_ref[0, sl, :], qipad[...])
        sc = jnp.zeros((TKC, TQ), F32)
        for h in range(H_IDX):
            sc = sc + jnp.maximum(s[:, h * TQ:(h + 1) * TQ], 0.0) * w[h:h + 1, :]
        sc = jnp.where(sc == 0.0, 0.0, sc)
        if diag:
            sc = jnp.where(row > col, -jnp.inf, sc)
        key_scr[sl, :] = _sortable_key(sc)

    def sc_body(j, carry):
        score_chunk(j, False)
        return carry
    lax.fori_loop(0, qi, sc_body, 0)
    score_chunk(qi, True)

    def count(pred):
        def body(j, cnt):
            sl = pl.ds(pl.multiple_of(j * TKC, TKC), TKC)
            hit = jnp.where(pred(key_scr[sl, :], j), 1, 0).astype(I32)
            return cnt + jnp.sum(hit.reshape(TKC // 8, 8, TQ), axis=0)
        cnt8 = lax.fori_loop(0, n_chunk, body, jnp.zeros((8, TQ), I32))
        return jnp.sum(cnt8, axis=0, keepdims=True)

    needs_search = n_chunk * TQ > k_top

    @pl.when(jnp.logical_not(needs_search))
    def _():
        def body(j, carry):
            am_scr[pl.ds(pl.multiple_of(j * TKC, TKC), TKC), :] = jnp.zeros((TKC, TQ), F32)
            return carry
        lax.fori_loop(0, n_chunk, body, 0)

    @pl.when(needs_search)
    def _():
        def it_body(it, t):
            cand = t ^ lax.shift_left(jnp.int32(1), 31 - it)
            cnt = count(lambda k, j: k >= cand)
            return jnp.where(cnt >= k_top, cand, t)
        thr = lax.fori_loop(0, 32, it_body, jnp.full((1, TQ), INT_MIN, I32))
        cnt_gt = count(lambda k, j: k > thr)
        cnt_eq = count(lambda k, j: k == thr)
        need = k_top - cnt_gt
        n_bits = max(1, (key_scr.shape[0] - 1).bit_length())
        j_scr[...] = jnp.full((1, TQ), key_scr.shape[0], I32)

        @pl.when(jnp.max(jnp.where(cnt_eq > need, 1, 0)) > 0)
        def _():
            def tie_body(it, jv):
                cand = jv | lax.shift_left(jnp.int32(1), n_bits - 1 - it)
                cnt = count(lambda k, j: (k == thr) & ((row + j * TKC) < cand))
                return jnp.where(cnt < need, cand, jv)
            j_scr[...] = lax.fori_loop(0, n_bits, tie_body, jnp.zeros((1, TQ), I32))
        jv = j_scr[...]

        def body(j, carry):
            sl = pl.ds(pl.multiple_of(j * TKC, TKC), TKC)
            k = key_scr[sl, :]
            sel = (k > thr) | ((k == thr) & ((row + j * TKC) <= jv))
            am_scr[sl, :] = jnp.where(sel, 0.0, NEG)
            return carry
        lax.fori_loop(0, n_chunk, body, 0)

    m_scr[...] = jnp.full(m_scr.shape, NEG, F32)
    acc_scr[...] = jnp.zeros_like(acc_scr)

    def chunk(j, kind):
        sl = pl.ds(pl.multiple_of(j * TKC, TKC), TKC)
        s = _dot(k_ref[0, sl, :], qpad[...])
        am = am_scr[sl, :]
        parts = []
        for h in range(H_A):
            sh = s[:, h * TQ:(h + 1) * TQ] + am
            if kind is not None:
                sh = sh + bias_ref[h, kind]
            parts.append(sh)
        s = jnp.concatenate(parts, axis=1)
        _online_update(s, m_scr, acc_scr, [_v_aug(vT_ref, j, g, HD) for g in range(KVH_A)])

    _chunk_loop(qi, chunk)

    for h in range(H_A):
        g, hh = h // grp, h % grp
        a = acc_scr[g][:, hh * TQ:(hh + 1) * TQ]
        oT_ref[0, h * HD:(h + 1) * HD, :] = a[0:HD] / a[HD:HD + 1]


def _attn_a(aqT, aqiT, awiT, akb, akib, avT, bias_pt):
    b, _, l = aqT.shape
    k_top = min(K_TOP_MAX, l // 4)
    blk_t = lambda r: pl.BlockSpec((1, r, TQ), lambda i, j: (i, 0, j))
    full = lambda s1, s2: pl.BlockSpec((1, s1, s2), lambda i, j: (i, 0, 0))
    return pl.pallas_call(
        functools.partial(_attn_a_kernel, k_top=k_top),
        grid=(b, l // TQ),
        in_specs=[blk_t(384), blk_t(512), blk_t(16), full(l, 128), full(l, 64), full(128, l),
                  pl.BlockSpec((H_A, 2, TKC, TQ), lambda i, j: (0, 0, 0, 0))],
        out_specs=blk_t(384),
        out_shape=jax.ShapeDtypeStruct((b, 384, l), F32),
        scratch_shapes=[pltpu.VMEM((128, H_A * TQ), BF16),
                        pltpu.VMEM((D_IDX, H_IDX * TQ), BF16),
                        pltpu.VMEM((l, TQ), I32),
                        pltpu.VMEM((l, TQ), F32),
                        pltpu.VMEM((1, TQ), I32),
                        pltpu.VMEM((1, H_A * TQ), F32),
                        pltpu.VMEM((KVH_A, HD + 16, (H_A // KVH_A) * TQ), F32)],
        compiler_params=_cparams("arbitrary", "arbitrary"),
        name="attn_dsa_prompt",
    )(aqT, aqiT, awiT, akb, akib, avT, bias_pt)


def _lambda_value(lam_ref, lam_init):
    lv = lam_ref[...]
    s1 = jnp.sum(lv[0:1] * lv[1:2], axis=1, keepdims=True)
    s2 = jnp.sum(lv[2:3] * lv[3:4], axis=1, keepdims=True)
    return jnp.exp(s1) - jnp.exp(s2) + lam_init


def _attn_b_kernel(qT_ref, k_ref, vT_ref, bias_ref, lam_ref, gsub_ref, oT_ref,
                   qpad, m_scr, acc_scr, *, lam_init):
    b = pl.program_id(0)
    qi = pl.program_id(1)
    grp = H_B // KVH_B

    @pl.when((b == 0) & (qi == 0))
    def _():
        qpad[...] = jnp.zeros_like(qpad)
    for h in range(H_B):
        for c in range(2):
            slot = (h // grp) * 2 + c
            hc = h * 2 + c
            qpad[slot * HD:(slot + 1) * HD, hc * TQ:(hc + 1) * TQ] = qT_ref[0, hc * HD:(hc + 1) * HD, :]

    m_scr[...] = jnp.full(m_scr.shape, NEG, F32)
    acc_scr[...] = jnp.zeros_like(acc_scr)

    def chunk(j, kind):
        sl = pl.ds(pl.multiple_of(j * TKC, TKC), TKC)
        s = _dot(k_ref[0, sl, :], qpad[...])
        if kind is not None:
            parts = []
            for hc in range(2 * H_B):
                parts.append(s[:, hc * TQ:(hc + 1) * TQ] + bias_ref[hc // 2, kind])
            s = jnp.concatenate(parts, axis=1)
        _online_update(s, m_scr, acc_scr, [_v_aug(vT_ref, j, g, 2 * HD) for g in range(KVH_B)])

    _chunk_loop(qi, chunk)

    lam = _lambda_value(lam_ref, lam_init)
    gs = gsub_ref[...]
    for h in range(H_B):
        g, hh = h // grp, h % grp
        a1 = acc_scr[g][:, (hh * 2) * TQ:(hh * 2 + 1) * TQ]
        a2 = acc_scr[g][:, (hh * 2 + 1) * TQ:(hh * 2 + 2) * TQ]
        o = a1[0:2 * HD] / a1[2 * HD:2 * HD + 1] - lam * (a2[0:2 * HD] / a2[2 * HD:2 * HD + 1])
        ms = jnp.mean(o * o, axis=0, keepdims=True)
        oT_ref[0, h * 2 * HD:(h + 1) * 2 * HD, :] = (o * lax.rsqrt(ms + EPS) * gs) * (1.0 - lam_init)


def _attn_b(bqT, bkb, bvT, bias_pt, lamvec, gsub_col, lam_init):
    b, _, l = bqT.shape
    blk_t = lambda r: pl.BlockSpec((1, r, TQ), lambda i, j: (i, 0, j))
    full = lambda s1, s2: pl.BlockSpec((1, s1, s2), lambda i, j: (i, 0, 0))
    return pl.pallas_call(
        functools.partial(_attn_b_kernel, lam_init=lam_init),
        grid=(b, l // TQ),
        in_specs=[blk_t(512), full(l, 256), full(256, l),
                  pl.BlockSpec((H_B, 2, TKC, TQ), lambda i, j: (0, 0, 0, 0)),
                  pl.BlockSpec((4, HD), lambda i, j: (0, 0)),
                  pl.BlockSpec((2 * HD, 1), lambda i, j: (0, 0))],
        out_specs=blk_t(512),
        out_shape=jax.ShapeDtypeStruct((b, 512, l), F32),
        scratch_shapes=[pltpu.VMEM((256, 2 * H_B * TQ), BF16),
                        pltpu.VMEM((1, 2 * H_B * TQ), F32),
                        pltpu.VMEM((KVH_B, 2 * HD + 16, 2 * (H_B // KVH_B) * TQ), F32)],
        compiler_params=_cparams("arbitrary", "arbitrary"),
        name="attn_diff_prompt",
    )(bqT, bkb, bvT, bias_pt, lamvec, gsub_col)


def _attn_c_kernel(qT_ref, k_ref, kaug_ref, vT_ref, oT_ref, qpad, m_scr, acc_scr):
    b = pl.program_id(0)
    qi = pl.program_id(1)
    grp = H_C // KVH_C

    @pl.when((b == 0) & (qi == 0))
    def _():
        r = lax.broadcasted_iota(I32, (256, H_C * TQ), 0) - 128
        cblk = lax.broadcasted_iota(I32, (256, H_C * TQ), 1) // TQ
        ones = (r >= 0) & (r < 24) & ((r % 8) == cblk)
        qpad[...] = jnp.where(ones, 1.0, 0.0).astype(BF16)
    for h in range(H_C):
        g = h // grp
        qpad[g * HD:(g + 1) * HD, h * TQ:(h + 1) * TQ] = qT_ref[0, h * HD:(h + 1) * HD, :]

    m_scr[...] = jnp.full(m_scr.shape, NEG, F32)
    acc_scr[...] = jnp.zeros_like(acc_scr)
    row = lax.broadcasted_iota(I32, (TKC, TQ), 0)
    col = lax.broadcasted_iota(I32, (TKC, TQ), 1)
    causal = jnp.where(row > col, NEG, 0.0)

    def chunk(j, kind):
        sl = pl.ds(pl.multiple_of(j * TKC, TKC), TKC)
        s = _dot(k_ref[0, sl, :], qpad[0:128, :]) + _dot(kaug_ref[0, sl, :], qpad[128:256, :])
        if kind == 0:
            s = s + jnp.concatenate([causal] * H_C, axis=1)
        _online_update(s, m_scr, acc_scr, [_v_aug(vT_ref, j, g, HD) for g in range(KVH_C)])

    def far(j, carry):
        chunk(j, None)
        return carry
    lax.fori_loop(0, qi, far, 0)
    chunk(qi, 0)

    for h in range(H_C):
        g, hh = h // grp, h % grp
        a = acc_scr[g][:, hh * TQ:(hh + 1) * TQ]
        oT_ref[0, h * HD:(h + 1) * HD, :] = a[0:HD] / a[HD:HD + 1]


def _attn_c(cqT, ckb, kaug, cvT):
    b, _, l = cqT.shape
    blk_t = lambda r: pl.BlockSpec((1, r, TQ), lambda i, j: (i, 0, j))
    full = lambda s1, s2: pl.BlockSpec((1, s1, s2), lambda i, j: (i, 0, 0))
    return pl.pallas_call(
        _attn_c_kernel,
        grid=(b, l // TQ),
        in_specs=[blk_t(384), full(l, 128), full(l, 128), full(128, l)],
        out_specs=blk_t(384),
        out_shape=jax.ShapeDtypeStruct((b, 384, l), F32),
        scratch_shapes=[pltpu.VMEM((256, H_C * TQ), BF16),
                        pltpu.VMEM((1, H_C * TQ), F32),
                        pltpu.VMEM((KVH_C, HD + 16, (H_C // KVH_C) * TQ), F32)],
        compiler_params=_cparams("arbitrary", "arbitrary"),
        name="attn_forget_prompt",
    )(cqT, ckb, kaug, cvT)


G_DEC = 4


def _stack_heads(ref, tok, n, width):
    return jnp.concatenate([ref[tok, h * width:(h + 1) * width] for h in range(n)], axis=0)


def _pad_new(x):
    return jnp.concatenate([x, jnp.zeros((PAGE - x.shape[0], x.shape[1]), x.dtype)], axis=0)


def _probabilities(s_pages):
    mx = s_pages[0]
    for s in s_pages[1:]:
        mx = jnp.maximum(mx, s)
    m = jnp.max(mx, axis=1, keepdims=True)
    p_pages = [jnp.exp(s - m) for s in s_pages]
    lsum = p_pages[0]
    for p in p_pages[1:]:
        lsum = lsum + p
    return p_pages, jnp.sum(lsum, axis=1, keepdims=True)


def _pv(p_pages, vt_pages, v_new):
    acc = _dot(p_pages[-1].astype(BF16), v_new)
    for p, vt in zip(p_pages[:-1], vt_pages):
        acc = acc + _dot_nt(p.astype(BF16), vt)
    return acc


def _page_specs(cache, layer, gsz, n_pages, block, index):
    specs = []
    for g in range(gsz):
        for p in range(n_pages):
            specs.append(pl.BlockSpec(
                (1,) + block,
                functools.partial(lambda i, pt, g_, p_: (layer,) + index(pt[i * gsz + g_, p_]), g_=g, p_=p)))
    return specs, [cache] * (gsz * n_pages)


def _decode_a_kernel(pt_ref, qa_ref, qi_ref, wi_ref, ak_n, av_n, aki_n, bias_ref, *rest, gsz, n_pages, k_top):
    npg = gsz * n_pages
    kt_pages, vt_pages, kit_pages = rest[0:npg], rest[npg:2 * npg], rest[2 * npg:3 * npg]
    oa_ref = rest[3 * npg]
    r = qa_ref.shape[0] // gsz
    rows = gsz * r
    n_all = n_pages + 1
    lane = lax.broadcasted_iota(I32, (rows, PAGE), 1)
    qrow = lax.rem(lax.broadcasted_iota(I32, (rows, PAGE), 0), r)
    new_visible = lane <= qrow

    sc_pages = [[] for _ in range(n_all)]
    for g in range(gsz):
        tok = slice(g * r, (g + 1) * r)
        qi2 = _stack_heads(qi_ref, tok, H_IDX, 2 * D_IDX)[:, 0:D_IDX]
        wcol = _stack_heads(wi_ref, tok, H_IDX, LANES) * IDX_W_SCALE
        for p in range(n_all):
            if p < n_pages:
                z = _dot(qi2, kit_pages[g * n_pages + p][0, 0].astype(BF16))
            else:
                z = _dot_nt(qi2, _pad_new(aki_n[tok, :]).astype(BF16))
            z = jnp.maximum(z, 0.0) * wcol
            sc = z[0:r]
            for h in range(1, H_IDX):
                sc = sc + z[h * r:(h + 1) * r]
            sc_pages[p].append(sc)
    key_pages = []
    for p in range(n_all):
        sc = jnp.concatenate(sc_pages[p], axis=0)
        sc = jnp.where(sc == 0.0, 0.0, sc)
        if p == n_pages:
            sc = jnp.where(new_visible, sc, -jnp.inf)
        key_pages.append(_sortable_key(sc))

    def count(pred):
        tot = None
        for p, k in enumerate(key_pages):
            hit = jnp.where(pred(k, p), 1.0, 0.0)
            tot = hit if tot is None else tot + hit
        return jnp.sum(tot, axis=1, keepdims=True)

    thr = jnp.full((rows, 1), INT_MIN, I32)
    for it in range(32):
        cand = thr ^ jnp.int32(-2 ** 31 if it == 0 else 1 << (31 - it))
        cnt = count(lambda k, p: k >= cand)
        thr = jnp.where(cnt >= k_top, cand, thr)
    cnt_gt = count(lambda k, p: k > thr)
    cnt_eq = count(lambda k, p: k == thr)
    need = k_top - cnt_gt
    n_bits = max(1, (n_all * PAGE - 1).bit_length())

    def tie_search():
        jv = jnp.zeros((rows, 1), I32)
        for it in range(n_bits):
            cand = jv | jnp.int32(1 << (n_bits - 1 - it))
            cnt = count(lambda k, p: (k == thr) & ((lane + p * PAGE) < cand))
            jv = jnp.where(cnt < need, cand, jv)
        return jv

    any_excess = jnp.max(jnp.where(cnt_eq > need, 1, 0)) > 0
    jv = lax.cond(any_excess, tie_search, lambda: jnp.full((rows, 1), n_all * PAGE, I32))
    am_pages = []
    for p, k in enumerate(key_pages):
        sel = (k > thr) | ((k == thr) & ((lane + p * PAGE) <= jv))
        am_pages.append(jnp.where(sel, 0.0, NEG))

    bias_last = jnp.concatenate([bias_ref[h, 0] for h in range(H_A)], axis=0)
    bias_new = jnp.concatenate([bias_ref[h, 1] for h in range(H_A)], axis=0)
    for g in range(gsz):
        tok = slice(g * r, (g + 1) * r)
        qa2 = _stack_heads(qa_ref, tok, H_A, 2 * HD)
        s_pages = []
        for p in range(n_all):
            if p < n_pages:
                s = _dot(qa2, kt_pages[g * n_pages + p][0, 0].reshape(2 * HD, PAGE).astype(BF16))
            else:
                s = _dot_nt(qa2, _pad_new(ak_n[tok, :]).astype(BF16))
            s = s + jnp.concatenate([am_pages[p][tok]] * H_A, axis=0)
            if p == n_pages - 1:
                s = s + bias_last
            if p == n_pages:
                s = s + bias_new
            s_pages.append(s)
        p_pages, lsum = _probabilities(s_pages)
        vts = [vt_pages[g * n_pages + p][0, 0].reshape(2 * HD, PAGE).astype(BF16) for p in range(n_pages)]
        o = _pv(p_pages, vts, _pad_new(av_n[tok, :]).astype(BF16)) / lsum
        oa_ref[tok, :] = jnp.concatenate([o[h * r:(h + 1) * r] for h in range(H_A)], axis=1)


def _decode_a(page_table, layer, qa, qi, wi, ak_n, av_n, aki_n, kt, vt, kit, bias_dec, r):
    nb, n_pages = page_table.shape
    gsz = G_DEC
    k_top = min(K_TOP_MAX, (n_pages * PAGE + r) // 4)
    tok = lambda w: pl.BlockSpec((gsz * r, w), lambda i, pt: (i, 0))
    in_specs = [tok(768), tok(1024), tok(1024), tok(128), tok(128), tok(64),
                pl.BlockSpec((H_A, 2, r, LANES), lambda i, pt: (0, 0, 0, 0))]
    args = []
    for cache, block in ((kt, (1, KVH_A, HD, PAGE)), (vt, (1, KVH_A, HD, PAGE))):
        s, a = _page_specs(cache, layer, gsz, n_pages, block, lambda pg: (pg, 0, 0, 0))
        in_specs += s
        args += a
    s, a = _page_specs(kit, layer, gsz, n_pages, (1, D_IDX, PAGE), lambda pg: (pg, 0, 0))
    in_specs += s
    args += a
    grid_spec = pltpu.PrefetchScalarGridSpec(
        num_scalar_prefetch=1, grid=(nb // gsz,), in_specs=in_specs,
        out_specs=pl.BlockSpec((gsz * r, 768), lambda i, pt: (i, 0)))
    return pl.pallas_call(
        functools.partial(_decode_a_kernel, gsz=gsz, n_pages=n_pages, k_top=k_top),
        grid_spec=grid_spec,
        out_shape=jax.ShapeDtypeStruct((nb * r, 768), F32),
        compiler_params=_cparams("arbitrary"),
        name="decode_dsa",
    )(page_table, qa, qi, wi, ak_n, av_n, aki_n, bias_dec[:H_A], *args)


def _decode_b_kernel(pt_ref, qb_ref, bk_n, bv_n, bias_ref, lam_ref, gsub_ref, *rest, gsz, n_pages, lam_init):
    npg = gsz * n_pages
    kt_pages, v_pages = rest[0:npg], rest[npg:2 * npg]
    ob_ref = rest[2 * npg]
    r = qb_ref.shape[0] // gsz
    n_all = n_pages + 1
    grp = H_B // KVH_B
    half = 2 * grp * r
    bias_last = jnp.concatenate([bias_ref[h, 0] for h in range(H_B) for _ in range(2)], axis=0)
    bias_new = jnp.concatenate([bias_ref[h, 1] for h in range(H_B) for _ in range(2)], axis=0)
    lam = _lambda_value(lam_ref, lam_init)
    gs = gsub_ref[...]
    for g in range(gsz):
        tok = slice(g * r, (g + 1) * r)
        qb2 = _stack_heads(qb_ref, tok, 2 * H_B, 4 * HD)
        s_pages = []
        for p in range(n_all):
            if p < n_pages:
                s = _dot(qb2, kt_pages[g * n_pages + p][0, 0].reshape(4 * HD, PAGE).astype(BF16))
            else:
                s = _dot_nt(qb2, _pad_new(bk_n[tok, :]).astype(BF16))
            if p == n_pages - 1:
                s = s + bias_last
            if p == n_pages:
                s = s + bias_new
            s_pages.append(s)
        p_pages, lsum = _probabilities(s_pages)
        v_new = _pad_new(bv_n[tok, :]).astype(BF16)
        outs = []
        for kv in range(KVH_B):
            rs = slice(kv * half, (kv + 1) * half)
            acc = _dot(p_pages[-1][rs].astype(BF16), v_new[:, kv * 2 * HD:(kv + 1) * 2 * HD])
            for p in range(n_pages):
                v = v_pages[g * n_pages + p][0, 0, pl.ds(kv, PAGE, stride=KVH_B), :].astype(BF16)
                acc = acc + _dot(p_pages[p][rs].astype(BF16), v)
            o = acc / lsum[rs]
            for hh in range(grp):
                od = o[(2 * hh) * r:(2 * hh + 1) * r] - lam * o[(2 * hh + 1) * r:(2 * hh + 2) * r]
                ms = jnp.mean(od * od, axis=1, keepdims=True)
                outs.append((od * lax.rsqrt(ms + EPS) * gs) * (1.0 - lam_init))
        ob_ref[tok, :] = jnp.concatenate(outs, axis=1)


def _decode_b(page_table, layer, qb, bk_n, bv_n, kt, v2, bias_dec, lamvec, gsub_row, lam_init, r):
    nb, n_pages = page_table.shape
    gsz = G_DEC
    tok = lambda w: pl.BlockSpec((gsz * r, w), lambda i, pt: (i, 0))
    in_specs = [tok(2048), tok(256), tok(256),
                pl.BlockSpec((H_B, 2, r, LANES), lambda i, pt: (0, 0, 0, 0)),
                pl.BlockSpec((4, HD), lambda i, pt: (0, 0)),
                pl.BlockSpec((1, 2 * HD), lambda i, pt: (0, 0))]
    s1, a1 = _page_specs(kt, layer, gsz, n_pages, (1, KVH_B, 2, HD, PAGE), lambda pg: (pg, 0, 0, 0, 0))
    s2, a2 = _page_specs(v2, layer, gsz, n_pages, (1, KVH_B * PAGE, 2 * HD), lambda pg: (pg, 0, 0))
    grid_spec = pltpu.PrefetchScalarGridSpec(
        num_scalar_prefetch=1, grid=(nb // gsz,), in_specs=in_specs + s1 + s2,
        out_specs=pl.BlockSpec((gsz * r, 512), lambda i, pt: (i, 0)))
    return pl.pallas_call(
        functools.partial(_decode_b_kernel, gsz=gsz, n_pages=n_pages, lam_init=lam_init),
        grid_spec=grid_spec,
        out_shape=jax.ShapeDtypeStruct((nb * r, 512), F32),
        compiler_params=_cparams("arbitrary"),
        name="decode_diff",
    )(page_table, qb, bk_n, bv_n, bias_dec[H_A:], lamvec, gsub_row, *a1, *a2)


def _decode_c_kernel(pt_ref, qc_ref, ck_n, cv_n, lft_n, *rest, gsz, n_pages):
    npg = gsz * n_pages
    kt_pages, vt_pages, lf_pages = rest[0:npg], rest[npg:2 * npg], rest[2 * npg:3 * npg]
    oc_ref = rest[3 * npg]
    i = pl.program_id(0)
    r = qc_ref.shape[0] // gsz
    n_all = n_pages + 1
    lane = lax.broadcasted_iota(I32, (r, PAGE), 1)
    qrow = lax.broadcasted_iota(I32, (r, PAGE), 0)
    causal_new = jnp.concatenate([jnp.where(lane <= qrow, 0.0, NEG)] * H_C, axis=0)
    ri = lax.broadcasted_iota(I32, (PAGE, PAGE), 0)
    ci = lax.broadcasted_iota(I32, (PAGE, PAGE), 1)
    upper = jnp.where(ri <= ci, 1.0, 0.0).astype(BF16)
    ones = jnp.ones((PAGE, PAGE), BF16)
    nr = 8 * n_all
    rr = lax.broadcasted_iota(I32, (nr, nr), 0)
    cc = lax.broadcasted_iota(I32, (nr, nr), 1)
    prev_pages = jnp.where(((rr % 8) == (cc % 8)) & ((cc // 8) < (rr // 8)), 1.0, 0.0).astype(BF16)
    zrow = jnp.zeros((8 - H_C, PAGE), F32)
    for g in range(gsz):
        tok = slice(g * r, (g + 1) * r)
        xs = []
        for p in range(n_pages):
            sub = lax.rem(pt_ref[i * gsz + g, p], 8)
            xs += [lf_pages[g * n_pages + p][0, h, pl.ds(sub, 1), :] for h in range(H_C)] + [zrow]
        x = jnp.concatenate(xs + [lft_n[g]], axis=0)
        cum = _dot3_lhs(x, upper) + _dot3_rhs(prev_pages, _dot3_lhs(x, ones))
        qc2 = _stack_heads(qc_ref, tok, H_C, 2 * HD)
        s_pages = []
        for p in range(n_all):
            if p < n_pages:
                s = _dot(qc2, kt_pages[g * n_pages + p][0, 0].reshape(2 * HD, PAGE).astype(BF16))
            else:
                s = _dot_nt(qc2, _pad_new(ck_n[tok, :]).astype(BF16)) + causal_new
            decay = jnp.concatenate(
                [jnp.broadcast_to(cum[p * 8 + h:p * 8 + h + 1, :], (r, PAGE)) for h in range(H_C)], axis=0)
            s_pages.append(s - decay)
        p_pages, lsum = _probabilities(s_pages)
        vts = [vt_pages[g * n_pages + p][0, 0].reshape(2 * HD, PAGE).astype(BF16) for p in range(n_pages)]
        o = _pv(p_pages, vts, _pad_new(cv_n[tok, :]).astype(BF16)) / lsum
        oc_ref[tok, :] = jnp.concatenate([o[h * r:(h + 1) * r] for h in range(H_C)], axis=1)


def _decode_c(page_table, layer, qc, ck_n, cv_n, lft_new, kt, vt, lft, r):
    nb, n_pages = page_table.shape
    gsz = G_DEC
    tok = lambda w: pl.BlockSpec((gsz * r, w), lambda i, pt: (i, 0))
    in_specs = [tok(768), tok(128), tok(128), pl.BlockSpec((gsz, 8, LANES), lambda i, pt: (i, 0, 0))]
    args = []
    for cache in (kt, vt):
        s, a = _page_specs(cache, layer, gsz, n_pages, (1, KVH_C, HD, PAGE), lambda pg: (pg, 0, 0, 0))
        in_specs += s
        args += a
    specs = []
    for g in range(gsz):
        for p in range(n_pages):
            specs.append(pl.BlockSpec(
                (1, H_C, 8, PAGE),
                functools.partial(lambda i, pt, g_, p_: (layer, 0, pt[i * gsz + g_, p_] // 8, 0), g_=g, p_=p)))
    in_specs += specs
    args += [lft] * (gsz * n_pages)
    grid_spec = pltpu.PrefetchScalarGridSpec(
        num_scalar_prefetch=1, grid=(nb // gsz,), in_specs=in_specs,
        out_specs=pl.BlockSpec((gsz * r, 768), lambda i, pt: (i, 0)))
    return pl.pallas_call(
        functools.partial(_decode_c_kernel, gsz=gsz, n_pages=n_pages),
        grid_spec=grid_spec,
        out_shape=jax.ShapeDtypeStruct((nb * r, 768), F32),
        compiler_params=_cparams("arbitrary"),
        name="decode_forget",
    )(page_table, qc, ck_n, cv_n, lft_new, *args)


def _merge_kernel(x_ref, mod_ref, g_ref, oa_ref, ob_ref, oc_ref, wg_ref, wpa_ref, wpb_ref, wpc_ref, wo_ref,
                  o_ref, *, transposed):
    x = x_ref[...]
    gsz, r, _ = x.shape
    mod = mod_ref[...]
    h = _norm_mod(x, g_ref[...], mod[:, :, D:2 * D], mod[:, :, 0:D]).reshape(gsz * r, D).astype(BF16)
    gates = _sigmoid(_dot(h, wg_ref[...]))
    if transposed:
        oa, ob, oc = oa_ref[0].T, ob_ref[0].T, oc_ref[0].T
    else:
        oa, ob, oc = oa_ref[...], ob_ref[...], oc_ref[...]
    merged = (gates[:, 0:D] * _dot(oa.astype(BF16), wpa_ref[...])
              + gates[:, D:2 * D] * _dot(ob.astype(BF16), wpb_ref[...])
              + gates[:, 2 * D:3 * D] * _dot(oc.astype(BF16), wpc_ref[...]))
    y = _dot(merged.astype(BF16), wo_ref[...]).reshape(gsz, r, D)
    o_ref[...] = x + mod[:, :, 2 * D:3 * D] * y


def _merge(x, mod_a, g, oa, ob, oc, wg, wpa, wpb, wpc, wo, gsz, r, transposed):
    nb, rr, _ = x.shape
    const = lambda a: pl.BlockSpec(a.shape, lambda i, j: (0,) * a.ndim)
    if transposed:
        grid = (nb, rr // r)
        x_spec = pl.BlockSpec((1, r, D), lambda i, j: (i, j, 0))
        mod_spec = pl.BlockSpec((1, 1, 3 * D), lambda i, j: (i, 0, 0))
        o_spec = lambda a: pl.BlockSpec((1, a.shape[1], r), lambda i, j: (i, 0, j))
    else:
        grid = (nb // gsz, 1)
        x_spec = pl.BlockSpec((gsz, rr, D), lambda i, j: (i, 0, 0))
        mod_spec = pl.BlockSpec((gsz, 1, 3 * D), lambda i, j: (i, 0, 0))
        o_spec = lambda a: pl.BlockSpec((gsz * rr, a.shape[1]), lambda i, j: (i, 0))
    return pl.pallas_call(
        functools.partial(_merge_kernel, transposed=transposed),
        grid=grid,
        in_specs=[x_spec, mod_spec, const(g), o_spec(oa), o_spec(ob), o_spec(oc),
                  const(wg), const(wpa), const(wpb), const(wpc), const(wo)],
        out_specs=x_spec,
        out_shape=jax.ShapeDtypeStruct(x.shape, F32),
        compiler_params=_cparams("arbitrary", "arbitrary"),
        name="merge_prompt" if transposed else "merge_sample",
    )(x, mod_a, g, oa, ob, oc, wg, wpa, wpb, wpc, wo)


def _ffn_kernel(x_ref, mod_ref, g_ref, w1_ref, w2_ref, gf_ref, o_ref, *, final):
    x = x_ref[...]
    gsz, r, _ = x.shape
    mod = mod_ref[...]
    h = _norm_mod(x, g_ref[...], mod[:, :, D:2 * D], mod[:, :, 0:D]).reshape(gsz * r, D).astype(BF16)
    u = jnp.maximum(_dot(h, w1_ref[...]), 0.0)
    y = _dot((u * u).astype(BF16), w2_ref[...]).reshape(gsz, r, D)
    x2 = x + mod[:, :, 2 * D:3 * D] * y
    if final:
        ms = jnp.mean(x2 * x2, axis=-1, keepdims=True)
        x2 = x2 * lax.rsqrt(ms + EPS) * gf_ref[...]
    o_ref[...] = x2


def _ffn(x, mod_b, g, w1, w2, g_final, gsz, r, final, name):
    nb, rr, _ = x.shape
    const = lambda a: pl.BlockSpec(a.shape, lambda i, j: (0,) * a.ndim)
    if gsz == 1:
        grid = (nb, rr // r)
        x_spec = pl.BlockSpec((1, r, D), lambda i, j: (i, j, 0))
        mod_spec = pl.BlockSpec((1, 1, 3 * D), lambda i, j: (i, 0, 0))
    else:
        grid = (nb // gsz, 1)
        x_spec = pl.BlockSpec((gsz, rr, D), lambda i, j: (i, 0, 0))
        mod_spec = pl.BlockSpec((gsz, 1, 3 * D), lambda i, j: (i, 0, 0))
    return pl.pallas_call(
        functools.partial(_ffn_kernel, final=final),
        grid=grid,
        in_specs=[x_spec, mod_spec, const(g), const(w1), const(w2), const(g_final)],
        out_specs=x_spec,
        out_shape=jax.ShapeDtypeStruct(x.shape, F32),
        compiler_params=_cparams("arbitrary", "arbitrary"),
        name=name,
    )(x, mod_b, g, w1, w2, g_final)


TM_PROMPT = 512
G_SAMPLE = 16


def kernel(x_prompt, x_sample, c_prompt, c_sample, cache_a_k, cache_a_v, cache_a_kidx, cache_b_k, cache_b_v, cache_c_k, cache_c_v, cache_c_logf, page_table, t5_table, w_ada, b_ada, g_mix, g_ffn, w_in, b_forget, lam_q1, lam_k1, lam_q2, lam_k2, g_subln, w_gate, w_pa, w_pb, w_pc, w_out, w_ff1, w_ff2, g_final):
    depth = w_in.shape[0]
    nbp, seq, _ = x_prompt.shape
    nbs, dec_seq, _ = x_sample.shape
    n_pool, page = cache_a_k.shape[1], cache_a_k.shape[2]
    n_pages = page_table.shape[1]
    past_len = n_pages * page
    assert page == PAGE and seq % max(TM_PROMPT, _CUM_T) == 0 and dec_seq == 8
    assert nbs % G_SAMPLE == 0 and nbs % G_DEC == 0 and n_pool % 8 == 0
    tm = min(TM_PROMPT, seq)

    nc = nbp + nbs
    mod = _ada(jnp.concatenate([c_prompt, c_sample], axis=0), w_ada, b_ada)
    mod = mod.reshape(depth, nc, 1, 6 * D)
    bias_pt, bias_dec = _bias_tiles(t5_table, past_len, dec_seq)

    kv_t = lambda c: jnp.transpose(c, (0, 1, 3, 4, 2))
    a_kt, a_vt, c_kt, c_vt = kv_t(cache_a_k), kv_t(cache_a_v), kv_t(cache_c_k), kv_t(cache_c_v)
    a_kit = jnp.transpose(cache_a_kidx, (0, 1, 3, 2))
    b_kt = jnp.transpose(cache_b_k, (0, 1, 3, 4, 5, 2))
    b_v2 = cache_b_v.reshape(depth, n_pool, page * KVH_B, 2 * HD)
    c_lft = jnp.transpose(cache_c_logf, (0, 3, 1, 2))

    xp, xs = x_prompt, x_sample
    rows_p, rows_s = [], []
    g_final2 = g_final.reshape(1, D)
    for l in range(depth):
        lam_init = 0.8 - 0.6 * math.exp(-0.3 * l)
        wr, wt = _prep_in_prompt(w_in[l])
        ws = _prep_in_sample(w_in[l])
        bfb = _cf_block(b_forget[l].reshape(1, H_C))
        wg = w_gate[l].astype(BF16)
        wpa, wpb, wpc = w_pa[l].astype(BF16), w_pb[l].astype(BF16), w_pc[l].astype(BF16)
        wpa_s = _pad_proj_rows(w_pa[l], H_A, lambda h: h // (H_A // KVH_A)).astype(BF16)
        wpc_s = _pad_proj_rows(w_pc[l], H_C, lambda h: h // (H_C // KVH_C)).astype(BF16)
        wo = w_out[l].astype(BF16)
        w1, w2 = w_ff1[l].astype(BF16), w_ff2[l].astype(BF16)
        gm, gf = g_mix[l].reshape(1, D), g_ffn[l].reshape(1, D)
        lamvec = jnp.stack([lam_q1[l], lam_k1[l], lam_q2[l], lam_k2[l]])
        mod_pa, mod_pb = mod[l, :nbp, :, :3 * D], mod[l, :nbp, :, 3 * D:]
        mod_sa, mod_sb = mod[l, nbp:, :, :3 * D], mod[l, nbp:, :, 3 * D:]
        last = l == depth - 1

        (bv, lf, akb, akib, bkb, ckb, lfrep, akT, avTf, akiT, bkT, ckT, cvTf, awiT,
         aqT, aqiT, bqT, cqT, avT, bvT, cvT) = _inproj_prompt(xp, mod_pa, gm, wr, wt, bfb, tm)
        rows_p.append((akT, avTf, akiT, bkT, bv, ckT, cvTf, lf))
        kaug = _cum_prompt(lfrep)
        oaT = _attn_a(aqT, aqiT, awiT, akb, akib, avT, bias_pt[:H_A])
        obT = _attn_b(bqT, bkb, bvT, bias_pt[H_A:], lamvec, g_subln[l].reshape(2 * HD, 1), lam_init)
        ocT = _attn_c(cqT, ckb, kaug, cvT)
        x1 = _merge(xp, mod_pa, gm, oaT, obT, ocT, wg, wpa, wpb, wpc, wo, 1, tm, True)
        xp = _ffn(x1, mod_pb, gf, w1, w2, g_final2, 1, tm, last, "ffn_prompt")

        (sak, sav, saki, sbk, sbv, sck, scv, slf, qa, qi, wi, qb, qc, slfrep) = _inproj_sample(
            xs, mod_sa, gm, ws, bfb, G_SAMPLE)
        rows_s.append((sak, sav, saki, sbk, sbv, sck, scv, slf))
        lft_new = jnp.swapaxes(slfrep.reshape(nbs, dec_seq, LANES)[:, :, 0:8], 1, 2)
        lft_new = jnp.concatenate([lft_new, jnp.zeros((nbs, 8, LANES - dec_seq), F32)], axis=2)
        lane_head = jnp.arange(8)[None, :, None] < H_C
        lft_new = jnp.where(lane_head, lft_new, 0.0)
        oa = _decode_a(page_table, l, qa, qi, wi, sak, sav, saki, a_kt, a_vt, a_kit, bias_dec, dec_seq)
        ob = _decode_b(page_table, l, qb, sbk, sbv, b_kt, b_v2, bias_dec, lamvec, g_subln[l].reshape(1, 2 * HD),
                       lam_init, dec_seq)
        oc = _decode_c(page_table, l, qc, sck, scv, lft_new, c_kt, c_vt, c_lft, dec_seq)
        x1s = _merge(xs, mod_sa, gm, oa, ob, oc, wg, wpa_s, wpb, wpc_s, wo, G_SAMPLE, dec_seq, False)
        xs = _ffn(x1s, mod_sb, gf, w1, w2, g_final2, G_SAMPLE, dec_seq, last, "ffn_sample")

    def stack(rows, i):
        return jnp.stack([r[i] for r in rows])

    def stack_sample(i, shape):
        return stack(rows_s, i).reshape((depth, nbs, dec_seq) + shape)

    def stack_prompt_t(i, shape):
        n = len(shape)
        y = stack(rows_p, i).reshape((depth, nbp) + shape + (seq,))
        return jnp.transpose(y, (0, 1, n + 2) + tuple(range(2, n + 2)))

    out_p = (stack_prompt_t(0, (KVH_A, HD)), stack_prompt_t(1, (KVH_A, HD)), stack_prompt_t(2, (D_IDX,)),
             stack_prompt_t(3, (KVH_B, 2, HD)), stack(rows_p, 4).reshape(depth, nbp, seq, KVH_B, 2 * HD),
             stack_prompt_t(5, (KVH_C, HD)), stack_prompt_t(6, (KVH_C, HD)), stack(rows_p, 7))
    out_s = (stack_sample(0, (KVH_A, HD)), stack_sample(1, (KVH_A, HD)), stack_sample(2, (D_IDX,)),
             stack_sample(3, (KVH_B, 2, HD)), stack_sample(4, (KVH_B, 2 * HD)), stack_sample(5, (KVH_C, HD)),
             stack_sample(6, (KVH_C, HD)), stack_sample(7, (H_C,)))
    return (xp, xs) + out_p + out_s
```

```python
import functools
import math

import numpy as np
import jax
import jax.numpy as jnp
from jax import lax
from jax.experimental import pallas as pl
from jax.experimental.pallas import tpu as pltpu

F32 = jnp.float32
BF16 = jnp.bfloat16
I32 = jnp.int32

D = 1024
HD = 64
H_A, KVH_A = 6, 2
H_IDX, D_IDX = 8, 64
K_TOP_MAX = 256
IDX_W_SCALE = (H_IDX ** -0.5) * (D_IDX ** -0.5)
H_B, KVH_B = 4, 2
H_C, KVH_C = 6, 2
N_BUCKETS, T5_MAX_EXACT, T5_MAX_DIST = 32, 16, 128
D_FF = 4 * D
EPS = 1e-6
N_HEADS_T5 = H_A + H_B
PAGE = 128

LANES = 128
TQ = 128
TKC = 128
FAR_BLOCKS = 4
NEG = -1e30
INT_MIN = -2 ** 31
VMEM_LIMIT = 56 * 1024 * 1024

_W = dict(aq=H_A * HD, ak=KVH_A * HD, av=KVH_A * HD, aqi=H_IDX * D_IDX, aki=D_IDX, awi=H_IDX,
          bq=H_B * 2 * HD, bk=KVH_B * 2 * HD, bv=KVH_B * 2 * HD,
          cq=H_C * HD, ck=KVH_C * HD, cv=KVH_C * HD, cf=H_C)
_OFF = {}
_o = 0
for _k, _v in _W.items():
    _OFF[_k] = _o
    _o += _v

_R_AK, _R_AV, _R_AKI, _R_BK, _R_BV, _R_CK, _R_CV, _R_CF = 0, 128, 256, 384, 640, 896, 1024, 1152
N_ROWS = 1280
_P_AK, _P_AKI, _P_BK, _P_CK, _P_CF, _P_BV = 0, 128, 256, 512, 640, 768
N_PROWS = 1024
_T_AQ, _T_AQI, _T_AWI, _T_BQ, _T_CQ, _T_AV, _T_BV, _T_CV = 0, 384, 896, 912, 1424, 1808, 1936, 2192
_T_AK, _T_AKI, _T_BK, _T_CK = 2320, 2448, 2512, 2768
N_T = 2896
_S_AQ, _S_AQI, _S_AWI, _S_BQ, _S_CQ = 0, 768, 1792, 2816, 4864
N_SQ = 5632


def _cparams(*sem):
    return pltpu.CompilerParams(dimension_semantics=sem, vmem_limit_bytes=VMEM_LIMIT)


def _dot(a, b):
    return jnp.dot(a, b, preferred_element_type=F32)


def _dot_nt(a, b):
    return lax.dot_general(a, b, (((1,), (1,)), ((), ())), preferred_element_type=F32)


def _split3(x):
    hi = x.astype(BF16)
    r1 = x - hi.astype(F32)
    mid = r1.astype(BF16)
    lo = (r1 - mid.astype(F32)).astype(BF16)
    return hi, mid, lo


def _dot3_rhs(a_bf16, x):
    hi, mid, lo = _split3(x)
    return _dot(a_bf16, hi) + _dot(a_bf16, mid) + _dot(a_bf16, lo)


def _dot3_lhs(x, b_bf16):
    hi, mid, lo = _split3(x)
    return _dot(hi, b_bf16) + _dot(mid, b_bf16) + _dot(lo, b_bf16)


def _norm_mod(x, g, sc, sh):
    ms = jnp.mean(x * x, axis=-1, keepdims=True)
    return (x * lax.rsqrt(ms + EPS) * g) * (1.0 + sc) + sh


def _log_sigmoid(z):
    return jnp.minimum(z, 0.0) - jnp.log1p(jnp.exp(-jnp.abs(z)))


def _sigmoid(z):
    return 1.0 / (1.0 + jnp.exp(-z))


def _sortable_key(x):
    bits = lax.bitcast_convert_type(x, I32)
    return bits ^ ((bits >> 31) & 0x7FFFFFFF)


def _ada_kernel(c_ref, w_ref, b_ref, o_ref):
    c = c_ref[...]
    s = (c * _sigmoid(c)).astype(BF16)
    o_ref[0] = _dot(s, w_ref[0].astype(BF16)) + b_ref[0]


def _ada(c_all, w_ada, b_ada):
    depth = w_ada.shape[0]
    nc = c_all.shape[0]
    nt = 6
    return pl.pallas_call(
        _ada_kernel,
        grid=(depth, nt),
        in_specs=[pl.BlockSpec((nc, D), lambda l, j: (0, 0)),
                  pl.BlockSpec((1, D, D), lambda l, j: (l, 0, j)),
                  pl.BlockSpec((1, 1, D), lambda l, j: (l, 0, j))],
        out_specs=pl.BlockSpec((1, nc, D), lambda l, j: (l, 0, j)),
        out_shape=jax.ShapeDtypeStruct((depth, nc, 6 * D), F32),
        compiler_params=_cparams("arbitrary", "arbitrary"),
        name="ada",
    )(c_all, w_ada, b_ada.reshape(depth, 1, 6 * D))


def _t5_bucket_np(rel):
    n = np.maximum(rel, 0)
    nf = np.maximum(n, 1).astype(np.float32)
    large = T5_MAX_EXACT + (np.log(nf / np.float32(T5_MAX_EXACT)) / np.float32(math.log(T5_MAX_DIST / T5_MAX_EXACT))
                            * np.float32(N_BUCKETS - T5_MAX_EXACT)).astype(np.int32)
    return np.where(n < T5_MAX_EXACT, n, np.minimum(large, N_BUCKETS - 1)).astype(np.int32)


def _bias_bucket_tables(past_len, dec_seq):
    s = np.arange(TKC)[:, None]
    t = np.arange(TQ)[None, :]
    tiles = []
    for d in (0, 1):
        rel = d * TKC + t - s
        tiles.append(np.where(rel >= 0, _t5_bucket_np(rel), -1))
    prompt = np.stack(tiles).astype(np.int32)
    i = np.arange(dec_seq)[:, None]
    lane = np.arange(PAGE)[None, :]
    rel_last = (past_len + i) - (past_len - PAGE + lane)
    rel_new = i - lane
    dl = _t5_bucket_np(rel_last)
    dn = np.where((rel_new >= 0) & (lane < dec_seq), _t5_bucket_np(rel_new), -1)
    dec = np.stack([dl, dn]).astype(np.int32)
    return prompt, dec


def _bias_kernel(tab_ref, pb_ref, db_ref, pt_ref, dt_ref):
    pb = pb_ref[...]
    db = db_ref[...]
    for h in range(N_HEADS_T5):
        def lut(bk):
            acc = jnp.zeros(bk.shape, F32)
            for b in range(N_BUCKETS):
                acc = jnp.where(bk == b, tab_ref[b, h], acc)
            return jnp.where(bk < 0, NEG, acc - tab_ref[N_BUCKETS - 1, h])
        pt_ref[h] = lut(pb)
        dt_ref[h] = lut(db)


def _bias_tiles(t5_table, past_len, dec_seq):
    pb, db = _bias_bucket_tables(past_len, dec_seq)
    return pl.pallas_call(
        _bias_kernel,
        in_specs=[pl.BlockSpec(memory_space=pltpu.SMEM),
                  pl.BlockSpec(memory_space=pltpu.VMEM),
                  pl.BlockSpec(memory_space=pltpu.VMEM)],
        out_specs=[pl.BlockSpec(memory_space=pltpu.VMEM), pl.BlockSpec(memory_space=pltpu.VMEM)],
        out_shape=[jax.ShapeDtypeStruct((N_HEADS_T5,) + pb.shape, F32),
                   jax.ShapeDtypeStruct((N_HEADS_T5,) + db.shape, F32)],
        name="t5_bias_tiles",
    )(t5_table, jnp.asarray(pb), jnp.asarray(db))


def _seg(w, name):
    return w[:, _OFF[name]:_OFF[name] + _W[name]]


def _cf_block(cf):
    n = cf.shape[0]
    z2 = jnp.zeros((n, 2), cf.dtype)
    return jnp.concatenate([cf, z2, cf, z2, cf, z2, jnp.zeros((n, LANES - 24), cf.dtype)], axis=1)


def _rows_block(w):
    z64 = jnp.zeros((D, 64), w.dtype)
    return jnp.concatenate([_seg(w, "ak"), _seg(w, "av"), _seg(w, "aki"), z64, _seg(w, "bk"), _seg(w, "bv"),
                            _seg(w, "ck"), _seg(w, "cv"), _cf_block(_seg(w, "cf"))], axis=1)


def _prep_in_prompt(w):
    scale = HD ** -0.5
    z8 = jnp.zeros((D, 8), w.dtype)
    z64 = jnp.zeros((D, 64), w.dtype)
    w_t = jnp.concatenate([_seg(w, "aq") * scale, _seg(w, "aqi"), _seg(w, "awi"), z8, _seg(w, "bq") * scale,
                           _seg(w, "cq") * scale, _seg(w, "av"), _seg(w, "bv"), _seg(w, "cv"),
                           _seg(w, "ak"), _seg(w, "aki"), _seg(w, "bk"), _seg(w, "ck")], axis=1)
    w_r = jnp.concatenate([_seg(w, "ak"), _seg(w, "aki"), z64, _seg(w, "bk"), _seg(w, "ck"),
                           _cf_block(_seg(w, "cf")), _seg(w, "bv")], axis=1)
    return w_r.astype(BF16), w_t.T.astype(BF16)


def _pad_heads(wq, n_heads, kv_of, width, n_slots):
    blocks = []
    for h in range(n_heads):
        blk = jnp.zeros((D, n_slots * HD), wq.dtype)
        s = kv_of(h)
        blk = blk.at[:, s * HD:(s + 1) * HD].set(wq[:, h * width:(h + 1) * width])
        blocks.append(blk)
    return jnp.concatenate(blocks, axis=1)


def _prep_in_sample(w):
    scale = HD ** -0.5
    aq = _pad_heads(_seg(w, "aq") * scale, H_A, lambda h: h // (H_A // KVH_A), HD, 2)
    aqi = _pad_heads(_seg(w, "aqi"), H_IDX, lambda h: 0, D_IDX, 2)
    awi = jnp.repeat(_seg(w, "awi"), LANES, axis=1)
    bq = _pad_heads(_seg(w, "bq") * scale, 2 * H_B, lambda hc: (hc // 2 // (H_B // KVH_B)) * 2 + hc % 2, HD, 4)
    cq = _pad_heads(_seg(w, "cq") * scale, H_C, lambda h: h // (H_C // KVH_C), HD, 2)
    return jnp.concatenate([aq, aqi, awi, bq, cq, _rows_block(w)], axis=1).astype(BF16)


def _pad_proj_rows(wp, n_heads, kv_of):
    blocks = []
    for h in range(n_heads):
        blk = jnp.zeros((2 * HD, D), wp.dtype)
        s = kv_of(h)
        blk = blk.at[s * HD:(s + 1) * HD].set(wp[h * HD:(h + 1) * HD])
        blocks.append(blk)
    return jnp.concatenate(blocks, axis=0)


def _store_rows(pr, bf, ak_o, av_o, aki_o, bk_o, bv_o, ck_o, cv_o, lf_o, idx):
    ak_o[idx] = pr[:, _R_AK:_R_AK + 128]
    av_o[idx] = pr[:, _R_AV:_R_AV + 128]
    aki_o[idx] = pr[:, _R_AKI:_R_AKI + 64]
    bk_o[idx] = pr[:, _R_BK:_R_BK + 256]
    bv_o[idx] = pr[:, _R_BV:_R_BV + 256]
    ck_o[idx] = pr[:, _R_CK:_R_CK + 128]
    cv_o[idx] = pr[:, _R_CV:_R_CV + 128]
    lf = _log_sigmoid(pr[:, _R_CF:_R_CF + 128] + bf)
    lf_o[idx] = lf[:, 0:H_C]
    return lf


def _inproj_p_kernel(x_ref, mod_ref, g_ref, wr_ref, wt_ref, bf_ref,
                     bv_o, lf_o, akb_o, akib_o, bkb_o, ckb_o, lfrep_o,
                     akT_o, avT_o, akiT_o, bkT_o, ckT_o, cvT_o, awiT_o,
                     aqT_o, aqiT_o, bqT_o, cqT_o, avTb_o, bvTb_o, cvTb_o):
    x = x_ref[0]
    mod = mod_ref[0]
    h = _norm_mod(x, g_ref[...], mod[:, D:2 * D], mod[:, 0:D])
    pr = _dot(h.astype(BF16), wr_ref[...])
    bv_o[0] = pr[:, _P_BV:_P_BV + 256]
    lf = _log_sigmoid(pr[:, _P_CF:_P_CF + 128] + bf_ref[...])
    lf_o[0] = lf[:, 0:H_C]
    lfrep_o[0] = lf
    akb_o[0] = pr[:, _P_AK:_P_AK + 128].astype(BF16)
    akib_o[0] = pr[:, _P_AKI:_P_AKI + 64].astype(BF16)
    bkb_o[0] = pr[:, _P_BK:_P_BK + 256].astype(BF16)
    ckb_o[0] = pr[:, _P_CK:_P_CK + 128].astype(BF16)
    pt = _dot(wt_ref[...], h.T.astype(BF16))
    akT_o[0] = pt[_T_AK:_T_AK + 128]
    avT_o[0] = pt[_T_AV:_T_AV + 128]
    akiT_o[0] = pt[_T_AKI:_T_AKI + 64]
    bkT_o[0] = pt[_T_BK:_T_BK + 256]
    ckT_o[0] = pt[_T_CK:_T_CK + 128]
    cvT_o[0] = pt[_T_CV:_T_CV + 128]
    awiT_o[0] = pt[_T_AWI:_T_AWI + 16]
    aqT_o[0] = pt[_T_AQ:_T_AQ + 384].astype(BF16)
    aqiT_o[0] = pt[_T_AQI:_T_AQI + 512].astype(BF16)
    bqT_o[0] = pt[_T_BQ:_T_BQ + 512].astype(BF16)
    cqT_o[0] = pt[_T_CQ:_T_CQ + 384].astype(BF16)
    avTb_o[0] = pt[_T_AV:_T_AV + 128].astype(BF16)
    bvTb_o[0] = pt[_T_BV:_T_BV + 256].astype(BF16)
    cvTb_o[0] = pt[_T_CV:_T_CV + 128].astype(BF16)


def _inproj_prompt(x, mod_a, g, wr, wt, bfb, tm):
    b, l, _ = x.shape
    row = lambda w, dt: jax.ShapeDtypeStruct((b, l, w), dt)
    tr = lambda r, dt: jax.ShapeDtypeStruct((b, r, l), dt)
    row_spec = lambda w: pl.BlockSpec((1, tm, w), lambda i, j: (i, j, 0))
    tr_spec = lambda r: pl.BlockSpec((1, r, tm), lambda i, j: (i, 0, j))
    rows = [(256, F32), (H_C, F32), (128, BF16), (64, BF16), (256, BF16), (128, BF16), (128, F32)]
    trs = [(128, F32), (128, F32), (64, F32), (256, F32), (128, F32), (128, F32), (16, F32),
           (384, BF16), (512, BF16), (512, BF16), (384, BF16), (128, BF16), (256, BF16), (128, BF16)]
    return pl.pallas_call(
        _inproj_p_kernel,
        grid=(b, l // tm),
        in_specs=[pl.BlockSpec((1, tm, D), lambda i, j: (i, j, 0)),
                  pl.BlockSpec((1, 1, 3 * D), lambda i, j: (i, 0, 0)),
                  pl.BlockSpec((1, D), lambda i, j: (0, 0)),
                  pl.BlockSpec((D, N_PROWS), lambda i, j: (0, 0)),
                  pl.BlockSpec((N_T, D), lambda i, j: (0, 0)),
                  pl.BlockSpec((1, LANES), lambda i, j: (0, 0))],
        out_specs=[row_spec(w) for w, _ in rows] + [tr_spec(r) for r, _ in trs],
        out_shape=[row(w, dt) for w, dt in rows] + [tr(r, dt) for r, dt in trs],
        compiler_params=_cparams("arbitrary", "arbitrary"),
        name="inproj_prompt",
    )(x, mod_a, g, wr, wt, bfb)


def _inproj_s_kernel(x_ref, mod_ref, g_ref, w_ref, bf_ref,
                     ak_o, av_o, aki_o, bk_o, bv_o, ck_o, cv_o, lf_o,
                     qa_o, qi_o, wi_o, qb_o, qc_o, lfrep_o):
    x = x_ref[...]
    gsz, r, _ = x.shape
    mod = mod_ref[...]
    h = _norm_mod(x, g_ref[...], mod[:, :, D:2 * D], mod[:, :, 0:D]).reshape(gsz * r, D)
    pr = _dot(h.astype(BF16), w_ref[...])
    lf = _store_rows(pr[:, N_SQ:], bf_ref[...], ak_o, av_o, aki_o, bk_o, bv_o, ck_o, cv_o, lf_o,
                     (slice(None), slice(None)))
    lfrep_o[...] = lf
    qa_o[...] = pr[:, _S_AQ:_S_AQ + 768].astype(BF16)
    qi_o[...] = pr[:, _S_AQI:_S_AQI + 1024].astype(BF16)
    wi_o[...] = pr[:, _S_AWI:_S_AWI + 1024]
    qb_o[...] = pr[:, _S_BQ:_S_BQ + 2048].astype(BF16)
    qc_o[...] = pr[:, _S_CQ:_S_CQ + 768].astype(BF16)


def _inproj_sample(x, mod_a, g, w, bfb, gsz):
    nb, r, _ = x.shape
    n = nb * r
    tm = gsz * r
    row_w = [128, 128, 64, 256, 256, 128, 128, H_C]
    outs = [(w_, F32) for w_ in row_w] + [(768, BF16), (1024, BF16), (1024, F32), (2048, BF16), (768, BF16), (128, F32)]
    return pl.pallas_call(
        _inproj_s_kernel,
        grid=(nb // gsz,),
        in_specs=[pl.BlockSpec((gsz, r, D), lambda i: (i, 0, 0)),
                  pl.BlockSpec((gsz, 1, 3 * D), lambda i: (i, 0, 0)),
                  pl.BlockSpec((1, D), lambda i: (0, 0)),
                  pl.BlockSpec((D, N_SQ + N_ROWS), lambda i: (0, 0)),
                  pl.BlockSpec((1, LANES), lambda i: (0, 0))],
        out_specs=[pl.BlockSpec((tm, w_), lambda i: (i, 0)) for w_, _ in outs],
        out_shape=[jax.ShapeDtypeStruct((n, w_), dt) for w_, dt in outs],
        compiler_params=_cparams("arbitrary"),
        name="inproj_sample",
    )(x, mod_a, g, w, bfb)


_CUM_T = 256


def _cum_kernel(lf_ref, k_ref, o_ref):
    l = lf_ref.shape[1]
    r = lax.broadcasted_iota(I32, (_CUM_T, _CUM_T), 0)
    c = lax.broadcasted_iota(I32, (_CUM_T, _CUM_T), 1)
    tri = jnp.where(c <= r, 1.0, 0.0).astype(BF16)
    lane = lax.broadcasted_iota(I32, (_CUM_T, LANES), 1)
    carry = jnp.zeros((1, LANES), F32)
    for i in range(l // _CUM_T):
        rows = slice(i * _CUM_T, (i + 1) * _CUM_T)
        cum = _dot3_rhs(tri, lf_ref[0, rows, :]) + carry
        carry = cum[_CUM_T - 1:_CUM_T, :]
        hi, mid, lo = _split3(cum)
        piece = jnp.where(lane < 8, hi, jnp.where(lane < 16, mid, lo))
        o_ref[0, rows, 0:LANES] = k_ref[0, rows, :]
        o_ref[0, rows, LANES:2 * LANES] = -piece


def _cum_prompt(lfrep, ckb):
    b, l, _ = lfrep.shape
    return pl.pallas_call(
        _cum_kernel,
        grid=(b,),
        in_specs=[pl.BlockSpec((1, l, LANES), lambda i: (i, 0, 0)),
                  pl.BlockSpec((1, l, LANES), lambda i: (i, 0, 0))],
        out_specs=pl.BlockSpec((1, l, 2 * LANES), lambda i: (i, 0, 0)),
        out_shape=jax.ShapeDtypeStruct((b, l, 2 * LANES), BF16),
        compiler_params=_cparams("arbitrary"),
        name="forget_cumsum",
    )(lfrep, ckb)


def _online_update(s, m_scr, acc_scr, vta):
    m_prev = m_scr[...]
    m_new = jnp.maximum(m_prev, jnp.max(s, axis=0, keepdims=True))
    alpha = jnp.exp(m_prev - m_new)
    p = jnp.exp(s - m_new).astype(BF16)
    m_scr[...] = m_new
    n_g = len(vta)
    w = s.shape[1] // n_g
    for g in range(n_g):
        pv = _dot(vta[g], p[:, g * w:(g + 1) * w])
        acc_scr[g] = acc_scr[g] * alpha[:, g * w:(g + 1) * w] + pv


def _key_rows(start, n):
    return pl.ds(pl.multiple_of(start * TKC, TKC), n * TKC)


def _v_aug(vt_ref, start, n, g, rows):
    v = vt_ref[0, g * rows:(g + 1) * rows, _key_rows(start, n)]
    return jnp.concatenate([v, jnp.ones((16, n * TKC), BF16)], axis=0)


def _add_near(s, n, tile1, tile0):
    parts = []
    if n > 2:
        parts.append(s[0:(n - 2) * TKC])
    if n >= 2:
        blk = s[(n - 2) * TKC:(n - 1) * TKC]
        parts.append(blk if tile1 is None else blk + tile1)
    parts.append(s[(n - 1) * TKC:] + tile0)
    return parts[0] if len(parts) == 1 else jnp.concatenate(parts, axis=0)


def _chunk_loop(qi, chunk):
    n_big = jnp.maximum(qi - 1, 0) // FAR_BLOCKS

    def far(c, carry):
        chunk(c * FAR_BLOCKS, FAR_BLOCKS, False)
        return carry
    lax.fori_loop(0, n_big, far, 0)
    start = n_big * FAR_BLOCKS
    n_tail = qi + 1 - start
    for v in range(1, FAR_BLOCKS + 2):
        @pl.when(n_tail == v)
        def _(v=v):
            chunk(start, v, True)


def _attn_a_kernel(qT_ref, qiT_ref, wT_ref, k_ref, ki_ref, vT_ref, bias_ref, oT_ref,
                   qpad, qipad, key_scr, am_scr, j_scr, m_scr, acc_scr, *, k_top):
    b = pl.program_id(0)
    qi = pl.program_id(1)
    n_chunk = qi + 1
    grp = H_A // KVH_A

    @pl.when((b == 0) & (qi == 0))
    def _():
        qpad[...] = jnp.zeros_like(qpad)
    for h in range(H_A):
        g = h // grp
        qpad[g * HD:(g + 1) * HD, h * TQ:(h + 1) * TQ] = qT_ref[0, h * HD:(h + 1) * HD, :]
    for h in range(H_IDX):
        qipad[:, h * TQ:(h + 1) * TQ] = qiT_ref[0, h * D_IDX:(h + 1) * D_IDX, :]
    w = wT_ref[0, 0:H_IDX, :] * IDX_W_SCALE
    grows = FAR_BLOCKS * TKC
    n_grp = (n_chunk + FAR_BLOCKS - 1) // FAR_BLOCKS
    row = lax.broadcasted_iota(I32, (TKC, TQ), 0)
    col = lax.broadcasted_iota(I32, (TKC, TQ), 1) + qi * TQ
    grow = lax.broadcasted_iota(I32, (grows, TQ), 0)

    def score_group(gi, carry):
        for bi in range(FAR_BLOCKS):
            start = pl.multiple_of(gi * grows + bi * TKC, TKC)
            sl = pl.ds(start, TKC)
            s = _dot(ki_ref[0, sl, :], qipad[...])
            sc = jnp.zeros((TKC, TQ), F32)
            for h in range(H_IDX):
                sc = sc + jnp.maximum(s[:, h * TQ:(h + 1) * TQ], 0.0) * w[h:h + 1, :]
            sc = jnp.where(sc == 0.0, 0.0, sc)
            sc = jnp.where(row + start > col, -jnp.inf, sc)
            key_scr[sl, :] = _sortable_key(sc)
        return carry
    lax.fori_loop(0, n_grp, score_group, 0)

    def count(pred):
        def body(gi, cnt):
            start = pl.multiple_of(gi * grows, grows)
            hit = jnp.where(pred(key_scr[pl.ds(start, grows), :], start), 1, 0).astype(I32)
            return cnt + jnp.sum(hit.reshape(grows // 8, 8, TQ), axis=0)
        cnt8 = lax.fori_loop(0, n_grp, body, jnp.zeros((8, TQ), I32))
        return jnp.sum(cnt8, axis=0, keepdims=True)

    needs_search = n_chunk * TQ > k_top

    @pl.when(jnp.logical_not(needs_search))
    def _():
        def body(gi, carry):
            am_scr[pl.ds(pl.multiple_of(gi * grows, grows), grows), :] = jnp.zeros((grows, TQ), F32)
            return carry
        lax.fori_loop(0, n_grp, body, 0)

    @pl.when(needs_search)
    def _():
        def it_body(it, t):
            cand = t ^ lax.shift_left(jnp.int32(1), 31 - it)
            cnt = count(lambda k, start: k >= cand)
            return jnp.where(cnt >= k_top, cand, t)
        thr = lax.fori_loop(0, 32, it_body, jnp.full((1, TQ), INT_MIN, I32))
        cnt_gt = count(lambda k, start: k > thr)
        cnt_eq = count(lambda k, start: k == thr)
        need = k_top - cnt_gt
        n_bits = max(1, (key_scr.shape[0] - 1).bit_length())
        j_scr[...] = jnp.full((1, TQ), key_scr.shape[0], I32)

        @pl.when(jnp.max(jnp.where(cnt_eq > need, 1, 0)) > 0)
        def _():
            def tie_body(it, jv):
                cand = jv | lax.shift_left(jnp.int32(1), n_bits - 1 - it)
                cnt = count(lambda k, start: (k == thr) & ((grow + start) < cand))
                return jnp.where(cnt < need, cand, jv)
            j_scr[...] = lax.fori_loop(0, n_bits, tie_body, jnp.zeros((1, TQ), I32))
        jv = j_scr[...]

        def body(gi, carry):
            start = pl.multiple_of(gi * grows, grows)
            k = key_scr[pl.ds(start, grows), :]
            sel = (k > thr) | ((k == thr) & ((grow + start) <= jv))
            am_scr[pl.ds(start, grows), :] = jnp.where(sel, 0.0, NEG)
            return carry
        lax.fori_loop(0, n_grp, body, 0)

    m_scr[...] = jnp.full(m_scr.shape, NEG, F32)
    acc_scr[...] = jnp.zeros_like(acc_scr)

    def chunk(start, n, near):
        sl = _key_rows(start, n)
        s = _dot(k_ref[0, sl, :], qpad[...])
        s = s + jnp.concatenate([am_scr[sl, :]] * H_A, axis=1)
        if near:
            tile = lambda kind: jnp.concatenate([bias_ref[h, kind] for h in range(H_A)], axis=1)
            s = _add_near(s, n, tile(1), tile(0))
        _online_update(s, m_scr, acc_scr, [_v_aug(vT_ref, start, n, g, HD) for g in range(KVH_A)])

    _chunk_loop(qi, chunk)

    for h in range(H_A):
        g, hh = h // grp, h % grp
        a = acc_scr[g][:, hh * TQ:(hh + 1) * TQ]
        oT_ref[0, h * HD:(h + 1) * HD, :] = a[0:HD] / a[HD:HD + 1]


def _attn_a(aqT, aqiT, awiT, akb, akib, avT, bias_pt):
    b, _, l = aqT.shape
    k_top = min(K_TOP_MAX, l // 4)
    blk_t = lambda r: pl.BlockSpec((1, r, TQ), lambda i, j: (i, 0, j))
    full = lambda s1, s2: pl.BlockSpec((1, s1, s2), lambda i, j: (i, 0, 0))
    return pl.pallas_call(
        functools.partial(_attn_a_kernel, k_top=k_top),
        grid=(b, l // TQ),
        in_specs=[blk_t(384), blk_t(512), blk_t(16), full(l, 128), full(l, 64), full(128, l),
                  pl.BlockSpec((H_A, 2, TKC, TQ), lambda i, j: (0, 0, 0, 0))],
        out_specs=blk_t(384),
        out_shape=jax.ShapeDtypeStruct((b, 384, l), F32),
        scratch_shapes=[pltpu.VMEM((128, H_A * TQ), BF16),
                        pltpu.VMEM((D_IDX, H_IDX * TQ), BF16),
                        pltpu.VMEM((l, TQ), I32),
                        pltpu.VMEM((l, TQ), F32),
                        pltpu.VMEM((1, TQ), I32),
                        pltpu.VMEM((1, H_A * TQ), F32),
                        pltpu.VMEM((KVH_A, HD + 16, (H_A // KVH_A) * TQ), F32)],
        compiler_params=_cparams("arbitrary", "arbitrary"),
        name="attn_dsa_prompt",
    )(aqT, aqiT, awiT, akb, akib, avT, bias_pt)


def _lambda_value(lam_ref, lam_init):
    lv = lam_ref[...]
    s1 = jnp.sum(lv[0:1] * lv[1:2], axis=1, keepdims=True)
    s2 = jnp.sum(lv[2:3] * lv[3:4], axis=1, keepdims=True)
    return jnp.exp(s1) - jnp.exp(s2) + lam_init


def _attn_b_kernel(qT_ref, k_ref, vT_ref, bias_ref, lam_ref, gsub_ref, oT_ref,
                   qpad, m_scr, acc_scr, *, lam_init):
    b = pl.program_id(0)
    qi = pl.program_id(1)
    grp = H_B // KVH_B

    @pl.when((b == 0) & (qi == 0))
    def _():
        qpad[...] = jnp.zeros_like(qpad)
    for h in range(H_B):
        for c in range(2):
            slot = (h // grp) * 2 + c
            hc = h * 2 + c
            qpad[slot * HD:(slot + 1) * HD, hc * TQ:(hc + 1) * TQ] = qT_ref[0, hc * HD:(hc + 1) * HD, :]

    m_scr[...] = jnp.full(m_scr.shape, NEG, F32)
    acc_scr[...] = jnp.zeros_like(acc_scr)

    def chunk(start, n, near):
        s = _dot(k_ref[0, _key_rows(start, n), :], qpad[...])
        if near:
            tile = lambda kind: jnp.concatenate([bias_ref[hc // 2, kind] for hc in range(2 * H_B)], axis=1)
            s = _add_near(s, n, tile(1), tile(0))
        _online_update(s, m_scr, acc_scr, [_v_aug(vT_ref, start, n, g, 2 * HD) for g in range(KVH_B)])

    _chunk_loop(qi, chunk)

    lam = _lambda_value(lam_ref, lam_init)
    gs = gsub_ref[...]
    for h in range(H_B):
        g, hh = h // grp, h % grp
        a1 = acc_scr[g][:, (hh * 2) * TQ:(hh * 2 + 1) * TQ]
        a2 = acc_scr[g][:, (hh * 2 + 1) * TQ:(hh * 2 + 2) * TQ]
        o = a1[0:2 * HD] / a1[2 * HD:2 * HD + 1] - lam * (a2[0:2 * HD] / a2[2 * HD:2 * HD + 1])
        ms = jnp.mean(o * o, axis=0, keepdims=True)
        oT_ref[0, h * 2 * HD:(h + 1) * 2 * HD, :] = (o * lax.rsqrt(ms + EPS) * gs) * (1.0 - lam_init)


def _attn_b(bqT, bkb, bvT, bias_pt, lamvec, gsub_col, lam_init):
    b, _, l = bqT.shape
    blk_t = lambda r: pl.BlockSpec((1, r, TQ), lambda i, j: (i, 0, j))
    full = lambda s1, s2: pl.BlockSpec((1, s1, s2), lambda i, j: (i, 0, 0))
    return pl.pallas_call(
        functools.partial(_attn_b_kernel, lam_init=lam_init),
        grid=(b, l // TQ),
        in_specs=[blk_t(512), full(l, 256), full(256, l),
                  pl.BlockSpec((H_B, 2, TKC, TQ), lambda i, j: (0, 0, 0, 0)),
                  pl.BlockSpec((4, HD), lambda i, j: (0, 0)),
                  pl.BlockSpec((2 * HD, 1), lambda i, j: (0, 0))],
        out_specs=blk_t(512),
        out_shape=jax.ShapeDtypeStruct((b, 512, l), F32),
        scratch_shapes=[pltpu.VMEM((256, 2 * H_B * TQ), BF16),
                        pltpu.VMEM((1, 2 * H_B * TQ), F32),
                        pltpu.VMEM((KVH_B, 2 * HD + 16, 2 * (H_B // KVH_B) * TQ), F32)],
        compiler_params=_cparams("arbitrary", "arbitrary"),
        name="attn_diff_prompt",
    )(bqT, bkb, bvT, bias_pt, lamvec, gsub_col)


def _attn_c_kernel(qT_ref, k_ref, vT_ref, oT_ref, qpad, m_scr, acc_scr):
    b = pl.program_id(0)
    qi = pl.program_id(1)
    grp = H_C // KVH_C

    @pl.when((b == 0) & (qi == 0))
    def _():
        r = lax.broadcasted_iota(I32, (256, H_C * TQ), 0) - 128
        cblk = lax.broadcasted_iota(I32, (256, H_C * TQ), 1) // TQ
        ones = (r >= 0) & (r < 24) & ((r % 8) == cblk)
        qpad[...] = jnp.where(ones, 1.0, 0.0).astype(BF16)
    for h in range(H_C):
        g = h // grp
        qpad[g * HD:(g + 1) * HD, h * TQ:(h + 1) * TQ] = qT_ref[0, h * HD:(h + 1) * HD, :]

    m_scr[...] = jnp.full(m_scr.shape, NEG, F32)
    acc_scr[...] = jnp.zeros_like(acc_scr)
    row = lax.broadcasted_iota(I32, (TKC, TQ), 0)
    col = lax.broadcasted_iota(I32, (TKC, TQ), 1)
    causal = jnp.where(row > col, NEG, 0.0)

    def chunk(start, n, near):
        s = _dot(k_ref[0, _key_rows(start, n), :], qpad[...])
        if near:
            s = _add_near(s, n, None, jnp.concatenate([causal] * H_C, axis=1))
        _online_update(s, m_scr, acc_scr, [_v_aug(vT_ref, start, n, g, HD) for g in range(KVH_C)])

    _chunk_loop(qi, chunk)

    for h in range(H_C):
        g, hh = h // grp, h % grp
        a = acc_scr[g][:, hh * TQ:(hh + 1) * TQ]
        oT_ref[0, h * HD:(h + 1) * HD, :] = a[0:HD] / a[HD:HD + 1]


def _attn_c(cqT, kcat, cvT):
    b, _, l = cqT.shape
    blk_t = lambda r: pl.BlockSpec((1, r, TQ), lambda i, j: (i, 0, j))
    full = lambda s1, s2: pl.BlockSpec((1, s1, s2), lambda i, j: (i, 0, 0))
    return pl.pallas_call(
        _attn_c_kernel,
        grid=(b, l // TQ),
        in_specs=[blk_t(384), full(l, 256), full(128, l)],
        out_specs=blk_t(384),
        out_shape=jax.ShapeDtypeStruct((b, 384, l), F32),
        scratch_shapes=[pltpu.VMEM((256, H_C * TQ), BF16),
                        pltpu.VMEM((1, H_C * TQ), F32),
                        pltpu.VMEM((KVH_C, HD + 16, (H_C // KVH_C) * TQ), F32)],
        compiler_params=_cparams("arbitrary", "arbitrary"),
        name="attn_forget_prompt",
    )(cqT, kcat, cvT)


G_DEC = 4


def _stack_heads(ref, tok, n, width):
    return jnp.concatenate([ref[tok, h * width:(h + 1) * width] for h in range(n)], axis=0)


def _pad_new(x):
    return jnp.concatenate([x, jnp.zeros((PAGE - x.shape[0], x.shape[1]), x.dtype)], axis=0)


def _probabilities(s_pages):
    mx = s_pages[0]
    for s in s_pages[1:]:
        mx = jnp.maximum(mx, s)
    m = jnp.max(mx, axis=1, keepdims=True)
    p_pages = [jnp.exp(s - m) for s in s_pages]
    lsum = p_pages[0]
    for p in p_pages[1:]:
        lsum = lsum + p
    return p_pages, jnp.sum(lsum, axis=1, keepdims=True)


def _pv(p_pages, vt_pages, v_new):
    acc = _dot(p_pages[-1].astype(BF16), v_new)
    for p, vt in zip(p_pages[:-1], vt_pages):
        acc = acc + _dot_nt(p.astype(BF16), vt)
    return acc


def _page_specs(cache, layer, gsz, n_pages, block, index):
    specs = []
    for g in range(gsz):
        for p in range(n_pages):
            specs.append(pl.BlockSpec(
                (1,) + block,
                functools.partial(lambda i, pt, g_, p_: (layer,) + index(pt[i * gsz + g_, p_]), g_=g, p_=p)))
    return specs, [cache] * (gsz * n_pages)


def _decode_a_kernel(pt_ref, qa_ref, qi_ref, wi_ref, ak_n, av_n, aki_n, bias_ref, *rest, gsz, n_pages, k_top):
    npg = gsz * n_pages
    kt_pages, vt_pages, kit_pages = rest[0:npg], rest[npg:2 * npg], rest[2 * npg:3 * npg]
    oa_ref = rest[3 * npg]
    r = qa_ref.shape[0] // gsz
    rows = gsz * r
    n_all = n_pages + 1
    lane = lax.broadcasted_iota(I32, (rows, PAGE), 1)
    qrow = lax.rem(lax.broadcasted_iota(I32, (rows, PAGE), 0), r)
    new_visible = lane <= qrow

    sc_pages = [[] for _ in range(n_all)]
    for g in range(gsz):
        tok = slice(g * r, (g + 1) * r)
        qi2 = _stack_heads(qi_ref, tok, H_IDX, 2 * D_IDX)[:, 0:D_IDX]
        wcol = _stack_heads(wi_ref, tok, H_IDX, LANES) * IDX_W_SCALE
        for p in range(n_all):
            if p < n_pages:
                z = _dot(qi2, kit_pages[g * n_pages + p][0, 0].astype(BF16))
            else:
                z = _dot_nt(qi2, _pad_new(aki_n[tok, :]).astype(BF16))
            z = jnp.maximum(z, 0.0) * wcol
            sc = z[0:r]
            for h in range(1, H_IDX):
                sc = sc + z[h * r:(h + 1) * r]
            sc_pages[p].append(sc)
    key_pages = []
    for p in range(n_all):
        sc = jnp.concatenate(sc_pages[p], axis=0)
        sc = jnp.where(sc == 0.0, 0.0, sc)
        if p == n_pages:
            sc = jnp.where(new_visible, sc, -jnp.inf)
        key_pages.append(_sortable_key(sc))

    def count(pred):
        tot = None
        for p, k in enumerate(key_pages):
            hit = jnp.where(pred(k, p), 1.0, 0.0)
            tot = hit if tot is None else tot + hit
        return jnp.sum(tot, axis=1, keepdims=True)

    thr = jnp.full((rows, 1), INT_MIN, I32)
    for it in range(32):
        cand = thr ^ jnp.int32(-2 ** 31 if it == 0 else 1 << (31 - it))
        cnt = count(lambda k, p: k >= cand)
        thr = jnp.where(cnt >= k_top, cand, thr)
    cnt_gt = count(lambda k, p: k > thr)
    cnt_eq = count(lambda k, p: k == thr)
    need = k_top - cnt_gt
    n_bits = max(1, (n_all * PAGE - 1).bit_length())

    def tie_search():
        jv = jnp.zeros((rows, 1), I32)
        for it in range(n_bits):
            cand = jv | jnp.int32(1 << (n_bits - 1 - it))
            cnt = count(lambda k, p: (k == thr) & ((lane + p * PAGE) < cand))
            jv = jnp.where(cnt < need, cand, jv)
        return jv

    any_excess = jnp.max(jnp.where(cnt_eq > need, 1, 0)) > 0
    jv = lax.cond(any_excess, tie_search, lambda: jnp.full((rows, 1), n_all * PAGE, I32))
    am_pages = []
    for p, k in enumerate(key_pages):
        sel = (k > thr) | ((k == thr) & ((lane + p * PAGE) <= jv))
        am_pages.append(jnp.where(sel, 0.0, NEG))

    bias_last = jnp.concatenate([bias_ref[h, 0] for h in range(H_A)], axis=0)
    bias_new = jnp.concatenate([bias_ref[h, 1] for h in range(H_A)], axis=0)
    for g in range(gsz):
        tok = slice(g * r, (g + 1) * r)
        qa2 = _stack_heads(qa_ref, tok, H_A, 2 * HD)
        s_pages = []
        for p in range(n_all):
            if p < n_pages:
                s = _dot(qa2, kt_pages[g * n_pages + p][0, 0].reshape(2 * HD, PAGE).astype(BF16))
            else:
                s = _dot_nt(qa2, _pad_new(ak_n[tok, :]).astype(BF16))
            s = s + jnp.concatenate([am_pages[p][tok]] * H_A, axis=0)
            if p == n_pages - 1:
                s = s + bias_last
            if p == n_pages:
                s = s + bias_new
            s_pages.append(s)
        p_pages, lsum = _probabilities(s_pages)
        vts = [vt_pages[g * n_pages + p][0, 0].reshape(2 * HD, PAGE).astype(BF16) for p in range(n_pages)]
        o = _pv(p_pages, vts, _pad_new(av_n[tok, :]).astype(BF16)) / lsum
        oa_ref[tok, :] = jnp.concatenate([o[h * r:(h + 1) * r] for h in range(H_A)], axis=1)


def _decode_a(page_table, layer, qa, qi, wi, ak_n, av_n, aki_n, kt, vt, kit, bias_dec, r):
    nb, n_pages = page_table.shape
    gsz = G_DEC
    k_top = min(K_TOP_MAX, (n_pages * PAGE + r) // 4)
    tok = lambda w: pl.BlockSpec((gsz * r, w), lambda i, pt: (i, 0))
    in_specs = [tok(768), tok(1024), tok(1024), tok(128), tok(128), tok(64),
                pl.BlockSpec((H_A, 2, r, LANES), lambda i, pt: (0, 0, 0, 0))]
    args = []
    for cache, block in ((kt, (1, KVH_A, HD, PAGE)), (vt, (1, KVH_A, HD, PAGE))):
        s, a = _page_specs(cache, layer, gsz, n_pages, block, lambda pg: (pg, 0, 0, 0))
        in_specs += s
        args += a
    s, a = _page_specs(kit, layer, gsz, n_pages, (1, D_IDX, PAGE), lambda pg: (pg, 0, 0))
    in_specs += s
    args += a
    grid_spec = pltpu.PrefetchScalarGridSpec(
        num_scalar_prefetch=1, grid=(nb // gsz,), in_specs=in_specs,
        out_specs=pl.BlockSpec((gsz * r, 768), lambda i, pt: (i, 0)))
    return pl.pallas_call(
        functools.partial(_decode_a_kernel, gsz=gsz, n_pages=n_pages, k_top=k_top),
        grid_spec=grid_spec,
        out_shape=jax.ShapeDtypeStruct((nb * r, 768), F32),
        compiler_params=_cparams("arbitrary"),
        name="decode_dsa",
    )(page_table, qa, qi, wi, ak_n, av_n, aki_n, bias_dec[:H_A], *args)


def _decode_b_kernel(pt_ref, qb_ref, bk_n, bv_n, bias_ref, lam_ref, gsub_ref, *rest, gsz, n_pages, lam_init):
    npg = gsz * n_pages
    kt_pages, v_pages = rest[0:npg], rest[npg:2 * npg]
    ob_ref = rest[2 * npg]
    r = qb_ref.shape[0] // gsz
    n_all = n_pages + 1
    grp = H_B // KVH_B
    half = 2 * grp * r
    bias_last = jnp.concatenate([bias_ref[h, 0] for h in range(H_B) for _ in range(2)], axis=0)
    bias_new = jnp.concatenate([bias_ref[h, 1] for h in range(H_B) for _ in range(2)], axis=0)
    lam = _lambda_value(lam_ref, lam_init)
    gs = gsub_ref[...]
    for g in range(gsz):
        tok = slice(g * r, (g + 1) * r)
        qb2 = _stack_heads(qb_ref, tok, 2 * H_B, 4 * HD)
        s_pages = []
        for p in range(n_all):
            if p < n_pages:
                s = _dot(qb2, kt_pages[g * n_pages + p][0, 0].reshape(4 * HD, PAGE).astype(BF16))
            else:
                s = _dot_nt(qb2, _pad_new(bk_n[tok, :]).astype(BF16))
            if p == n_pages - 1:
                s = s + bias_last
            if p == n_pages:
                s = s + bias_new
            s_pages.append(s)
        p_pages, lsum = _probabilities(s_pages)
        v_new = _pad_new(bv_n[tok, :]).astype(BF16)
        outs = []
        for kv in range(KVH_B):
            rs = slice(kv * half, (kv + 1) * half)
            acc = _dot(p_pages[-1][rs].astype(BF16), v_new[:, kv * 2 * HD:(kv + 1) * 2 * HD])
            for p in range(n_pages):
                v = v_pages[g * n_pages + p][0, 0, pl.ds(kv, PAGE, stride=KVH_B), :].astype(BF16)
                acc = acc + _dot(p_pages[p][rs].astype(BF16), v)
            o = acc / lsum[rs]
            for hh in range(grp):
                od = o[(2 * hh) * r:(2 * hh + 1) * r] - lam * o[(2 * hh + 1) * r:(2 * hh + 2) * r]
                ms = jnp.mean(od * od, axis=1, keepdims=True)
                outs.append((od * lax.rsqrt(ms + EPS) * gs) * (1.0 - lam_init))
        ob_ref[tok, :] = jnp.concatenate(outs, axis=1)


def _decode_b(page_table, layer, qb, bk_n, bv_n, kt, v2, bias_dec, lamvec, gsub_row, lam_init, r):
    nb, n_pages = page_table.shape
    gsz = G_DEC
    tok = lambda w: pl.BlockSpec((gsz * r, w), lambda i, pt: (i, 0))
    in_specs = [tok(2048), tok(256), tok(256),
                pl.BlockSpec((H_B, 2, r, LANES), lambda i, pt: (0, 0, 0, 0)),
                pl.BlockSpec((4, HD), lambda i, pt: (0, 0)),
                pl.BlockSpec((1, 2 * HD), lambda i, pt: (0, 0))]
    s1, a1 = _page_specs(kt, layer, gsz, n_pages, (1, KVH_B, 2, HD, PAGE), lambda pg: (pg, 0, 0, 0, 0))
    s2, a2 = _page_specs(v2, layer, gsz, n_pages, (1, KVH_B * PAGE, 2 * HD), lambda pg: (pg, 0, 0))
    grid_spec = pltpu.PrefetchScalarGridSpec(
        num_scalar_prefetch=1, grid=(nb // gsz,), in_specs=in_specs + s1 + s2,
        out_specs=pl.BlockSpec((gsz * r, 512), lambda i, pt: (i, 0)))
    return pl.pallas_call(
        functools.partial(_decode_b_kernel, gsz=gsz, n_pages=n_pages, lam_init=lam_init),
        grid_spec=grid_spec,
        out_shape=jax.ShapeDtypeStruct((nb * r, 512), F32),
        compiler_params=_cparams("arbitrary"),
        name="decode_diff",
    )(page_table, qb, bk_n, bv_n, bias_dec[H_A:], lamvec, gsub_row, *a1, *a2)


def _decode_c_kernel(pt_ref, qc_ref, ck_n, cv_n, lft_n, *rest, gsz, n_pages):
    npg = gsz * n_pages
    kt_pages, vt_pages, lf_pages = rest[0:npg], rest[npg:2 * npg], rest[2 * npg:3 * npg]
    oc_ref = rest[3 * npg]
    i = pl.program_id(0)
    r = qc_ref.shape[0] // gsz
    n_all = n_pages + 1
    lane = lax.broadcasted_iota(I32, (r, PAGE), 1)
    qrow = lax.broadcasted_iota(I32, (r, PAGE), 0)
    causal_new = jnp.concatenate([jnp.where(lane <= qrow, 0.0, NEG)] * H_C, axis=0)
    ri = lax.broadcasted_iota(I32, (PAGE, PAGE), 0)
    ci = lax.broadcasted_iota(I32, (PAGE, PAGE), 1)
    upper = jnp.where(ri <= ci, 1.0, 0.0).astype(BF16)
    ones = jnp.ones((PAGE, PAGE), BF16)
    nr = 8 * n_all
    rr = lax.broadcasted_iota(I32, (nr, nr), 0)
    cc = lax.broadcasted_iota(I32, (nr, nr), 1)
    prev_pages = jnp.where(((rr % 8) == (cc % 8)) & ((cc // 8) < (rr // 8)), 1.0, 0.0).astype(BF16)
    zrow = jnp.zeros((8 - H_C, PAGE), F32)
    for g in range(gsz):
        tok = slice(g * r, (g + 1) * r)
        xs = []
        for p in range(n_pages):
            sub = lax.rem(pt_ref[i * gsz + g, p], 8)
            xs += [lf_pages[g * n_pages + p][0, h, pl.ds(sub, 1), :] for h in range(H_C)] + [zrow]
        x = jnp.concatenate(xs + [lft_n[g]], axis=0)
        cum = _dot3_lhs(x, upper) + _dot3_rhs(prev_pages, _dot3_lhs(x, ones))
        qc2 = _stack_heads(qc_ref, tok, H_C, 2 * HD)
        s_pages = []
        for p in range(n_all):
            if p < n_pages:
                s = _dot(qc2, kt_pages[g * n_pages + p][0, 0].reshape(2 * HD, PAGE).astype(BF16))
            else:
                s = _dot_nt(qc2, _pad_new(ck_n[tok, :]).astype(BF16)) + causal_new
            decay = jnp.concatenate(
                [jnp.broadcast_to(cum[p * 8 + h:p * 8 + h + 1, :], (r, PAGE)) for h in range(H_C)], axis=0)
            s_pages.append(s - decay)
        p_pages, lsum = _probabilities(s_pages)
        vts = [vt_pages[g * n_pages + p][0, 0].reshape(2 * HD, PAGE).astype(BF16) for p in range(n_pages)]
        o = _pv(p_pages, vts, _pad_new(cv_n[tok, :]).astype(BF16)) / lsum
        oc_ref[tok, :] = jnp.concatenate([o[h * r:(h + 1) * r] for h in range(H_C)], axis=1)


def _decode_c(page_table, layer, qc, ck_n, cv_n, lft_new, kt, vt, lft, r):
    nb, n_pages = page_table.shape
    gsz = G_DEC
    tok = lambda w: pl.BlockSpec((gsz * r, w), lambda i, pt: (i, 0))
    in_specs = [tok(768), tok(128), tok(128), pl.BlockSpec((gsz, 8, LANES), lambda i, pt: (i, 0, 0))]
    args = []
    for cache in (kt, vt):
        s, a = _page_specs(cache, layer, gsz, n_pages, (1, KVH_C, HD, PAGE), lambda pg: (pg, 0, 0, 0))
        in_specs += s
        args += a
    specs = []
    for g in range(gsz):
        for p in range(n_pages):
            specs.append(pl.BlockSpec(
                (1, H_C, 8, PAGE),
                functools.partial(lambda i, pt, g_, p_: (layer, 0, pt[i * gsz + g_, p_] // 8, 0), g_=g, p_=p)))
    in_specs += specs
    args += [lft] * (gsz * n_pages)
    grid_spec = pltpu.PrefetchScalarGridSpec(
        num_scalar_prefetch=1, grid=(nb // gsz,), in_specs=in_specs,
        out_specs=pl.BlockSpec((gsz * r, 768), lambda i, pt: (i, 0)))
    return pl.pallas_call(
        functools.partial(_decode_c_kernel, gsz=gsz, n_pages=n_pages),
        grid_spec=grid_spec,
        out_shape=jax.ShapeDtypeStruct((nb * r, 768), F32),
        compiler_params=_cparams("arbitrary"),
        name="decode_forget",
    )(page_table, qc, ck_n, cv_n, lft_new, *args)


def _merge_kernel(x_ref, mod_ref, g_ref, oa_ref, ob_ref, oc_ref, wg_ref, wpa_ref, wpb_ref, wpc_ref, wo_ref,
                  o_ref, *, transposed):
    x = x_ref[...]
    gsz, r, _ = x.shape
    mod = mod_ref[...]
    h = _norm_mod(x, g_ref[...], mod[:, :, D:2 * D], mod[:, :, 0:D]).reshape(gsz * r, D).astype(BF16)
    gates = _sigmoid(_dot(h, wg_ref[...]))
    if transposed:
        oa, ob, oc = oa_ref[0].T, ob_ref[0].T, oc_ref[0].T
    else:
        oa, ob, oc = oa_ref[...], ob_ref[...], oc_ref[...]
    merged = (gates[:, 0:D] * _dot(oa.astype(BF16), wpa_ref[...])
              + gates[:, D:2 * D] * _dot(ob.astype(BF16), wpb_ref[...])
              + gates[:, 2 * D:3 * D] * _dot(oc.astype(BF16), wpc_ref[...]))
    y = _dot(merged.astype(BF16), wo_ref[...]).reshape(gsz, r, D)
    o_ref[...] = x + mod[:, :, 2 * D:3 * D] * y


def _merge(x, mod_a, g, oa, ob, oc, wg, wpa, wpb, wpc, wo, gsz, r, transposed):
    nb, rr, _ = x.shape
    const = lambda a: pl.BlockSpec(a.shape, lambda i, j: (0,) * a.ndim)
    if transposed:
        grid = (nb, rr // r)
        x_spec = pl.BlockSpec((1, r, D), lambda i, j: (i, j, 0))
        mod_spec = pl.BlockSpec((1, 1, 3 * D), lambda i, j: (i, 0, 0))
        o_spec = lambda a: pl.BlockSpec((1, a.shape[1], r), lambda i, j: (i, 0, j))
    else:
        grid = (nb // gsz, 1)
        x_spec = pl.BlockSpec((gsz, rr, D), lambda i, j: (i, 0, 0))
        mod_spec = pl.BlockSpec((gsz, 1, 3 * D), lambda i, j: (i, 0, 0))
        o_spec = lambda a: pl.BlockSpec((gsz * rr, a.shape[1]), lambda i, j: (i, 0))
    return pl.pallas_call(
        functools.partial(_merge_kernel, transposed=transposed),
        grid=grid,
        in_specs=[x_spec, mod_spec, const(g), o_spec(oa), o_spec(ob), o_spec(oc),
                  const(wg), const(wpa), const(wpb), const(wpc), const(wo)],
        out_specs=x_spec,
        out_shape=jax.ShapeDtypeStruct(x.shape, F32),
        compiler_params=_cparams("arbitrary", "arbitrary"),
        name="merge_prompt" if transposed else "merge_sample",
    )(x, mod_a, g, oa, ob, oc, wg, wpa, wpb, wpc, wo)


def _ffn_kernel(x_ref, mod_ref, g_ref, w1_ref, w2_ref, gf_ref, o_ref, *, final):
    x = x_ref[...]
    gsz, r, _ = x.shape
    mod = mod_ref[...]
    h = _norm_mod(x, g_ref[...], mod[:, :, D:2 * D], mod[:, :, 0:D]).reshape(gsz * r, D).astype(BF16)
    u = jnp.maximum(_dot(h, w1_ref[...]), 0.0)
    y = _dot((u * u).astype(BF16), w2_ref[...]).reshape(gsz, r, D)
    x2 = x + mod[:, :, 2 * D:3 * D] * y
    if final:
        ms = jnp.mean(x2 * x2, axis=-1, keepdims=True)
        x2 = x2 * lax.rsqrt(ms + EPS) * gf_ref[...]
    o_ref[...] = x2


def _ffn(x, mod_b, g, w1, w2, g_final, gsz, r, final, name):
    nb, rr, _ = x.shape
    const = lambda a: pl.BlockSpec(a.shape, lambda i, j: (0,) * a.ndim)
    if gsz == 1:
        grid = (nb, rr // r)
        x_spec = pl.BlockSpec((1, r, D), lambda i, j: (i, j, 0))
        mod_spec = pl.BlockSpec((1, 1, 3 * D), lambda i, j: (i, 0, 0))
    else:
        grid = (nb // gsz, 1)
        x_spec = pl.BlockSpec((gsz, rr, D), lambda i, j: (i, 0, 0))
        mod_spec = pl.BlockSpec((gsz, 1, 3 * D), lambda i, j: (i, 0, 0))
    return pl.pallas_call(
        functools.partial(_ffn_kernel, final=final),
        grid=grid,
        in_specs=[x_spec, mod_spec, const(g), const(w1), const(w2), const(g_final)],
        out_specs=x_spec,
        out_shape=jax.ShapeDtypeStruct(x.shape, F32),
        compiler_params=_cparams("arbitrary", "arbitrary"),
        name=name,
    )(x, mod_b, g, w1, w2, g_final)


TM_PROMPT = 512
G_SAMPLE = 16


def kernel(x_prompt, x_sample, c_prompt, c_sample, cache_a_k, cache_a_v, cache_a_kidx, cache_b_k, cache_b_v, cache_c_k, cache_c_v, cache_c_logf, page_table, t5_table, w_ada, b_ada, g_mix, g_ffn, w_in, b_forget, lam_q1, lam_k1, lam_q2, lam_k2, g_subln, w_gate, w_pa, w_pb, w_pc, w_out, w_ff1, w_ff2, g_final):
    depth = w_in.shape[0]
    nbp, seq, _ = x_prompt.shape
    nbs, dec_seq, _ = x_sample.shape
    n_pool, page = cache_a_k.shape[1], cache_a_k.shape[2]
    n_pages = page_table.shape[1]
    past_len = n_pages * page
    assert page == PAGE and seq % max(TM_PROMPT, _CUM_T, FAR_BLOCKS * TKC) == 0 and dec_seq == 8
    assert nbs % G_SAMPLE == 0 and nbs % G_DEC == 0 and n_pool % 8 == 0
    tm = min(TM_PROMPT, seq)

    nc = nbp + nbs
    mod = _ada(jnp.concatenate([c_prompt, c_sample], axis=0), w_ada, b_ada)
    mod = mod.reshape(depth, nc, 1, 6 * D)
    bias_pt, bias_dec = _bias_tiles(t5_table, past_len, dec_seq)

    kv_t = lambda c: jnp.transpose(c, (0, 1, 3, 4, 2))
    a_kt, a_vt, c_kt, c_vt = kv_t(cache_a_k), kv_t(cache_a_v), kv_t(cache_c_k), kv_t(cache_c_v)
    a_kit = jnp.transpose(cache_a_kidx, (0, 1, 3, 2))
    b_kt = jnp.transpose(cache_b_k, (0, 1, 3, 4, 5, 2))
    b_v2 = cache_b_v.reshape(depth, n_pool, page * KVH_B, 2 * HD)
    c_lft = jnp.transpose(cache_c_logf, (0, 3, 1, 2))

    xp, xs = x_prompt, x_sample
    rows_p, rows_s = [], []
    g_final2 = g_final.reshape(1, D)
    for l in range(depth):
        lam_init = 0.8 - 0.6 * math.exp(-0.3 * l)
        wr, wt = _prep_in_prompt(w_in[l])
        ws = _prep_in_sample(w_in[l])
        bfb = _cf_block(b_forget[l].reshape(1, H_C))
        wg = w_gate[l].astype(BF16)
        wpa, wpb, wpc = w_pa[l].astype(BF16), w_pb[l].astype(BF16), w_pc[l].astype(BF16)
        wpa_s = _pad_proj_rows(w_pa[l], H_A, lambda h: h // (H_A // KVH_A)).astype(BF16)
        wpc_s = _pad_proj_rows(w_pc[l], H_C, lambda h: h // (H_C // KVH_C)).astype(BF16)
        wo = w_out[l].astype(BF16)
        w1, w2 = w_ff1[l].astype(BF16), w_ff2[l].astype(BF16)
        gm, gf = g_mix[l].reshape(1, D), g_ffn[l].reshape(1, D)
        lamvec = jnp.stack([lam_q1[l], lam_k1[l], lam_q2[l], lam_k2[l]])
        mod_pa, mod_pb = mod[l, :nbp, :, :3 * D], mod[l, :nbp, :, 3 * D:]
        mod_sa, mod_sb = mod[l, nbp:, :, :3 * D], mod[l, nbp:, :, 3 * D:]
        last = l == depth - 1

        (bv, lf, akb, akib, bkb, ckb, lfrep, akT, avTf, akiT, bkT, ckT, cvTf, awiT,
         aqT, aqiT, bqT, cqT, avT, bvT, cvT) = _inproj_prompt(xp, mod_pa, gm, wr, wt, bfb, tm)
        rows_p.append((akT, avTf, akiT, bkT, bv, ckT, cvTf, lf))
        kcat = _cum_prompt(lfrep, ckb)
        oaT = _attn_a(aqT, aqiT, awiT, akb, akib, avT, bias_pt[:H_A])
        obT = _attn_b(bqT, bkb, bvT, bias_pt[H_A:], lamvec, g_subln[l].reshape(2 * HD, 1), lam_init)
        ocT = _attn_c(cqT, kcat, cvT)
        x1 = _merge(xp, mod_pa, gm, oaT, obT, ocT, wg, wpa, wpb, wpc, wo, 1, tm, True)
        xp = _ffn(x1, mod_pb, gf, w1, w2, g_final2, 1, tm, last, "ffn_prompt")

        (sak, sav, saki, sbk, sbv, sck, scv, slf, qa, qi, wi, qb, qc, slfrep) = _inproj_sample(
            xs, mod_sa, gm, ws, bfb, G_SAMPLE)
        rows_s.append((sak, sav, saki, sbk, sbv, sck, scv, slf))
        lft_new = jnp.swapaxes(slfrep.reshape(nbs, dec_seq, LANES)[:, :, 0:8], 1, 2)
        lft_new = jnp.concatenate([lft_new, jnp.zeros((nbs, 8, LANES - dec_seq), F32)], axis=2)
        lane_head = jnp.arange(8)[None, :, None] < H_C
        lft_new = jnp.where(lane_head, lft_new, 0.0)
        oa = _decode_a(page_table, l, qa, qi, wi, sak, sav, saki, a_kt, a_vt, a_kit, bias_dec, dec_seq)
        ob = _decode_b(page_table, l, qb, sbk, sbv, b_kt, b_v2, bias_dec, lamvec, g_subln[l].reshape(1, 2 * HD),
                       lam_init, dec_seq)
        oc = _decode_c(page_table, l, qc, sck, scv, lft_new, c_kt, c_vt, c_lft, dec_seq)
        x1s = _merge(xs, mod_sa, gm, oa, ob, oc, wg, wpa_s, wpb, wpc_s, wo, G_SAMPLE, dec_seq, False)
        xs = _ffn(x1s, mod_sb, gf, w1, w2, g_final2, G_SAMPLE, dec_seq, last, "ffn_sample")

    def stack(rows, i):
        return jnp.stack([r[i] for r in rows])

    def stack_sample(i, shape):
        return stack(rows_s, i).reshape((depth, nbs, dec_seq) + shape)

    def stack_prompt_t(i, shape):
        n = len(shape)
        y = stack(rows_p, i).reshape((depth, nbp) + shape + (seq,))
        return jnp.transpose(y, (0, 1, n + 2) + tuple(range(2, n + 2)))

    out_p = (stack_prompt_t(0, (KVH_A, HD)), stack_prompt_t(1, (KVH_A, HD)), stack_prompt_t(2, (D_IDX,)),
             stack_prompt_t(3, (KVH_B, 2, HD)), stack(rows_p, 4).reshape(depth, nbp, seq, KVH_B, 2 * HD),
             stack_prompt_t(5, (KVH_C, HD)), stack_prompt_t(6, (KVH_C, HD)), stack(rows_p, 7))
    out_s = (stack_sample(0, (KVH_A, HD)), stack_sample(1, (KVH_A, HD)), stack_sample(2, (D_IDX,)),
             stack_sample(3, (KVH_B, 2, HD)), stack_sample(4, (KVH_B, 2 * HD)), stack_sample(5, (KVH_C, HD)),
             stack_sample(6, (KVH_C, HD)), stack_sample(7, (H_C,)))
    return (xp, xs) + out_p + out_s
```

```python
import functools
import math

import numpy as np
import jax
import jax.numpy as jnp
from jax import lax
from jax.experimental import pallas as pl
from jax.experimental.pallas import tpu as pltpu

F32 = jnp.float32
BF16 = jnp.bfloat16
I32 = jnp.int32

D = 1024
HD = 64
H_A, KVH_A = 6, 2
H_IDX, D_IDX = 8, 64
K_TOP_MAX = 256
IDX_W_SCALE = (H_IDX ** -0.5) * (D_IDX ** -0.5)
H_B, KVH_B = 4, 2
H_C, KVH_C = 6, 2
N_BUCKETS, T5_MAX_EXACT, T5_MAX_DIST = 32, 16, 128
D_FF = 4 * D
EPS = 1e-6
N_HEADS_T5 = H_A + H_B
PAGE = 128

LANES = 128
TQ = 128
TKC = 128
FAR_BLOCKS = 4
NEG = -1e30
INT_MIN = -2 ** 31
VMEM_LIMIT = 56 * 1024 * 1024

_W = dict(aq=H_A * HD, ak=KVH_A * HD, av=KVH_A * HD, aqi=H_IDX * D_IDX, aki=D_IDX, awi=H_IDX,
          bq=H_B * 2 * HD, bk=KVH_B * 2 * HD, bv=KVH_B * 2 * HD,
          cq=H_C * HD, ck=KVH_C * HD, cv=KVH_C * HD, cf=H_C)
_OFF = {}
_o = 0
for _k, _v in _W.items():
    _OFF[_k] = _o
    _o += _v

_R_AK, _R_AV, _R_AKI, _R_BK, _R_BV, _R_CK, _R_CV, _R_CF = 0, 128, 256, 384, 640, 896, 1024, 1152
N_ROWS = 1280
_P_AK, _P_AKI, _P_BK, _P_CK, _P_CF, _P_BV = 0, 128, 256, 512, 640, 768
N_PROWS = 1024
_T_AQ, _T_AQI, _T_AWI, _T_BQ, _T_CQ, _T_AV, _T_BV, _T_CV = 0, 384, 896, 912, 1424, 1808, 1936, 2192
_T_AK, _T_AKI, _T_BK, _T_CK = 2320, 2448, 2512, 2768
N_T = 2896
_S_AQ, _S_AQI, _S_AWI, _S_BQ, _S_CQ = 0, 768, 1792, 2816, 4864
N_SQ = 5632


def _cparams(*sem):
    return pltpu.CompilerParams(dimension_semantics=sem, vmem_limit_bytes=VMEM_LIMIT)


def _dot(a, b):
    return jnp.dot(a, b, preferred_element_type=F32)


def _dot_nt(a, b):
    return lax.dot_general(a, b, (((1,), (1,)), ((), ())), preferred_element_type=F32)


def _split3(x):
    hi = x.astype(BF16)
    r1 = x - hi.astype(F32)
    mid = r1.astype(BF16)
    lo = (r1 - mid.astype(F32)).astype(BF16)
    return hi, mid, lo


def _dot3_rhs(a_bf16, x):
    hi, mid, lo = _split3(x)
    return _dot(a_bf16, hi) + _dot(a_bf16, mid) + _dot(a_bf16, lo)


def _dot3_lhs(x, b_bf16):
    hi, mid, lo = _split3(x)
    return _dot(hi, b_bf16) + _dot(mid, b_bf16) + _dot(lo, b_bf16)


def _norm_mod(x, g, sc, sh):
    ms = jnp.mean(x * x, axis=-1, keepdims=True)
    return (x * lax.rsqrt(ms + EPS) * g) * (1.0 + sc) + sh


def _log_sigmoid(z):
    return jnp.minimum(z, 0.0) - jnp.log1p(jnp.exp(-jnp.abs(z)))


def _sigmoid(z):
    return 1.0 / (1.0 + jnp.exp(-z))


def _sortable_key(x):
    bits = lax.bitcast_convert_type(x, I32)
    return bits ^ ((bits >> 31) & 0x7FFFFFFF)


def _ada_kernel(c_ref, w_ref, b_ref, o_ref):
    c = c_ref[...]
    s = (c * _sigmoid(c)).astype(BF16)
    o_ref[0] = _dot(s, w_ref[0].astype(BF16)) + b_ref[0]


def _ada(c_all, w_ada, b_ada):
    depth = w_ada.shape[0]
    nc = c_all.shape[0]
    nt = 6
    return pl.pallas_call(
        _ada_kernel,
        grid=(depth, nt),
        in_specs=[pl.BlockSpec((nc, D), lambda l, j: (0, 0)),
                  pl.BlockSpec((1, D, D), lambda l, j: (l, 0, j)),
                  pl.BlockSpec((1, 1, D), lambda l, j: (l, 0, j))],
        out_specs=pl.BlockSpec((1, nc, D), lambda l, j: (l, 0, j)),
        out_shape=jax.ShapeDtypeStruct((depth, nc, 6 * D), F32),
        compiler_params=_cparams("arbitrary", "arbitrary"),
        name="ada",
    )(c_all, w_ada, b_ada.reshape(depth, 1, 6 * D))


def _t5_bucket_np(rel):
    n = np.maximum(rel, 0)
    nf = np.maximum(n, 1).astype(np.float32)
    large = T5_MAX_EXACT + (np.log(nf / np.float32(T5_MAX_EXACT)) / np.float32(math.log(T5_MAX_DIST / T5_MAX_EXACT))
                            * np.float32(N_BUCKETS - T5_MAX_EXACT)).astype(np.int32)
    return np.where(n < T5_MAX_EXACT, n, np.minimum(large, N_BUCKETS - 1)).astype(np.int32)


def _bias_bucket_tables(past_len, dec_seq):
    s = np.arange(TKC)[:, None]
    t = np.arange(TQ)[None, :]
    tiles = []
    for d in (0, 1):
        rel = d * TKC + t - s
        tiles.append(np.where(rel >= 0, _t5_bucket_np(rel), -1))
    prompt = np.stack(tiles).astype(np.int32)
    i = np.arange(dec_seq)[:, None]
    lane = np.arange(PAGE)[None, :]
    rel_last = (past_len + i) - (past_len - PAGE + lane)
    rel_new = i - lane
    dl = _t5_bucket_np(rel_last)
    dn = np.where((rel_new >= 0) & (lane < dec_seq), _t5_bucket_np(rel_new), -1)
    dec = np.stack([dl, dn]).astype(np.int32)
    return prompt, dec


def _bias_kernel(tab_ref, pb_ref, db_ref, pt_ref, dt_ref):
    pb = pb_ref[...]
    db = db_ref[...]
    for h in range(N_HEADS_T5):
        def lut(bk):
            acc = jnp.zeros(bk.shape, F32)
            for b in range(N_BUCKETS):
                acc = jnp.where(bk == b, tab_ref[b, h], acc)
            return jnp.where(bk < 0, NEG, acc - tab_ref[N_BUCKETS - 1, h])
        pt_ref[h] = lut(pb)
        dt_ref[h] = lut(db)


def _bias_tiles(t5_table, past_len, dec_seq):
    pb, db = _bias_bucket_tables(past_len, dec_seq)
    return pl.pallas_call(
        _bias_kernel,
        in_specs=[pl.BlockSpec(memory_space=pltpu.SMEM),
                  pl.BlockSpec(memory_space=pltpu.VMEM),
                  pl.BlockSpec(memory_space=pltpu.VMEM)],
        out_specs=[pl.BlockSpec(memory_space=pltpu.VMEM), pl.BlockSpec(memory_space=pltpu.VMEM)],
        out_shape=[jax.ShapeDtypeStruct((N_HEADS_T5,) + pb.shape, F32),
                   jax.ShapeDtypeStruct((N_HEADS_T5,) + db.shape, F32)],
        name="t5_bias_tiles",
    )(t5_table, jnp.asarray(pb), jnp.asarray(db))


def _seg(w, name):
    return w[:, _OFF[name]:_OFF[name] + _W[name]]


def _cf_block(cf):
    n = cf.shape[0]
    z2 = jnp.zeros((n, 2), cf.dtype)
    return jnp.concatenate([cf, z2, cf, z2, cf, z2, jnp.zeros((n, LANES - 24), cf.dtype)], axis=1)


def _rows_block(w):
    z64 = jnp.zeros((D, 64), w.dtype)
    return jnp.concatenate([_seg(w, "ak"), _seg(w, "av"), _seg(w, "aki"), z64, _seg(w, "bk"), _seg(w, "bv"),
                            _seg(w, "ck"), _seg(w, "cv"), _cf_block(_seg(w, "cf"))], axis=1)


def _prep_in_prompt(w):
    scale = HD ** -0.5
    z8 = jnp.zeros((D, 8), w.dtype)
    z64 = jnp.zeros((D, 64), w.dtype)
    w_t = jnp.concatenate([_seg(w, "aq") * scale, _seg(w, "aqi"), _seg(w, "awi"), z8, _seg(w, "bq") * scale,
                           _seg(w, "cq") * scale, _seg(w, "av"), _seg(w, "bv"), _seg(w, "cv"),
                           _seg(w, "ak"), _seg(w, "aki"), _seg(w, "bk"), _seg(w, "ck")], axis=1)
    w_r = jnp.concatenate([_seg(w, "ak"), _seg(w, "aki"), z64, _seg(w, "bk"), _seg(w, "ck"),
                           _cf_block(_seg(w, "cf")), _seg(w, "bv")], axis=1)
    return w_r.astype(BF16), w_t.T.astype(BF16)


def _pad_heads(wq, n_heads, kv_of, width, n_slots):
    blocks = []
    for h in range(n_heads):
        blk = jnp.zeros((D, n_slots * HD), wq.dtype)
        s = kv_of(h)
        blk = blk.at[:, s * HD:(s + 1) * HD].set(wq[:, h * width:(h + 1) * width])
        blocks.append(blk)
    return jnp.concatenate(blocks, axis=1)


def _prep_in_sample(w):
    scale = HD ** -0.5
    aq = _pad_heads(_seg(w, "aq") * scale, H_A, lambda h: h // (H_A // KVH_A), HD, 2)
    aqi = _pad_heads(_seg(w, "aqi"), H_IDX, lambda h: 0, D_IDX, 2)
    awi = jnp.repeat(_seg(w, "awi"), LANES, axis=1)
    bq = _pad_heads(_seg(w, "bq") * scale, 2 * H_B, lambda hc: (hc // 2 // (H_B // KVH_B)) * 2 + hc % 2, HD, 4)
    cq = _pad_heads(_seg(w, "cq") * scale, H_C, lambda h: h // (H_C // KVH_C), HD, 2)
    return jnp.concatenate([aq, aqi, awi, bq, cq, _rows_block(w)], axis=1).astype(BF16)


def _pad_proj_rows(wp, n_heads, kv_of):
    blocks = []
    for h in range(n_heads):
        blk = jnp.zeros((2 * HD, D), wp.dtype)
        s = kv_of(h)
        blk = blk.at[s * HD:(s + 1) * HD].set(wp[h * HD:(h + 1) * HD])
        blocks.append(blk)
    return jnp.concatenate(blocks, axis=0)


def _store_rows(pr, bf, ak_o, av_o, aki_o, bk_o, bv_o, ck_o, cv_o, lf_o, idx):
    ak_o[idx] = pr[:, _R_AK:_R_AK + 128]
    av_o[idx] = pr[:, _R_AV:_R_AV + 128]
    aki_o[idx] = pr[:, _R_AKI:_R_AKI + 64]
    bk_o[idx] = pr[:, _R_BK:_R_BK + 256]
    bv_o[idx] = pr[:, _R_BV:_R_BV + 256]
    ck_o[idx] = pr[:, _R_CK:_R_CK + 128]
    cv_o[idx] = pr[:, _R_CV:_R_CV + 128]
    lf = _log_sigmoid(pr[:, _R_CF:_R_CF + 128] + bf)
    lf_o[idx] = lf[:, 0:H_C]
    return lf


def _inproj_p_kernel(x_ref, mod_ref, g_ref, wr_ref, wt_ref, bf_ref,
                     bv_o, lf_o, akb_o, akib_o, bkb_o, ckb_o, lfrep_o,
                     akT_o, avT_o, akiT_o, bkT_o, ckT_o, cvT_o, awiT_o,
                     aqT_o, aqiT_o, bqT_o, cqT_o, avTb_o, bvTb_o, cvTb_o):
    x = x_ref[0]
    mod = mod_ref[0]
    h = _norm_mod(x, g_ref[...], mod[:, D:2 * D], mod[:, 0:D])
    pr = _dot(h.astype(BF16), wr_ref[...])
    bv_o[0] = pr[:, _P_BV:_P_BV + 256]
    lf = _log_sigmoid(pr[:, _P_CF:_P_CF + 128] + bf_ref[...])
    lf_o[0] = lf[:, 0:H_C]
    lfrep_o[0] = lf
    akb_o[0] = pr[:, _P_AK:_P_AK + 128].astype(BF16)
    akib_o[0] = pr[:, _P_AKI:_P_AKI + 64].astype(BF16)
    bkb_o[0] = pr[:, _P_BK:_P_BK + 256].astype(BF16)
    ckb_o[0] = pr[:, _P_CK:_P_CK + 128].astype(BF16)
    pt = _dot(wt_ref[...], h.T.astype(BF16))
    akT_o[0] = pt[_T_AK:_T_AK + 128]
    avT_o[0] = pt[_T_AV:_T_AV + 128]
    akiT_o[0] = pt[_T_AKI:_T_AKI + 64]
    bkT_o[0] = pt[_T_BK:_T_BK + 256]
    ckT_o[0] = pt[_T_CK:_T_CK + 128]
    cvT_o[0] = pt[_T_CV:_T_CV + 128]
    awiT_o[0] = pt[_T_AWI:_T_AWI + 16]
    aqT_o[0] = pt[_T_AQ:_T_AQ + 384].astype(BF16)
    aqiT_o[0] = pt[_T_AQI:_T_AQI + 512].astype(BF16)
    bqT_o[0] = pt[_T_BQ:_T_BQ + 512].astype(BF16)
    cqT_o[0] = pt[_T_CQ:_T_CQ + 384].astype(BF16)
    avTb_o[0] = pt[_T_AV:_T_AV + 128].astype(BF16)
    bvTb_o[0] = pt[_T_BV:_T_BV + 256].astype(BF16)
    cvTb_o[0] = pt[_T_CV:_T_CV + 128].astype(BF16)


def _inproj_prompt(x, mod_a, g, wr, wt, bfb, tm):
    b, l, _ = x.shape
    row = lambda w, dt: jax.ShapeDtypeStruct((b, l, w), dt)
    tr = lambda r, dt: jax.ShapeDtypeStruct((b, r, l), dt)
    row_spec = lambda w: pl.BlockSpec((1, tm, w), lambda i, j: (i, j, 0))
    tr_spec = lambda r: pl.BlockSpec((1, r, tm), lambda i, j: (i, 0, j))
    rows = [(256, F32), (H_C, F32), (128, BF16), (64, BF16), (256, BF16), (128, BF16), (128, F32)]
    trs = [(128, F32), (128, F32), (64, F32), (256, F32), (128, F32), (128, F32), (16, F32),
           (384, BF16), (512, BF16), (512, BF16), (384, BF16), (128, BF16), (256, BF16), (128, BF16)]
    return pl.pallas_call(
        _inproj_p_kernel,
        grid=(b, l // tm),
        in_specs=[pl.BlockSpec((1, tm, D), lambda i, j: (i, j, 0)),
                  pl.BlockSpec((1, 1, 3 * D), lambda i, j: (i, 0, 0)),
                  pl.BlockSpec((1, D), lambda i, j: (0, 0)),
                  pl.BlockSpec((D, N_PROWS), lambda i, j: (0, 0)),
                  pl.BlockSpec((N_T, D), lambda i, j: (0, 0)),
                  pl.BlockSpec((1, LANES), lambda i, j: (0, 0))],
        out_specs=[row_spec(w) for w, _ in rows] + [tr_spec(r) for r, _ in trs],
        out_shape=[row(w, dt) for w, dt in rows] + [tr(r, dt) for r, dt in trs],
        compiler_params=_cparams("arbitrary", "arbitrary"),
        name="inproj_prompt",
    )(x, mod_a, g, wr, wt, bfb)


def _inproj_s_kernel(x_ref, mod_ref, g_ref, w_ref, bf_ref,
                     ak_o, av_o, aki_o, bk_o, bv_o, ck_o, cv_o, lf_o,
                     qa_o, qi_o, wi_o, qb_o, qc_o, lfrep_o):
    x = x_ref[...]
    gsz, r, _ = x.shape
    mod = mod_ref[...]
    h = _norm_mod(x, g_ref[...], mod[:, :, D:2 * D], mod[:, :, 0:D]).reshape(gsz * r, D)
    pr = _dot(h.astype(BF16), w_ref[...])
    lf = _store_rows(pr[:, N_SQ:], bf_ref[...], ak_o, av_o, aki_o, bk_o, bv_o, ck_o, cv_o, lf_o,
                     (slice(None), slice(None)))
    lfrep_o[...] = lf
    qa_o[...] = pr[:, _S_AQ:_S_AQ + 768].astype(BF16)
    qi_o[...] = pr[:, _S_AQI:_S_AQI + 1024].astype(BF16)
    wi_o[...] = pr[:, _S_AWI:_S_AWI + 1024]
    qb_o[...] = pr[:, _S_BQ:_S_BQ + 2048].astype(BF16)
    qc_o[...] = pr[:, _S_CQ:_S_CQ + 768].astype(BF16)


def _inproj_sample(x, mod_a, g, w, bfb, gsz):
    nb, r, _ = x.shape
    n = nb * r
    tm = gsz * r
    row_w = [128, 128, 64, 256, 256, 128, 128, H_C]
    outs = [(w_, F32) for w_ in row_w] + [(768, BF16), (1024, BF16), (1024, F32), (2048, BF16), (768, BF16), (128, F32)]
    return pl.pallas_call(
        _inproj_s_kernel,
        grid=(nb // gsz,),
        in_specs=[pl.BlockSpec((gsz, r, D), lambda i: (i, 0, 0)),
                  pl.BlockSpec((gsz, 1, 3 * D), lambda i: (i, 0, 0)),
                  pl.BlockSpec((1, D), lambda i: (0, 0)),
                  pl.BlockSpec((D, N_SQ + N_ROWS), lambda i: (0, 0)),
                  pl.BlockSpec((1, LANES), lambda i: (0, 0))],
        out_specs=[pl.BlockSpec((tm, w_), lambda i: (i, 0)) for w_, _ in outs],
        out_shape=[jax.ShapeDtypeStruct((n, w_), dt) for w_, dt in outs],
        compiler_params=_cparams("arbitrary"),
        name="inproj_sample",
    )(x, mod_a, g, w, bfb)


_CUM_T = 256


def _cum_kernel(lf_ref, k_ref, o_ref):
    l = lf_ref.shape[1]
    r = lax.broadcasted_iota(I32, (_CUM_T, _CUM_T), 0)
    c = lax.broadcasted_iota(I32, (_CUM_T, _CUM_T), 1)
    tri = jnp.where(c <= r, 1.0, 0.0).astype(BF16)
    lane = lax.broadcasted_iota(I32, (_CUM_T, LANES), 1)
    carry = jnp.zeros((1, LANES), F32)
    for i in range(l // _CUM_T):
        rows = slice(i * _CUM_T, (i + 1) * _CUM_T)
        cum = _dot3_rhs(tri, lf_ref[0, rows, :]) + carry
        carry = cum[_CUM_T - 1:_CUM_T, :]
        hi, mid, lo = _split3(cum)
        piece = jnp.where(lane < 8, hi, jnp.where(lane < 16, mid, lo))
        o_ref[0, rows, 0:LANES] = k_ref[0, rows, :]
        o_ref[0, rows, LANES:2 * LANES] = -piece


def _cum_prompt(lfrep, ckb):
    b, l, _ = lfrep.shape
    return pl.pallas_call(
        _cum_kernel,
        grid=(b,),
        in_specs=[pl.BlockSpec((1, l, LANES), lambda i: (i, 0, 0)),
                  pl.BlockSpec((1, l, LANES), lambda i: (i, 0, 0))],
        out_specs=pl.BlockSpec((1, l, 2 * LANES), lambda i: (i, 0, 0)),
        out_shape=jax.ShapeDtypeStruct((b, l, 2 * LANES), BF16),
        compiler_params=_cparams("arbitrary"),
        name="forget_cumsum",
    )(lfrep, ckb)


def _online_update(s, m_scr, acc_scr, vta):
    m_prev = m_scr[...]
    m_new = jnp.maximum(m_prev, jnp.max(s, axis=0, keepdims=True))
    alpha = jnp.exp(m_prev - m_new)
    p = jnp.exp(s - m_new).astype(BF16)
    m_scr[...] = m_new
    n_g = len(vta)
    w = s.shape[1] // n_g
    for g in range(n_g):
        pv = _dot(vta[g], p[:, g * w:(g + 1) * w])
        acc_scr[g] = acc_scr[g] * alpha[:, g * w:(g + 1) * w] + pv


def _key_rows(start, n):
    return pl.ds(pl.multiple_of(start * TKC, TKC), n * TKC)


def _v_aug(vt_ref, start, n, g, rows):
    v = vt_ref[0, g * rows:(g + 1) * rows, _key_rows(start, n)]
    return jnp.concatenate([v, jnp.ones((16, n * TKC), BF16)], axis=0)


def _add_near(s, n, tile1, tile0):
    parts = []
    if n > 2:
        parts.append(s[0:(n - 2) * TKC])
    if n >= 2:
        blk = s[(n - 2) * TKC:(n - 1) * TKC]
        parts.append(blk if tile1 is None else blk + tile1)
    parts.append(s[(n - 1) * TKC:] + tile0)
    return parts[0] if len(parts) == 1 else jnp.concatenate(parts, axis=0)


def _chunk_loop(qi, chunk):
    n_big = jnp.maximum(qi - 1, 0) // FAR_BLOCKS

    def far(c, carry):
        chunk(c * FAR_BLOCKS, FAR_BLOCKS, False)
        return carry
    lax.fori_loop(0, n_big, far, 0)
    start = n_big * FAR_BLOCKS
    n_tail = qi + 1 - start
    for v in range(1, FAR_BLOCKS + 2):
        @pl.when(n_tail == v)
        def _(v=v):
            chunk(start, v, True)


def _attn_a_kernel(qT_ref, qiT_ref, wT_ref, k_ref, ki_ref, vT_ref, bias_ref, oT_ref,
                   qpad, qipad, key_scr, am_scr, j_scr, m_scr, acc_scr, *, k_top):
    b = pl.program_id(0)
    qi = pl.program_id(1)
    n_chunk = qi + 1
    grp = H_A // KVH_A

    @pl.when((b == 0) & (qi == 0))
    def _():
        qpad[...] = jnp.zeros_like(qpad)
    for h in range(H_A):
        g = h // grp
        qpad[g * HD:(g + 1) * HD, h * TQ:(h + 1) * TQ] = qT_ref[0, h * HD:(h + 1) * HD, :]
    for h in range(H_IDX):
        qipad[:, h * TQ:(h + 1) * TQ] = qiT_ref[0, h * D_IDX:(h + 1) * D_IDX, :]
    w = wT_ref[0, 0:H_IDX, :] * IDX_W_SCALE
    grows = FAR_BLOCKS * TKC
    n_grp = (n_chunk + FAR_BLOCKS - 1) // FAR_BLOCKS
    row = lax.broadcasted_iota(I32, (TKC, TQ), 0)
    col = lax.broadcasted_iota(I32, (TKC, TQ), 1) + qi * TQ
    grow = lax.broadcasted_iota(I32, (grows, TQ), 0)

    def score_group(gi, carry):
        for bi in range(FAR_BLOCKS):
            start = pl.multiple_of(gi * grows + bi * TKC, TKC)
            sl = pl.ds(start, TKC)
            s = _dot(ki_ref[0, sl, :], qipad[...])
            sc = jnp.zeros((TKC, TQ), F32)
            for h in range(H_IDX):
                sc = sc + jnp.maximum(s[:, h * TQ:(h + 1) * TQ], 0.0) * w[h:h + 1, :]
            sc = jnp.where(sc == 0.0, 0.0, sc)
            sc = jnp.where(row + start > col, -jnp.inf, sc)
            key_scr[sl, :] = _sortable_key(sc)
        return carry
    lax.fori_loop(0, n_grp, score_group, 0)

    def count(pred):
        def body(gi, cnt):
            start = pl.multiple_of(gi * grows, grows)
            hit = jnp.where(pred(key_scr[pl.ds(start, grows), :], start), 1, 0).astype(I32)
            return cnt + jnp.sum(hit.reshape(grows // 8, 8, TQ), axis=0)
        cnt8 = lax.fori_loop(0, n_grp, body, jnp.zeros((8, TQ), I32))
        return jnp.sum(cnt8, axis=0, keepdims=True)

    needs_search = n_chunk * TQ > k_top

    @pl.when(jnp.logical_not(needs_search))
    def _():
        def body(gi, carry):
            am_scr[pl.ds(pl.multiple_of(gi * grows, grows), grows), :] = jnp.zeros((grows, TQ), F32)
            return carry
        lax.fori_loop(0, n_grp, body, 0)

    @pl.when(needs_search)
    def _():
        def it_body(it, t):
            cand = t ^ lax.shift_left(jnp.int32(1), 31 - it)
            cnt = count(lambda k, start: k >= cand)
            return jnp.where(cnt >= k_top, cand, t)
        thr = lax.fori_loop(0, 32, it_body, jnp.full((1, TQ), INT_MIN, I32))
        cnt_gt = count(lambda k, start: k > thr)
        cnt_eq = count(lambda k, start: k == thr)
        need = k_top - cnt_gt
        n_bits = max(1, (key_scr.shape[0] - 1).bit_length())
        j_scr[...] = jnp.full((1, TQ), key_scr.shape[0], I32)

        @pl.when(jnp.max(jnp.where(cnt_eq > need, 1, 0)) > 0)
        def _():
            def tie_body(it, jv):
                cand = jv | lax.shift_left(jnp.int32(1), n_bits - 1 - it)
                cnt = count(lambda k, start: (k == thr) & ((grow + start) < cand))
                return jnp.where(cnt < need, cand, jv)
            j_scr[...] = lax.fori_loop(0, n_bits, tie_body, jnp.zeros((1, TQ), I32))
        jv = j_scr[...]

        def body(gi, carry):
            start = pl.multiple_of(gi * grows, grows)
            k = key_scr[pl.ds(start, grows), :]
            sel = (k > thr) | ((k == thr) & ((grow + start) <= jv))
            am_scr[pl.ds(start, grows), :] = jnp.where(sel, 0.0, NEG)
            return carry
        lax.fori_loop(0, n_grp, body, 0)

    m_scr[...] = jnp.full(m_scr.shape, NEG, F32)
    acc_scr[...] = jnp.zeros_like(acc_scr)

    def chunk(start, n, near):
        sl = _key_rows(start, n)
        s = _dot(k_ref[0, sl, :], qpad[...])
        s = s + jnp.concatenate([am_scr[sl, :]] * H_A, axis=1)
        if near:
            tile = lambda kind: jnp.concatenate([bias_ref[h, kind] for h in range(H_A)], axis=1)
            s = _add_near(s, n, tile(1), tile(0))
        _online_update(s, m_scr, acc_scr, [_v_aug(vT_ref, start, n, g, HD) for g in range(KVH_A)])

    _chunk_loop(qi, chunk)

    for h in range(H_A):
        g, hh = h // grp, h % grp
        a = acc_scr[g][:, hh * TQ:(hh + 1) * TQ]
        oT_ref[0, h * HD:(h + 1) * HD, :] = a[0:HD] / a[HD:HD + 1]


def _attn_a(aqT, aqiT, awiT, akb, akib, avT, bias_pt):
    b, _, l = aqT.shape
    k_top = min(K_TOP_MAX, l // 4)
    blk_t = lambda r: pl.BlockSpec((1, r, TQ), lambda i, j: (i, 0, j))
    full = lambda s1, s2: pl.BlockSpec((1, s1, s2), lambda i, j: (i, 0, 0))
    return pl.pallas_call(
        functools.partial(_attn_a_kernel, k_top=k_top),
        grid=(b, l // TQ),
        in_specs=[blk_t(384), blk_t(512), blk_t(16), full(l, 128), full(l, 64), full(128, l),
                  pl.BlockSpec((H_A, 2, TKC, TQ), lambda i, j: (0, 0, 0, 0))],
        out_specs=blk_t(384),
        out_shape=jax.ShapeDtypeStruct((b, 384, l), F32),
        scratch_shapes=[pltpu.VMEM((128, H_A * TQ), BF16),
                        pltpu.VMEM((D_IDX, H_IDX * TQ), BF16),
                        pltpu.VMEM((l, TQ), I32),
                        pltpu.VMEM((l, TQ), F32),
                        pltpu.VMEM((1, TQ), I32),
                        pltpu.VMEM((1, H_A * TQ), F32),
                        pltpu.VMEM((KVH_A, HD + 16, (H_A // KVH_A) * TQ), F32)],
        compiler_params=_cparams("arbitrary", "arbitrary"),
        name="attn_dsa_prompt",
    )(aqT, aqiT, awiT, akb, akib, avT, bias_pt)


def _lambda_value(lam_ref, lam_init):
    lv = lam_ref[...]
    s1 = jnp.sum(lv[0:1] * lv[1:2], axis=1, keepdims=True)
    s2 = jnp.sum(lv[2:3] * lv[3:4], axis=1, keepdims=True)
    return jnp.exp(s1) - jnp.exp(s2) + lam_init


def _attn_b_kernel(qT_ref, k_ref, vT_ref, bias_ref, lam_ref, gsub_ref, oT_ref,
                   qpad, m_scr, acc_scr, *, lam_init):
    b = pl.program_id(0)
    qi = pl.program_id(1)
    grp = H_B // KVH_B

    @pl.when((b == 0) & (qi == 0))
    def _():
        qpad[...] = jnp.zeros_like(qpad)
    for h in range(H_B):
        for c in range(2):
            slot = (h // grp) * 2 + c
            hc = h * 2 + c
            qpad[slot * HD:(slot + 1) * HD, hc * TQ:(hc + 1) * TQ] = qT_ref[0, hc * HD:(hc + 1) * HD, :]

    m_scr[...] = jnp.full(m_scr.shape, NEG, F32)
    acc_scr[...] = jnp.zeros_like(acc_scr)

    def chunk(start, n, near):
        s = _dot(k_ref[0, _key_rows(start, n), :], qpad[...])
        if near:
            tile = lambda kind: jnp.concatenate([bias_ref[hc // 2, kind] for hc in range(2 * H_B)], axis=1)
            s = _add_near(s, n, tile(1), tile(0))
        _online_update(s, m_scr, acc_scr, [_v_aug(vT_ref, start, n, g, 2 * HD) for g in range(KVH_B)])

    _chunk_loop(qi, chunk)

    lam = _lambda_value(lam_ref, lam_init)
    gs = gsub_ref[...]
    for h in range(H_B):
        g, hh = h // grp, h % grp
        a1 = acc_scr[g][:, (hh * 2) * TQ:(hh * 2 + 1) * TQ]
        a2 = acc_scr[g][:, (hh * 2 + 1) * TQ:(hh * 2 + 2) * TQ]
        o = a1[0:2 * HD] / a1[2 * HD:2 * HD + 1] - lam * (a2[0:2 * HD] / a2[2 * HD:2 * HD + 1])
        ms = jnp.mean(o * o, axis=0, keepdims=True)
        oT_ref[0, h * 2 * HD:(h + 1) * 2 * HD, :] = (o * lax.rsqrt(ms + EPS) * gs) * (1.0 - lam_init)


def _attn_b(bqT, bkb, bvT, bias_pt, lamvec, gsub_col, lam_init):
    b, _, l = bqT.shape
    blk_t = lambda r: pl.BlockSpec((1, r, TQ), lambda i, j: (i, 0, j))
    full = lambda s1, s2: pl.BlockSpec((1, s1, s2), lambda i, j: (i, 0, 0))
    return pl.pallas_call(
        functools.partial(_attn_b_kernel, lam_init=lam_init),
        grid=(b, l // TQ),
        in_specs=[blk_t(512), full(l, 256), full(256, l),
                  pl.BlockSpec((H_B, 2, TKC, TQ), lambda i, j: (0, 0, 0, 0)),
                  pl.BlockSpec((4, HD), lambda i, j: (0, 0)),
                  pl.BlockSpec((2 * HD, 1), lambda i, j: (0, 0))],
        out_specs=blk_t(512),
        out_shape=jax.ShapeDtypeStruct((b, 512, l), F32),
        scratch_shapes=[pltpu.VMEM((256, 2 * H_B * TQ), BF16),
                        pltpu.VMEM((1, 2 * H_B * TQ), F32),
                        pltpu.VMEM((KVH_B, 2 * HD + 16, 2 * (H_B // KVH_B) * TQ), F32)],
        compiler_params=_cparams("arbitrary", "arbitrary"),
        name="attn_diff_prompt",
    )(bqT, bkb, bvT, bias_pt, lamvec, gsub_col)


def _attn_c_kernel(qT_ref, k_ref, vT_ref, oT_ref, qpad, m_scr, acc_scr):
    b = pl.program_id(0)
    qi = pl.program_id(1)
    grp = H_C // KVH_C

    @pl.when((b == 0) & (qi == 0))
    def _():
        r = lax.broadcasted_iota(I32, (256, H_C * TQ), 0) - 128
        cblk = lax.broadcasted_iota(I32, (256, H_C * TQ), 1) // TQ
        ones = (r >= 0) & (r < 24) & ((r % 8) == cblk)
        qpad[...] = jnp.where(ones, 1.0, 0.0).astype(BF16)
    for h in range(H_C):
        g = h // grp
        qpad[g * HD:(g + 1) * HD, h * TQ:(h + 1) * TQ] = qT_ref[0, h * HD:(h + 1) * HD, :]

    m_scr[...] = jnp.full(m_scr.shape, NEG, F32)
    acc_scr[...] = jnp.zeros_like(acc_scr)
    row = lax.broadcasted_iota(I32, (TKC, TQ), 0)
    col = lax.broadcasted_iota(I32, (TKC, TQ), 1)
    causal = jnp.where(row > col, NEG, 0.0)

    def chunk(start, n, near):
        s = _dot(k_ref[0, _key_rows(start, n), :], qpad[...])
        if near:
            s = _add_near(s, n, None, jnp.concatenate([causal] * H_C, axis=1))
        _online_update(s, m_scr, acc_scr, [_v_aug(vT_ref, start, n, g, HD) for g in range(KVH_C)])

    _chunk_loop(qi, chunk)

    for h in range(H_C):
        g, hh = h // grp, h % grp
        a = acc_scr[g][:, hh * TQ:(hh + 1) * TQ]
        oT_ref[0, h * HD:(h + 1) * HD, :] = a[0:HD] / a[HD:HD + 1]


def _attn_c(cqT, kcat, cvT):
    b, _, l = cqT.shape
    blk_t = lambda r: pl.BlockSpec((1, r, TQ), lambda i, j: (i, 0, j))
    full = lambda s1, s2: pl.BlockSpec((1, s1, s2), lambda i, j: (i, 0, 0))
    return pl.pallas_call(
        _attn_c_kernel,
        grid=(b, l // TQ),
        in_specs=[blk_t(384), full(l, 256), full(128, l)],
        out_specs=blk_t(384),
        out_shape=jax.ShapeDtypeStruct((b, 384, l), F32),
        scratch_shapes=[pltpu.VMEM((256, H_C * TQ), BF16),
                        pltpu.VMEM((1, H_C * TQ), F32),
                        pltpu.VMEM((KVH_C, HD + 16, (H_C // KVH_C) * TQ), F32)],
        compiler_params=_cparams("arbitrary", "arbitrary"),
        name="attn_forget_prompt",
    )(cqT, kcat, cvT)


G_DEC = 4


def _stack_heads(ref, tok, n, width):
    return jnp.concatenate([ref[tok, h * width:(h + 1) * width] for h in range(n)], axis=0)


def _pad_new(x):
    return jnp.concatenate([x, jnp.zeros((PAGE - x.shape[0], x.shape[1]), x.dtype)], axis=0)


def _probabilities(s_pages):
    mx = s_pages[0]
    for s in s_pages[1:]:
        mx = jnp.maximum(mx, s)
    m = jnp.max(mx, axis=1, keepdims=True)
    p_pages = [jnp.exp(s - m) for s in s_pages]
    lsum = p_pages[0]
    for p in p_pages[1:]:
        lsum = lsum + p
    return p_pages, jnp.sum(lsum, axis=1, keepdims=True)


def _pv(p_pages, vt_pages, v_new):
    acc = _dot(p_pages[-1].astype(BF16), v_new)
    for p, vt in zip(p_pages[:-1], vt_pages):
        acc = acc + _dot_nt(p.astype(BF16), vt)
    return acc


class _PageFetch:
    def __init__(self, pt_ref, gsz, n_pages, hbms, indexers, bufs, sems):
        self.pt_ref, self.gsz, self.n_pages = pt_ref, gsz, n_pages
        self.hbms, self.indexers, self.bufs, self.sems = hbms, indexers, bufs, sems
        self.step = pl.program_id(0)
        self.last = pl.num_programs(0) - 1
        self.cur = lax.rem(self.step, 2)

    def _copy(self, c, page_id, buf, j):
        return pltpu.make_async_copy(self.hbms[c].at[self.indexers[c](page_id)], self.bufs[c].at[buf, j],
                                     self.sems.at[c, buf])

    def _start(self, step, buf, j):
        page_id = self.pt_ref[step * self.gsz + j // self.n_pages, j % self.n_pages]
        for c in range(len(self.hbms)):
            self._copy(c, page_id, buf, j).start()

    def _wait(self, buf):
        for c in range(len(self.hbms)):
            for j in range(self.gsz * self.n_pages):
                self._copy(c, 0, buf, j).wait()

    def begin(self):
        @pl.when(self.step == 0)
        def _():
            for j in range(self.gsz * self.n_pages):
                self._start(0, 0, j)
        self._wait(self.cur)

    def prefetch(self, j):
        self._start(jnp.minimum(self.step + 1, self.last), 1 - self.cur, j)

    def finish(self):
        @pl.when(self.step == self.last)
        def _():
            self._wait(1 - self.cur)

    def page(self, c, j):
        return self.bufs[c][self.cur, j]


def _any_spec():
    return pl.BlockSpec(memory_space=pl.ANY)


def _decode_a_kernel(pt_ref, qa_ref, qi_ref, wi_ref, ak_n, av_n, aki_n, bias_ref, kt_hbm, vt_hbm, kit_hbm,
                     oa_ref, kt_buf, vt_buf, kit_buf, sems, *, layer, gsz, n_pages, k_top):
    at_page = lambda pg: (layer, pg)
    fetch = _PageFetch(pt_ref, gsz, n_pages, (kt_hbm, vt_hbm, kit_hbm), (at_page,) * 3,
                       (kt_buf, vt_buf, kit_buf), sems)
    fetch.begin()
    r = qa_ref.shape[0] // gsz
    rows = gsz * r
    n_all = n_pages + 1
    lane = lax.broadcasted_iota(I32, (rows, PAGE), 1)
    qrow = lax.rem(lax.broadcasted_iota(I32, (rows, PAGE), 0), r)
    new_visible = lane <= qrow

    sc_pages = [[] for _ in range(n_all)]
    for g in range(gsz):
        tok = slice(g * r, (g + 1) * r)
        qi2 = _stack_heads(qi_ref, tok, H_IDX, 2 * D_IDX)[:, 0:D_IDX]
        wcol = _stack_heads(wi_ref, tok, H_IDX, LANES) * IDX_W_SCALE
        for p in range(n_all):
            if p < n_pages:
                fetch.prefetch(g * n_pages + p)
                z = _dot(qi2, fetch.page(2, g * n_pages + p).astype(BF16))
            else:
                z = _dot_nt(qi2, _pad_new(aki_n[tok, :]).astype(BF16))
            z = jnp.maximum(z, 0.0) * wcol
            sc = z[0:r]
            for h in range(1, H_IDX):
                sc = sc + z[h * r:(h + 1) * r]
            sc_pages[p].append(sc)
    key_pages = []
    for p in range(n_all):
        sc = jnp.concatenate(sc_pages[p], axis=0)
        sc = jnp.where(sc == 0.0, 0.0, sc)
        if p == n_pages:
            sc = jnp.where(new_visible, sc, -jnp.inf)
        key_pages.append(_sortable_key(sc))

    def count(pred):
        tot = None
        for p, k in enumerate(key_pages):
            hit = jnp.where(pred(k, p), 1.0, 0.0)
            tot = hit if tot is None else tot + hit
        return jnp.sum(tot, axis=1, keepdims=True)

    thr = jnp.full((rows, 1), INT_MIN, I32)
    for it in range(32):
        cand = thr ^ jnp.int32(-2 ** 31 if it == 0 else 1 << (31 - it))
        cnt = count(lambda k, p: k >= cand)
        thr = jnp.where(cnt >= k_top, cand, thr)
    cnt_gt = count(lambda k, p: k > thr)
    cnt_eq = count(lambda k, p: k == thr)
    need = k_top - cnt_gt
    n_bits = max(1, (n_all * PAGE - 1).bit_length())

    def tie_search():
        jv = jnp.zeros((rows, 1), I32)
        for it in range(n_bits):
            cand = jv | jnp.int32(1 << (n_bits - 1 - it))
            cnt = count(lambda k, p: (k == thr) & ((lane + p * PAGE) < cand))
            jv = jnp.where(cnt < need, cand, jv)
        return jv

    any_excess = jnp.max(jnp.where(cnt_eq > need, 1, 0)) > 0
    jv = lax.cond(any_excess, tie_search, lambda: jnp.full((rows, 1), n_all * PAGE, I32))
    am_pages = []
    for p, k in enumerate(key_pages):
        sel = (k > thr) | ((k == thr) & ((lane + p * PAGE) <= jv))
        am_pages.append(jnp.where(sel, 0.0, NEG))

    bias_last = jnp.concatenate([bias_ref[h, 0] for h in range(H_A)], axis=0)
    bias_new = jnp.concatenate([bias_ref[h, 1] for h in range(H_A)], axis=0)
    for g in range(gsz):
        tok = slice(g * r, (g + 1) * r)
        qa2 = _stack_heads(qa_ref, tok, H_A, 2 * HD)
        s_pages = []
        for p in range(n_all):
            if p < n_pages:
                s = _dot(qa2, fetch.page(0, g * n_pages + p).reshape(2 * HD, PAGE).astype(BF16))
            else:
                s = _dot_nt(qa2, _pad_new(ak_n[tok, :]).astype(BF16))
            s = s + jnp.concatenate([am_pages[p][tok]] * H_A, axis=0)
            if p == n_pages - 1:
                s = s + bias_last
            if p == n_pages:
                s = s + bias_new
            s_pages.append(s)
        p_pages, lsum = _probabilities(s_pages)
        vts = [fetch.page(1, g * n_pages + p).reshape(2 * HD, PAGE).astype(BF16) for p in range(n_pages)]
        o = _pv(p_pages, vts, _pad_new(av_n[tok, :]).astype(BF16)) / lsum
        oa_ref[tok, :] = jnp.concatenate([o[h * r:(h + 1) * r] for h in range(H_A)], axis=1)
    fetch.finish()


def _decode_call(kernel_fn, name, page_table, tok_args, tok_widths, const_args, caches, page_shapes, out_w, r):
    nb, n_pages = page_table.shape
    gsz = G_DEC
    in_specs = [pl.BlockSpec((gsz * r, w), lambda i, pt: (i, 0)) for w in tok_widths]
    in_specs += [pl.BlockSpec(a.shape, functools.partial(lambda i, pt, n: (0,) * n, n=a.ndim)) for a in const_args]
    in_specs += [_any_spec() for _ in caches]
    grid_spec = pltpu.PrefetchScalarGridSpec(
        num_scalar_prefetch=1, grid=(nb // gsz,), in_specs=in_specs,
        out_specs=pl.BlockSpec((gsz * r, out_w), lambda i, pt: (i, 0)),
        scratch_shapes=[pltpu.VMEM((2, gsz * n_pages) + s, F32) for s in page_shapes]
        + [pltpu.SemaphoreType.DMA((len(caches), 2))])
    return pl.pallas_call(
        kernel_fn,
        grid_spec=grid_spec,
        out_shape=jax.ShapeDtypeStruct((nb * r, out_w), F32),
        compiler_params=_cparams("arbitrary"),
        name=name,
    )(page_table, *tok_args, *const_args, *caches)


def _decode_a(page_table, layer, qa, qi, wi, ak_n, av_n, aki_n, kt, vt, kit, bias_dec, r):
    n_pages = page_table.shape[1]
    k_top = min(K_TOP_MAX, (n_pages * PAGE + r) // 4)
    return _decode_call(
        functools.partial(_decode_a_kernel, layer=layer, gsz=G_DEC, n_pages=n_pages, k_top=k_top), "decode_dsa",
        page_table, (qa, qi, wi, ak_n, av_n, aki_n), (768, 1024, 1024, 128, 128, 64), (bias_dec[:H_A],),
        (kt, vt, kit), ((KVH_A, HD, PAGE), (KVH_A, HD, PAGE), (D_IDX, PAGE)), 768, r)


def _decode_b_kernel(pt_ref, qb_ref, bk_n, bv_n, bias_ref, lam_ref, gsub_ref, kt_hbm, v_hbm,
                     ob_ref, kt_buf, v_buf, sems, *, layer, gsz, n_pages, lam_init):
    at_page = lambda pg: (layer, pg)
    fetch = _PageFetch(pt_ref, gsz, n_pages, (kt_hbm, v_hbm), (at_page,) * 2, (kt_buf, v_buf), sems)
    fetch.begin()
    r = qb_ref.shape[0] // gsz
    n_all = n_pages + 1
    grp = H_B // KVH_B
    half = 2 * grp * r
    bias_last = jnp.concatenate([bias_ref[h, 0] for h in range(H_B) for _ in range(2)], axis=0)
    bias_new = jnp.concatenate([bias_ref[h, 1] for h in range(H_B) for _ in range(2)], axis=0)
    lam = _lambda_value(lam_ref, lam_init)
    gs = gsub_ref[...]
    for g in range(gsz):
        tok = slice(g * r, (g + 1) * r)
        qb2 = _stack_heads(qb_ref, tok, 2 * H_B, 4 * HD)
        s_pages = []
        for p in range(n_all):
            if p < n_pages:
                fetch.prefetch(g * n_pages + p)
                s = _dot(qb2, fetch.page(0, g * n_pages + p).reshape(4 * HD, PAGE).astype(BF16))
            else:
                s = _dot_nt(qb2, _pad_new(bk_n[tok, :]).astype(BF16))
            if p == n_pages - 1:
                s = s + bias_last
            if p == n_pages:
                s = s + bias_new
            s_pages.append(s)
        p_pages, lsum = _probabilities(s_pages)
        v_new = _pad_new(bv_n[tok, :]).astype(BF16)
        outs = []
        for kv in range(KVH_B):
            rs = slice(kv * half, (kv + 1) * half)
            acc = _dot(p_pages[-1][rs].astype(BF16), v_new[:, kv * 2 * HD:(kv + 1) * 2 * HD])
            for p in range(n_pages):
                v = v_buf[fetch.cur, g * n_pages + p, pl.ds(kv, PAGE, stride=KVH_B), :].astype(BF16)
                acc = acc + _dot(p_pages[p][rs].astype(BF16), v)
            o = acc / lsum[rs]
            for hh in range(grp):
                od = o[(2 * hh) * r:(2 * hh + 1) * r] - lam * o[(2 * hh + 1) * r:(2 * hh + 2) * r]
                ms = jnp.mean(od * od, axis=1, keepdims=True)
                outs.append((od * lax.rsqrt(ms + EPS) * gs) * (1.0 - lam_init))
        ob_ref[tok, :] = jnp.concatenate(outs, axis=1)
    fetch.finish()


def _decode_b(page_table, layer, qb, bk_n, bv_n, kt, v2, bias_dec, lamvec, gsub_row, lam_init, r):
    n_pages = page_table.shape[1]
    return _decode_call(
        functools.partial(_decode_b_kernel, layer=layer, gsz=G_DEC, n_pages=n_pages, lam_init=lam_init),
        "decode_diff", page_table, (qb, bk_n, bv_n), (2048, 256, 256), (bias_dec[H_A:], lamvec, gsub_row),
        (kt, v2), ((KVH_B, 2, HD, PAGE), (KVH_B * PAGE, 2 * HD)), 512, r)


def _decode_c_kernel(pt_ref, qc_ref, ck_n, cv_n, lft_n, kt_hbm, vt_hbm, lf_hbm,
                     oc_ref, kt_buf, vt_buf, lf_buf, sems, *, layer, gsz, n_pages):
    at_page = lambda pg: (layer, pg)
    fetch = _PageFetch(pt_ref, gsz, n_pages, (kt_hbm, vt_hbm, lf_hbm),
                       (at_page, at_page, lambda pg: (layer, slice(None), pg)), (kt_buf, vt_buf, lf_buf), sems)
    fetch.begin()
    r = qc_ref.shape[0] // gsz
    n_all = n_pages + 1
    lane = lax.broadcasted_iota(I32, (r, PAGE), 1)
    qrow = lax.broadcasted_iota(I32, (r, PAGE), 0)
    causal_new = jnp.concatenate([jnp.where(lane <= qrow, 0.0, NEG)] * H_C, axis=0)
    ri = lax.broadcasted_iota(I32, (PAGE, PAGE), 0)
    ci = lax.broadcasted_iota(I32, (PAGE, PAGE), 1)
    upper = jnp.where(ri <= ci, 1.0, 0.0).astype(BF16)
    ones = jnp.ones((PAGE, PAGE), BF16)
    nr = 8 * n_all
    rr = lax.broadcasted_iota(I32, (nr, nr), 0)
    cc = lax.broadcasted_iota(I32, (nr, nr), 1)
    prev_pages = jnp.where(((rr % 8) == (cc % 8)) & ((cc // 8) < (rr // 8)), 1.0, 0.0).astype(BF16)
    zrow = jnp.zeros((8 - H_C, PAGE), F32)
    for g in range(gsz):
        tok = slice(g * r, (g + 1) * r)
        xs = []
        for p in range(n_pages):
            fetch.prefetch(g * n_pages + p)
            xs += [fetch.page(2, g * n_pages + p), zrow]
        x = jnp.concatenate(xs + [lft_n[g * 8:(g + 1) * 8, :]], axis=0)
        cum = _dot3_lhs(x, upper) + _dot3_rhs(prev_pages, _dot3_lhs(x, ones))
        qc2 = _stack_heads(qc_ref, tok, H_C, 2 * HD)
        s_pages = []
        for p in range(n_all):
            if p < n_pages:
                s = _dot(qc2, fetch.page(0, g * n_pages + p).reshape(2 * HD, PAGE).astype(BF16))
            else:
                s = _dot_nt(qc2, _pad_new(ck_n[tok, :]).astype(BF16)) + causal_new
            decay = jnp.concatenate(
                [jnp.broadcast_to(cum[p * 8 + h:p * 8 + h + 1, :], (r, PAGE)) for h in range(H_C)], axis=0)
            s_pages.append(s - decay)
        p_pages, lsum = _probabilities(s_pages)
        vts = [fetch.page(1, g * n_pages + p).reshape(2 * HD, PAGE).astype(BF16) for p in range(n_pages)]
        o = _pv(p_pages, vts, _pad_new(cv_n[tok, :]).astype(BF16)) / lsum
        oc_ref[tok, :] = jnp.concatenate([o[h * r:(h + 1) * r] for h in range(H_C)], axis=1)
    fetch.finish()


def _decode_c(page_table, layer, qc, ck_n, cv_n, lft_new, kt, vt, lft, r):
    n_pages = page_table.shape[1]
    assert r == 8
    return _decode_call(
        functools.partial(_decode_c_kernel, layer=layer, gsz=G_DEC, n_pages=n_pages), "decode_forget",
        page_table, (qc, ck_n, cv_n, lft_new.reshape(-1, LANES)), (768, 128, 128, LANES), (),
        (kt, vt, lft), ((KVH_C, HD, PAGE), (KVH_C, HD, PAGE), (H_C, PAGE)), 768, r)


def _merge_kernel(x_ref, mod_ref, g_ref, oa_ref, ob_ref, oc_ref, wg_ref, wpa_ref, wpb_ref, wpc_ref, wo_ref,
                  o_ref, *, transposed):
    x = x_ref[...]
    gsz, r, _ = x.shape
    mod = mod_ref[...]
    h = _norm_mod(x, g_ref[...], mod[:, :, D:2 * D], mod[:, :, 0:D]).reshape(gsz * r, D).astype(BF16)
    gates = _sigmoid(_dot(h, wg_ref[...]))
    if transposed:
        oa, ob, oc = oa_ref[0].T, ob_ref[0].T, oc_ref[0].T
    else:
        oa, ob, oc = oa_ref[...], ob_ref[...], oc_ref[...]
    merged = (gates[:, 0:D] * _dot(oa.astype(BF16), wpa_ref[...])
              + gates[:, D:2 * D] * _dot(ob.astype(BF16), wpb_ref[...])
              + gates[:, 2 * D:3 * D] * _dot(oc.astype(BF16), wpc_ref[...]))
    y = _dot(merged.astype(BF16), wo_ref[...]).reshape(gsz, r, D)
    o_ref[...] = x + mod[:, :, 2 * D:3 * D] * y


def _merge(x, mod_a, g, oa, ob, oc, wg, wpa, wpb, wpc, wo, gsz, r, transposed):
    nb, rr, _ = x.shape
    const = lambda a: pl.BlockSpec(a.shape, lambda i, j: (0,) * a.ndim)
    if transposed:
        grid = (nb, rr // r)
        x_spec = pl.BlockSpec((1, r, D), lambda i, j: (i, j, 0))
        mod_spec = pl.BlockSpec((1, 1, 3 * D), lambda i, j: (i, 0, 0))
        o_spec = lambda a: pl.BlockSpec((1, a.shape[1], r), lambda i, j: (i, 0, j))
    else:
        grid = (nb // gsz, 1)
        x_spec = pl.BlockSpec((gsz, rr, D), lambda i, j: (i, 0, 0))
        mod_spec = pl.BlockSpec((gsz, 1, 3 * D), lambda i, j: (i, 0, 0))
        o_spec = lambda a: pl.BlockSpec((gsz * rr, a.shape[1]), lambda i, j: (i, 0))
    return pl.pallas_call(
        functools.partial(_merge_kernel, transposed=transposed),
        grid=grid,
        in_specs=[x_spec, mod_spec, const(g), o_spec(oa), o_spec(ob), o_spec(oc),
                  const(wg), const(wpa), const(wpb), const(wpc), const(wo)],
        out_specs=x_spec,
        out_shape=jax.ShapeDtypeStruct(x.shape, F32),
        compiler_params=_cparams("arbitrary", "arbitrary"),
        name="merge_prompt" if transposed else "merge_sample",
    )(x, mod_a, g, oa, ob, oc, wg, wpa, wpb, wpc, wo)


def _ffn_kernel(x_ref, mod_ref, g_ref, w1_ref, w2_ref, gf_ref, o_ref, *, final):
    x = x_ref[...]
    gsz, r, _ = x.shape
    mod = mod_ref[...]
    h = _norm_mod(x, g_ref[...], mod[:, :, D:2 * D], mod[:, :, 0:D]).reshape(gsz * r, D).astype(BF16)
    u = jnp.maximum(_dot(h, w1_ref[...]), 0.0)
    y = _dot((u * u).astype(BF16), w2_ref[...]).reshape(gsz, r, D)
    x2 = x + mod[:, :, 2 * D:3 * D] * y
    if final:
        ms = jnp.mean(x2 * x2, axis=-1, keepdims=True)
        x2 = x2 * lax.rsqrt(ms + EPS) * gf_ref[...]
    o_ref[...] = x2


def _ffn(x, mod_b, g, w1, w2, g_final, gsz, r, final, name):
    nb, rr, _ = x.shape
    const = lambda a: pl.BlockSpec(a.shape, lambda i, j: (0,) * a.ndim)
    if gsz == 1:
        grid = (nb, rr // r)
        x_spec = pl.BlockSpec((1, r, D), lambda i, j: (i, j, 0))
        mod_spec = pl.BlockSpec((1, 1, 3 * D), lambda i, j: (i, 0, 0))
    else:
        grid = (nb // gsz, 1)
        x_spec = pl.BlockSpec((gsz, rr, D), lambda i, j: (i, 0, 0))
        mod_spec = pl.BlockSpec((gsz, 1, 3 * D), lambda i, j: (i, 0, 0))
    return pl.pallas_call(
        functools.partial(_ffn_kernel, final=final),
        grid=grid,
        in_specs=[x_spec, mod_spec, const(g), const(w1), const(w2), const(g_final)],
        out_specs=x_spec,
        out_shape=jax.ShapeDtypeStruct(x.shape, F32),
        compiler_params=_cparams("arbitrary", "arbitrary"),
        name=name,
    )(x, mod_b, g, w1, w2, g_final)


TM_PROMPT = 512
G_SAMPLE = 16


def kernel(x_prompt, x_sample, c_prompt, c_sample, cache_a_k, cache_a_v, cache_a_kidx, cache_b_k, cache_b_v, cache_c_k, cache_c_v, cache_c_logf, page_table, t5_table, w_ada, b_ada, g_mix, g_ffn, w_in, b_forget, lam_q1, lam_k1, lam_q2, lam_k2, g_subln, w_gate, w_pa, w_pb, w_pc, w_out, w_ff1, w_ff2, g_final):
    depth = w_in.shape[0]
    nbp, seq, _ = x_prompt.shape
    nbs, dec_seq, _ = x_sample.shape
    n_pool, page = cache_a_k.shape[1], cache_a_k.shape[2]
    n_pages = page_table.shape[1]
    past_len = n_pages * page
    assert page == PAGE and seq % max(TM_PROMPT, _CUM_T, FAR_BLOCKS * TKC) == 0 and dec_seq == 8
    assert nbs % G_SAMPLE == 0 and nbs % G_DEC == 0 and n_pool % 8 == 0
    tm = min(TM_PROMPT, seq)

    nc = nbp + nbs
    mod = _ada(jnp.concatenate([c_prompt, c_sample], axis=0), w_ada, b_ada)
    mod = mod.reshape(depth, nc, 1, 6 * D)
    bias_pt, bias_dec = _bias_tiles(t5_table, past_len, dec_seq)

    kv_t = lambda c: jnp.transpose(c, (0, 1, 3, 4, 2))
    a_kt, a_vt, c_kt, c_vt = kv_t(cache_a_k), kv_t(cache_a_v), kv_t(cache_c_k), kv_t(cache_c_v)
    a_kit = jnp.transpose(cache_a_kidx, (0, 1, 3, 2))
    b_kt = jnp.transpose(cache_b_k, (0, 1, 3, 4, 5, 2))
    b_v2 = cache_b_v.reshape(depth, n_pool, page * KVH_B, 2 * HD)
    c_lft = jnp.transpose(cache_c_logf, (0, 3, 1, 2))

    xp, xs = x_prompt, x_sample
    rows_p, rows_s = [], []
    g_final2 = g_final.reshape(1, D)
    for l in range(depth):
        lam_init = 0.8 - 0.6 * math.exp(-0.3 * l)
        wr, wt = _prep_in_prompt(w_in[l])
        ws = _prep_in_sample(w_in[l])
        bfb = _cf_block(b_forget[l].reshape(1, H_C))
        wg = w_gate[l].astype(BF16)
        wpa, wpb, wpc = w_pa[l].astype(BF16), w_pb[l].astype(BF16), w_pc[l].astype(BF16)
        wpa_s = _pad_proj_rows(w_pa[l], H_A, lambda h: h // (H_A // KVH_A)).astype(BF16)
        wpc_s = _pad_proj_rows(w_pc[l], H_C, lambda h: h // (H_C // KVH_C)).astype(BF16)
        wo = w_out[l].astype(BF16)
        w1, w2 = w_ff1[l].astype(BF16), w_ff2[l].astype(BF16)
        gm, gf = g_mix[l].reshape(1, D), g_ffn[l].reshape(1, D)
        lamvec = jnp.stack([lam_q1[l], lam_k1[l], lam_q2[l], lam_k2[l]])
        mod_pa, mod_pb = mod[l, :nbp, :, :3 * D], mod[l, :nbp, :, 3 * D:]
        mod_sa, mod_sb = mod[l, nbp:, :, :3 * D], mod[l, nbp:, :, 3 * D:]
        last = l == depth - 1

        (bv, lf, akb, akib, bkb, ckb, lfrep, akT, avTf, akiT, bkT, ckT, cvTf, awiT,
         aqT, aqiT, bqT, cqT, avT, bvT, cvT) = _inproj_prompt(xp, mod_pa, gm, wr, wt, bfb, tm)
        rows_p.append((akT, avTf, akiT, bkT, bv, ckT, cvTf, lf))
        kcat = _cum_prompt(lfrep, ckb)
        oaT = _attn_a(aqT, aqiT, awiT, akb, akib, avT, bias_pt[:H_A])
        obT = _attn_b(bqT, bkb, bvT, bias_pt[H_A:], lamvec, g_subln[l].reshape(2 * HD, 1), lam_init)
        ocT = _attn_c(cqT, kcat, cvT)
        x1 = _merge(xp, mod_pa, gm, oaT, obT, ocT, wg, wpa, wpb, wpc, wo, 1, tm, True)
        xp = _ffn(x1, mod_pb, gf, w1, w2, g_final2, 1, tm, last, "ffn_prompt")

        (sak, sav, saki, sbk, sbv, sck, scv, slf, qa, qi, wi, qb, qc, slfrep) = _inproj_sample(
            xs, mod_sa, gm, ws, bfb, G_SAMPLE)
        rows_s.append((sak, sav, saki, sbk, sbv, sck, scv, slf))
        lft_new = jnp.swapaxes(slfrep.reshape(nbs, dec_seq, LANES)[:, :, 0:8], 1, 2)
        lft_new = jnp.concatenate([lft_new, jnp.zeros((nbs, 8, LANES - dec_seq), F32)], axis=2)
        lane_head = jnp.arange(8)[None, :, None] < H_C
        lft_new = jnp.where(lane_head, lft_new, 0.0)
        oa = _decode_a(page_table, l, qa, qi, wi, sak, sav, saki, a_kt, a_vt, a_kit, bias_dec, dec_seq)
        ob = _decode_b(page_table, l, qb, sbk, sbv, b_kt, b_v2, bias_dec, lamvec, g_subln[l].reshape(1, 2 * HD),
                       lam_init, dec_seq)
        oc = _decode_c(page_table, l, qc, sck, scv, lft_new, c_kt, c_vt, c_lft, dec_seq)
        x1s = _merge(xs, mod_sa, gm, oa, ob, oc, wg, wpa_s, wpb, wpc_s, wo, G_SAMPLE, dec_seq, False)
        xs = _ffn(x1s, mod_sb, gf, w1, w2, g_final2, G_SAMPLE, dec_seq, last, "ffn_sample")

    def stack(rows, i):
        return jnp.stack([r[i] for r in rows])

    def stack_sample(i, shape):
        return stack(rows_s, i).reshape((depth, nbs, dec_seq) + shape)

    def stack_prompt_t(i, shape):
        n = len(shape)
        y = stack(rows_p, i).reshape((depth, nbp) + shape + (seq,))
        return jnp.transpose(y, (0, 1, n + 2) + tuple(range(2, n + 2)))

    out_p = (stack_prompt_t(0, (KVH_A, HD)), stack_prompt_t(1, (KVH_A, HD)), stack_prompt_t(2, (D_IDX,)),
             stack_prompt_t(3, (KVH_B, 2, HD)), stack(rows_p, 4).reshape(depth, nbp, seq, KVH_B, 2 * HD),
             stack_prompt_t(5, (KVH_C, HD)), stack_prompt_t(6, (KVH_C, HD)), stack(rows_p, 7))
    out_s = (stack_sample(0, (KVH_A, HD)), stack_sample(1, (KVH_A, HD)), stack_sample(2, (D_IDX,)),
             stack_sample(3, (KVH_B, 2, HD)), stack_sample(4, (KVH_B, 2 * HD)), stack_sample(5, (KVH_C, HD)),
             stack_sample(6, (KVH_C, HD)), stack_sample(7, (H_C,)))
    return (xp, xs) + out_p + out_s
```

```python
import functools
import math

import numpy as np
import jax
import jax.numpy as jnp
from jax import lax
from jax.experimental import pallas as pl
from jax.experimental.pallas import tpu as pltpu

F32 = jnp.float32
BF16 = jnp.bfloat16
I32 = jnp.int32

D = 1024
HD = 64
H_A, KVH_A = 6, 2
H_IDX, D_IDX = 8, 64
K_TOP_MAX = 256
IDX_W_SCALE = (H_IDX ** -0.5) * (D_IDX ** -0.5)
H_B, KVH_B = 4, 2
H_C, KVH_C = 6, 2
N_BUCKETS, T5_MAX_EXACT, T5_MAX_DIST = 32, 16, 128
D_FF = 4 * D
EPS = 1e-6
N_HEADS_T5 = H_A + H_B
PAGE = 128

LANES = 128
TQ = 256
TKC = 128
Q_BLOCKS = TQ // TKC
NEAR_BLOCKS = Q_BLOCKS + 1
FAR_BLOCKS = 4
NEG = -1e30
INT_MIN = -2 ** 31
VMEM_LIMIT = 56 * 1024 * 1024

_W = dict(aq=H_A * HD, ak=KVH_A * HD, av=KVH_A * HD, aqi=H_IDX * D_IDX, aki=D_IDX, awi=H_IDX,
          bq=H_B * 2 * HD, bk=KVH_B * 2 * HD, bv=KVH_B * 2 * HD,
          cq=H_C * HD, ck=KVH_C * HD, cv=KVH_C * HD, cf=H_C)
_OFF = {}
_o = 0
for _k, _v in _W.items():
    _OFF[_k] = _o
    _o += _v

_R_AK, _R_AV, _R_AKI, _R_BK, _R_BV, _R_CK, _R_CV, _R_CF = 0, 128, 256, 384, 640, 896, 1024, 1152
N_ROWS = 1280
_P_AK, _P_AKI, _P_BK, _P_CK, _P_CF, _P_BV = 0, 128, 256, 512, 640, 768
N_PROWS = 1024
_T_AQ, _T_AQI, _T_AWI, _T_BQ, _T_CQ, _T_AV, _T_BV, _T_CV = 0, 384, 896, 912, 1424, 1808, 1936, 2192
_T_AK, _T_AKI, _T_BK, _T_CK = 2320, 2448, 2512, 2768
N_T = 2896
_S_AQ, _S_AQI, _S_AWI, _S_BQ, _S_CQ = 0, 768, 1792, 2816, 4864
N_SQ = 5632


def _cparams(*sem):
    return pltpu.CompilerParams(dimension_semantics=sem, vmem_limit_bytes=VMEM_LIMIT)


def _dot(a, b):
    return jnp.dot(a, b, preferred_element_type=F32)


def _dot_nt(a, b):
    return lax.dot_general(a, b, (((1,), (1,)), ((), ())), preferred_element_type=F32)


def _split3(x):
    hi = x.astype(BF16)
    r1 = x - hi.astype(F32)
    mid = r1.astype(BF16)
    lo = (r1 - mid.astype(F32)).astype(BF16)
    return hi, mid, lo


def _dot3_rhs(a_bf16, x):
    hi, mid, lo = _split3(x)
    return _dot(a_bf16, hi) + _dot(a_bf16, mid) + _dot(a_bf16, lo)


def _dot3_lhs(x, b_bf16):
    hi, mid, lo = _split3(x)
    return _dot(hi, b_bf16) + _dot(mid, b_bf16) + _dot(lo, b_bf16)


def _norm_mod(x, g, sc, sh):
    ms = jnp.mean(x * x, axis=-1, keepdims=True)
    return (x * lax.rsqrt(ms + EPS) * g) * (1.0 + sc) + sh


def _log_sigmoid(z):
    return jnp.minimum(z, 0.0) - jnp.log1p(jnp.exp(-jnp.abs(z)))


def _sigmoid(z):
    return 1.0 / (1.0 + jnp.exp(-z))


def _sortable_key(x):
    bits = lax.bitcast_convert_type(x, I32)
    return bits ^ ((bits >> 31) & 0x7FFFFFFF)


def _ada_kernel(c_ref, w_ref, b_ref, o_ref):
    c = c_ref[...]
    s = (c * _sigmoid(c)).astype(BF16)
    o_ref[0] = _dot(s, w_ref[0].astype(BF16)) + b_ref[0]


def _ada(c_all, w_ada, b_ada):
    depth = w_ada.shape[0]
    nc = c_all.shape[0]
    nt = 6
    return pl.pallas_call(
        _ada_kernel,
        grid=(depth, nt),
        in_specs=[pl.BlockSpec((nc, D), lambda l, j: (0, 0)),
                  pl.BlockSpec((1, D, D), lambda l, j: (l, 0, j)),
                  pl.BlockSpec((1, 1, D), lambda l, j: (l, 0, j))],
        out_specs=pl.BlockSpec((1, nc, D), lambda l, j: (l, 0, j)),
        out_shape=jax.ShapeDtypeStruct((depth, nc, 6 * D), F32),
        compiler_params=_cparams("arbitrary", "arbitrary"),
        name="ada",
    )(c_all, w_ada, b_ada.reshape(depth, 1, 6 * D))


def _t5_bucket_np(rel):
    n = np.maximum(rel, 0)
    nf = np.maximum(n, 1).astype(np.float32)
    large = T5_MAX_EXACT + (np.log(nf / np.float32(T5_MAX_EXACT)) / np.float32(math.log(T5_MAX_DIST / T5_MAX_EXACT))
                            * np.float32(N_BUCKETS - T5_MAX_EXACT)).astype(np.int32)
    return np.where(n < T5_MAX_EXACT, n, np.minimum(large, N_BUCKETS - 1)).astype(np.int32)


def _bias_bucket_tables(past_len, dec_seq):
    s = np.arange(TKC)[:, None]
    t = np.arange(TQ)[None, :]
    tiles = []
    for u in range(NEAR_BLOCKS):
        rel = t - s - (TQ - TKC) + u * TKC
        tiles.append(np.where(rel >= 0, _t5_bucket_np(rel), -1))
    prompt = np.stack(tiles).astype(np.int32)
    i = np.arange(dec_seq)[:, None]
    lane = np.arange(PAGE)[None, :]
    rel_last = (past_len + i) - (past_len - PAGE + lane)
    rel_new = i - lane
    dl = _t5_bucket_np(rel_last)
    dn = np.where((rel_new >= 0) & (lane < dec_seq), _t5_bucket_np(rel_new), -1)
    dec = np.stack([dl, dn]).astype(np.int32)
    return prompt, dec


def _bias_kernel(tab_ref, pb_ref, db_ref, pt_ref, dt_ref):
    pb = pb_ref[...]
    db = db_ref[...]
    for h in range(N_HEADS_T5):
        def lut(bk):
            acc = jnp.zeros(bk.shape, F32)
            for b in range(N_BUCKETS):
                acc = jnp.where(bk == b, tab_ref[b, h], acc)
            return jnp.where(bk < 0, NEG, acc - tab_ref[N_BUCKETS - 1, h])
        pt_ref[h] = lut(pb)
        dt_ref[h] = lut(db)


def _bias_tiles(t5_table, past_len, dec_seq):
    pb, db = _bias_bucket_tables(past_len, dec_seq)
    return pl.pallas_call(
        _bias_kernel,
        in_specs=[pl.BlockSpec(memory_space=pltpu.SMEM),
                  pl.BlockSpec(memory_space=pltpu.VMEM),
                  pl.BlockSpec(memory_space=pltpu.VMEM)],
        out_specs=[pl.BlockSpec(memory_space=pltpu.VMEM), pl.BlockSpec(memory_space=pltpu.VMEM)],
        out_shape=[jax.ShapeDtypeStruct((N_HEADS_T5,) + pb.shape, F32),
                   jax.ShapeDtypeStruct((N_HEADS_T5,) + db.shape, F32)],
        name="t5_bias_tiles",
    )(t5_table, jnp.asarray(pb), jnp.asarray(db))


def _seg(w, name):
    return w[:, _OFF[name]:_OFF[name] + _W[name]]


def _cf_block(cf):
    n = cf.shape[0]
    z2 = jnp.zeros((n, 2), cf.dtype)
    return jnp.concatenate([cf, z2, cf, z2, cf, z2, jnp.zeros((n, LANES - 24), cf.dtype)], axis=1)


def _rows_block(w):
    z64 = jnp.zeros((D, 64), w.dtype)
    return jnp.concatenate([_seg(w, "ak"), _seg(w, "av"), _seg(w, "aki"), z64, _seg(w, "bk"), _seg(w, "bv"),
                            _seg(w, "ck"), _seg(w, "cv"), _cf_block(_seg(w, "cf"))], axis=1)


def _prep_in_prompt(w):
    scale = HD ** -0.5
    z8 = jnp.zeros((D, 8), w.dtype)
    z64 = jnp.zeros((D, 64), w.dtype)
    w_t = jnp.concatenate([_seg(w, "aq") * scale, _seg(w, "aqi"), _seg(w, "awi"), z8, _seg(w, "bq") * scale,
                           _seg(w, "cq") * scale, _seg(w, "av"), _seg(w, "bv"), _seg(w, "cv"),
                           _seg(w, "ak"), _seg(w, "aki"), _seg(w, "bk"), _seg(w, "ck")], axis=1)
    w_r = jnp.concatenate([_seg(w, "ak"), _seg(w, "aki"), z64, _seg(w, "bk"), _seg(w, "ck"),
                           _cf_block(_seg(w, "cf")), _seg(w, "bv")], axis=1)
    return w_r.astype(BF16), w_t.T.astype(BF16)


def _pad_heads(wq, n_heads, kv_of, width, n_slots):
    blocks = []
    for h in range(n_heads):
        blk = jnp.zeros((D, n_slots * HD), wq.dtype)
        s = kv_of(h)
        blk = blk.at[:, s * HD:(s + 1) * HD].set(wq[:, h * width:(h + 1) * width])
        blocks.append(blk)
    return jnp.concatenate(blocks, axis=1)


def _prep_in_sample(w):
    scale = HD ** -0.5
    aq = _pad_heads(_seg(w, "aq") * scale, H_A, lambda h: h // (H_A // KVH_A), HD, 2)
    aqi = _pad_heads(_seg(w, "aqi"), H_IDX, lambda h: 0, D_IDX, 2)
    awi = jnp.repeat(_seg(w, "awi"), LANES, axis=1)
    bq = _pad_heads(_seg(w, "bq") * scale, 2 * H_B, lambda hc: (hc // 2 // (H_B // KVH_B)) * 2 + hc % 2, HD, 4)
    cq = _pad_heads(_seg(w, "cq") * scale, H_C, lambda h: h // (H_C // KVH_C), HD, 2)
    return jnp.concatenate([aq, aqi, awi, bq, cq, _rows_block(w)], axis=1).astype(BF16)


def _pad_proj_rows(wp, n_heads, kv_of):
    blocks = []
    for h in range(n_heads):
        blk = jnp.zeros((2 * HD, D), wp.dtype)
        s = kv_of(h)
        blk = blk.at[s * HD:(s + 1) * HD].set(wp[h * HD:(h + 1) * HD])
        blocks.append(blk)
    return jnp.concatenate(blocks, axis=0)


def _store_rows(pr, bf, ak_o, av_o, aki_o, bk_o, bv_o, ck_o, cv_o, lf_o, idx):
    ak_o[idx] = pr[:, _R_AK:_R_AK + 128]
    av_o[idx] = pr[:, _R_AV:_R_AV + 128]
    aki_o[idx] = pr[:, _R_AKI:_R_AKI + 64]
    bk_o[idx] = pr[:, _R_BK:_R_BK + 256]
    bv_o[idx] = pr[:, _R_BV:_R_BV + 256]
    ck_o[idx] = pr[:, _R_CK:_R_CK + 128]
    cv_o[idx] = pr[:, _R_CV:_R_CV + 128]
    lf = _log_sigmoid(pr[:, _R_CF:_R_CF + 128] + bf)
    lf_o[idx] = lf[:, 0:H_C]
    return lf


def _inproj_p_kernel(x_ref, mod_ref, g_ref, wr_ref, wt_ref, bf_ref,
                     bv_o, lf_o, akb_o, akib_o, bkb_o, ckb_o, lfrep_o,
                     akT_o, avT_o, akiT_o, bkT_o, ckT_o, cvT_o, awiT_o,
                     aqT_o, aqiT_o, bqT_o, cqT_o, avTb_o, bvTb_o, cvTb_o):
    x = x_ref[0]
    mod = mod_ref[0]
    h = _norm_mod(x, g_ref[...], mod[:, D:2 * D], mod[:, 0:D])
    pr = _dot(h.astype(BF16), wr_ref[...])
    bv_o[0] = pr[:, _P_BV:_P_BV + 256]
    lf = _log_sigmoid(pr[:, _P_CF:_P_CF + 128] + bf_ref[...])
    lf_o[0] = lf[:, 0:H_C]
    lfrep_o[0] = lf
    akb_o[0] = pr[:, _P_AK:_P_AK + 128].astype(BF16)
    akib_o[0] = pr[:, _P_AKI:_P_AKI + 64].astype(BF16)
    bkb_o[0] = pr[:, _P_BK:_P_BK + 256].astype(BF16)
    ckb_o[0] = pr[:, _P_CK:_P_CK + 128].astype(BF16)
    pt = _dot(wt_ref[...], h.T.astype(BF16))
    akT_o[0] = pt[_T_AK:_T_AK + 128]
    avT_o[0] = pt[_T_AV:_T_AV + 128]
    akiT_o[0] = pt[_T_AKI:_T_AKI + 64]
    bkT_o[0] = pt[_T_BK:_T_BK + 256]
    ckT_o[0] = pt[_T_CK:_T_CK + 128]
    cvT_o[0] = pt[_T_CV:_T_CV + 128]
    awiT_o[0] = pt[_T_AWI:_T_AWI + 16]
    aqT_o[0] = pt[_T_AQ:_T_AQ + 384].astype(BF16)
    aqiT_o[0] = pt[_T_AQI:_T_AQI + 512].astype(BF16)
    bqT_o[0] = pt[_T_BQ:_T_BQ + 512].astype(BF16)
    cqT_o[0] = pt[_T_CQ:_T_CQ + 384].astype(BF16)
    avTb_o[0] = pt[_T_AV:_T_AV + 128].astype(BF16)
    bvTb_o[0] = pt[_T_BV:_T_BV + 256].astype(BF16)
    cvTb_o[0] = pt[_T_CV:_T_CV + 128].astype(BF16)


def _inproj_prompt(x, mod_a, g, wr, wt, bfb, tm):
    b, l, _ = x.shape
    row = lambda w, dt: jax.ShapeDtypeStruct((b, l, w), dt)
    tr = lambda r, dt: jax.ShapeDtypeStruct((b, r, l), dt)
    row_spec = lambda w: pl.BlockSpec((1, tm, w), lambda i, j: (i, j, 0))
    tr_spec = lambda r: pl.BlockSpec((1, r, tm), lambda i, j: (i, 0, j))
    rows = [(256, F32), (H_C, F32), (128, BF16), (64, BF16), (256, BF16), (128, BF16), (128, F32)]
    trs = [(128, F32), (128, F32), (64, F32), (256, F32), (128, F32), (128, F32), (16, F32),
           (384, BF16), (512, BF16), (512, BF16), (384, BF16), (128, BF16), (256, BF16), (128, BF16)]
    return pl.pallas_call(
        _inproj_p_kernel,
        grid=(b, l // tm),
        in_specs=[pl.BlockSpec((1, tm, D), lambda i, j: (i, j, 0)),
                  pl.BlockSpec((1, 1, 3 * D), lambda i, j: (i, 0, 0)),
                  pl.BlockSpec((1, D), lambda i, j: (0, 0)),
                  pl.BlockSpec((D, N_PROWS), lambda i, j: (0, 0)),
                  pl.BlockSpec((N_T, D), lambda i, j: (0, 0)),
                  pl.BlockSpec((1, LANES), lambda i, j: (0, 0))],
        out_specs=[row_spec(w) for w, _ in rows] + [tr_spec(r) for r, _ in trs],
        out_shape=[row(w, dt) for w, dt in rows] + [tr(r, dt) for r, dt in trs],
        compiler_params=_cparams("arbitrary", "arbitrary"),
        name="inproj_prompt",
    )(x, mod_a, g, wr, wt, bfb)


def _inproj_s_kernel(x_ref, mod_ref, g_ref, w_ref, bf_ref,
                     ak_o, av_o, aki_o, bk_o, bv_o, ck_o, cv_o, lf_o,
                     qa_o, qi_o, wi_o, qb_o, qc_o, lfrep_o):
    x = x_ref[...]
    gsz, r, _ = x.shape
    mod = mod_ref[...]
    h = _norm_mod(x, g_ref[...], mod[:, :, D:2 * D], mod[:, :, 0:D]).reshape(gsz * r, D)
    pr = _dot(h.astype(BF16), w_ref[...])
    lf = _store_rows(pr[:, N_SQ:], bf_ref[...], ak_o, av_o, aki_o, bk_o, bv_o, ck_o, cv_o, lf_o,
                     (slice(None), slice(None)))
    lfrep_o[...] = lf
    qa_o[...] = pr[:, _S_AQ:_S_AQ + 768].astype(BF16)
    qi_o[...] = pr[:, _S_AQI:_S_AQI + 1024].astype(BF16)
    wi_o[...] = pr[:, _S_AWI:_S_AWI + 1024]
    qb_o[...] = pr[:, _S_BQ:_S_BQ + 2048].astype(BF16)
    qc_o[...] = pr[:, _S_CQ:_S_CQ + 768].astype(BF16)


def _inproj_sample(x, mod_a, g, w, bfb, gsz):
    nb, r, _ = x.shape
    n = nb * r
    tm = gsz * r
    row_w = [128, 128, 64, 256, 256, 128, 128, H_C]
    outs = [(w_, F32) for w_ in row_w] + [(768, BF16), (1024, BF16), (1024, F32), (2048, BF16), (768, BF16), (128, F32)]
    return pl.pallas_call(
        _inproj_s_kernel,
        grid=(nb // gsz,),
        in_specs=[pl.BlockSpec((gsz, r, D), lambda i: (i, 0, 0)),
                  pl.BlockSpec((gsz, 1, 3 * D), lambda i: (i, 0, 0)),
                  pl.BlockSpec((1, D), lambda i: (0, 0)),
                  pl.BlockSpec((D, N_SQ + N_ROWS), lambda i: (0, 0)),
                  pl.BlockSpec((1, LANES), lambda i: (0, 0))],
        out_specs=[pl.BlockSpec((tm, w_), lambda i: (i, 0)) for w_, _ in outs],
        out_shape=[jax.ShapeDtypeStruct((n, w_), dt) for w_, dt in outs],
        compiler_params=_cparams("arbitrary"),
        name="inproj_sample",
    )(x, mod_a, g, w, bfb)


_CUM_T = 256


def _cum_kernel(lf_ref, k_ref, o_ref):
    l = lf_ref.shape[1]
    r = lax.broadcasted_iota(I32, (_CUM_T, _CUM_T), 0)
    c = lax.broadcasted_iota(I32, (_CUM_T, _CUM_T), 1)
    tri = jnp.where(c <= r, 1.0, 0.0).astype(BF16)
    lane = lax.broadcasted_iota(I32, (_CUM_T, LANES), 1)
    carry = jnp.zeros((1, LANES), F32)
    for i in range(l // _CUM_T):
        rows = slice(i * _CUM_T, (i + 1) * _CUM_T)
        cum = _dot3_rhs(tri, lf_ref[0, rows, :]) + carry
        carry = cum[_CUM_T - 1:_CUM_T, :]
        hi, mid, lo = _split3(cum)
        piece = jnp.where(lane < 8, hi, jnp.where(lane < 16, mid, lo))
        o_ref[0, rows, 0:LANES] = k_ref[0, rows, :]
        o_ref[0, rows, LANES:2 * LANES] = -piece


def _cum_prompt(lfrep, ckb):
    b, l, _ = lfrep.shape
    return pl.pallas_call(
        _cum_kernel,
        grid=(b,),
        in_specs=[pl.BlockSpec((1, l, LANES), lambda i: (i, 0, 0)),
                  pl.BlockSpec((1, l, LANES), lambda i: (i, 0, 0))],
        out_specs=pl.BlockSpec((1, l, 2 * LANES), lambda i: (i, 0, 0)),
        out_shape=jax.ShapeDtypeStruct((b, l, 2 * LANES), BF16),
        compiler_params=_cparams("arbitrary"),
        name="forget_cumsum",
    )(lfrep, ckb)


def _online_update(s, m_scr, acc_scr, vta):
    m_prev = m_scr[...]
    m_new = jnp.maximum(m_prev, jnp.max(s, axis=0, keepdims=True))
    alpha = jnp.exp(m_prev - m_new)
    p = jnp.exp(s - m_new).astype(BF16)
    m_scr[...] = m_new
    n_g = len(vta)
    w = s.shape[1] // n_g
    for g in range(n_g):
        pv = _dot(vta[g], p[:, g * w:(g + 1) * w])
        acc_scr[g] = acc_scr[g] * alpha[:, g * w:(g + 1) * w] + pv


def _key_rows(start, n):
    return pl.ds(pl.multiple_of(start * TKC, TKC), n * TKC)


def _v_aug(vt_ref, start, n, g, rows):
    v = vt_ref[0, g * rows:(g + 1) * rows, _key_rows(start, n)]
    return jnp.concatenate([v, jnp.ones((16, n * TKC), BF16)], axis=0)


def _add_near(s, n, tiles):
    parts = []
    first_near = max(n - len(tiles), 0)
    if first_near > 0:
        parts.append(s[0:first_near * TKC])
    for blk in range(first_near, n):
        tile = tiles[n - 1 - blk]
        rows = s[blk * TKC:(blk + 1) * TKC]
        parts.append(rows if tile is None else rows + tile)
    return parts[0] if len(parts) == 1 else jnp.concatenate(parts, axis=0)


def _chunk_loop(qi, chunk):
    n_tot = (qi + 1) * Q_BLOCKS
    n_big = jnp.maximum(n_tot - NEAR_BLOCKS, 0) // FAR_BLOCKS

    def far(c, carry):
        chunk(c * FAR_BLOCKS, FAR_BLOCKS, False)
        return carry
    lax.fori_loop(0, n_big, far, 0)
    start = n_big * FAR_BLOCKS
    n_tail = n_tot - start
    for v in range(Q_BLOCKS, FAR_BLOCKS + NEAR_BLOCKS, Q_BLOCKS):
        @pl.when(n_tail == v)
        def _(v=v):
            chunk(start, v, True)


def _attn_a_kernel(qT_ref, qiT_ref, wT_ref, k_ref, ki_ref, vT_ref, bias_ref, oT_ref,
                   qpad, qipad, key_scr, am_scr, j_scr, m_scr, acc_scr, *, k_top):
    b = pl.program_id(0)
    qi = pl.program_id(1)
    n_chunk = (qi + 1) * Q_BLOCKS
    grp = H_A // KVH_A

    @pl.when((b == 0) & (qi == 0))
    def _():
        qpad[...] = jnp.zeros_like(qpad)
    for h in range(H_A):
        g = h // grp
        qpad[g * HD:(g + 1) * HD, h * TQ:(h + 1) * TQ] = qT_ref[0, h * HD:(h + 1) * HD, :]
    for h in range(H_IDX):
        qipad[:, h * TQ:(h + 1) * TQ] = qiT_ref[0, h * D_IDX:(h + 1) * D_IDX, :]
    w = wT_ref[0, 0:H_IDX, :] * IDX_W_SCALE
    grows = FAR_BLOCKS * TKC
    n_grp = (n_chunk + FAR_BLOCKS - 1) // FAR_BLOCKS
    row = lax.broadcasted_iota(I32, (TKC, TQ), 0)
    col = lax.broadcasted_iota(I32, (TKC, TQ), 1) + qi * TQ
    grow = lax.broadcasted_iota(I32, (grows, TQ), 0)

    def score_group(gi, carry):
        for bi in range(FAR_BLOCKS):
            start = pl.multiple_of(gi * grows + bi * TKC, TKC)
            sl = pl.ds(start, TKC)
            s = _dot(ki_ref[0, sl, :], qipad[...])
            sc = jnp.zeros((TKC, TQ), F32)
            for h in range(H_IDX):
                sc = sc + jnp.maximum(s[:, h * TQ:(h + 1) * TQ], 0.0) * w[h:h + 1, :]
            sc = jnp.where(sc == 0.0, 0.0, sc)
            sc = jnp.where(row + start > col, -jnp.inf, sc)
            key_scr[sl, :] = _sortable_key(sc)
        return carry
    lax.fori_loop(0, n_grp, score_group, 0)

    def count(pred):
        def body(gi, cnt):
            start = pl.multiple_of(gi * grows, grows)
            hit = jnp.where(pred(key_scr[pl.ds(start, grows), :], start), 1, 0).astype(I32)
            return cnt + jnp.sum(hit.reshape(grows // 8, 8, TQ), axis=0)
        cnt8 = lax.fori_loop(0, n_grp, body, jnp.zeros((8, TQ), I32))
        return jnp.sum(cnt8, axis=0, keepdims=True)

    needs_search = (qi + 1) * TQ > k_top

    @pl.when(jnp.logical_not(needs_search))
    def _():
        def body(gi, carry):
            am_scr[pl.ds(pl.multiple_of(gi * grows, grows), grows), :] = jnp.zeros((grows, TQ), F32)
            return carry
        lax.fori_loop(0, n_grp, body, 0)

    @pl.when(needs_search)
    def _():
        def it_body(it, t):
            cand = t ^ lax.shift_left(jnp.int32(1), 31 - it)
            cnt = count(lambda k, start: k >= cand)
            return jnp.where(cnt >= k_top, cand, t)
        thr = lax.fori_loop(0, 32, it_body, jnp.full((1, TQ), INT_MIN, I32))
        cnt_gt = count(lambda k, start: k > thr)
        cnt_eq = count(lambda k, start: k == thr)
        need = k_top - cnt_gt
        n_bits = max(1, (key_scr.shape[0] - 1).bit_length())
        j_scr[...] = jnp.full((1, TQ), key_scr.shape[0], I32)

        @pl.when(jnp.max(jnp.where(cnt_eq > need, 1, 0)) > 0)
        def _():
            def tie_body(it, jv):
                cand = jv | lax.shift_left(jnp.int32(1), n_bits - 1 - it)
                cnt = count(lambda k, start: (k == thr) & ((grow + start) < cand))
                return jnp.where(cnt < need, cand, jv)
            j_scr[...] = lax.fori_loop(0, n_bits, tie_body, jnp.zeros((1, TQ), I32))
        jv = j_scr[...]

        def body(gi, carry):
            start = pl.multiple_of(gi * grows, grows)
            k = key_scr[pl.ds(start, grows), :]
            sel = (k > thr) | ((k == thr) & ((grow + start) <= jv))
            am_scr[pl.ds(start, grows), :] = jnp.where(sel, 0.0, NEG)
            return carry
        lax.fori_loop(0, n_grp, body, 0)

    m_scr[...] = jnp.full(m_scr.shape, NEG, F32)
    acc_scr[...] = jnp.zeros_like(acc_scr)

    def chunk(start, n, near):
        sl = _key_rows(start, n)
        s = _dot(k_ref[0, sl, :], qpad[...])
        s = s + jnp.concatenate([am_scr[sl, :]] * H_A, axis=1)
        if near:
            s = _add_near(s, n, [jnp.concatenate([bias_ref[h, u] for h in range(H_A)], axis=1)
                                 for u in range(NEAR_BLOCKS)])
        _online_update(s, m_scr, acc_scr, [_v_aug(vT_ref, start, n, g, HD) for g in range(KVH_A)])

    _chunk_loop(qi, chunk)

    for h in range(H_A):
        g, hh = h // grp, h % grp
        a = acc_scr[g][:, hh * TQ:(hh + 1) * TQ]
        oT_ref[0, h * HD:(h + 1) * HD, :] = a[0:HD] / a[HD:HD + 1]


def _attn_a(aqT, aqiT, awiT, akb, akib, avT, bias_pt):
    b, _, l = aqT.shape
    k_top = min(K_TOP_MAX, l // 4)
    blk_t = lambda r: pl.BlockSpec((1, r, TQ), lambda i, j: (i, 0, j))
    full = lambda s1, s2: pl.BlockSpec((1, s1, s2), lambda i, j: (i, 0, 0))
    return pl.pallas_call(
        functools.partial(_attn_a_kernel, k_top=k_top),
        grid=(b, l // TQ),
        in_specs=[blk_t(384), blk_t(512), blk_t(16), full(l, 128), full(l, 64), full(128, l),
                  pl.BlockSpec((H_A, NEAR_BLOCKS, TKC, TQ), lambda i, j: (0, 0, 0, 0))],
        out_specs=blk_t(384),
        out_shape=jax.ShapeDtypeStruct((b, 384, l), F32),
        scratch_shapes=[pltpu.VMEM((128, H_A * TQ), BF16),
                        pltpu.VMEM((D_IDX, H_IDX * TQ), BF16),
                        pltpu.VMEM((l, TQ), I32),
                        pltpu.VMEM((l, TQ), F32),
                        pltpu.VMEM((1, TQ), I32),
                        pltpu.VMEM((1, H_A * TQ), F32),
                        pltpu.VMEM((KVH_A, HD + 16, (H_A // KVH_A) * TQ), F32)],
        compiler_params=_cparams("arbitrary", "arbitrary"),
        name="attn_dsa_prompt",
    )(aqT, aqiT, awiT, akb, akib, avT, bias_pt)


def _lambda_value(lam_ref, lam_init):
    lv = lam_ref[...]
    s1 = jnp.sum(lv[0:1] * lv[1:2], axis=1, keepdims=True)
    s2 = jnp.sum(lv[2:3] * lv[3:4], axis=1, keepdims=True)
    return jnp.exp(s1) - jnp.exp(s2) + lam_init


def _attn_b_kernel(qT_ref, k_ref, vT_ref, bias_ref, lam_ref, gsub_ref, oT_ref,
                   qpad, m_scr, acc_scr, *, lam_init):
    b = pl.program_id(0)
    qi = pl.program_id(1)
    grp = H_B // KVH_B

    @pl.when((b == 0) & (qi == 0))
    def _():
        qpad[...] = jnp.zeros_like(qpad)
    for h in range(H_B):
        for c in range(2):
            slot = (h // grp) * 2 + c
            hc = h * 2 + c
            qpad[slot * HD:(slot + 1) * HD, hc * TQ:(hc + 1) * TQ] = qT_ref[0, hc * HD:(hc + 1) * HD, :]

    m_scr[...] = jnp.full(m_scr.shape, NEG, F32)
    acc_scr[...] = jnp.zeros_like(acc_scr)

    def chunk(start, n, near):
        s = _dot(k_ref[0, _key_rows(start, n), :], qpad[...])
        if near:
            s = _add_near(s, n, [jnp.concatenate([bias_ref[hc // 2, u] for hc in range(2 * H_B)], axis=1)
                                 for u in range(NEAR_BLOCKS)])
        _online_update(s, m_scr, acc_scr, [_v_aug(vT_ref, start, n, g, 2 * HD) for g in range(KVH_B)])

    _chunk_loop(qi, chunk)

    lam = _lambda_value(lam_ref, lam_init)
    gs = gsub_ref[...]
    for h in range(H_B):
        g, hh = h // grp, h % grp
        a1 = acc_scr[g][:, (hh * 2) * TQ:(hh * 2 + 1) * TQ]
        a2 = acc_scr[g][:, (hh * 2 + 1) * TQ:(hh * 2 + 2) * TQ]
        o = a1[0:2 * HD] / a1[2 * HD:2 * HD + 1] - lam * (a2[0:2 * HD] / a2[2 * HD:2 * HD + 1])
        ms = jnp.mean(o * o, axis=0, keepdims=True)
        oT_ref[0, h * 2 * HD:(h + 1) * 2 * HD, :] = (o * lax.rsqrt(ms + EPS) * gs) * (1.0 - lam_init)


def _attn_b(bqT, bkb, bvT, bias_pt, lamvec, gsub_col, lam_init):
    b, _, l = bqT.shape
    blk_t = lambda r: pl.BlockSpec((1, r, TQ), lambda i, j: (i, 0, j))
    full = lambda s1, s2: pl.BlockSpec((1, s1, s2), lambda i, j: (i, 0, 0))
    return pl.pallas_call(
        functools.partial(_attn_b_kernel, lam_init=lam_init),
        grid=(b, l // TQ),
        in_specs=[blk_t(512), full(l, 256), full(256, l),
                  pl.BlockSpec((H_B, NEAR_BLOCKS, TKC, TQ), lambda i, j: (0, 0, 0, 0)),
                  pl.BlockSpec((4, HD), lambda i, j: (0, 0)),
                  pl.BlockSpec((2 * HD, 1), lambda i, j: (0, 0))],
        out_specs=blk_t(512),
        out_shape=jax.ShapeDtypeStruct((b, 512, l), F32),
        scratch_shapes=[pltpu.VMEM((256, 2 * H_B * TQ), BF16),
                        pltpu.VMEM((1, 2 * H_B * TQ), F32),
                        pltpu.VMEM((KVH_B, 2 * HD + 16, 2 * (H_B // KVH_B) * TQ), F32)],
        compiler_params=_cparams("arbitrary", "arbitrary"),
        name="attn_diff_prompt",
    )(bqT, bkb, bvT, bias_pt, lamvec, gsub_col)


def _attn_c_kernel(qT_ref, k_ref, vT_ref, oT_ref, qpad, m_scr, acc_scr):
    b = pl.program_id(0)
    qi = pl.program_id(1)
    grp = H_C // KVH_C

    @pl.when((b == 0) & (qi == 0))
    def _():
        r = lax.broadcasted_iota(I32, (256, H_C * TQ), 0) - 128
        cblk = lax.broadcasted_iota(I32, (256, H_C * TQ), 1) // TQ
        ones = (r >= 0) & (r < 24) & ((r % 8) == cblk)
        qpad[...] = jnp.where(ones, 1.0, 0.0).astype(BF16)
    for h in range(H_C):
        g = h // grp
        qpad[g * HD:(g + 1) * HD, h * TQ:(h + 1) * TQ] = qT_ref[0, h * HD:(h + 1) * HD, :]

    m_scr[...] = jnp.full(m_scr.shape, NEG, F32)
    acc_scr[...] = jnp.zeros_like(acc_scr)
    row = lax.broadcasted_iota(I32, (TKC, TQ), 0)
    col = lax.broadcasted_iota(I32, (TKC, TQ), 1)
    causal = [jnp.concatenate([jnp.where(row - col + (TQ - TKC) - u * TKC > 0, NEG, 0.0)] * H_C, axis=1)
              for u in range(Q_BLOCKS)] + [None]

    def chunk(start, n, near):
        s = _dot(k_ref[0, _key_rows(start, n), :], qpad[...])
        if near:
            s = _add_near(s, n, causal)
        _online_update(s, m_scr, acc_scr, [_v_aug(vT_ref, start, n, g, HD) for g in range(KVH_C)])

    _chunk_loop(qi, chunk)

    for h in range(H_C):
        g, hh = h // grp, h % grp
        a = acc_scr[g][:, hh * TQ:(hh + 1) * TQ]
        oT_ref[0, h * HD:(h + 1) * HD, :] = a[0:HD] / a[HD:HD + 1]


def _attn_c(cqT, kcat, cvT):
    b, _, l = cqT.shape
    blk_t = lambda r: pl.BlockSpec((1, r, TQ), lambda i, j: (i, 0, j))
    full = lambda s1, s2: pl.BlockSpec((1, s1, s2), lambda i, j: (i, 0, 0))
    return pl.pallas_call(
        _attn_c_kernel,
        grid=(b, l // TQ),
        in_specs=[blk_t(384), full(l, 256), full(128, l)],
        out_specs=blk_t(384),
        out_shape=jax.ShapeDtypeStruct((b, 384, l), F32),
        scratch_shapes=[pltpu.VMEM((256, H_C * TQ), BF16),
                        pltpu.VMEM((1, H_C * TQ), F32),
                        pltpu.VMEM((KVH_C, HD + 16, (H_C // KVH_C) * TQ), F32)],
        compiler_params=_cparams("arbitrary", "arbitrary"),
        name="attn_forget_prompt",
    )(cqT, kcat, cvT)


G_DEC = 4


def _stack_heads(ref, tok, n, width):
    return jnp.concatenate([ref[tok, h * width:(h + 1) * width] for h in range(n)], axis=0)


def _pad_new(x):
    return jnp.concatenate([x, jnp.zeros((PAGE - x.shape[0], x.shape[1]), x.dtype)], axis=0)


def _probabilities(s_pages):
    mx = s_pages[0]
    for s in s_pages[1:]:
        mx = jnp.maximum(mx, s)
    m = jnp.max(mx, axis=1, keepdims=True)
    p_pages = [jnp.exp(s - m) for s in s_pages]
    lsum = p_pages[0]
    for p in p_pages[1:]:
        lsum = lsum + p
    return p_pages, jnp.sum(lsum, axis=1, keepdims=True)


def _pv(p_pages, vt_pages, v_new):
    acc = _dot(p_pages[-1].astype(BF16), v_new)
    for p, vt in zip(p_pages[:-1], vt_pages):
        acc = acc + _dot_nt(p.astype(BF16), vt)
    return acc


class _PageFetch:
    def __init__(self, pt_ref, gsz, n_pages, hbms, indexers, bufs, sems):
        self.pt_ref, self.gsz, self.n_pages = pt_ref, gsz, n_pages
        self.hbms, self.indexers, self.bufs, self.sems = hbms, indexers, bufs, sems
        self.step = pl.program_id(0)
        self.last = pl.num_programs(0) - 1
        self.cur = lax.rem(self.step, 2)

    def _copy(self, c, page_id, buf, j):
        return pltpu.make_async_copy(self.hbms[c].at[self.indexers[c](page_id)], self.bufs[c].at[buf, j],
                                     self.sems.at[c, buf])

    def _start(self, step, buf, j):
        page_id = self.pt_ref[step * self.gsz + j // self.n_pages, j % self.n_pages]
        for c in range(len(self.hbms)):
            self._copy(c, page_id, buf, j).start()

    def _wait(self, buf):
        for c in range(len(self.hbms)):
            for j in range(self.gsz * self.n_pages):
                self._copy(c, 0, buf, j).wait()

    def begin(self):
        @pl.when(self.step == 0)
        def _():
            for j in range(self.gsz * self.n_pages):
                self._start(0, 0, j)
        self._wait(self.cur)

    def prefetch(self, j):
        self._start(jnp.minimum(self.step + 1, self.last), 1 - self.cur, j)

    def finish(self):
        @pl.when(self.step == self.last)
        def _():
            self._wait(1 - self.cur)

    def page(self, c, j):
        return self.bufs[c][self.cur, j]


def _any_spec():
    return pl.BlockSpec(memory_space=pl.ANY)


def _decode_a_kernel(pt_ref, qa_ref, qi_ref, wi_ref, ak_n, av_n, aki_n, bias_ref, kt_hbm, vt_hbm, kit_hbm,
                     oa_ref, kt_buf, vt_buf, kit_buf, sems, *, layer, gsz, n_pages, k_top):
    at_page = lambda pg: (layer, pg)
    fetch = _PageFetch(pt_ref, gsz, n_pages, (kt_hbm, vt_hbm, kit_hbm), (at_page,) * 3,
                       (kt_buf, vt_buf, kit_buf), sems)
    fetch.begin()
    r = qa_ref.shape[0] // gsz
    rows = gsz * r
    n_all = n_pages + 1
    lane = lax.broadcasted_iota(I32, (rows, PAGE), 1)
    qrow = lax.rem(lax.broadcasted_iota(I32, (rows, PAGE), 0), r)
    new_visible = lane <= qrow

    sc_pages = [[] for _ in range(n_all)]
    for g in range(gsz):
        tok = slice(g * r, (g + 1) * r)
        qi2 = _stack_heads(qi_ref, tok, H_IDX, 2 * D_IDX)[:, 0:D_IDX]
        wcol = _stack_heads(wi_ref, tok, H_IDX, LANES) * IDX_W_SCALE
        for p in range(n_all):
            if p < n_pages:
                fetch.prefetch(g * n_pages + p)
                z = _dot(qi2, fetch.page(2, g * n_pages + p).astype(BF16))
            else:
                z = _dot_nt(qi2, _pad_new(aki_n[tok, :]).astype(BF16))
            z = jnp.maximum(z, 0.0) * wcol
            sc = z[0:r]
            for h in range(1, H_IDX):
                sc = sc + z[h * r:(h + 1) * r]
            sc_pages[p].append(sc)
    key_pages = []
    for p in range(n_all):
        sc = jnp.concatenate(sc_pages[p], axis=0)
        sc = jnp.where(sc == 0.0, 0.0, sc)
        if p == n_pages:
            sc = jnp.where(new_visible, sc, -jnp.inf)
        key_pages.append(_sortable_key(sc))

    def count(pred):
        tot = None
        for p, k in enumerate(key_pages):
            hit = jnp.where(pred(k, p), 1.0, 0.0)
            tot = hit if tot is None else tot + hit
        return jnp.sum(tot, axis=1, keepdims=True)

    thr = jnp.full((rows, 1), INT_MIN, I32)
    for it in range(32):
        cand = thr ^ jnp.int32(-2 ** 31 if it == 0 else 1 << (31 - it))
        cnt = count(lambda k, p: k >= cand)
        thr = jnp.where(cnt >= k_top, cand, thr)
    cnt_gt = count(lambda k, p: k > thr)
    cnt_eq = count(lambda k, p: k == thr)
    need = k_top - cnt_gt
    n_bits = max(1, (n_all * PAGE - 1).bit_length())

    def tie_search():
        jv = jnp.zeros((rows, 1), I32)
        for it in range(n_bits):
            cand = jv | jnp.int32(1 << (n_bits - 1 - it))
            cnt = count(lambda k, p: (k == thr) & ((lane + p * PAGE) < cand))
            jv = jnp.where(cnt < need, cand, jv)
        return jv

    any_excess = jnp.max(jnp.where(cnt_eq > need, 1, 0)) > 0
    jv = lax.cond(any_excess, tie_search, lambda: jnp.full((rows, 1), n_all * PAGE, I32))
    am_pages = []
    for p, k in enumerate(key_pages):
        sel = (k > thr) | ((k == thr) & ((lane + p * PAGE) <= jv))
        am_pages.append(jnp.where(sel, 0.0, NEG))

    bias_last = jnp.concatenate([bias_ref[h, 0] for h in range(H_A)], axis=0)
    bias_new = jnp.concatenate([bias_ref[h, 1] for h in range(H_A)], axis=0)
    for g in range(gsz):
        tok = slice(g * r, (g + 1) * r)
        qa2 = _stack_heads(qa_ref, tok, H_A, 2 * HD)
        s_pages = []
        for p in range(n_all):
            if p < n_pages:
                s = _dot(qa2, fetch.page(0, g * n_pages + p).reshape(2 * HD, PAGE).astype(BF16))
            else:
                s = _dot_nt(qa2, _pad_new(ak_n[tok, :]).astype(BF16))
            s = s + jnp.concatenate([am_pages[p][tok]] * H_A, axis=0)
            if p == n_pages - 1:
                s = s + bias_last
            if p == n_pages:
                s = s + bias_new
            s_pages.append(s)
        p_pages, lsum = _probabilities(s_pages)
        vts = [fetch.page(1, g * n_pages + p).reshape(2 * HD, PAGE).astype(BF16) for p in range(n_pages)]
        o = _pv(p_pages, vts, _pad_new(av_n[tok, :]).astype(BF16)) / lsum
        oa_ref[tok, :] = jnp.concatenate([o[h * r:(h + 1) * r] for h in range(H_A)], axis=1)
    fetch.finish()


def _decode_call(kernel_fn, name, page_table, tok_args, tok_widths, const_args, caches, page_shapes, out_w, r):
    nb, n_pages = page_table.shape
    gsz = G_DEC
    in_specs = [pl.BlockSpec((gsz * r, w), lambda i, pt: (i, 0)) for w in tok_widths]
    in_specs += [pl.BlockSpec(a.shape, functools.partial(lambda i, pt, n: (0,) * n, n=a.ndim)) for a in const_args]
    in_specs += [_any_spec() for _ in caches]
    grid_spec = pltpu.PrefetchScalarGridSpec(
        num_scalar_prefetch=1, grid=(nb // gsz,), in_specs=in_specs,
        out_specs=pl.BlockSpec((gsz * r, out_w), lambda i, pt: (i, 0)),
        scratch_shapes=[pltpu.VMEM((2, gsz * n_pages) + s, F32) for s in page_shapes]
        + [pltpu.SemaphoreType.DMA((len(caches), 2))])
    return pl.pallas_call(
        kernel_fn,
        grid_spec=grid_spec,
        out_shape=jax.ShapeDtypeStruct((nb * r, out_w), F32),
        compiler_params=_cparams("arbitrary"),
        name=name,
    )(page_table, *tok_args, *const_args, *caches)


def _decode_a(page_table, layer, qa, qi, wi, ak_n, av_n, aki_n, kt, vt, kit, bias_dec, r):
    n_pages = page_table.shape[1]
    k_top = min(K_TOP_MAX, (n_pages * PAGE + r) // 4)
    return _decode_call(
        functools.partial(_decode_a_kernel, layer=layer, gsz=G_DEC, n_pages=n_pages, k_top=k_top), "decode_dsa",
        page_table, (qa, qi, wi, ak_n, av_n, aki_n), (768, 1024, 1024, 128, 128, 64), (bias_dec[:H_A],),
        (kt, vt, kit), ((KVH_A, HD, PAGE), (KVH_A, HD, PAGE), (D_IDX, PAGE)), 768, r)


def _decode_b_kernel(pt_ref, qb_ref, bk_n, bv_n, bias_ref, lam_ref, gsub_ref, kt_hbm, v_hbm,
                     ob_ref, kt_buf, v_buf, sems, *, layer, gsz, n_pages, lam_init):
    at_page = lambda pg: (layer, pg)
    fetch = _PageFetch(pt_ref, gsz, n_pages, (kt_hbm, v_hbm), (at_page,) * 2, (kt_buf, v_buf), sems)
    fetch.begin()
    r = qb_ref.shape[0] // gsz
    n_all = n_pages + 1
    grp = H_B // KVH_B
    half = 2 * grp * r
    bias_last = jnp.concatenate([bias_ref[h, 0] for h in range(H_B) for _ in range(2)], axis=0)
    bias_new = jnp.concatenate([bias_ref[h, 1] for h in range(H_B) for _ in range(2)], axis=0)
    lam = _lambda_value(lam_ref, lam_init)
    gs = gsub_ref[...]
    for g in range(gsz):
        tok = slice(g * r, (g + 1) * r)
        qb2 = _stack_heads(qb_ref, tok, 2 * H_B, 4 * HD)
        s_pages = []
        for p in range(n_all):
            if p < n_pages:
                fetch.prefetch(g * n_pages + p)
                s = _dot(qb2, fetch.page(0, g * n_pages + p).reshape(4 * HD, PAGE).astype(BF16))
            else:
                s = _dot_nt(qb2, _pad_new(bk_n[tok, :]).astype(BF16))
            if p == n_pages - 1:
                s = s + bias_last
            if p == n_pages:
                s = s + bias_new
            s_pages.append(s)
        p_pages, lsum = _probabilities(s_pages)
        v_new = _pad_new(bv_n[tok, :]).astype(BF16)
        outs = []
        for kv in range(KVH_B):
            rs = slice(kv * half, (kv + 1) * half)
            acc = _dot(p_pages[-1][rs].astype(BF16), v_new[:, kv * 2 * HD:(kv + 1) * 2 * HD])
            for p in range(n_pages):
                v = v_buf[fetch.cur, g * n_pages + p, pl.ds(kv, PAGE, stride=KVH_B), :].astype(BF16)
                acc = acc + _dot(p_pages[p][rs].astype(BF16), v)
            o = acc / lsum[rs]
            for hh in range(grp):
                od = o[(2 * hh) * r:(2 * hh + 1) * r] - lam * o[(2 * hh + 1) * r:(2 * hh + 2) * r]
                ms = jnp.mean(od * od, axis=1, keepdims=True)
                outs.append((od * lax.rsqrt(ms + EPS) * gs) * (1.0 - lam_init))
        ob_ref[tok, :] = jnp.concatenate(outs, axis=1)
    fetch.finish()


def _decode_b(page_table, layer, qb, bk_n, bv_n, kt, v2, bias_dec, lamvec, gsub_row, lam_init, r):
    n_pages = page_table.shape[1]
    return _decode_call(
        functools.partial(_decode_b_kernel, layer=layer, gsz=G_DEC, n_pages=n_pages, lam_init=lam_init),
        "decode_diff", page_table, (qb, bk_n, bv_n), (2048, 256, 256), (bias_dec[H_A:], lamvec, gsub_row),
        (kt, v2), ((KVH_B, 2, HD, PAGE), (KVH_B * PAGE, 2 * HD)), 512, r)


def _decode_c_kernel(pt_ref, qc_ref, ck_n, cv_n, lft_n, kt_hbm, vt_hbm, lf_hbm,
                     oc_ref, kt_buf, vt_buf, lf_buf, sems, *, layer, gsz, n_pages):
    at_page = lambda pg: (layer, pg)
    fetch = _PageFetch(pt_ref, gsz, n_pages, (kt_hbm, vt_hbm, lf_hbm),
                       (at_page, at_page, lambda pg: (layer, slice(None), pg)), (kt_buf, vt_buf, lf_buf), sems)
    fetch.begin()
    r = qc_ref.shape[0] // gsz
    n_all = n_pages + 1
    lane = lax.broadcasted_iota(I32, (r, PAGE), 1)
    qrow = lax.broadcasted_iota(I32, (r, PAGE), 0)
    causal_new = jnp.concatenate([jnp.where(lane <= qrow, 0.0, NEG)] * H_C, axis=0)
    ri = lax.broadcasted_iota(I32, (PAGE, PAGE), 0)
    ci = lax.broadcasted_iota(I32, (PAGE, PAGE), 1)
    upper = jnp.where(ri <= ci, 1.0, 0.0).astype(BF16)
    ones = jnp.ones((PAGE, PAGE), BF16)
    nr = 8 * n_all
    rr = lax.broadcasted_iota(I32, (nr, nr), 0)
    cc = lax.broadcasted_iota(I32, (nr, nr), 1)
    prev_pages = jnp.where(((rr % 8) == (cc % 8)) & ((cc // 8) < (rr // 8)), 1.0, 0.0).astype(BF16)
    zrow = jnp.zeros((8 - H_C, PAGE), F32)
    for g in range(gsz):
        tok = slice(g * r, (g + 1) * r)
        xs = []
        for p in range(n_pages):
            fetch.prefetch(g * n_pages + p)
            xs += [fetch.page(2, g * n_pages + p), zrow]
        x = jnp.concatenate(xs + [lft_n[g * 8:(g + 1) * 8, :]], axis=0)
        cum = _dot3_lhs(x, upper) + _dot3_rhs(prev_pages, _dot3_lhs(x, ones))
        qc2 = _stack_heads(qc_ref, tok, H_C, 2 * HD)
        s_pages = []
        for p in range(n_all):
            if p < n_pages:
                s = _dot(qc2, fetch.page(0, g * n_pages + p).reshape(2 * HD, PAGE).astype(BF16))
            else:
                s = _dot_nt(qc2, _pad_new(ck_n[tok, :]).astype(BF16)) + causal_new
            decay = jnp.concatenate(
                [jnp.broadcast_to(cum[p * 8 + h:p * 8 + h + 1, :], (r, PAGE)) for h in range(H_C)], axis=0)
            s_pages.append(s - decay)
        p_pages, lsum = _probabilities(s_pages)
        vts = [fetch.page(1, g * n_pages + p).reshape(2 * HD, PAGE).astype(BF16) for p in range(n_pages)]
        o = _pv(p_pages, vts, _pad_new(cv_n[tok, :]).astype(BF16)) / lsum
        oc_ref[tok, :] = jnp.concatenate([o[h * r:(h + 1) * r] for h in range(H_C)], axis=1)
    fetch.finish()


def _decode_c(page_table, layer, qc, ck_n, cv_n, lft_new, kt, vt, lft, r):
    n_pages = page_table.shape[1]
    assert r == 8
    return _decode_call(
        functools.partial(_decode_c_kernel, layer=layer, gsz=G_DEC, n_pages=n_pages), "decode_forget",
        page_table, (qc, ck_n, cv_n, lft_new.reshape(-1, LANES)), (768, 128, 128, LANES), (),
        (kt, vt, lft), ((KVH_C, HD, PAGE), (KVH_C, HD, PAGE), (H_C, PAGE)), 768, r)


def _merge_kernel(x_ref, mod_ref, g_ref, oa_ref, ob_ref, oc_ref, wg_ref, wpa_ref, wpb_ref, wpc_ref, wo_ref,
                  o_ref, *, transposed):
    x = x_ref[...]
    gsz, r, _ = x.shape
    mod = mod_ref[...]
    h = _norm_mod(x, g_ref[...], mod[:, :, D:2 * D], mod[:, :, 0:D]).reshape(gsz * r, D).astype(BF16)
    gates = _sigmoid(_dot(h, wg_ref[...]))
    if transposed:
        oa, ob, oc = oa_ref[0].T, ob_ref[0].T, oc_ref[0].T
    else:
        oa, ob, oc = oa_ref[...], ob_ref[...], oc_ref[...]
    merged = (gates[:, 0:D] * _dot(oa.astype(BF16), wpa_ref[...])
              + gates[:, D:2 * D] * _dot(ob.astype(BF16), wpb_ref[...])
              + gates[:, 2 * D:3 * D] * _dot(oc.astype(BF16), wpc_ref[...]))
    y = _dot(merged.astype(BF16), wo_ref[...]).reshape(gsz, r, D)
    o_ref[...] = x + mod[:, :, 2 * D:3 * D] * y


def _merge(x, mod_a, g, oa, ob, oc, wg, wpa, wpb, wpc, wo, gsz, r, transposed):
    nb, rr, _ = x.shape
    const = lambda a: pl.BlockSpec(a.shape, lambda i, j: (0,) * a.ndim)
    if transposed:
        grid = (nb, rr // r)
        x_spec = pl.BlockSpec((1, r, D), lambda i, j: (i, j, 0))
        mod_spec = pl.BlockSpec((1, 1, 3 * D), lambda i, j: (i, 0, 0))
        o_spec = lambda a: pl.BlockSpec((1, a.shape[1], r), lambda i, j: (i, 0, j))
    else:
        grid = (nb // gsz, 1)
        x_spec = pl.BlockSpec((gsz, rr, D), lambda i, j: (i, 0, 0))
        mod_spec = pl.BlockSpec((gsz, 1, 3 * D), lambda i, j: (i, 0, 0))
        o_spec = lambda a: pl.BlockSpec((gsz * rr, a.shape[1]), lambda i, j: (i, 0))
    return pl.pallas_call(
        functools.partial(_merge_kernel, transposed=transposed),
        grid=grid,
        in_specs=[x_spec, mod_spec, const(g), o_spec(oa), o_spec(ob), o_spec(oc),
                  const(wg), const(wpa), const(wpb), const(wpc), const(wo)],
        out_specs=x_spec,
        out_shape=jax.ShapeDtypeStruct(x.shape, F32),
        compiler_params=_cparams("arbitrary", "arbitrary"),
        name="merge_prompt" if transposed else "merge_sample",
    )(x, mod_a, g, oa, ob, oc, wg, wpa, wpb, wpc, wo)


def _ffn_kernel(x_ref, mod_ref, g_ref, w1_ref, w2_ref, gf_ref, o_ref, *, final):
    x = x_ref[...]
    gsz, r, _ = x.shape
    mod = mod_ref[...]
    h = _norm_mod(x, g_ref[...], mod[:, :, D:2 * D], mod[:, :, 0:D]).reshape(gsz * r, D).astype(BF16)
    u = jnp.maximum(_dot(h, w1_ref[...]), 0.0)
    y = _dot((u * u).astype(BF16), w2_ref[...]).reshape(gsz, r, D)
    x2 = x + mod[:, :, 2 * D:3 * D] * y
    if final:
        ms = jnp.mean(x2 * x2, axis=-1, keepdims=True)
        x2 = x2 * lax.rsqrt(ms + EPS) * gf_ref[...]
    o_ref[...] = x2


def _ffn(x, mod_b, g, w1, w2, g_final, gsz, r, final, name):
    nb, rr, _ = x.shape
    const = lambda a: pl.BlockSpec(a.shape, lambda i, j: (0,) * a.ndim)
    if gsz == 1:
        grid = (nb, rr // r)
        x_spec = pl.BlockSpec((1, r, D), lambda i, j: (i, j, 0))
        mod_spec = pl.BlockSpec((1, 1, 3 * D), lambda i, j: (i, 0, 0))
    else:
        grid = (nb // gsz, 1)
        x_spec = pl.BlockSpec((gsz, rr, D), lambda i, j: (i, 0, 0))
        mod_spec = pl.BlockSpec((gsz, 1, 3 * D), lambda i, j: (i, 0, 0))
    return pl.pallas_call(
        functools.partial(_ffn_kernel, final=final),
        grid=grid,
        in_specs=[x_spec, mod_spec, const(g), const(w1), const(w2), const(g_final)],
        out_specs=x_spec,
        out_shape=jax.ShapeDtypeStruct(x.shape, F32),
        compiler_params=_cparams("arbitrary", "arbitrary"),
        name=name,
    )(x, mod_b, g, w1, w2, g_final)


TM_PROMPT = 512
G_SAMPLE = 16


def kernel(x_prompt, x_sample, c_prompt, c_sample, cache_a_k, cache_a_v, cache_a_kidx, cache_b_k, cache_b_v, cache_c_k, cache_c_v, cache_c_logf, page_table, t5_table, w_ada, b_ada, g_mix, g_ffn, w_in, b_forget, lam_q1, lam_k1, lam_q2, lam_k2, g_subln, w_gate, w_pa, w_pb, w_pc, w_out, w_ff1, w_ff2, g_final):
    depth = w_in.shape[0]
    nbp, seq, _ = x_prompt.shape
    nbs, dec_seq, _ = x_sample.shape
    n_pool, page = cache_a_k.shape[1], cache_a_k.shape[2]
    n_pages = page_table.shape[1]
    past_len = n_pages * page
    assert page == PAGE and seq % max(TM_PROMPT, _CUM_T, FAR_BLOCKS * TKC) == 0 and dec_seq == 8
    assert nbs % G_SAMPLE == 0 and nbs % G_DEC == 0 and n_pool % 8 == 0
    tm = min(TM_PROMPT, seq)

    nc = nbp + nbs
    mod = _ada(jnp.concatenate([c_prompt, c_sample], axis=0), w_ada, b_ada)
    mod = mod.reshape(depth, nc, 1, 6 * D)
    bias_pt, bias_dec = _bias_tiles(t5_table, past_len, dec_seq)

    kv_t = lambda c: jnp.transpose(c, (0, 1, 3, 4, 2))
    a_kt, a_vt, c_kt, c_vt = kv_t(cache_a_k), kv_t(cache_a_v), kv_t(cache_c_k), kv_t(cache_c_v)
    a_kit = jnp.transpose(cache_a_kidx, (0, 1, 3, 2))
    b_kt = jnp.transpose(cache_b_k, (0, 1, 3, 4, 5, 2))
    b_v2 = cache_b_v.reshape(depth, n_pool, page * KVH_B, 2 * HD)
    c_lft = jnp.transpose(cache_c_logf, (0, 3, 1, 2))

    xp, xs = x_prompt, x_sample
    rows_p, rows_s = [], []
    g_final2 = g_final.reshape(1, D)
    for l in range(depth):
        lam_init = 0.8 - 0.6 * math.exp(-0.3 * l)
        wr, wt = _prep_in_prompt(w_in[l])
        ws = _prep_in_sample(w_in[l])
        bfb = _cf_block(b_forget[l].reshape(1, H_C))
        wg = w_gate[l].astype(BF16)
        wpa, wpb, wpc = w_pa[l].astype(BF16), w_pb[l].astype(BF16), w_pc[l].astype(BF16)
        wpa_s = _pad_proj_rows(w_pa[l], H_A, lambda h: h // (H_A // KVH_A)).astype(BF16)
        wpc_s = _pad_proj_rows(w_pc[l], H_C, lambda h: h // (H_C // KVH_C)).astype(BF16)
        wo = w_out[l].astype(BF16)
        w1, w2 = w_ff1[l].astype(BF16), w_ff2[l].astype(BF16)
        gm, gf = g_mix[l].reshape(1, D), g_ffn[l].reshape(1, D)
        lamvec = jnp.stack([lam_q1[l], lam_k1[l], lam_q2[l], lam_k2[l]])
        mod_pa, mod_pb = mod[l, :nbp, :, :3 * D], mod[l, :nbp, :, 3 * D:]
        mod_sa, mod_sb = mod[l, nbp:, :, :3 * D], mod[l, nbp:, :, 3 * D:]
        last = l == depth - 1

        (bv, lf, akb, akib, bkb, ckb, lfrep, akT, avTf, akiT, bkT, ckT, cvTf, awiT,
         aqT, aqiT, bqT, cqT, avT, bvT, cvT) = _inproj_prompt(xp, mod_pa, gm, wr, wt, bfb, tm)
        rows_p.append((akT, avTf, akiT, bkT, bv, ckT, cvTf, lf))
        kcat = _cum_prompt(lfrep, ckb)
        oaT = _attn_a(aqT, aqiT, awiT, akb, akib, avT, bias_pt[:H_A])
        obT = _attn_b(bqT, bkb, bvT, bias_pt[H_A:], lamvec, g_subln[l].reshape(2 * HD, 1), lam_init)
        ocT = _attn_c(cqT, kcat, cvT)
        x1 = _merge(xp, mod_pa, gm, oaT, obT, ocT, wg, wpa, wpb, wpc, wo, 1, tm, True)
        xp = _ffn(x1, mod_pb, gf, w1, w2, g_final2, 1, tm, last, "ffn_prompt")

        (sak, sav, saki, sbk, sbv, sck, scv, slf, qa, qi, wi, qb, qc, slfrep) = _inproj_sample(
            xs, mod_sa, gm, ws, bfb, G_SAMPLE)
        rows_s.append((sak, sav, saki, sbk, sbv, sck, scv, slf))
        lft_new = jnp.swapaxes(slfrep.reshape(nbs, dec_seq, LANES)[:, :, 0:8], 1, 2)
        lft_new = jnp.concatenate([lft_new, jnp.zeros((nbs, 8, LANES - dec_seq), F32)], axis=2)
        lane_head = jnp.arange(8)[None, :, None] < H_C
        lft_new = jnp.where(lane_head, lft_new, 0.0)
        oa = _decode_a(page_table, l, qa, qi, wi, sak, sav, saki, a_kt, a_vt, a_kit, bias_dec, dec_seq)
        ob = _decode_b(page_table, l, qb, sbk, sbv, b_kt, b_v2, bias_dec, lamvec, g_subln[l].reshape(1, 2 * HD),
                       lam_init, dec_seq)
        oc = _decode_c(page_table, l, qc, sck, scv, lft_new, c_kt, c_vt, c_lft, dec_seq)
        x1s = _merge(xs, mod_sa, gm, oa, ob, oc, wg, wpa_s, wpb, wpc_s, wo, G_SAMPLE, dec_seq, False)
        xs = _ffn(x1s, mod_sb, gf, w1, w2, g_final2, G_SAMPLE, dec_seq, last, "ffn_sample")

    def stack(rows, i):
        return jnp.stack([r[i] for r in rows])

    def stack_sample(i, shape):
        return stack(rows_s, i).reshape((depth, nbs, dec_seq) + shape)

    def stack_prompt_t(i, shape):
        n = len(shape)
        y = stack(rows_p, i).reshape((depth, nbp) + shape + (seq,))
        return jnp.transpose(y, (0, 1, n + 2) + tuple(range(2, n + 2)))

    out_p = (stack_prompt_t(0, (KVH_A, HD)), stack_prompt_t(1, (KVH_A, HD)), stack_prompt_t(2, (D_IDX,)),
             stack_prompt_t(3, (KVH_B, 2, HD)), stack(rows_p, 4).reshape(depth, nbp, seq, KVH_B, 2 * HD),
             stack_prompt_t(5, (KVH_C, HD)), stack_prompt_t(6, (KVH_C, HD)), stack(rows_p, 7))
    out_s = (stack_sample(0, (KVH_A, HD)), stack_sample(1, (KVH_A, HD)), stack_sample(2, (D_IDX,)),
             stack_sample(3, (KVH_B, 2, HD)), stack_sample(4, (KVH_B, 2 * HD)), stack_sample(5, (KVH_C, HD)),
             stack_sample(6, (KVH_C, HD)), stack_sample(7, (H_C,)))
    return (xp, xs) + out_p + out_s
```

```python
import functools
import math

import numpy as np
import jax
import jax.numpy as jnp
from jax import lax
from jax.experimental import pallas as pl
from jax.experimental.pallas import tpu as pltpu

F32 = jnp.float32
BF16 = jnp.bfloat16
I32 = jnp.int32

D = 1024
HD = 64
H_A, KVH_A = 6, 2
H_IDX, D_IDX = 8, 64
K_TOP_MAX = 256
IDX_W_SCALE = (H_IDX ** -0.5) * (D_IDX ** -0.5)
H_B, KVH_B = 4, 2
H_C, KVH_C = 6, 2
N_BUCKETS, T5_MAX_EXACT, T5_MAX_DIST = 32, 16, 128
D_FF = 4 * D
EPS = 1e-6
N_HEADS_T5 = H_A + H_B
PAGE = 128

LANES = 128
TQ = 256
TKC = 128
Q_BLOCKS = TQ // TKC
NEAR_BLOCKS = Q_BLOCKS + 1
FAR_BLOCKS = 4
NEG = -1e30
INT_MIN = -2 ** 31
VMEM_LIMIT = 56 * 1024 * 1024

_W = dict(aq=H_A * HD, ak=KVH_A * HD, av=KVH_A * HD, aqi=H_IDX * D_IDX, aki=D_IDX, awi=H_IDX,
          bq=H_B * 2 * HD, bk=KVH_B * 2 * HD, bv=KVH_B * 2 * HD,
          cq=H_C * HD, ck=KVH_C * HD, cv=KVH_C * HD, cf=H_C)
_OFF = {}
_o = 0
for _k, _v in _W.items():
    _OFF[_k] = _o
    _o += _v

_R_AK, _R_AV, _R_AKI, _R_BK, _R_BV, _R_CK, _R_CV, _R_CF = 0, 128, 256, 384, 640, 896, 1024, 1152
N_ROWS = 1280
_P_AK, _P_AKI, _P_BK, _P_CK, _P_CF, _P_BV = 0, 128, 256, 512, 640, 768
N_PROWS = 1024
_T_AQ, _T_AQI, _T_AWI, _T_BQ, _T_CQ, _T_AV, _T_BV, _T_CV = 0, 384, 896, 912, 1424, 1808, 1936, 2192
_T_AK, _T_AKI, _T_BK, _T_CK = 2320, 2448, 2512, 2768
N_T = 2896
_S_AQ, _S_AQI, _S_AWI, _S_BQ, _S_CQ = 0, 768, 1792, 2816, 4864
N_SQ = 5632


def _cparams(*sem):
    return pltpu.CompilerParams(dimension_semantics=sem, vmem_limit_bytes=VMEM_LIMIT)


def _dot(a, b):
    return jnp.dot(a, b, preferred_element_type=F32)


def _dot_nt(a, b):
    return lax.dot_general(a, b, (((1,), (1,)), ((), ())), preferred_element_type=F32)


def _split3(x):
    hi = x.astype(BF16)
    r1 = x - hi.astype(F32)
    mid = r1.astype(BF16)
    lo = (r1 - mid.astype(F32)).astype(BF16)
    return hi, mid, lo


def _dot3_rhs(a_bf16, x):
    hi, mid, lo = _split3(x)
    return _dot(a_bf16, hi) + _dot(a_bf16, mid) + _dot(a_bf16, lo)


def _dot3_lhs(x, b_bf16):
    hi, mid, lo = _split3(x)
    return _dot(hi, b_bf16) + _dot(mid, b_bf16) + _dot(lo, b_bf16)


def _norm_mod(x, g, sc, sh):
    ms = jnp.mean(x * x, axis=-1, keepdims=True)
    return (x * lax.rsqrt(ms + EPS) * g) * (1.0 + sc) + sh


def _log_sigmoid(z):
    return jnp.minimum(z, 0.0) - jnp.log1p(jnp.exp(-jnp.abs(z)))


def _sigmoid(z):
    return 1.0 / (1.0 + jnp.exp(-z))


def _sortable_key(x):
    bits = lax.bitcast_convert_type(x, I32)
    return bits ^ ((bits >> 31) & 0x7FFFFFFF)


def _ada_kernel(c_ref, w_ref, b_ref, o_ref):
    c = c_ref[...]
    s = (c * _sigmoid(c)).astype(BF16)
    o_ref[0] = _dot(s, w_ref[0].astype(BF16)) + b_ref[0]


def _ada(c_all, w_ada, b_ada):
    depth = w_ada.shape[0]
    nc = c_all.shape[0]
    nt = 6
    return pl.pallas_call(
        _ada_kernel,
        grid=(depth, nt),
        in_specs=[pl.BlockSpec((nc, D), lambda l, j: (0, 0)),
                  pl.BlockSpec((1, D, D), lambda l, j: (l, 0, j)),
                  pl.BlockSpec((1, 1, D), lambda l, j: (l, 0, j))],
        out_specs=pl.BlockSpec((1, nc, D), lambda l, j: (l, 0, j)),
        out_shape=jax.ShapeDtypeStruct((depth, nc, 6 * D), F32),
        compiler_params=_cparams("arbitrary", "arbitrary"),
        name="ada",
    )(c_all, w_ada, b_ada.reshape(depth, 1, 6 * D))


def _t5_bucket_np(rel):
    n = np.maximum(rel, 0)
    nf = np.maximum(n, 1).astype(np.float32)
    large = T5_MAX_EXACT + (np.log(nf / np.float32(T5_MAX_EXACT)) / np.float32(math.log(T5_MAX_DIST / T5_MAX_EXACT))
                            * np.float32(N_BUCKETS - T5_MAX_EXACT)).astype(np.int32)
    return np.where(n < T5_MAX_EXACT, n, np.minimum(large, N_BUCKETS - 1)).astype(np.int32)


def _bias_bucket_tables(past_len, dec_seq):
    s = np.arange(TKC)[:, None]
    t = np.arange(TQ)[None, :]
    tiles = []
    for u in range(NEAR_BLOCKS):
        rel = t - s - (TQ - TKC) + u * TKC
        tiles.append(np.where(rel >= 0, _t5_bucket_np(rel), -1))
    prompt = np.stack(tiles).astype(np.int32)
    i = np.arange(dec_seq)[:, None]
    lane = np.arange(PAGE)[None, :]
    rel_last = (past_len + i) - (past_len - PAGE + lane)
    rel_new = i - lane
    dl = _t5_bucket_np(rel_last)
    dn = np.where((rel_new >= 0) & (lane < dec_seq), _t5_bucket_np(rel_new), -1)
    dec = np.stack([dl, dn]).astype(np.int32)
    return prompt, dec


def _bias_kernel(tab_ref, pb_ref, db_ref, pt_ref, dt_ref):
    pb = pb_ref[...]
    db = db_ref[...]
    for h in range(N_HEADS_T5):
        def lut(bk):
            acc = jnp.zeros(bk.shape, F32)
            for b in range(N_BUCKETS):
                acc = jnp.where(bk == b, tab_ref[b, h], acc)
            return jnp.where(bk < 0, NEG, acc - tab_ref[N_BUCKETS - 1, h])
        pt_ref[h] = lut(pb)
        dt_ref[h] = lut(db)


def _bias_tiles(t5_table, past_len, dec_seq):
    pb, db = _bias_bucket_tables(past_len, dec_seq)
    return pl.pallas_call(
        _bias_kernel,
        in_specs=[pl.BlockSpec(memory_space=pltpu.SMEM),
                  pl.BlockSpec(memory_space=pltpu.VMEM),
                  pl.BlockSpec(memory_space=pltpu.VMEM)],
        out_specs=[pl.BlockSpec(memory_space=pltpu.VMEM), pl.BlockSpec(memory_space=pltpu.VMEM)],
        out_shape=[jax.ShapeDtypeStruct((N_HEADS_T5,) + pb.shape, F32),
                   jax.ShapeDtypeStruct((N_HEADS_T5,) + db.shape, F32)],
        name="t5_bias_tiles",
    )(t5_table, jnp.asarray(pb), jnp.asarray(db))


def _cf_block(cf):
    n = cf.shape[0]
    z2 = jnp.zeros((n, 2), cf.dtype)
    return jnp.concatenate([cf, z2, cf, z2, cf, z2, jnp.zeros((n, LANES - 24), cf.dtype)], axis=1)


class _Plan:
    def __init__(self):
        self.idx, self.scale = [], []

    def seg(self, name, lo=0, hi=None, scale=1.0):
        hi = _W[name] if hi is None else hi
        return self.raw(np.arange(_OFF[name] + lo, _OFF[name] + hi), scale)

    def raw(self, idx, scale=1.0):
        self.idx.append(np.asarray(idx, np.int32))
        self.scale.append(np.full(len(idx), scale, np.float32))
        return self

    def zeros(self, n):
        return self.raw(np.zeros(n, np.int32), 0.0)

    def cf_block(self):
        for _ in range(3):
            self.seg("cf").zeros(8 - H_C)
        return self.zeros(LANES - 24)

    def padded_heads(self, name, n_heads, slot_of, n_slots, scale=1.0):
        for h in range(n_heads):
            s = slot_of(h)
            self.zeros(s * HD).seg(name, h * HD, (h + 1) * HD, scale).zeros((n_slots - 1 - s) * HD)
        return self

    def rows_block(self):
        return (self.seg("ak").seg("av").seg("aki").zeros(64).seg("bk").seg("bv").seg("ck").seg("cv").cf_block())

    def gather(self, w, axis):
        idx = np.concatenate(self.idx)
        scale = np.concatenate(self.scale).reshape([-1 if a == axis else 1 for a in range(w.ndim)])
        return (jnp.take(w, jnp.asarray(idx), axis=axis, mode="clip") * jnp.asarray(scale)).astype(BF16)


_QSCALE = HD ** -0.5


def _prep_in_prompt(w_in):
    w_t = (_Plan().seg("aq", scale=_QSCALE).seg("aqi").seg("awi").zeros(8).seg("bq", scale=_QSCALE)
           .seg("cq", scale=_QSCALE).seg("av").seg("bv").seg("cv").seg("ak").seg("aki").seg("bk").seg("ck"))
    w_r = _Plan().seg("ak").seg("aki").zeros(64).seg("bk").seg("ck").cf_block().seg("bv")
    return w_r.gather(w_in, 2), jnp.swapaxes(w_t.gather(w_in, 2), 1, 2)


def _prep_in_sample(w_in):
    plan = _Plan()
    plan.padded_heads("aq", H_A, lambda h: h // (H_A // KVH_A), 2, _QSCALE)
    plan.padded_heads("aqi", H_IDX, lambda h: 0, 2)
    plan.raw(np.repeat(np.arange(_OFF["awi"], _OFF["awi"] + H_IDX), LANES))
    plan.padded_heads("bq", 2 * H_B, lambda hc: (hc // 2 // (H_B // KVH_B)) * 2 + hc % 2, 4, _QSCALE)
    plan.padded_heads("cq", H_C, lambda h: h // (H_C // KVH_C), 2, _QSCALE)
    return plan.rows_block().gather(w_in, 2)


def _pad_proj_rows(wp, n_heads, kv_of):
    plan = _Plan()
    for h in range(n_heads):
        s = kv_of(h)
        plan.zeros(s * HD).raw(np.arange(h * HD, (h + 1) * HD)).zeros((1 - s) * HD)
    return plan.gather(wp, 1)


def _store_rows(pr, bf, ak_o, av_o, aki_o, bk_o, bv_o, ck_o, cv_o, lf_o, idx):
    ak_o[idx] = pr[:, _R_AK:_R_AK + 128]
    av_o[idx] = pr[:, _R_AV:_R_AV + 128]
    aki_o[idx] = pr[:, _R_AKI:_R_AKI + 64]
    bk_o[idx] = pr[:, _R_BK:_R_BK + 256]
    bv_o[idx] = pr[:, _R_BV:_R_BV + 256]
    ck_o[idx] = pr[:, _R_CK:_R_CK + 128]
    cv_o[idx] = pr[:, _R_CV:_R_CV + 128]
    lf = _log_sigmoid(pr[:, _R_CF:_R_CF + 128] + bf)
    lf_o[idx] = lf[:, 0:H_C]
    return lf


N_STACKED = 8


def _inproj_p_kernel(x_ref, mod_ref, g_ref, wr_ref, wt_ref, bf_ref, *refs):
    (bv_o, lf_o, akT_o, avT_o, akiT_o, bkT_o, ckT_o, cvT_o,
     akb_o, akib_o, bkb_o, ckb_o, lfrep_o, awiT_o,
     aqT_o, aqiT_o, bqT_o, cqT_o, avTb_o, bvTb_o, cvTb_o) = refs[N_STACKED:]
    x = x_ref[0]
    mod = mod_ref[0]
    h = _norm_mod(x, g_ref[...], mod[:, D:2 * D], mod[:, 0:D])
    pr = _dot(h.astype(BF16), wr_ref[...])
    bv_o[0] = pr[:, _P_BV:_P_BV + 256]
    lf = _log_sigmoid(pr[:, _P_CF:_P_CF + 128] + bf_ref[...])
    lf_o[0] = lf[:, 0:H_C]
    lfrep_o[0] = lf
    akb_o[0] = pr[:, _P_AK:_P_AK + 128].astype(BF16)
    akib_o[0] = pr[:, _P_AKI:_P_AKI + 64].astype(BF16)
    bkb_o[0] = pr[:, _P_BK:_P_BK + 256].astype(BF16)
    ckb_o[0] = pr[:, _P_CK:_P_CK + 128].astype(BF16)
    pt = _dot(wt_ref[...], h.T.astype(BF16))
    akT_o[0] = pt[_T_AK:_T_AK + 128]
    avT_o[0] = pt[_T_AV:_T_AV + 128]
    akiT_o[0] = pt[_T_AKI:_T_AKI + 64]
    bkT_o[0] = pt[_T_BK:_T_BK + 256]
    ckT_o[0] = pt[_T_CK:_T_CK + 128]
    cvT_o[0] = pt[_T_CV:_T_CV + 128]
    awiT_o[0] = pt[_T_AWI:_T_AWI + 16]
    aqT_o[0] = pt[_T_AQ:_T_AQ + 384].astype(BF16)
    aqiT_o[0] = pt[_T_AQI:_T_AQI + 512].astype(BF16)
    bqT_o[0] = pt[_T_BQ:_T_BQ + 512].astype(BF16)
    cqT_o[0] = pt[_T_CQ:_T_CQ + 384].astype(BF16)
    avTb_o[0] = pt[_T_AV:_T_AV + 128].astype(BF16)
    bvTb_o[0] = pt[_T_BV:_T_BV + 256].astype(BF16)
    cvTb_o[0] = pt[_T_CV:_T_CV + 128].astype(BF16)


def _layer_spec(a, layer):
    return pl.BlockSpec((None,) + a.shape[1:], lambda *_: (layer,) + (0,) * (a.ndim - 1))


def _stacked_row_shapes(depth, b, l):
    return ([jax.ShapeDtypeStruct((depth, b, l, w), F32) for w in (256, H_C)]
            + [jax.ShapeDtypeStruct((depth, b, r, l), F32) for r in (128, 128, 64, 256, 128, 128)])


def _inproj_prompt(x, mod_a, g, wr, wt, bfb, tm, layer, stacked):
    b, l, _ = x.shape
    row = lambda w, dt: jax.ShapeDtypeStruct((b, l, w), dt)
    tr = lambda r, dt: jax.ShapeDtypeStruct((b, r, l), dt)
    row_spec = lambda w: pl.BlockSpec((1, tm, w), lambda i, j: (i, j, 0))
    tr_spec = lambda r: pl.BlockSpec((1, r, tm), lambda i, j: (i, 0, j))
    st_specs = ([pl.BlockSpec((None, 1, tm, a.shape[3]), lambda i, j: (layer, i, j, 0)) for a in stacked[:2]]
                + [pl.BlockSpec((None, 1, a.shape[2], tm), lambda i, j: (layer, i, 0, j)) for a in stacked[2:]])
    rows = [(128, BF16), (64, BF16), (256, BF16), (128, BF16), (128, F32)]
    trs = [(16, F32), (384, BF16), (512, BF16), (512, BF16), (384, BF16), (128, BF16), (256, BF16), (128, BF16)]
    n_in = 6
    outs = pl.pallas_call(
        _inproj_p_kernel,
        grid=(b, l // tm),
        in_specs=[pl.BlockSpec((1, tm, D), lambda i, j: (i, j, 0)),
                  pl.BlockSpec((1, 1, 3 * D), lambda i, j: (i, 0, 0)),
                  pl.BlockSpec((1, D), lambda i, j: (0, 0)),
                  _layer_spec(wr, layer), _layer_spec(wt, layer),
                  pl.BlockSpec((1, LANES), lambda i, j: (0, 0))] + [_any_spec() for _ in stacked],
        out_specs=st_specs + [row_spec(w) for w, _ in rows] + [tr_spec(r) for r, _ in trs],
        out_shape=[jax.ShapeDtypeStruct(a.shape, a.dtype) for a in stacked]
        + [row(w, dt) for w, dt in rows] + [tr(r, dt) for r, dt in trs],
        input_output_aliases={n_in + k: k for k in range(N_STACKED)},
        compiler_params=_cparams("arbitrary", "arbitrary"),
        name="inproj_prompt",
    )(x, mod_a, g, wr, wt, bfb, *stacked)
    return outs[:N_STACKED], outs[N_STACKED:]


def _inproj_s_kernel(x_ref, mod_ref, g_ref, w_ref, bf_ref,
                     ak_o, av_o, aki_o, bk_o, bv_o, ck_o, cv_o, lf_o,
                     qa_o, qi_o, wi_o, qb_o, qc_o, lfrep_o):
    x = x_ref[...]
    gsz, r, _ = x.shape
    mod = mod_ref[...]
    h = _norm_mod(x, g_ref[...], mod[:, :, D:2 * D], mod[:, :, 0:D]).reshape(gsz * r, D)
    pr = _dot(h.astype(BF16), w_ref[...])
    lf = _store_rows(pr[:, N_SQ:], bf_ref[...], ak_o, av_o, aki_o, bk_o, bv_o, ck_o, cv_o, lf_o,
                     (slice(None), slice(None)))
    lfrep_o[...] = lf
    qa_o[...] = pr[:, _S_AQ:_S_AQ + 768].astype(BF16)
    qi_o[...] = pr[:, _S_AQI:_S_AQI + 1024].astype(BF16)
    wi_o[...] = pr[:, _S_AWI:_S_AWI + 1024]
    qb_o[...] = pr[:, _S_BQ:_S_BQ + 2048].astype(BF16)
    qc_o[...] = pr[:, _S_CQ:_S_CQ + 768].astype(BF16)


def _inproj_sample(x, mod_a, g, w, bfb, gsz, layer):
    nb, r, _ = x.shape
    n = nb * r
    tm = gsz * r
    row_w = [128, 128, 64, 256, 256, 128, 128, H_C]
    outs = [(w_, F32) for w_ in row_w] + [(768, BF16), (1024, BF16), (1024, F32), (2048, BF16), (768, BF16), (128, F32)]
    return pl.pallas_call(
        _inproj_s_kernel,
        grid=(nb // gsz,),
        in_specs=[pl.BlockSpec((gsz, r, D), lambda i: (i, 0, 0)),
                  pl.BlockSpec((gsz, 1, 3 * D), lambda i: (i, 0, 0)),
                  pl.BlockSpec((1, D), lambda i: (0, 0)),
                  _layer_spec(w, layer),
                  pl.BlockSpec((1, LANES), lambda i: (0, 0))],
        out_specs=[pl.BlockSpec((tm, w_), lambda i: (i, 0)) for w_, _ in outs],
        out_shape=[jax.ShapeDtypeStruct((n, w_), dt) for w_, dt in outs],
        compiler_params=_cparams("arbitrary"),
        name="inproj_sample",
    )(x, mod_a, g, w, bfb)


_CUM_T = 256


def _cum_kernel(lf_ref, k_ref, o_ref):
    l = lf_ref.shape[1]
    r = lax.broadcasted_iota(I32, (_CUM_T, _CUM_T), 0)
    c = lax.broadcasted_iota(I32, (_CUM_T, _CUM_T), 1)
    tri = jnp.where(c <= r, 1.0, 0.0).astype(BF16)
    lane = lax.broadcasted_iota(I32, (_CUM_T, LANES), 1)
    carry = jnp.zeros((1, LANES), F32)
    for i in range(l // _CUM_T):
        rows = slice(i * _CUM_T, (i + 1) * _CUM_T)
        cum = _dot3_rhs(tri, lf_ref[0, rows, :]) + carry
        carry = cum[_CUM_T - 1:_CUM_T, :]
        hi, mid, lo = _split3(cum)
        piece = jnp.where(lane < 8, hi, jnp.where(lane < 16, mid, lo))
        o_ref[0, rows, 0:LANES] = k_ref[0, rows, :]
        o_ref[0, rows, LANES:2 * LANES] = -piece


def _cum_prompt(lfrep, ckb):
    b, l, _ = lfrep.shape
    return pl.pallas_call(
        _cum_kernel,
        grid=(b,),
        in_specs=[pl.BlockSpec((1, l, LANES), lambda i: (i, 0, 0)),
                  pl.BlockSpec((1, l, LANES), lambda i: (i, 0, 0))],
        out_specs=pl.BlockSpec((1, l, 2 * LANES), lambda i: (i, 0, 0)),
        out_shape=jax.ShapeDtypeStruct((b, l, 2 * LANES), BF16),
        compiler_params=_cparams("arbitrary"),
        name="forget_cumsum",
    )(lfrep, ckb)


def _online_update(s, m_scr, acc_scr, vta):
    m_prev = m_scr[...]
    m_new = jnp.maximum(m_prev, jnp.max(s, axis=0, keepdims=True))
    alpha = jnp.exp(m_prev - m_new)
    p = jnp.exp(s - m_new).astype(BF16)
    m_scr[...] = m_new
    n_g = len(vta)
    w = s.shape[1] // n_g
    for g in range(n_g):
        pv = _dot(vta[g], p[:, g * w:(g + 1) * w])
        acc_scr[g] = acc_scr[g] * alpha[:, g * w:(g + 1) * w] + pv


def _key_rows(start, n):
    return pl.ds(pl.multiple_of(start * TKC, TKC), n * TKC)


def _v_aug(vt_ref, start, n, g, rows):
    v = vt_ref[0, g * rows:(g + 1) * rows, _key_rows(start, n)]
    return jnp.concatenate([v, jnp.ones((16, n * TKC), BF16)], axis=0)


def _add_near(s, n, tiles):
    parts = []
    first_near = max(n - len(tiles), 0)
    if first_near > 0:
        parts.append(s[0:first_near * TKC])
    for blk in range(first_near, n):
        tile = tiles[n - 1 - blk]
        rows = s[blk * TKC:(blk + 1) * TKC]
        parts.append(rows if tile is None else rows + tile)
    return parts[0] if len(parts) == 1 else jnp.concatenate(parts, axis=0)


def _chunk_loop(qi, chunk):
    n_tot = (qi + 1) * Q_BLOCKS
    n_big = jnp.maximum(n_tot - NEAR_BLOCKS, 0) // FAR_BLOCKS

    def far(c, carry):
        chunk(c * FAR_BLOCKS, FAR_BLOCKS, False)
        return carry
    lax.fori_loop(0, n_big, far, 0)
    start = n_big * FAR_BLOCKS
    n_tail = n_tot - start
    for v in range(Q_BLOCKS, FAR_BLOCKS + NEAR_BLOCKS, Q_BLOCKS):
        @pl.when(n_tail == v)
        def _(v=v):
            chunk(start, v, True)


def _attn_a_kernel(qT_ref, qiT_ref, wT_ref, k_ref, ki_ref, vT_ref, bias_ref, oT_ref,
                   qpad, qipad, key_scr, am_scr, j_scr, m_scr, acc_scr, *, k_top):
    b = pl.program_id(0)
    qi = pl.program_id(1)
    n_chunk = (qi + 1) * Q_BLOCKS
    grp = H_A // KVH_A

    @pl.when((b == 0) & (qi == 0))
    def _():
        qpad[...] = jnp.zeros_like(qpad)
    for h in range(H_A):
        g = h // grp
        qpad[g * HD:(g + 1) * HD, h * TQ:(h + 1) * TQ] = qT_ref[0, h * HD:(h + 1) * HD, :]
    for h in range(H_IDX):
        qipad[:, h * TQ:(h + 1) * TQ] = qiT_ref[0, h * D_IDX:(h + 1) * D_IDX, :]
    w = wT_ref[0, 0:H_IDX, :] * IDX_W_SCALE
    grows = FAR_BLOCKS * TKC
    n_grp = (n_chunk + FAR_BLOCKS - 1) // FAR_BLOCKS
    row = lax.broadcasted_iota(I32, (TKC, TQ), 0)
    col = lax.broadcasted_iota(I32, (TKC, TQ), 1) + qi * TQ
    grow = lax.broadcasted_iota(I32, (grows, TQ), 0)

    def score_group(gi, carry):
        for bi in range(FAR_BLOCKS):
            start = pl.multiple_of(gi * grows + bi * TKC, TKC)
            sl = pl.ds(start, TKC)
            s = _dot(ki_ref[0, sl, :], qipad[...])
            sc = jnp.zeros((TKC, TQ), F32)
            for h in range(H_IDX):
                sc = sc + jnp.maximum(s[:, h * TQ:(h + 1) * TQ], 0.0) * w[h:h + 1, :]
            sc = jnp.where(sc == 0.0, 0.0, sc)
            sc = jnp.where(row + start > col, -jnp.inf, sc)
            key_scr[sl, :] = _sortable_key(sc)
        return carry
    lax.fori_loop(0, n_grp, score_group, 0)

    def count(pred):
        def body(gi, cnt):
            start = pl.multiple_of(gi * grows, grows)
            hit = jnp.where(pred(key_scr[pl.ds(start, grows), :], start), 1, 0).astype(I32)
            return cnt + jnp.sum(hit.reshape(grows // 8, 8, TQ), axis=0)
        cnt8 = lax.fori_loop(0, n_grp, body, jnp.zeros((8, TQ), I32))
        return jnp.sum(cnt8, axis=0, keepdims=True)

    needs_search = (qi + 1) * TQ > k_top

    @pl.when(jnp.logical_not(needs_search))
    def _():
        def body(gi, carry):
            am_scr[pl.ds(pl.multiple_of(gi * grows, grows), grows), :] = jnp.zeros((grows, TQ), F32)
            return carry
        lax.fori_loop(0, n_grp, body, 0)

    @pl.when(needs_search)
    def _():
        def it_body(it, t):
            cand = t ^ lax.shift_left(jnp.int32(1), 31 - it)
            cnt = count(lambda k, start: k >= cand)
            return jnp.where(cnt >= k_top, cand, t)
        thr = lax.fori_loop(0, 32, it_body, jnp.full((1, TQ), INT_MIN, I32))
        cnt_gt = count(lambda k, start: k > thr)
        cnt_eq = count(lambda k, start: k == thr)
        need = k_top - cnt_gt
        n_bits = max(1, (key_scr.shape[0] - 1).bit_length())
        j_scr[...] = jnp.full((1, TQ), key_scr.shape[0], I32)

        @pl.when(jnp.max(jnp.where(cnt_eq > need, 1, 0)) > 0)
        def _():
            def tie_body(it, jv):
                cand = jv | lax.shift_left(jnp.int32(1), n_bits - 1 - it)
                cnt = count(lambda k, start: (k == thr) & ((grow + start) < cand))
                return jnp.where(cnt < need, cand, jv)
            j_scr[...] = lax.fori_loop(0, n_bits, tie_body, jnp.zeros((1, TQ), I32))
        jv = j_scr[...]

        def body(gi, carry):
            start = pl.multiple_of(gi * grows, grows)
            k = key_scr[pl.ds(start, grows), :]
            sel = (k > thr) | ((k == thr) & ((grow + start) <= jv))
            am_scr[pl.ds(start, grows), :] = jnp.where(sel, 0.0, NEG)
            return carry
        lax.fori_loop(0, n_grp, body, 0)

    m_scr[...] = jnp.full(m_scr.shape, NEG, F32)
    acc_scr[...] = jnp.zeros_like(acc_scr)

    def chunk(start, n, near):
        sl = _key_rows(start, n)
        s = _dot(k_ref[0, sl, :], qpad[...])
        s = s + jnp.concatenate([am_scr[sl, :]] * H_A, axis=1)
        if near:
            s = _add_near(s, n, [jnp.concatenate([bias_ref[h, u] for h in range(H_A)], axis=1)
                                 for u in range(NEAR_BLOCKS)])
        _online_update(s, m_scr, acc_scr, [_v_aug(vT_ref, start, n, g, HD) for g in range(KVH_A)])

    _chunk_loop(qi, chunk)

    for h in range(H_A):
        g, hh = h // grp, h % grp
        a = acc_scr[g][:, hh * TQ:(hh + 1) * TQ]
        oT_ref[0, h * HD:(h + 1) * HD, :] = a[0:HD] / a[HD:HD + 1]


def _attn_a(aqT, aqiT, awiT, akb, akib, avT, bias_pt):
    b, _, l = aqT.shape
    k_top = min(K_TOP_MAX, l // 4)
    blk_t = lambda r: pl.BlockSpec((1, r, TQ), lambda i, j: (i, 0, j))
    full = lambda s1, s2: pl.BlockSpec((1, s1, s2), lambda i, j: (i, 0, 0))
    return pl.pallas_call(
        functools.partial(_attn_a_kernel, k_top=k_top),
        grid=(b, l // TQ),
        in_specs=[blk_t(384), blk_t(512), blk_t(16), full(l, 128), full(l, 64), full(128, l),
                  pl.BlockSpec((H_A, NEAR_BLOCKS, TKC, TQ), lambda i, j: (0, 0, 0, 0))],
        out_specs=blk_t(384),
        out_shape=jax.ShapeDtypeStruct((b, 384, l), F32),
        scratch_shapes=[pltpu.VMEM((128, H_A * TQ), BF16),
                        pltpu.VMEM((D_IDX, H_IDX * TQ), BF16),
                        pltpu.VMEM((l, TQ), I32),
                        pltpu.VMEM((l, TQ), F32),
                        pltpu.VMEM((1, TQ), I32),
                        pltpu.VMEM((1, H_A * TQ), F32),
                        pltpu.VMEM((KVH_A, HD + 16, (H_A // KVH_A) * TQ), F32)],
        compiler_params=_cparams("arbitrary", "arbitrary"),
        name="attn_dsa_prompt",
    )(aqT, aqiT, awiT, akb, akib, avT, bias_pt)


def _lambda_value(lam_ref, lam_init):
    lv = lam_ref[...]
    s1 = jnp.sum(lv[0:1] * lv[1:2], axis=1, keepdims=True)
    s2 = jnp.sum(lv[2:3] * lv[3:4], axis=1, keepdims=True)
    return jnp.exp(s1) - jnp.exp(s2) + lam_init


def _attn_b_kernel(qT_ref, k_ref, vT_ref, bias_ref, lam_ref, gsub_ref, oT_ref,
                   qpad, m_scr, acc_scr, *, lam_init):
    b = pl.program_id(0)
    qi = pl.program_id(1)
    grp = H_B // KVH_B

    @pl.when((b == 0) & (qi == 0))
    def _():
        qpad[...] = jnp.zeros_like(qpad)
    for h in range(H_B):
        for c in range(2):
            slot = (h // grp) * 2 + c
            hc = h * 2 + c
            qpad[slot * HD:(slot + 1) * HD, hc * TQ:(hc + 1) * TQ] = qT_ref[0, hc * HD:(hc + 1) * HD, :]

    m_scr[...] = jnp.full(m_scr.shape, NEG, F32)
    acc_scr[...] = jnp.zeros_like(acc_scr)

    def chunk(start, n, near):
        s = _dot(k_ref[0, _key_rows(start, n), :], qpad[...])
        if near:
            s = _add_near(s, n, [jnp.concatenate([bias_ref[hc // 2, u] for hc in range(2 * H_B)], axis=1)
                                 for u in range(NEAR_BLOCKS)])
        _online_update(s, m_scr, acc_scr, [_v_aug(vT_ref, start, n, g, 2 * HD) for g in range(KVH_B)])

    _chunk_loop(qi, chunk)

    lam = _lambda_value(lam_ref, lam_init)
    gs = gsub_ref[...]
    for h in range(H_B):
        g, hh = h // grp, h % grp
        a1 = acc_scr[g][:, (hh * 2) * TQ:(hh * 2 + 1) * TQ]
        a2 = acc_scr[g][:, (hh * 2 + 1) * TQ:(hh * 2 + 2) * TQ]
        o = a1[0:2 * HD] / a1[2 * HD:2 * HD + 1] - lam * (a2[0:2 * HD] / a2[2 * HD:2 * HD + 1])
        ms = jnp.mean(o * o, axis=0, keepdims=True)
        oT_ref[0, h * 2 * HD:(h + 1) * 2 * HD, :] = (o * lax.rsqrt(ms + EPS) * gs) * (1.0 - lam_init)


def _attn_b(bqT, bkb, bvT, bias_pt, lamvec, gsub_col, lam_init):
    b, _, l = bqT.shape
    blk_t = lambda r: pl.BlockSpec((1, r, TQ), lambda i, j: (i, 0, j))
    full = lambda s1, s2: pl.BlockSpec((1, s1, s2), lambda i, j: (i, 0, 0))
    return pl.pallas_call(
        functools.partial(_attn_b_kernel, lam_init=lam_init),
        grid=(b, l // TQ),
        in_specs=[blk_t(512), full(l, 256), full(256, l),
                  pl.BlockSpec((H_B, NEAR_BLOCKS, TKC, TQ), lambda i, j: (0, 0, 0, 0)),
                  pl.BlockSpec((4, HD), lambda i, j: (0, 0)),
                  pl.BlockSpec((2 * HD, 1), lambda i, j: (0, 0))],
        out_specs=blk_t(512),
        out_shape=jax.ShapeDtypeStruct((b, 512, l), F32),
        scratch_shapes=[pltpu.VMEM((256, 2 * H_B * TQ), BF16),
                        pltpu.VMEM((1, 2 * H_B * TQ), F32),
                        pltpu.VMEM((KVH_B, 2 * HD + 16, 2 * (H_B // KVH_B) * TQ), F32)],
        compiler_params=_cparams("arbitrary", "arbitrary"),
        name="attn_diff_prompt",
    )(bqT, bkb, bvT, bias_pt, lamvec, gsub_col)


def _attn_c_kernel(qT_ref, k_ref, vT_ref, oT_ref, qpad, m_scr, acc_scr):
    b = pl.program_id(0)
    qi = pl.program_id(1)
    grp = H_C // KVH_C

    @pl.when((b == 0) & (qi == 0))
    def _():
        r = lax.broadcasted_iota(I32, (256, H_C * TQ), 0) - 128
        cblk = lax.broadcasted_iota(I32, (256, H_C * TQ), 1) // TQ
        ones = (r >= 0) & (r < 24) & ((r % 8) == cblk)
        qpad[...] = jnp.where(ones, 1.0, 0.0).astype(BF16)
    for h in range(H_C):
        g = h // grp
        qpad[g * HD:(g + 1) * HD, h * TQ:(h + 1) * TQ] = qT_ref[0, h * HD:(h + 1) * HD, :]

    m_scr[...] = jnp.full(m_scr.shape, NEG, F32)
    acc_scr[...] = jnp.zeros_like(acc_scr)
    row = lax.broadcasted_iota(I32, (TKC, TQ), 0)
    col = lax.broadcasted_iota(I32, (TKC, TQ), 1)
    causal = [jnp.concatenate([jnp.where(row - col + (TQ - TKC) - u * TKC > 0, NEG, 0.0)] * H_C, axis=1)
              for u in range(Q_BLOCKS)] + [None]

    def chunk(start, n, near):
        s = _dot(k_ref[0, _key_rows(start, n), :], qpad[...])
        if near:
            s = _add_near(s, n, causal)
        _online_update(s, m_scr, acc_scr, [_v_aug(vT_ref, start, n, g, HD) for g in range(KVH_C)])

    _chunk_loop(qi, chunk)

    for h in range(H_C):
        g, hh = h // grp, h % grp
        a = acc_scr[g][:, hh * TQ:(hh + 1) * TQ]
        oT_ref[0, h * HD:(h + 1) * HD, :] = a[0:HD] / a[HD:HD + 1]


def _attn_c(cqT, kcat, cvT):
    b, _, l = cqT.shape
    blk_t = lambda r: pl.BlockSpec((1, r, TQ), lambda i, j: (i, 0, j))
    full = lambda s1, s2: pl.BlockSpec((1, s1, s2), lambda i, j: (i, 0, 0))
    return pl.pallas_call(
        _attn_c_kernel,
        grid=(b, l // TQ),
        in_specs=[blk_t(384), full(l, 256), full(128, l)],
        out_specs=blk_t(384),
        out_shape=jax.ShapeDtypeStruct((b, 384, l), F32),
        scratch_shapes=[pltpu.VMEM((256, H_C * TQ), BF16),
                        pltpu.VMEM((1, H_C * TQ), F32),
                        pltpu.VMEM((KVH_C, HD + 16, (H_C // KVH_C) * TQ), F32)],
        compiler_params=_cparams("arbitrary", "arbitrary"),
        name="attn_forget_prompt",
    )(cqT, kcat, cvT)


G_DEC = 4


def _stack_heads(ref, tok, n, width):
    return jnp.concatenate([ref[tok, h * width:(h + 1) * width] for h in range(n)], axis=0)


def _pad_new(x):
    return jnp.concatenate([x, jnp.zeros((PAGE - x.shape[0], x.shape[1]), x.dtype)], axis=0)


def _probabilities(s_pages):
    mx = s_pages[0]
    for s in s_pages[1:]:
        mx = jnp.maximum(mx, s)
    m = jnp.max(mx, axis=1, keepdims=True)
    p_pages = [jnp.exp(s - m) for s in s_pages]
    lsum = p_pages[0]
    for p in p_pages[1:]:
        lsum = lsum + p
    return p_pages, jnp.sum(lsum, axis=1, keepdims=True)


def _pv(p_pages, vt_pages, v_new):
    acc = _dot(p_pages[-1].astype(BF16), v_new)
    for p, vt in zip(p_pages[:-1], vt_pages):
        acc = acc + _dot_nt(p.astype(BF16), vt)
    return acc


class _PageFetch:
    def __init__(self, pt_ref, gsz, n_pages, hbms, indexers, bufs, sems):
        self.pt_ref, self.gsz, self.n_pages = pt_ref, gsz, n_pages
        self.hbms, self.indexers, self.bufs, self.sems = hbms, indexers, bufs, sems
        self.step = pl.program_id(0)
        self.last = pl.num_programs(0) - 1
        self.cur = lax.rem(self.step, 2)

    def _copy(self, c, page_id, buf, j):
        return pltpu.make_async_copy(self.hbms[c].at[self.indexers[c](page_id)], self.bufs[c].at[buf, j],
                                     self.sems.at[c, buf])

    def _start(self, step, buf, j):
        page_id = self.pt_ref[step * self.gsz + j // self.n_pages, j % self.n_pages]
        for c in range(len(self.hbms)):
            self._copy(c, page_id, buf, j).start()

    def _wait(self, buf):
        for c in range(len(self.hbms)):
            for j in range(self.gsz * self.n_pages):
                self._copy(c, 0, buf, j).wait()

    def begin(self):
        @pl.when(self.step == 0)
        def _():
            for j in range(self.gsz * self.n_pages):
                self._start(0, 0, j)
        self._wait(self.cur)

    def prefetch(self, j):
        self._start(jnp.minimum(self.step + 1, self.last), 1 - self.cur, j)

    def finish(self):
        @pl.when(self.step == self.last)
        def _():
            self._wait(1 - self.cur)

    def page(self, c, j):
        return self.bufs[c][self.cur, j]


def _any_spec():
    return pl.BlockSpec(memory_space=pl.ANY)


def _decode_a_kernel(pt_ref, qa_ref, qi_ref, wi_ref, ak_n, av_n, aki_n, bias_ref, kt_hbm, vt_hbm, kit_hbm,
                     oa_ref, kt_buf, vt_buf, kit_buf, sems, *, layer, gsz, n_pages, k_top):
    at_page = lambda pg: (layer, pg)
    fetch = _PageFetch(pt_ref, gsz, n_pages, (kt_hbm, vt_hbm, kit_hbm), (at_page,) * 3,
                       (kt_buf, vt_buf, kit_buf), sems)
    fetch.begin()
    r = qa_ref.shape[0] // gsz
    rows = gsz * r
    n_all = n_pages + 1
    lane = lax.broadcasted_iota(I32, (rows, PAGE), 1)
    qrow = lax.rem(lax.broadcasted_iota(I32, (rows, PAGE), 0), r)
    new_visible = lane <= qrow

    sc_pages = [[] for _ in range(n_all)]
    for g in range(gsz):
        tok = slice(g * r, (g + 1) * r)
        qi2 = _stack_heads(qi_ref, tok, H_IDX, 2 * D_IDX)[:, 0:D_IDX]
        wcol = _stack_heads(wi_ref, tok, H_IDX, LANES) * IDX_W_SCALE
        for p in range(n_all):
            if p < n_pages:
                fetch.prefetch(g * n_pages + p)
                z = _dot(qi2, fetch.page(2, g * n_pages + p).astype(BF16))
            else:
                z = _dot_nt(qi2, _pad_new(aki_n[tok, :]).astype(BF16))
            z = jnp.maximum(z, 0.0) * wcol
            sc = z[0:r]
            for h in range(1, H_IDX):
                sc = sc + z[h * r:(h + 1) * r]
            sc_pages[p].append(sc)
    key_pages = []
    for p in range(n_all):
        sc = jnp.concatenate(sc_pages[p], axis=0)
        sc = jnp.where(sc == 0.0, 0.0, sc)
        if p == n_pages:
            sc = jnp.where(new_visible, sc, -jnp.inf)
        key_pages.append(_sortable_key(sc))

    def count(pred):
        tot = None
        for p, k in enumerate(key_pages):
            hit = jnp.where(pred(k, p), 1.0, 0.0)
            tot = hit if tot is None else tot + hit
        return jnp.sum(tot, axis=1, keepdims=True)

    thr = jnp.full((rows, 1), INT_MIN, I32)
    for it in range(32):
        cand = thr ^ jnp.int32(-2 ** 31 if it == 0 else 1 << (31 - it))
        cnt = count(lambda k, p: k >= cand)
        thr = jnp.where(cnt >= k_top, cand, thr)
    cnt_gt = count(lambda k, p: k > thr)
    cnt_eq = count(lambda k, p: k == thr)
    need = k_top - cnt_gt
    n_bits = max(1, (n_all * PAGE - 1).bit_length())

    def tie_search():
        jv = jnp.zeros((rows, 1), I32)
        for it in range(n_bits):
            cand = jv | jnp.int32(1 << (n_bits - 1 - it))
            cnt = count(lambda k, p: (k == thr) & ((lane + p * PAGE) < cand))
            jv = jnp.where(cnt < need, cand, jv)
        return jv

    any_excess = jnp.max(jnp.where(cnt_eq > need, 1, 0)) > 0
    jv = lax.cond(any_excess, tie_search, lambda: jnp.full((rows, 1), n_all * PAGE, I32))
    am_pages = []
    for p, k in enumerate(key_pages):
        sel = (k > thr) | ((k == thr) & ((lane + p * PAGE) <= jv))
        am_pages.append(jnp.where(sel, 0.0, NEG))

    bias_last = jnp.concatenate([bias_ref[h, 0] for h in range(H_A)], axis=0)
    bias_new = jnp.concatenate([bias_ref[h, 1] for h in range(H_A)], axis=0)
    for g in range(gsz):
        tok = slice(g * r, (g + 1) * r)
        qa2 = _stack_heads(qa_ref, tok, H_A, 2 * HD)
        s_pages = []
        for p in range(n_all):
            if p < n_pages:
                s = _dot(qa2, fetch.page(0, g * n_pages + p).reshape(2 * HD, PAGE).astype(BF16))
            else:
                s = _dot_nt(qa2, _pad_new(ak_n[tok, :]).astype(BF16))
            s = s + jnp.concatenate([am_pages[p][tok]] * H_A, axis=0)
            if p == n_pages - 1:
                s = s + bias_last
            if p == n_pages:
                s = s + bias_new
            s_pages.append(s)
        p_pages, lsum = _probabilities(s_pages)
        vts = [fetch.page(1, g * n_pages + p).reshape(2 * HD, PAGE).astype(BF16) for p in range(n_pages)]
        o = _pv(p_pages, vts, _pad_new(av_n[tok, :]).astype(BF16)) / lsum
        oa_ref[tok, :] = jnp.concatenate([o[h * r:(h + 1) * r] for h in range(H_A)], axis=1)
    fetch.finish()


def _decode_call(kernel_fn, name, page_table, tok_args, tok_widths, const_args, caches, page_shapes, out_w, r):
    nb, n_pages = page_table.shape
    gsz = G_DEC
    in_specs = [pl.BlockSpec((gsz * r, w), lambda i, pt: (i, 0)) for w in tok_widths]
    in_specs += [pl.BlockSpec(a.shape, functools.partial(lambda i, pt, n: (0,) * n, n=a.ndim)) for a in const_args]
    in_specs += [_any_spec() for _ in caches]
    grid_spec = pltpu.PrefetchScalarGridSpec(
        num_scalar_prefetch=1, grid=(nb // gsz,), in_specs=in_specs,
        out_specs=pl.BlockSpec((gsz * r, out_w), lambda i, pt: (i, 0)),
        scratch_shapes=[pltpu.VMEM((2, gsz * n_pages) + s, F32) for s in page_shapes]
        + [pltpu.SemaphoreType.DMA((len(caches), 2))])
    return pl.pallas_call(
        kernel_fn,
        grid_spec=grid_spec,
        out_shape=jax.ShapeDtypeStruct((nb * r, out_w), F32),
        compiler_params=_cparams("arbitrary"),
        name=name,
    )(page_table, *tok_args, *const_args, *caches)


def _decode_a(page_table, layer, qa, qi, wi, ak_n, av_n, aki_n, kt, vt, kit, bias_dec, r):
    n_pages = page_table.shape[1]
    k_top = min(K_TOP_MAX, (n_pages * PAGE + r) // 4)
    return _decode_call(
        functools.partial(_decode_a_kernel, layer=layer, gsz=G_DEC, n_pages=n_pages, k_top=k_top), "decode_dsa",
        page_table, (qa, qi, wi, ak_n, av_n, aki_n), (768, 1024, 1024, 128, 128, 64), (bias_dec[:H_A],),
        (kt, vt, kit), ((KVH_A, HD, PAGE), (KVH_A, HD, PAGE), (D_IDX, PAGE)), 768, r)


def _decode_b_kernel(pt_ref, qb_ref, bk_n, bv_n, bias_ref, lam_ref, gsub_ref, kt_hbm, v_hbm,
                     ob_ref, kt_buf, v_buf, sems, *, layer, gsz, n_pages, lam_init):
    at_page = lambda pg: (layer, pg)
    fetch = _PageFetch(pt_ref, gsz, n_pages, (kt_hbm, v_hbm), (at_page,) * 2, (kt_buf, v_buf), sems)
    fetch.begin()
    for j in range(gsz * n_pages):
        fetch.prefetch(j)
    r = qb_ref.shape[0] // gsz
    n_all = n_pages + 1
    grp = H_B // KVH_B
    half = 2 * grp * r
    bias_last = jnp.concatenate([bias_ref[h, 0] for h in range(H_B) for _ in range(2)], axis=0)
    bias_new = jnp.concatenate([bias_ref[h, 1] for h in range(H_B) for _ in range(2)], axis=0)
    lam = _lambda_value(lam_ref, lam_init)
    gs = gsub_ref[...]
    for g in range(gsz):
        tok = slice(g * r, (g + 1) * r)
        qb2 = _stack_heads(qb_ref, tok, 2 * H_B, 4 * HD)
        s_pages = []
        for p in range(n_all):
            if p < n_pages:
                s = _dot(qb2, fetch.page(0, g * n_pages + p).reshape(4 * HD, PAGE).astype(BF16))
            else:
                s = _dot_nt(qb2, _pad_new(bk_n[tok, :]).astype(BF16))
            if p == n_pages - 1:
                s = s + bias_last
            if p == n_pages:
                s = s + bias_new
            s_pages.append(s)
        p_pages, lsum = _probabilities(s_pages)
        v_new = _pad_new(bv_n[tok, :]).astype(BF16)
        outs = []
        for kv in range(KVH_B):
            rs = slice(kv * half, (kv + 1) * half)
            acc = _dot(p_pages[-1][rs].astype(BF16), v_new[:, kv * 2 * HD:(kv + 1) * 2 * HD])
            for p in range(n_pages):
                v = v_buf[fetch.cur, g * n_pages + p, pl.ds(kv, PAGE, stride=KVH_B), :].astype(BF16)
                acc = acc + _dot(p_pages[p][rs].astype(BF16), v)
            o = acc / lsum[rs]
            for hh in range(grp):
                od = o[(2 * hh) * r:(2 * hh + 1) * r] - lam * o[(2 * hh + 1) * r:(2 * hh + 2) * r]
                ms = jnp.mean(od * od, axis=1, keepdims=True)
                outs.append((od * lax.rsqrt(ms + EPS) * gs) * (1.0 - lam_init))
        ob_ref[tok, :] = jnp.concatenate(outs, axis=1)
    fetch.finish()


def _decode_b(page_table, layer, qb, bk_n, bv_n, kt, v2, bias_dec, lamvec, gsub_row, lam_init, r):
    n_pages = page_table.shape[1]
    return _decode_call(
        functools.partial(_decode_b_kernel, layer=layer, gsz=G_DEC, n_pages=n_pages, lam_init=lam_init),
        "decode_diff", page_table, (qb, bk_n, bv_n), (2048, 256, 256), (bias_dec[H_A:], lamvec, gsub_row),
        (kt, v2), ((KVH_B, 2, HD, PAGE), (KVH_B * PAGE, 2 * HD)), 512, r)


def _decode_c_kernel(pt_ref, qc_ref, ck_n, cv_n, lft_n, kt_hbm, vt_hbm, lf_hbm,
                     oc_ref, kt_buf, vt_buf, lf_buf, sems, *, layer, gsz, n_pages):
    at_page = lambda pg: (layer, pg)
    fetch = _PageFetch(pt_ref, gsz, n_pages, (kt_hbm, vt_hbm, lf_hbm),
                       (at_page, at_page, lambda pg: (layer, slice(None), pg)), (kt_buf, vt_buf, lf_buf), sems)
    fetch.begin()
    r = qc_ref.shape[0] // gsz
    n_all = n_pages + 1
    lane = lax.broadcasted_iota(I32, (r, PAGE), 1)
    qrow = lax.broadcasted_iota(I32, (r, PAGE), 0)
    causal_new = jnp.concatenate([jnp.where(lane <= qrow, 0.0, NEG)] * H_C, axis=0)
    ri = lax.broadcasted_iota(I32, (PAGE, PAGE), 0)
    ci = lax.broadcasted_iota(I32, (PAGE, PAGE), 1)
    upper = jnp.where(ri <= ci, 1.0, 0.0).astype(BF16)
    ones = jnp.ones((PAGE, PAGE), BF16)
    nr = 8 * n_all
    rr = lax.broadcasted_iota(I32, (nr, nr), 0)
    cc = lax.broadcasted_iota(I32, (nr, nr), 1)
    prev_pages = jnp.where(((rr % 8) == (cc % 8)) & ((cc // 8) < (rr // 8)), 1.0, 0.0).astype(BF16)
    zrow = jnp.zeros((8 - H_C, PAGE), F32)
    for g in range(gsz):
        tok = slice(g * r, (g + 1) * r)
        xs = []
        for p in range(n_pages):
            fetch.prefetch(g * n_pages + p)
            xs += [fetch.page(2, g * n_pages + p), zrow]
        x = jnp.concatenate(xs + [lft_n[g * 8:(g + 1) * 8, :]], axis=0)
        cum = _dot3_lhs(x, upper) + _dot3_rhs(prev_pages, _dot3_lhs(x, ones))
        qc2 = _stack_heads(qc_ref, tok, H_C, 2 * HD)
        s_pages = []
        for p in range(n_all):
            if p < n_pages:
                s = _dot(qc2, fetch.page(0, g * n_pages + p).reshape(2 * HD, PAGE).astype(BF16))
            else:
                s = _dot_nt(qc2, _pad_new(ck_n[tok, :]).astype(BF16)) + causal_new
            decay = jnp.concatenate(
                [jnp.broadcast_to(cum[p * 8 + h:p * 8 + h + 1, :], (r, PAGE)) for h in range(H_C)], axis=0)
            s_pages.append(s - decay)
        p_pages, lsum = _probabilities(s_pages)
        vts = [fetch.page(1, g * n_pages + p).reshape(2 * HD, PAGE).astype(BF16) for p in range(n_pages)]
        o = _pv(p_pages, vts, _pad_new(cv_n[tok, :]).astype(BF16)) / lsum
        oc_ref[tok, :] = jnp.concatenate([o[h * r:(h + 1) * r] for h in range(H_C)], axis=1)
    fetch.finish()


def _decode_c(page_table, layer, qc, ck_n, cv_n, lft_new, kt, vt, lft, r):
    n_pages = page_table.shape[1]
    assert r == 8
    return _decode_call(
        functools.partial(_decode_c_kernel, layer=layer, gsz=G_DEC, n_pages=n_pages), "decode_forget",
        page_table, (qc, ck_n, cv_n, lft_new.reshape(-1, LANES)), (768, 128, 128, LANES), (),
        (kt, vt, lft), ((KVH_C, HD, PAGE), (KVH_C, HD, PAGE), (H_C, PAGE)), 768, r)


def _merge_kernel(x_ref, mod_ref, g_ref, oa_ref, ob_ref, oc_ref, wg_ref, wpa_ref, wpb_ref, wpc_ref, wo_ref,
                  o_ref, *, transposed):
    x = x_ref[...]
    gsz, r, _ = x.shape
    mod = mod_ref[...]
    h = _norm_mod(x, g_ref[...], mod[:, :, D:2 * D], mod[:, :, 0:D]).reshape(gsz * r, D).astype(BF16)
    gates = _sigmoid(_dot(h, wg_ref[...]))
    if transposed:
        oa, ob, oc = oa_ref[0].T, ob_ref[0].T, oc_ref[0].T
    else:
        oa, ob, oc = oa_ref[...], ob_ref[...], oc_ref[...]
    merged = (gates[:, 0:D] * _dot(oa.astype(BF16), wpa_ref[...])
              + gates[:, D:2 * D] * _dot(ob.astype(BF16), wpb_ref[...])
              + gates[:, 2 * D:3 * D] * _dot(oc.astype(BF16), wpc_ref[...]))
    y = _dot(merged.astype(BF16), wo_ref[...]).reshape(gsz, r, D)
    o_ref[...] = x + mod[:, :, 2 * D:3 * D] * y


def _merge(x, mod_a, g, oa, ob, oc, wg, wpa, wpb, wpc, wo, gsz, r, transposed, layer):
    nb, rr, _ = x.shape
    const = lambda a: pl.BlockSpec(a.shape, lambda i, j: (0,) * a.ndim)
    lw = lambda a: _layer_spec(a, layer)
    if transposed:
        grid = (nb, rr // r)
        x_spec = pl.BlockSpec((1, r, D), lambda i, j: (i, j, 0))
        mod_spec = pl.BlockSpec((1, 1, 3 * D), lambda i, j: (i, 0, 0))
        o_spec = lambda a: pl.BlockSpec((1, a.shape[1], r), lambda i, j: (i, 0, j))
    else:
        grid = (nb // gsz, 1)
        x_spec = pl.BlockSpec((gsz, rr, D), lambda i, j: (i, 0, 0))
        mod_spec = pl.BlockSpec((gsz, 1, 3 * D), lambda i, j: (i, 0, 0))
        o_spec = lambda a: pl.BlockSpec((gsz * rr, a.shape[1]), lambda i, j: (i, 0))
    return pl.pallas_call(
        functools.partial(_merge_kernel, transposed=transposed),
        grid=grid,
        in_specs=[x_spec, mod_spec, const(g), o_spec(oa), o_spec(ob), o_spec(oc),
                  lw(wg), lw(wpa), lw(wpb), lw(wpc), lw(wo)],
        out_specs=x_spec,
        out_shape=jax.ShapeDtypeStruct(x.shape, F32),
        compiler_params=_cparams("arbitrary", "arbitrary"),
        name="merge_prompt" if transposed else "merge_sample",
    )(x, mod_a, g, oa, ob, oc, wg, wpa, wpb, wpc, wo)


def _ffn_kernel(x_ref, mod_ref, g_ref, w1_ref, w2_ref, gf_ref, o_ref, *, final):
    x = x_ref[...]
    gsz, r, _ = x.shape
    mod = mod_ref[...]
    h = _norm_mod(x, g_ref[...], mod[:, :, D:2 * D], mod[:, :, 0:D]).reshape(gsz * r, D).astype(BF16)
    u = jnp.maximum(_dot(h, w1_ref[...]), 0.0)
    y = _dot((u * u).astype(BF16), w2_ref[...]).reshape(gsz, r, D)
    x2 = x + mod[:, :, 2 * D:3 * D] * y
    if final:
        ms = jnp.mean(x2 * x2, axis=-1, keepdims=True)
        x2 = x2 * lax.rsqrt(ms + EPS) * gf_ref[...]
    o_ref[...] = x2


def _ffn(x, mod_b, g, w1, w2, g_final, gsz, r, final, name, layer):
    nb, rr, _ = x.shape
    const = lambda a: pl.BlockSpec(a.shape, lambda i, j: (0,) * a.ndim)
    lw = lambda a: _layer_spec(a, layer)
    if gsz == 1:
        grid = (nb, rr // r)
        x_spec = pl.BlockSpec((1, r, D), lambda i, j: (i, j, 0))
        mod_spec = pl.BlockSpec((1, 1, 3 * D), lambda i, j: (i, 0, 0))
    else:
        grid = (nb // gsz, 1)
        x_spec = pl.BlockSpec((gsz, rr, D), lambda i, j: (i, 0, 0))
        mod_spec = pl.BlockSpec((gsz, 1, 3 * D), lambda i, j: (i, 0, 0))
    return pl.pallas_call(
        functools.partial(_ffn_kernel, final=final),
        grid=grid,
        in_specs=[x_spec, mod_spec, const(g), lw(w1), lw(w2), const(g_final)],
        out_specs=x_spec,
        out_shape=jax.ShapeDtypeStruct(x.shape, F32),
        compiler_params=_cparams("arbitrary", "arbitrary"),
        name=name,
    )(x, mod_b, g, w1, w2, g_final)


TM_PROMPT = 512
G_SAMPLE = 16


def kernel(x_prompt, x_sample, c_prompt, c_sample, cache_a_k, cache_a_v, cache_a_kidx, cache_b_k, cache_b_v, cache_c_k, cache_c_v, cache_c_logf, page_table, t5_table, w_ada, b_ada, g_mix, g_ffn, w_in, b_forget, lam_q1, lam_k1, lam_q2, lam_k2, g_subln, w_gate, w_pa, w_pb, w_pc, w_out, w_ff1, w_ff2, g_final):
    depth = w_in.shape[0]
    nbp, seq, _ = x_prompt.shape
    nbs, dec_seq, _ = x_sample.shape
    n_pool, page = cache_a_k.shape[1], cache_a_k.shape[2]
    n_pages = page_table.shape[1]
    past_len = n_pages * page
    assert page == PAGE and seq % max(TM_PROMPT, _CUM_T, FAR_BLOCKS * TKC) == 0 and dec_seq == 8
    assert nbs % G_SAMPLE == 0 and nbs % G_DEC == 0 and n_pool % 8 == 0
    tm = min(TM_PROMPT, seq)

    nc = nbp + nbs
    mod = _ada(jnp.concatenate([c_prompt, c_sample], axis=0), w_ada, b_ada)
    mod = mod.reshape(depth, nc, 1, 6 * D)
    bias_pt, bias_dec = _bias_tiles(t5_table, past_len, dec_seq)

    kv_t = lambda c: jnp.transpose(c, (0, 1, 3, 4, 2))
    a_kt, a_vt, c_kt, c_vt = kv_t(cache_a_k), kv_t(cache_a_v), kv_t(cache_c_k), kv_t(cache_c_v)
    a_kit = jnp.transpose(cache_a_kidx, (0, 1, 3, 2))
    b_kt = jnp.transpose(cache_b_k, (0, 1, 3, 4, 5, 2))
    b_v2 = cache_b_v.reshape(depth, n_pool, page * KVH_B, 2 * HD)
    c_lft = jnp.transpose(cache_c_logf, (0, 3, 1, 2))

    wr, wt = _prep_in_prompt(w_in)
    ws = _prep_in_sample(w_in)
    wg, wo = w_gate.astype(BF16), w_out.astype(BF16)
    wpa, wpb, wpc = w_pa.astype(BF16), w_pb.astype(BF16), w_pc.astype(BF16)
    wpa_s = _pad_proj_rows(w_pa, H_A, lambda h: h // (H_A // KVH_A))
    wpc_s = _pad_proj_rows(w_pc, H_C, lambda h: h // (H_C // KVH_C))
    w1, w2 = w_ff1.astype(BF16), w_ff2.astype(BF16)

    xp, xs = x_prompt, x_sample
    rows_s = []
    stacked = [jnp.zeros(s.shape, s.dtype) for s in _stacked_row_shapes(depth, nbp, seq)]
    g_final2 = g_final.reshape(1, D)
    for l in range(depth):
        lam_init = 0.8 - 0.6 * math.exp(-0.3 * l)
        bfb = _cf_block(b_forget[l].reshape(1, H_C))
        gm, gf = g_mix[l].reshape(1, D), g_ffn[l].reshape(1, D)
        lamvec = jnp.stack([lam_q1[l], lam_k1[l], lam_q2[l], lam_k2[l]])
        mod_pa, mod_pb = mod[l, :nbp, :, :3 * D], mod[l, :nbp, :, 3 * D:]
        mod_sa, mod_sb = mod[l, nbp:, :, :3 * D], mod[l, nbp:, :, 3 * D:]
        last = l == depth - 1

        stacked, (akb, akib, bkb, ckb, lfrep, awiT, aqT, aqiT, bqT, cqT, avT, bvT, cvT) = _inproj_prompt(
            xp, mod_pa, gm, wr, wt, bfb, tm, l, stacked)
        kcat = _cum_prompt(lfrep, ckb)
        oaT = _attn_a(aqT, aqiT, awiT, akb, akib, avT, bias_pt[:H_A])
        obT = _attn_b(bqT, bkb, bvT, bias_pt[H_A:], lamvec, g_subln[l].reshape(2 * HD, 1), lam_init)
        ocT = _attn_c(cqT, kcat, cvT)
        x1 = _merge(xp, mod_pa, gm, oaT, obT, ocT, wg, wpa, wpb, wpc, wo, 1, tm, True, l)
        xp = _ffn(x1, mod_pb, gf, w1, w2, g_final2, 1, tm, last, "ffn_prompt", l)

        (sak, sav, saki, sbk, sbv, sck, scv, slf, qa, qi, wi, qb, qc, slfrep) = _inproj_sample(
            xs, mod_sa, gm, ws, bfb, G_SAMPLE, l)
        rows_s.append((sak, sav, saki, sbk, sbv, sck, scv, slf))
        lft_new = jnp.swapaxes(slfrep.reshape(nbs, dec_seq, LANES)[:, :, 0:8], 1, 2)
        lft_new = jnp.concatenate([lft_new, jnp.zeros((nbs, 8, LANES - dec_seq), F32)], axis=2)
        lane_head = jnp.arange(8)[None, :, None] < H_C
        lft_new = jnp.where(lane_head, lft_new, 0.0)
        oa = _decode_a(page_table, l, qa, qi, wi, sak, sav, saki, a_kt, a_vt, a_kit, bias_dec, dec_seq)
        ob = _decode_b(page_table, l, qb, sbk, sbv, b_kt, b_v2, bias_dec, lamvec, g_subln[l].reshape(1, 2 * HD),
                       lam_init, dec_seq)
        oc = _decode_c(page_table, l, qc, sck, scv, lft_new, c_kt, c_vt, c_lft, dec_seq)
        x1s = _merge(xs, mod_sa, gm, oa, ob, oc, wg, wpa_s, wpb, wpc_s, wo, G_SAMPLE, dec_seq, False, l)
        xs = _ffn(x1s, mod_sb, gf, w1, w2, g_final2, G_SAMPLE, dec_seq, last, "ffn_sample", l)

    def stack(rows, i):
        return jnp.stack([r[i] for r in rows])

    def stack_sample(i, shape):
        return stack(rows_s, i).reshape((depth, nbs, dec_seq) + shape)

    def prompt_t(y, shape):
        n = len(shape)
        y = y.reshape((depth, nbp) + shape + (seq,))
        return jnp.transpose(y, (0, 1, n + 2) + tuple(range(2, n + 2)))

    bv, lf, akT, avT, akiT, bkT, ckT, cvT = stacked
    out_p = (prompt_t(akT, (KVH_A, HD)), prompt_t(avT, (KVH_A, HD)), prompt_t(akiT, (D_IDX,)),
             prompt_t(bkT, (KVH_B, 2, HD)), bv.reshape(depth, nbp, seq, KVH_B, 2 * HD),
             prompt_t(ckT, (KVH_C, HD)), prompt_t(cvT, (KVH_C, HD)), lf)
    out_s = (stack_sample(0, (KVH_A, HD)), stack_sample(1, (KVH_A, HD)), stack_sample(2, (D_IDX,)),
             stack_sample(3, (KVH_B, 2, HD)), stack_sample(4, (KVH_B, 2 * HD)), stack_sample(5, (KVH_C, HD)),
             stack_sample(6, (KVH_C, HD)), stack_sample(7, (H_C,)))
    return (xp, xs) + out_p + out_s
```

```python
import functools
import math

import numpy as np
import jax
import jax.numpy as jnp
from jax import lax
from jax.experimental import pallas as pl
from jax.experimental.pallas import tpu as pltpu

F32 = jnp.float32
BF16 = jnp.bfloat16
I32 = jnp.int32

D = 1024
HD = 64
H_A, KVH_A = 6, 2
H_IDX, D_IDX = 8, 64
K_TOP_MAX = 256
IDX_W_SCALE = (H_IDX ** -0.5) * (D_IDX ** -0.5)
H_B, KVH_B = 4, 2
H_C, KVH_C = 6, 2
N_BUCKETS, T5_MAX_EXACT, T5_MAX_DIST = 32, 16, 128
D_FF = 4 * D
EPS = 1e-6
N_HEADS_T5 = H_A + H_B
PAGE = 128

LANES = 128
TQ = 256
TKC = 128
Q_BLOCKS = TQ // TKC
NEAR_BLOCKS = Q_BLOCKS + 1
FAR_BLOCKS = 4
NEG = -1e30
INT_MIN = -2 ** 31
VMEM_LIMIT = 56 * 1024 * 1024

_W = dict(aq=H_A * HD, ak=KVH_A * HD, av=KVH_A * HD, aqi=H_IDX * D_IDX, aki=D_IDX, awi=H_IDX,
          bq=H_B * 2 * HD, bk=KVH_B * 2 * HD, bv=KVH_B * 2 * HD,
          cq=H_C * HD, ck=KVH_C * HD, cv=KVH_C * HD, cf=H_C)
_OFF = {}
_o = 0
for _k, _v in _W.items():
    _OFF[_k] = _o
    _o += _v

_R_AK, _R_AV, _R_AKI, _R_BK, _R_BV, _R_CK, _R_CV, _R_CF = 0, 128, 256, 384, 640, 896, 1024, 1152
N_ROWS = 1280
_P_AK, _P_AKI, _P_BK, _P_CK, _P_CF, _P_BV = 0, 128, 256, 512, 640, 768
N_PROWS = 1024
_T_AQ, _T_AQI, _T_AWI, _T_BQ, _T_CQ, _T_AV, _T_BV, _T_CV = 0, 384, 896, 912, 1424, 1808, 1936, 2192
_T_AK, _T_AKI, _T_BK, _T_CK = 2320, 2448, 2512, 2768
N_T = 2896
_S_AQ, _S_AQI, _S_AWI, _S_BQ, _S_CQ = 0, 768, 1792, 2816, 4864
N_SQ = 5632


def _cparams(*sem):
    return pltpu.CompilerParams(dimension_semantics=sem, vmem_limit_bytes=VMEM_LIMIT)


def _dot(a, b):
    return jnp.dot(a, b, preferred_element_type=F32)


def _dot_nt(a, b):
    return lax.dot_general(a, b, (((1,), (1,)), ((), ())), preferred_element_type=F32)


def _split3(x):
    hi = x.astype(BF16)
    r1 = x - hi.astype(F32)
    mid = r1.astype(BF16)
    lo = (r1 - mid.astype(F32)).astype(BF16)
    return hi, mid, lo


def _dot3_rhs(a_bf16, x):
    hi, mid, lo = _split3(x)
    return _dot(a_bf16, hi) + _dot(a_bf16, mid) + _dot(a_bf16, lo)


def _dot3_lhs(x, b_bf16):
    hi, mid, lo = _split3(x)
    return _dot(hi, b_bf16) + _dot(mid, b_bf16) + _dot(lo, b_bf16)


def _norm_mod(x, g, sc, sh):
    ms = jnp.mean(x * x, axis=-1, keepdims=True)
    return (x * lax.rsqrt(ms + EPS) * g) * (1.0 + sc) + sh


def _log_sigmoid(z):
    return jnp.minimum(z, 0.0) - jnp.log1p(jnp.exp(-jnp.abs(z)))


def _sigmoid(z):
    return 1.0 / (1.0 + jnp.exp(-z))


def _sortable_key(x):
    bits = lax.bitcast_convert_type(x, I32)
    return bits ^ ((bits >> 31) & 0x7FFFFFFF)


def _ada_kernel(c_ref, w_ref, b_ref, o_ref):
    c = c_ref[...]
    s = (c * _sigmoid(c)).astype(BF16)
    o_ref[0] = _dot(s, w_ref[0].astype(BF16)) + b_ref[0]


def _ada(c_all, w_ada, b_ada):
    depth = w_ada.shape[0]
    nc = c_all.shape[0]
    nt = 6
    return pl.pallas_call(
        _ada_kernel,
        grid=(depth, nt),
        in_specs=[pl.BlockSpec((nc, D), lambda l, j: (0, 0)),
                  pl.BlockSpec((1, D, D), lambda l, j: (l, 0, j)),
                  pl.BlockSpec((1, 1, D), lambda l, j: (l, 0, j))],
        out_specs=pl.BlockSpec((1, nc, D), lambda l, j: (l, 0, j)),
        out_shape=jax.ShapeDtypeStruct((depth, nc, 6 * D), F32),
        compiler_params=_cparams("arbitrary", "arbitrary"),
        name="ada",
    )(c_all, w_ada, b_ada.reshape(depth, 1, 6 * D))


def _t5_bucket_np(rel):
    n = np.maximum(rel, 0)
    nf = np.maximum(n, 1).astype(np.float32)
    large = T5_MAX_EXACT + (np.log(nf / np.float32(T5_MAX_EXACT)) / np.float32(math.log(T5_MAX_DIST / T5_MAX_EXACT))
                            * np.float32(N_BUCKETS - T5_MAX_EXACT)).astype(np.int32)
    return np.where(n < T5_MAX_EXACT, n, np.minimum(large, N_BUCKETS - 1)).astype(np.int32)


def _bias_bucket_tables(past_len, dec_seq):
    s = np.arange(TKC)[:, None]
    t = np.arange(TQ)[None, :]
    tiles = []
    for u in range(NEAR_BLOCKS):
        rel = t - s - (TQ - TKC) + u * TKC
        tiles.append(np.where(rel >= 0, _t5_bucket_np(rel), -1))
    prompt = np.stack(tiles).astype(np.int32)
    i = np.arange(dec_seq)[:, None]
    lane = np.arange(PAGE)[None, :]
    rel_last = (past_len + i) - (past_len - PAGE + lane)
    rel_new = i - lane
    dl = _t5_bucket_np(rel_last)
    dn = np.where((rel_new >= 0) & (lane < dec_seq), _t5_bucket_np(rel_new), -1)
    dec = np.stack([dl, dn]).astype(np.int32)
    return prompt, dec


def _bias_kernel(tab_ref, pb_ref, db_ref, pt_ref, dt_ref):
    pb = pb_ref[...]
    db = db_ref[...]
    for h in range(N_HEADS_T5):
        def lut(bk):
            acc = jnp.zeros(bk.shape, F32)
            for b in range(N_BUCKETS):
                acc = jnp.where(bk == b, tab_ref[b, h], acc)
            return jnp.where(bk < 0, NEG, acc - tab_ref[N_BUCKETS - 1, h])
        pt_ref[h] = lut(pb)
        dt_ref[h] = lut(db)


def _bias_tiles(t5_table, past_len, dec_seq):
    pb, db = _bias_bucket_tables(past_len, dec_seq)
    return pl.pallas_call(
        _bias_kernel,
        in_specs=[pl.BlockSpec(memory_space=pltpu.SMEM),
                  pl.BlockSpec(memory_space=pltpu.VMEM),
                  pl.BlockSpec(memory_space=pltpu.VMEM)],
        out_specs=[pl.BlockSpec(memory_space=pltpu.VMEM), pl.BlockSpec(memory_space=pltpu.VMEM)],
        out_shape=[jax.ShapeDtypeStruct((N_HEADS_T5,) + pb.shape, F32),
                   jax.ShapeDtypeStruct((N_HEADS_T5,) + db.shape, F32)],
        name="t5_bias_tiles",
    )(t5_table, jnp.asarray(pb), jnp.asarray(db))


def _cf_block(cf):
    n = cf.shape[0]
    z2 = jnp.zeros((n, 2), cf.dtype)
    return jnp.concatenate([cf, z2, cf, z2, cf, z2, jnp.zeros((n, LANES - 24), cf.dtype)], axis=1)


class _Plan:
    def __init__(self):
        self.idx, self.scale = [], []

    def seg(self, name, lo=0, hi=None, scale=1.0):
        hi = _W[name] if hi is None else hi
        return self.raw(np.arange(_OFF[name] + lo, _OFF[name] + hi), scale)

    def raw(self, idx, scale=1.0):
        self.idx.append(np.asarray(idx, np.int32))
        self.scale.append(np.full(len(idx), scale, np.float32))
        return self

    def zeros(self, n):
        return self.raw(np.zeros(n, np.int32), 0.0)

    def cf_block(self):
        for _ in range(3):
            self.seg("cf").zeros(8 - H_C)
        return self.zeros(LANES - 24)

    def padded_heads(self, name, n_heads, slot_of, n_slots, scale=1.0):
        for h in range(n_heads):
            s = slot_of(h)
            self.zeros(s * HD).seg(name, h * HD, (h + 1) * HD, scale).zeros((n_slots - 1 - s) * HD)
        return self

    def rows_block(self):
        return (self.seg("ak").seg("av").seg("aki").zeros(64).seg("bk").seg("bv").seg("ck").seg("cv").cf_block())

    def gather(self, w, axis):
        idx = np.concatenate(self.idx)
        scale = np.concatenate(self.scale).reshape([-1 if a == axis else 1 for a in range(w.ndim)])
        return jnp.take(w.astype(BF16), jnp.asarray(idx), axis=axis, mode="clip") * jnp.asarray(scale, BF16)


_QSCALE = HD ** -0.5


def _prep_in_prompt(w_in):
    w_t = (_Plan().seg("aq", scale=_QSCALE).seg("aqi").seg("awi").zeros(8).seg("bq", scale=_QSCALE)
           .seg("cq", scale=_QSCALE).seg("av").seg("bv").seg("cv").seg("ak").seg("aki").seg("bk").seg("ck"))
    w_r = _Plan().seg("ak").seg("aki").zeros(64).seg("bk").seg("ck").cf_block().seg("bv")
    return w_r.gather(w_in, 2), jnp.swapaxes(w_t.gather(w_in, 2), 1, 2)


def _prep_in_sample(w_in):
    plan = _Plan()
    plan.padded_heads("aq", H_A, lambda h: h // (H_A // KVH_A), 2, _QSCALE)
    plan.padded_heads("aqi", H_IDX, lambda h: 0, 2)
    plan.raw(np.repeat(np.arange(_OFF["awi"], _OFF["awi"] + H_IDX), LANES))
    plan.padded_heads("bq", 2 * H_B, lambda hc: (hc // 2 // (H_B // KVH_B)) * 2 + hc % 2, 4, _QSCALE)
    plan.padded_heads("cq", H_C, lambda h: h // (H_C // KVH_C), 2, _QSCALE)
    return plan.rows_block().gather(w_in, 2)


def _pad_proj_rows(wp, n_heads, kv_of):
    plan = _Plan()
    for h in range(n_heads):
        s = kv_of(h)
        plan.zeros(s * HD).raw(np.arange(h * HD, (h + 1) * HD)).zeros((1 - s) * HD)
    return plan.gather(wp, 1)


def _store_rows(pr, bf, ak_o, av_o, aki_o, bk_o, bv_o, ck_o, cv_o, lf_o, idx):
    ak_o[idx] = pr[:, _R_AK:_R_AK + 128]
    av_o[idx] = pr[:, _R_AV:_R_AV + 128]
    aki_o[idx] = pr[:, _R_AKI:_R_AKI + 64]
    bk_o[idx] = pr[:, _R_BK:_R_BK + 256]
    bv_o[idx] = pr[:, _R_BV:_R_BV + 256]
    ck_o[idx] = pr[:, _R_CK:_R_CK + 128]
    cv_o[idx] = pr[:, _R_CV:_R_CV + 128]
    lf = _log_sigmoid(pr[:, _R_CF:_R_CF + 128] + bf)
    lf_o[idx] = lf[:, 0:H_C]
    return lf


N_STACKED = 8


def _inproj_p_kernel(x_ref, mod_ref, g_ref, wr_ref, wt_ref, bf_ref, *refs):
    (bv_o, lf_o, akT_o, avT_o, akiT_o, bkT_o, ckT_o, cvT_o,
     akb_o, akib_o, bkb_o, ckb_o, lfrep_o, awiT_o,
     aqT_o, aqiT_o, bqT_o, cqT_o, avTb_o, bvTb_o, cvTb_o) = refs[N_STACKED:]
    x = x_ref[0]
    mod = mod_ref[0]
    h = _norm_mod(x, g_ref[...], mod[:, D:2 * D], mod[:, 0:D])
    pr = _dot(h.astype(BF16), wr_ref[...])
    bv_o[0] = pr[:, _P_BV:_P_BV + 256]
    lf = _log_sigmoid(pr[:, _P_CF:_P_CF + 128] + bf_ref[...])
    lf_o[0] = lf[:, 0:H_C]
    lfrep_o[0] = lf
    akb_o[0] = pr[:, _P_AK:_P_AK + 128].astype(BF16)
    akib_o[0] = pr[:, _P_AKI:_P_AKI + 64].astype(BF16)
    bkb_o[0] = pr[:, _P_BK:_P_BK + 256].astype(BF16)
    ckb_o[0] = pr[:, _P_CK:_P_CK + 128].astype(BF16)
    pt = _dot(wt_ref[...], h.T.astype(BF16))
    akT_o[0] = pt[_T_AK:_T_AK + 128]
    avT_o[0] = pt[_T_AV:_T_AV + 128]
    akiT_o[0] = pt[_T_AKI:_T_AKI + 64]
    bkT_o[0] = pt[_T_BK:_T_BK + 256]
    ckT_o[0] = pt[_T_CK:_T_CK + 128]
    cvT_o[0] = pt[_T_CV:_T_CV + 128]
    awiT_o[0] = pt[_T_AWI:_T_AWI + 16]
    aqT_o[0] = pt[_T_AQ:_T_AQ + 384].astype(BF16)
    aqiT_o[0] = pt[_T_AQI:_T_AQI + 512].astype(BF16)
    bqT_o[0] = pt[_T_BQ:_T_BQ + 512].astype(BF16)
    cqT_o[0] = pt[_T_CQ:_T_CQ + 384].astype(BF16)
    avTb_o[0] = pt[_T_AV:_T_AV + 128].astype(BF16)
    bvTb_o[0] = pt[_T_BV:_T_BV + 256].astype(BF16)
    cvTb_o[0] = pt[_T_CV:_T_CV + 128].astype(BF16)


def _layer_spec(a, layer):
    return pl.BlockSpec((None,) + a.shape[1:], lambda *_: (layer,) + (0,) * (a.ndim - 1))


def _stacked_row_shapes(depth, b, l):
    return ([jax.ShapeDtypeStruct((depth, b, l, w), F32) for w in (256, H_C)]
            + [jax.ShapeDtypeStruct((depth, b, r, l), F32) for r in (128, 128, 64, 256, 128, 128)])


def _inproj_prompt(x, mod_a, g, wr, wt, bfb, tm, layer, stacked):
    b, l, _ = x.shape
    row = lambda w, dt: jax.ShapeDtypeStruct((b, l, w), dt)
    tr = lambda r, dt: jax.ShapeDtypeStruct((b, r, l), dt)
    row_spec = lambda w: pl.BlockSpec((1, tm, w), lambda i, j: (i, j, 0))
    tr_spec = lambda r: pl.BlockSpec((1, r, tm), lambda i, j: (i, 0, j))
    st_specs = ([pl.BlockSpec((None, 1, tm, a.shape[3]), lambda i, j: (layer, i, j, 0)) for a in stacked[:2]]
                + [pl.BlockSpec((None, 1, a.shape[2], tm), lambda i, j: (layer, i, 0, j)) for a in stacked[2:]])
    rows = [(128, BF16), (64, BF16), (256, BF16), (128, BF16), (128, F32)]
    trs = [(16, F32), (384, BF16), (512, BF16), (512, BF16), (384, BF16), (128, BF16), (256, BF16), (128, BF16)]
    n_in = 6
    outs = pl.pallas_call(
        _inproj_p_kernel,
        grid=(b, l // tm),
        in_specs=[pl.BlockSpec((1, tm, D), lambda i, j: (i, j, 0)),
                  pl.BlockSpec((1, 1, 3 * D), lambda i, j: (i, 0, 0)),
                  pl.BlockSpec((1, D), lambda i, j: (0, 0)),
                  _layer_spec(wr, layer), _layer_spec(wt, layer),
                  pl.BlockSpec((1, LANES), lambda i, j: (0, 0))] + [_any_spec() for _ in stacked],
        out_specs=st_specs + [row_spec(w) for w, _ in rows] + [tr_spec(r) for r, _ in trs],
        out_shape=[jax.ShapeDtypeStruct(a.shape, a.dtype) for a in stacked]
        + [row(w, dt) for w, dt in rows] + [tr(r, dt) for r, dt in trs],
        input_output_aliases={n_in + k: k for k in range(N_STACKED)},
        compiler_params=_cparams("arbitrary", "arbitrary"),
        name="inproj_prompt",
    )(x, mod_a, g, wr, wt, bfb, *stacked)
    return outs[:N_STACKED], outs[N_STACKED:]


def _inproj_s_kernel(x_ref, mod_ref, g_ref, w_ref, bf_ref,
                     ak_o, av_o, aki_o, bk_o, bv_o, ck_o, cv_o, lf_o,
                     qa_o, qi_o, wi_o, qb_o, qc_o, lfrep_o):
    x = x_ref[...]
    gsz, r, _ = x.shape
    mod = mod_ref[...]
    h = _norm_mod(x, g_ref[...], mod[:, :, D:2 * D], mod[:, :, 0:D]).reshape(gsz * r, D)
    pr = _dot(h.astype(BF16), w_ref[...])
    lf = _store_rows(pr[:, N_SQ:], bf_ref[...], ak_o, av_o, aki_o, bk_o, bv_o, ck_o, cv_o, lf_o,
                     (slice(None), slice(None)))
    lfrep_o[...] = lf
    qa_o[...] = pr[:, _S_AQ:_S_AQ + 768].astype(BF16)
    qi_o[...] = pr[:, _S_AQI:_S_AQI + 1024].astype(BF16)
    wi_o[...] = pr[:, _S_AWI:_S_AWI + 1024]
    qb_o[...] = pr[:, _S_BQ:_S_BQ + 2048].astype(BF16)
    qc_o[...] = pr[:, _S_CQ:_S_CQ + 768].astype(BF16)


def _inproj_sample(x, mod_a, g, w, bfb, gsz, layer):
    nb, r, _ = x.shape
    n = nb * r
    tm = gsz * r
    row_w = [128, 128, 64, 256, 256, 128, 128, H_C]
    outs = [(w_, F32) for w_ in row_w] + [(768, BF16), (1024, BF16), (1024, F32), (2048, BF16), (768, BF16), (128, F32)]
    return pl.pallas_call(
        _inproj_s_kernel,
        grid=(nb // gsz,),
        in_specs=[pl.BlockSpec((gsz, r, D), lambda i: (i, 0, 0)),
                  pl.BlockSpec((gsz, 1, 3 * D), lambda i: (i, 0, 0)),
                  pl.BlockSpec((1, D), lambda i: (0, 0)),
                  _layer_spec(w, layer),
                  pl.BlockSpec((1, LANES), lambda i: (0, 0))],
        out_specs=[pl.BlockSpec((tm, w_), lambda i: (i, 0)) for w_, _ in outs],
        out_shape=[jax.ShapeDtypeStruct((n, w_), dt) for w_, dt in outs],
        compiler_params=_cparams("arbitrary"),
        name="inproj_sample",
    )(x, mod_a, g, w, bfb)


_CUM_T = 256


def _cum_kernel(lf_ref, k_ref, o_ref):
    l = lf_ref.shape[1]
    r = lax.broadcasted_iota(I32, (_CUM_T, _CUM_T), 0)
    c = lax.broadcasted_iota(I32, (_CUM_T, _CUM_T), 1)
    tri = jnp.where(c <= r, 1.0, 0.0).astype(BF16)
    lane = lax.broadcasted_iota(I32, (_CUM_T, LANES), 1)
    carry = jnp.zeros((1, LANES), F32)
    for i in range(l // _CUM_T):
        rows = slice(i * _CUM_T, (i + 1) * _CUM_T)
        cum = _dot3_rhs(tri, lf_ref[0, rows, :]) + carry
        carry = cum[_CUM_T - 1:_CUM_T, :]
        hi, mid, lo = _split3(cum)
        piece = jnp.where(lane < 8, hi, jnp.where(lane < 16, mid, lo))
        o_ref[0, rows, 0:LANES] = k_ref[0, rows, :]
        o_ref[0, rows, LANES:2 * LANES] = -piece


def _cum_prompt(lfrep, ckb):
    b, l, _ = lfrep.shape
    return pl.pallas_call(
        _cum_kernel,
        grid=(b,),
        in_specs=[pl.BlockSpec((1, l, LANES), lambda i: (i, 0, 0)),
                  pl.BlockSpec((1, l, LANES), lambda i: (i, 0, 0))],
        out_specs=pl.BlockSpec((1, l, 2 * LANES), lambda i: (i, 0, 0)),
        out_shape=jax.ShapeDtypeStruct((b, l, 2 * LANES), BF16),
        compiler_params=_cparams("arbitrary"),
        name="forget_cumsum",
    )(lfrep, ckb)


def _online_update(s, m_scr, acc_scr, vta):
    m_prev = m_scr[...]
    m_new = jnp.maximum(m_prev, jnp.max(s, axis=0, keepdims=True))
    alpha = jnp.exp(m_prev - m_new)
    p = jnp.exp(s - m_new).astype(BF16)
    m_scr[...] = m_new
    n_g = len(vta)
    w = s.shape[1] // n_g
    for g in range(n_g):
        pv = _dot(vta[g], p[:, g * w:(g + 1) * w])
        acc_scr[g] = acc_scr[g] * alpha[:, g * w:(g + 1) * w] + pv


def _key_rows(start, n):
    return pl.ds(pl.multiple_of(start * TKC, TKC), n * TKC)


def _v_aug(vt_ref, start, n, g, rows):
    v = vt_ref[0, g * rows:(g + 1) * rows, _key_rows(start, n)]
    return jnp.concatenate([v, jnp.ones((16, n * TKC), BF16)], axis=0)


def _add_near(s, n, tiles):
    parts = []
    first_near = max(n - len(tiles), 0)
    if first_near > 0:
        parts.append(s[0:first_near * TKC])
    for blk in range(first_near, n):
        tile = tiles[n - 1 - blk]
        rows = s[blk * TKC:(blk + 1) * TKC]
        parts.append(rows if tile is None else rows + tile)
    return parts[0] if len(parts) == 1 else jnp.concatenate(parts, axis=0)


def _chunk_loop(qi, chunk):
    n_tot = (qi + 1) * Q_BLOCKS
    n_big = jnp.maximum(n_tot - NEAR_BLOCKS, 0) // FAR_BLOCKS

    def far(c, carry):
        chunk(c * FAR_BLOCKS, FAR_BLOCKS, False)
        return carry
    lax.fori_loop(0, n_big, far, 0)
    start = n_big * FAR_BLOCKS
    n_tail = n_tot - start
    for v in range(Q_BLOCKS, FAR_BLOCKS + NEAR_BLOCKS, Q_BLOCKS):
        @pl.when(n_tail == v)
        def _(v=v):
            chunk(start, v, True)


def _attn_a_kernel(qT_ref, qiT_ref, wT_ref, k_ref, ki_ref, vT_ref, bias_ref, oT_ref,
                   qpad, qipad, key_scr, am_scr, j_scr, m_scr, acc_scr, *, k_top):
    b = pl.program_id(0)
    qi = pl.program_id(1)
    n_chunk = (qi + 1) * Q_BLOCKS
    grp = H_A // KVH_A

    @pl.when((b == 0) & (qi == 0))
    def _():
        qpad[...] = jnp.zeros_like(qpad)
    for h in range(H_A):
        g = h // grp
        qpad[g * HD:(g + 1) * HD, h * TQ:(h + 1) * TQ] = qT_ref[0, h * HD:(h + 1) * HD, :]
    for h in range(H_IDX):
        qipad[:, h * TQ:(h + 1) * TQ] = qiT_ref[0, h * D_IDX:(h + 1) * D_IDX, :]
    w = wT_ref[0, 0:H_IDX, :] * IDX_W_SCALE
    grows = FAR_BLOCKS * TKC
    n_grp = (n_chunk + FAR_BLOCKS - 1) // FAR_BLOCKS
    row = lax.broadcasted_iota(I32, (TKC, TQ), 0)
    col = lax.broadcasted_iota(I32, (TKC, TQ), 1) + qi * TQ
    grow = lax.broadcasted_iota(I32, (grows, TQ), 0)

    def score_group(gi, carry):
        for bi in range(FAR_BLOCKS):
            start = pl.multiple_of(gi * grows + bi * TKC, TKC)
            sl = pl.ds(start, TKC)
            s = _dot(ki_ref[0, sl, :], qipad[...])
            sc = jnp.zeros((TKC, TQ), F32)
            for h in range(H_IDX):
                sc = sc + jnp.maximum(s[:, h * TQ:(h + 1) * TQ], 0.0) * w[h:h + 1, :]
            sc = jnp.where(sc == 0.0, 0.0, sc)
            sc = jnp.where(row + start > col, -jnp.inf, sc)
            key_scr[sl, :] = _sortable_key(sc)
        return carry
    lax.fori_loop(0, n_grp, score_group, 0)

    def count(pred):
        def body(gi, cnt):
            start = pl.multiple_of(gi * grows, grows)
            hit = jnp.where(pred(key_scr[pl.ds(start, grows), :], start), 1, 0).astype(I32)
            return cnt + jnp.sum(hit.reshape(grows // 8, 8, TQ), axis=0)
        cnt8 = lax.fori_loop(0, n_grp, body, jnp.zeros((8, TQ), I32))
        return jnp.sum(cnt8, axis=0, keepdims=True)

    needs_search = (qi + 1) * TQ > k_top

    @pl.when(jnp.logical_not(needs_search))
    def _():
        def body(gi, carry):
            am_scr[pl.ds(pl.multiple_of(gi * grows, grows), grows), :] = jnp.zeros((grows, TQ), F32)
            return carry
        lax.fori_loop(0, n_grp, body, 0)

    @pl.when(needs_search)
    def _():
        def it_body(it, t):
            cand = t ^ lax.shift_left(jnp.int32(1), 31 - it)
            cnt = count(lambda k, start: k >= cand)
            return jnp.where(cnt >= k_top, cand, t)
        thr = lax.fori_loop(0, 32, it_body, jnp.full((1, TQ), INT_MIN, I32))

        def recode(gi, cnts):
            start = pl.multiple_of(gi * grows, grows)
            k = key_scr[pl.ds(start, grows), :]
            gt, eq = k > thr, k == thr
            key_scr[pl.ds(start, grows), :] = jnp.where(gt, -1, jnp.where(eq, grow + start, 2 ** 31 - 1))
            fold = lambda m: jnp.sum(jnp.where(m, 1, 0).astype(I32).reshape(grows // 8, 8, TQ), axis=0)
            return cnts[0] + fold(gt), cnts[1] + fold(eq)
        z8 = jnp.zeros((8, TQ), I32)
        gt8, eq8 = lax.fori_loop(0, n_grp, recode, (z8, z8))
        cnt_gt = jnp.sum(gt8, axis=0, keepdims=True)
        cnt_eq = jnp.sum(eq8, axis=0, keepdims=True)
        need = k_top - cnt_gt
        n_bits = max(1, (key_scr.shape[0] - 1).bit_length())
        j_scr[...] = jnp.full((1, TQ), key_scr.shape[0], I32)

        @pl.when(jnp.max(jnp.where(cnt_eq > need, 1, 0)) > 0)
        def _():
            def tie_body(it, jv):
                cand = jv | lax.shift_left(jnp.int32(1), n_bits - 1 - it)
                ties_before = count(lambda code, start: code < cand) - cnt_gt
                return jnp.where(ties_before < need, cand, jv)
            j_scr[...] = lax.fori_loop(0, n_bits, tie_body, jnp.zeros((1, TQ), I32))
        jv = j_scr[...]

        def body(gi, carry):
            start = pl.multiple_of(gi * grows, grows)
            am_scr[pl.ds(start, grows), :] = jnp.where(key_scr[pl.ds(start, grows), :] <= jv, 0.0, NEG)
            return carry
        lax.fori_loop(0, n_grp, body, 0)

    m_scr[...] = jnp.full(m_scr.shape, NEG, F32)
    acc_scr[...] = jnp.zeros_like(acc_scr)

    def chunk(start, n, near):
        sl = _key_rows(start, n)
        s = _dot(k_ref[0, sl, :], qpad[...])
        s = s + jnp.concatenate([am_scr[sl, :]] * H_A, axis=1)
        if near:
            s = _add_near(s, n, [jnp.concatenate([bias_ref[h, u] for h in range(H_A)], axis=1)
                                 for u in range(NEAR_BLOCKS)])
        _online_update(s, m_scr, acc_scr, [_v_aug(vT_ref, start, n, g, HD) for g in range(KVH_A)])

    _chunk_loop(qi, chunk)

    for h in range(H_A):
        g, hh = h // grp, h % grp
        a = acc_scr[g][:, hh * TQ:(hh + 1) * TQ]
        oT_ref[0, h * HD:(h + 1) * HD, :] = a[0:HD] / a[HD:HD + 1]


def _attn_a(aqT, aqiT, awiT, akb, akib, avT, bias_pt):
    b, _, l = aqT.shape
    k_top = min(K_TOP_MAX, l // 4)
    blk_t = lambda r: pl.BlockSpec((1, r, TQ), lambda i, j: (i, 0, j))
    full = lambda s1, s2: pl.BlockSpec((1, s1, s2), lambda i, j: (i, 0, 0))
    return pl.pallas_call(
        functools.partial(_attn_a_kernel, k_top=k_top),
        grid=(b, l // TQ),
        in_specs=[blk_t(384), blk_t(512), blk_t(16), full(l, 128), full(l, 64), full(128, l),
                  pl.BlockSpec((H_A, NEAR_BLOCKS, TKC, TQ), lambda i, j: (0, 0, 0, 0))],
        out_specs=blk_t(384),
        out_shape=jax.ShapeDtypeStruct((b, 384, l), F32),
        scratch_shapes=[pltpu.VMEM((128, H_A * TQ), BF16),
                        pltpu.VMEM((D_IDX, H_IDX * TQ), BF16),
                        pltpu.VMEM((l, TQ), I32),
                        pltpu.VMEM((l, TQ), F32),
                        pltpu.VMEM((1, TQ), I32),
                        pltpu.VMEM((1, H_A * TQ), F32),
                        pltpu.VMEM((KVH_A, HD + 16, (H_A // KVH_A) * TQ), F32)],
        compiler_params=_cparams("arbitrary", "arbitrary"),
        name="attn_dsa_prompt",
    )(aqT, aqiT, awiT, akb, akib, avT, bias_pt)


def _lambda_value(lam_ref, lam_init):
    lv = lam_ref[...]
    s1 = jnp.sum(lv[0:1] * lv[1:2], axis=1, keepdims=True)
    s2 = jnp.sum(lv[2:3] * lv[3:4], axis=1, keepdims=True)
    return jnp.exp(s1) - jnp.exp(s2) + lam_init


def _attn_b_kernel(qT_ref, k_ref, vT_ref, bias_ref, lam_ref, gsub_ref, oT_ref,
                   qpad, m_scr, acc_scr, *, lam_init):
    b = pl.program_id(0)
    qi = pl.program_id(1)
    grp = H_B // KVH_B

    @pl.when((b == 0) & (qi == 0))
    def _():
        qpad[...] = jnp.zeros_like(qpad)
    for h in range(H_B):
        for c in range(2):
            slot = (h // grp) * 2 + c
            hc = h * 2 + c
            qpad[slot * HD:(slot + 1) * HD, hc * TQ:(hc + 1) * TQ] = qT_ref[0, hc * HD:(hc + 1) * HD, :]

    m_scr[...] = jnp.full(m_scr.shape, NEG, F32)
    acc_scr[...] = jnp.zeros_like(acc_scr)

    def chunk(start, n, near):
        s = _dot(k_ref[0, _key_rows(start, n), :], qpad[...])
        if near:
            s = _add_near(s, n, [jnp.concatenate([bias_ref[hc // 2, u] for hc in range(2 * H_B)], axis=1)
                                 for u in range(NEAR_BLOCKS)])
        _online_update(s, m_scr, acc_scr, [_v_aug(vT_ref, start, n, g, 2 * HD) for g in range(KVH_B)])

    _chunk_loop(qi, chunk)

    lam = _lambda_value(lam_ref, lam_init)
    gs = gsub_ref[...]
    for h in range(H_B):
        g, hh = h // grp, h % grp
        a1 = acc_scr[g][:, (hh * 2) * TQ:(hh * 2 + 1) * TQ]
        a2 = acc_scr[g][:, (hh * 2 + 1) * TQ:(hh * 2 + 2) * TQ]
        o = a1[0:2 * HD] / a1[2 * HD:2 * HD + 1] - lam * (a2[0:2 * HD] / a2[2 * HD:2 * HD + 1])
        ms = jnp.mean(o * o, axis=0, keepdims=True)
        oT_ref[0, h * 2 * HD:(h + 1) * 2 * HD, :] = (o * lax.rsqrt(ms + EPS) * gs) * (1.0 - lam_init)


def _attn_b(bqT, bkb, bvT, bias_pt, lamvec, gsub_col, lam_init):
    b, _, l = bqT.shape
    blk_t = lambda r: pl.BlockSpec((1, r, TQ), lambda i, j: (i, 0, j))
    full = lambda s1, s2: pl.BlockSpec((1, s1, s2), lambda i, j: (i, 0, 0))
    return pl.pallas_call(
        functools.partial(_attn_b_kernel, lam_init=lam_init),
        grid=(b, l // TQ),
        in_specs=[blk_t(512), full(l, 256), full(256, l),
                  pl.BlockSpec((H_B, NEAR_BLOCKS, TKC, TQ), lambda i, j: (0, 0, 0, 0)),
                  pl.BlockSpec((4, HD), lambda i, j: (0, 0)),
                  pl.BlockSpec((2 * HD, 1), lambda i, j: (0, 0))],
        out_specs=blk_t(512),
        out_shape=jax.ShapeDtypeStruct((b, 512, l), F32),
        scratch_shapes=[pltpu.VMEM((256, 2 * H_B * TQ), BF16),
                        pltpu.VMEM((1, 2 * H_B * TQ), F32),
                        pltpu.VMEM((KVH_B, 2 * HD + 16, 2 * (H_B // KVH_B) * TQ), F32)],
        compiler_params=_cparams("arbitrary", "arbitrary"),
        name="attn_diff_prompt",
    )(bqT, bkb, bvT, bias_pt, lamvec, gsub_col)


def _attn_c_kernel(qT_ref, k_ref, vT_ref, oT_ref, qpad, m_scr, acc_scr):
    b = pl.program_id(0)
    qi = pl.program_id(1)
    grp = H_C // KVH_C

    @pl.when((b == 0) & (qi == 0))
    def _():
        r = lax.broadcasted_iota(I32, (256, H_C * TQ), 0) - 128
        cblk = lax.broadcasted_iota(I32, (256, H_C * TQ), 1) // TQ
        ones = (r >= 0) & (r < 24) & ((r % 8) == cblk)
        qpad[...] = jnp.where(ones, 1.0, 0.0).astype(BF16)
    for h in range(H_C):
        g = h // grp
        qpad[g * HD:(g + 1) * HD, h * TQ:(h + 1) * TQ] = qT_ref[0, h * HD:(h + 1) * HD, :]

    m_scr[...] = jnp.full(m_scr.shape, NEG, F32)
    acc_scr[...] = jnp.zeros_like(acc_scr)
    row = lax.broadcasted_iota(I32, (TKC, TQ), 0)
    col = lax.broadcasted_iota(I32, (TKC, TQ), 1)
    causal = [jnp.concatenate([jnp.where(row - col + (TQ - TKC) - u * TKC > 0, NEG, 0.0)] * H_C, axis=1)
              for u in range(Q_BLOCKS)] + [None]

    def chunk(start, n, near):
        s = _dot(k_ref[0, _key_rows(start, n), :], qpad[...])
        if near:
            s = _add_near(s, n, causal)
        _online_update(s, m_scr, acc_scr, [_v_aug(vT_ref, start, n, g, HD) for g in range(KVH_C)])

    _chunk_loop(qi, chunk)

    for h in range(H_C):
        g, hh = h // grp, h % grp
        a = acc_scr[g][:, hh * TQ:(hh + 1) * TQ]
        oT_ref[0, h * HD:(h + 1) * HD, :] = a[0:HD] / a[HD:HD + 1]


def _attn_c(cqT, kcat, cvT):
    b, _, l = cqT.shape
    blk_t = lambda r: pl.BlockSpec((1, r, TQ), lambda i, j: (i, 0, j))
    full = lambda s1, s2: pl.BlockSpec((1, s1, s2), lambda i, j: (i, 0, 0))
    return pl.pallas_call(
        _attn_c_kernel,
        grid=(b, l // TQ),
        in_specs=[blk_t(384), full(l, 256), full(128, l)],
        out_specs=blk_t(384),
        out_shape=jax.ShapeDtypeStruct((b, 384, l), F32),
        scratch_shapes=[pltpu.VMEM((256, H_C * TQ), BF16),
                        pltpu.VMEM((1, H_C * TQ), F32),
                        pltpu.VMEM((KVH_C, HD + 16, (H_C // KVH_C) * TQ), F32)],
        compiler_params=_cparams("arbitrary", "arbitrary"),
        name="attn_forget_prompt",
    )(cqT, kcat, cvT)


G_DEC = 4


def _stack_heads(ref, tok, n, width):
    return jnp.concatenate([ref[tok, h * width:(h + 1) * width] for h in range(n)], axis=0)


def _pad_new(x):
    return jnp.concatenate([x, jnp.zeros((PAGE - x.shape[0], x.shape[1]), x.dtype)], axis=0)


def _probabilities(s_pages):
    mx = s_pages[0]
    for s in s_pages[1:]:
        mx = jnp.maximum(mx, s)
    m = jnp.max(mx, axis=1, keepdims=True)
    p_pages = [jnp.exp(s - m) for s in s_pages]
    lsum = p_pages[0]
    for p in p_pages[1:]:
        lsum = lsum + p
    return p_pages, jnp.sum(lsum, axis=1, keepdims=True)


def _pv(p_pages, vt_pages, v_new):
    acc = _dot(p_pages[-1].astype(BF16), v_new)
    for p, vt in zip(p_pages[:-1], vt_pages):
        acc = acc + _dot_nt(p.astype(BF16), vt)
    return acc


class _PageFetch:
    def __init__(self, pt_ref, gsz, n_pages, hbms, indexers, bufs, sems):
        self.pt_ref, self.gsz, self.n_pages = pt_ref, gsz, n_pages
        self.hbms, self.indexers, self.bufs, self.sems = hbms, indexers, bufs, sems
        self.step = pl.program_id(0)
        self.last = pl.num_programs(0) - 1
        self.cur = lax.rem(self.step, 2)

    def _copy(self, c, page_id, buf, j):
        return pltpu.make_async_copy(self.hbms[c].at[self.indexers[c](page_id)], self.bufs[c].at[buf, j],
                                     self.sems.at[c, buf])

    def _start(self, step, buf, j):
        page_id = self.pt_ref[step * self.gsz + j // self.n_pages, j % self.n_pages]
        for c in range(len(self.hbms)):
            self._copy(c, page_id, buf, j).start()

    def _wait(self, buf):
        for c in range(len(self.hbms)):
            for j in range(self.gsz * self.n_pages):
                self._copy(c, 0, buf, j).wait()

    def begin(self):
        @pl.when(self.step == 0)
        def _():
            for j in range(self.gsz * self.n_pages):
                self._start(0, 0, j)
        self._wait(self.cur)

    def prefetch(self, j):
        self._start(jnp.minimum(self.step + 1, self.last), 1 - self.cur, j)

    def finish(self):
        @pl.when(self.step == self.last)
        def _():
            self._wait(1 - self.cur)

    def page(self, c, j):
        return self.bufs[c][self.cur, j]


def _any_spec():
    return pl.BlockSpec(memory_space=pl.ANY)


def _decode_a_kernel(pt_ref, qa_ref, qi_ref, wi_ref, ak_n, av_n, aki_n, bias_ref, kt_hbm, vt_hbm, kit_hbm,
                     oa_ref, kt_buf, vt_buf, kit_buf, sems, *, layer, gsz, n_pages, k_top):
    at_page = lambda pg: (layer, pg)
    fetch = _PageFetch(pt_ref, gsz, n_pages, (kt_hbm, vt_hbm, kit_hbm), (at_page,) * 3,
                       (kt_buf, vt_buf, kit_buf), sems)
    fetch.begin()
    r = qa_ref.shape[0] // gsz
    rows = gsz * r
    n_all = n_pages + 1
    lane = lax.broadcasted_iota(I32, (rows, PAGE), 1)
    qrow = lax.rem(lax.broadcasted_iota(I32, (rows, PAGE), 0), r)
    new_visible = lane <= qrow

    sc_pages = [[] for _ in range(n_all)]
    for g in range(gsz):
        tok = slice(g * r, (g + 1) * r)
        qi2 = _stack_heads(qi_ref, tok, H_IDX, 2 * D_IDX)[:, 0:D_IDX]
        wcol = _stack_heads(wi_ref, tok, H_IDX, LANES) * IDX_W_SCALE
        for p in range(n_all):
            if p < n_pages:
                fetch.prefetch(g * n_pages + p)
                z = _dot(qi2, fetch.page(2, g * n_pages + p).astype(BF16))
            else:
                z = _dot_nt(qi2, _pad_new(aki_n[tok, :]).astype(BF16))
            z = jnp.maximum(z, 0.0) * wcol
            sc = z[0:r]
            for h in range(1, H_IDX):
                sc = sc + z[h * r:(h + 1) * r]
            sc_pages[p].append(sc)
    key_pages = []
    for p in range(n_all):
        sc = jnp.concatenate(sc_pages[p], axis=0)
        sc = jnp.where(sc == 0.0, 0.0, sc)
        if p == n_pages:
            sc = jnp.where(new_visible, sc, -jnp.inf)
        key_pages.append(_sortable_key(sc))

    def count(pred):
        tot = None
        for p, k in enumerate(key_pages):
            hit = jnp.where(pred(k, p), 1.0, 0.0)
            tot = hit if tot is None else tot + hit
        return jnp.sum(tot, axis=1, keepdims=True)

    bit = lambda b: jnp.int32(-2 ** 31 if b == 31 else 1 << b)
    thr = jnp.full((rows, 1), INT_MIN, I32)
    for it in range(16):
        c_hi, c_lo = thr ^ bit(31 - 2 * it), thr ^ bit(30 - 2 * it)
        c_both = c_hi ^ bit(30 - 2 * it)
        n_hi, n_lo, n_both = (count(lambda k, p, c=c: k >= c) for c in (c_hi, c_lo, c_both))
        thr = jnp.where(n_both >= k_top, c_both, jnp.where(n_hi >= k_top, c_hi, jnp.where(n_lo >= k_top, c_lo, thr)))
    cnt_gt = count(lambda k, p: k > thr)
    cnt_eq = count(lambda k, p: k == thr)
    need = k_top - cnt_gt
    n_bits = max(1, (n_all * PAGE - 1).bit_length())

    def tie_search():
        jv = jnp.zeros((rows, 1), I32)
        for it in range(n_bits):
            cand = jv | jnp.int32(1 << (n_bits - 1 - it))
            cnt = count(lambda k, p: (k == thr) & ((lane + p * PAGE) < cand))
            jv = jnp.where(cnt < need, cand, jv)
        return jv

    any_excess = jnp.max(jnp.where(cnt_eq > need, 1, 0)) > 0
    jv = lax.cond(any_excess, tie_search, lambda: jnp.full((rows, 1), n_all * PAGE, I32))
    am_pages = []
    for p, k in enumerate(key_pages):
        sel = (k > thr) | ((k == thr) & ((lane + p * PAGE) <= jv))
        am_pages.append(jnp.where(sel, 0.0, NEG))

    bias_last = jnp.concatenate([bias_ref[h, 0] for h in range(H_A)], axis=0)
    bias_new = jnp.concatenate([bias_ref[h, 1] for h in range(H_A)], axis=0)
    for g in range(gsz):
        tok = slice(g * r, (g + 1) * r)
        qa2 = _stack_heads(qa_ref, tok, H_A, 2 * HD)
        s_pages = []
        for p in range(n_all):
            if p < n_pages:
                s = _dot(qa2, fetch.page(0, g * n_pages + p).reshape(2 * HD, PAGE).astype(BF16))
            else:
                s = _dot_nt(qa2, _pad_new(ak_n[tok, :]).astype(BF16))
            s = s + jnp.concatenate([am_pages[p][tok]] * H_A, axis=0)
            if p == n_pages - 1:
                s = s + bias_last
            if p == n_pages:
                s = s + bias_new
            s_pages.append(s)
        p_pages, lsum = _probabilities(s_pages)
        vts = [fetch.page(1, g * n_pages + p).reshape(2 * HD, PAGE).astype(BF16) for p in range(n_pages)]
        o = _pv(p_pages, vts, _pad_new(av_n[tok, :]).astype(BF16)) / lsum
        oa_ref[tok, :] = jnp.concatenate([o[h * r:(h + 1) * r] for h in range(H_A)], axis=1)
    fetch.finish()


def _decode_call(kernel_fn, name, page_table, tok_args, tok_widths, const_args, caches, page_shapes, out_w, r):
    nb, n_pages = page_table.shape
    gsz = G_DEC
    in_specs = [pl.BlockSpec((gsz * r, w), lambda i, pt: (i, 0)) for w in tok_widths]
    in_specs += [pl.BlockSpec(a.shape, functools.partial(lambda i, pt, n: (0,) * n, n=a.ndim)) for a in const_args]
    in_specs += [_any_spec() for _ in caches]
    grid_spec = pltpu.PrefetchScalarGridSpec(
        num_scalar_prefetch=1, grid=(nb // gsz,), in_specs=in_specs,
        out_specs=pl.BlockSpec((gsz * r, out_w), lambda i, pt: (i, 0)),
        scratch_shapes=[pltpu.VMEM((2, gsz * n_pages) + s, F32) for s in page_shapes]
        + [pltpu.SemaphoreType.DMA((len(caches), 2))])
    return pl.pallas_call(
        kernel_fn,
        grid_spec=grid_spec,
        out_shape=jax.ShapeDtypeStruct((nb * r, out_w), F32),
        compiler_params=_cparams("arbitrary"),
        name=name,
    )(page_table, *tok_args, *const_args, *caches)


def _decode_a(page_table, layer, qa, qi, wi, ak_n, av_n, aki_n, kt, vt, kit, bias_dec, r):
    n_pages = page_table.shape[1]
    k_top = min(K_TOP_MAX, (n_pages * PAGE + r) // 4)
    return _decode_call(
        functools.partial(_decode_a_kernel, layer=layer, gsz=G_DEC, n_pages=n_pages, k_top=k_top), "decode_dsa",
        page_table, (qa, qi, wi, ak_n, av_n, aki_n), (768, 1024, 1024, 128, 128, 64), (bias_dec[:H_A],),
        (kt, vt, kit), ((KVH_A, HD, PAGE), (KVH_A, HD, PAGE), (D_IDX, PAGE)), 768, r)


def _decode_b_kernel(pt_ref, qb_ref, bk_n, bv_n, bias_ref, lam_ref, gsub_ref, kt_hbm, v_hbm,
                     ob_ref, kt_buf, v_buf, sems, *, layer, gsz, n_pages, lam_init):
    at_page = lambda pg: (layer, pg)
    fetch = _PageFetch(pt_ref, gsz, n_pages, (kt_hbm, v_hbm), (at_page,) * 2, (kt_buf, v_buf), sems)
    fetch.begin()
    for j in range(gsz * n_pages):
        fetch.prefetch(j)
    r = qb_ref.shape[0] // gsz
    n_all = n_pages + 1
    grp = H_B // KVH_B
    half = 2 * grp * r
    bias_last = jnp.concatenate([bias_ref[h, 0] for h in range(H_B) for _ in range(2)], axis=0)
    bias_new = jnp.concatenate([bias_ref[h, 1] for h in range(H_B) for _ in range(2)], axis=0)
    lam = _lambda_value(lam_ref, lam_init)
    gs = gsub_ref[...]
    for g in range(gsz):
        tok = slice(g * r, (g + 1) * r)
        qb2 = _stack_heads(qb_ref, tok, 2 * H_B, 4 * HD)
        s_pages = []
        for p in range(n_all):
            if p < n_pages:
                s = _dot(qb2, fetch.page(0, g * n_pages + p).reshape(4 * HD, PAGE).astype(BF16))
            else:
                s = _dot_nt(qb2, _pad_new(bk_n[tok, :]).astype(BF16))
            if p == n_pages - 1:
                s = s + bias_last
            if p == n_pages:
                s = s + bias_new
            s_pages.append(s)
        p_pages, lsum = _probabilities(s_pages)
        v_new = _pad_new(bv_n[tok, :]).astype(BF16)
        outs = []
        for kv in range(KVH_B):
            rs = slice(kv * half, (kv + 1) * half)
            acc = _dot(p_pages[-1][rs].astype(BF16), v_new[:, kv * 2 * HD:(kv + 1) * 2 * HD])
            for p in range(n_pages):
                v = v_buf[fetch.cur, g * n_pages + p, pl.ds(kv, PAGE, stride=KVH_B), :].astype(BF16)
                acc = acc + _dot(p_pages[p][rs].astype(BF16), v)
            o = acc / lsum[rs]
            for hh in range(grp):
                od = o[(2 * hh) * r:(2 * hh + 1) * r] - lam * o[(2 * hh + 1) * r:(2 * hh + 2) * r]
                ms = jnp.mean(od * od, axis=1, keepdims=True)
                outs.append((od * lax.rsqrt(ms + EPS) * gs) * (1.0 - lam_init))
        ob_ref[tok, :] = jnp.concatenate(outs, axis=1)
    fetch.finish()


def _decode_b(page_table, layer, qb, bk_n, bv_n, kt, v2, bias_dec, lamvec, gsub_row, lam_init, r):
    n_pages = page_table.shape[1]
    return _decode_call(
        functools.partial(_decode_b_kernel, layer=layer, gsz=G_DEC, n_pages=n_pages, lam_init=lam_init),
        "decode_diff", page_table, (qb, bk_n, bv_n), (2048, 256, 256), (bias_dec[H_A:], lamvec, gsub_row),
        (kt, v2), ((KVH_B, 2, HD, PAGE), (KVH_B * PAGE, 2 * HD)), 512, r)


def _decode_c_kernel(pt_ref, qc_ref, ck_n, cv_n, lft_n, kt_hbm, vt_hbm, lf_hbm,
                     oc_ref, kt_buf, vt_buf, lf_buf, sems, *, layer, gsz, n_pages):
    at_page = lambda pg: (layer, pg)
    fetch = _PageFetch(pt_ref, gsz, n_pages, (kt_hbm, vt_hbm, lf_hbm),
                       (at_page, at_page, lambda pg: (layer, slice(None), pg)), (kt_buf, vt_buf, lf_buf), sems)
    fetch.begin()
    r = qc_ref.shape[0] // gsz
    n_all = n_pages + 1
    lane = lax.broadcasted_iota(I32, (r, PAGE), 1)
    qrow = lax.broadcasted_iota(I32, (r, PAGE), 0)
    causal_new = jnp.concatenate([jnp.where(lane <= qrow, 0.0, NEG)] * H_C, axis=0)
    ri = lax.broadcasted_iota(I32, (PAGE, PAGE), 0)
    ci = lax.broadcasted_iota(I32, (PAGE, PAGE), 1)
    upper = jnp.where(ri <= ci, 1.0, 0.0).astype(BF16)
    ones = jnp.ones((PAGE, PAGE), BF16)
    nr = 8 * n_all
    rr = lax.broadcasted_iota(I32, (nr, nr), 0)
    cc = lax.broadcasted_iota(I32, (nr, nr), 1)
    prev_pages = jnp.where(((rr % 8) == (cc % 8)) & ((cc // 8) < (rr // 8)), 1.0, 0.0).astype(BF16)
    zrow = jnp.zeros((8 - H_C, PAGE), F32)
    for g in range(gsz):
        tok = slice(g * r, (g + 1) * r)
        xs = []
        for p in range(n_pages):
            fetch.prefetch(g * n_pages + p)
            xs += [fetch.page(2, g * n_pages + p), zrow]
        x = jnp.concatenate(xs + [lft_n[g * 8:(g + 1) * 8, :]], axis=0)
        cum = _dot3_lhs(x, upper) + _dot3_rhs(prev_pages, _dot3_lhs(x, ones))
        qc2 = _stack_heads(qc_ref, tok, H_C, 2 * HD)
        s_pages = []
        for p in range(n_all):
            if p < n_pages:
                s = _dot(qc2, fetch.page(0, g * n_pages + p).reshape(2 * HD, PAGE).astype(BF16))
            else:
                s = _dot_nt(qc2, _pad_new(ck_n[tok, :]).astype(BF16)) + causal_new
            decay = jnp.concatenate(
                [jnp.broadcast_to(cum[p * 8 + h:p * 8 + h + 1, :], (r, PAGE)) for h in range(H_C)], axis=0)
            s_pages.append(s - decay)
        p_pages, lsum = _probabilities(s_pages)
        vts = [fetch.page(1, g * n_pages + p).reshape(2 * HD, PAGE).astype(BF16) for p in range(n_pages)]
        o = _pv(p_pages, vts, _pad_new(cv_n[tok, :]).astype(BF16)) / lsum
        oc_ref[tok, :] = jnp.concatenate([o[h * r:(h + 1) * r] for h in range(H_C)], axis=1)
    fetch.finish()


def _decode_c(page_table, layer, qc, ck_n, cv_n, lft_new, kt, vt, lft, r):
    n_pages = page_table.shape[1]
    assert r == 8
    return _decode_call(
        functools.partial(_decode_c_kernel, layer=layer, gsz=G_DEC, n_pages=n_pages), "decode_forget",
        page_table, (qc, ck_n, cv_n, lft_new.reshape(-1, LANES)), (768, 128, 128, LANES), (),
        (kt, vt, lft), ((KVH_C, HD, PAGE), (KVH_C, HD, PAGE), (H_C, PAGE)), 768, r)


def _merge_kernel(x_ref, mod_ref, g_ref, oa_ref, ob_ref, oc_ref, wg_ref, wpa_ref, wpb_ref, wpc_ref, wo_ref,
                  o_ref, *, transposed):
    x = x_ref[...]
    gsz, r, _ = x.shape
    mod = mod_ref[...]
    h = _norm_mod(x, g_ref[...], mod[:, :, D:2 * D], mod[:, :, 0:D]).reshape(gsz * r, D).astype(BF16)
    gates = _sigmoid(_dot(h, wg_ref[...]))
    if transposed:
        oa, ob, oc = oa_ref[0].T, ob_ref[0].T, oc_ref[0].T
    else:
        oa, ob, oc = oa_ref[...], ob_ref[...], oc_ref[...]
    merged = (gates[:, 0:D] * _dot(oa.astype(BF16), wpa_ref[...])
              + gates[:, D:2 * D] * _dot(ob.astype(BF16), wpb_ref[...])
              + gates[:, 2 * D:3 * D] * _dot(oc.astype(BF16), wpc_ref[...]))
    y = _dot(merged.astype(BF16), wo_ref[...]).reshape(gsz, r, D)
    o_ref[...] = x + mod[:, :, 2 * D:3 * D] * y


def _merge(x, mod_a, g, oa, ob, oc, wg, wpa, wpb, wpc, wo, gsz, r, transposed, layer):
    nb, rr, _ = x.shape
    const = lambda a: pl.BlockSpec(a.shape, lambda i, j: (0,) * a.ndim)
    lw = lambda a: _layer_spec(a, layer)
    if transposed:
        grid = (nb, rr // r)
        x_spec = pl.BlockSpec((1, r, D), lambda i, j: (i, j, 0))
        mod_spec = pl.BlockSpec((1, 1, 3 * D), lambda i, j: (i, 0, 0))
        o_spec = lambda a: pl.BlockSpec((1, a.shape[1], r), lambda i, j: (i, 0, j))
    else:
        grid = (nb // gsz, 1)
        x_spec = pl.BlockSpec((gsz, rr, D), lambda i, j: (i, 0, 0))
        mod_spec = pl.BlockSpec((gsz, 1, 3 * D), lambda i, j: (i, 0, 0))
        o_spec = lambda a: pl.BlockSpec((gsz * rr, a.shape[1]), lambda i, j: (i, 0))
    return pl.pallas_call(
        functools.partial(_merge_kernel, transposed=transposed),
        grid=grid,
        in_specs=[x_spec, mod_spec, const(g), o_spec(oa), o_spec(ob), o_spec(oc),
                  lw(wg), lw(wpa), lw(wpb), lw(wpc), lw(wo)],
        out_specs=x_spec,
        out_shape=jax.ShapeDtypeStruct(x.shape, F32),
        compiler_params=_cparams("arbitrary", "arbitrary"),
        name="merge_prompt" if transposed else "merge_sample",
    )(x, mod_a, g, oa, ob, oc, wg, wpa, wpb, wpc, wo)


def _ffn_kernel(x_ref, mod_ref, g_ref, w1_ref, w2_ref, gf_ref, o_ref, *, final):
    x = x_ref[...]
    gsz, r, _ = x.shape
    mod = mod_ref[...]
    h = _norm_mod(x, g_ref[...], mod[:, :, D:2 * D], mod[:, :, 0:D]).reshape(gsz * r, D).astype(BF16)
    u = jnp.maximum(_dot(h, w1_ref[...]), 0.0)
    y = _dot((u * u).astype(BF16), w2_ref[...]).reshape(gsz, r, D)
    x2 = x + mod[:, :, 2 * D:3 * D] * y
    if final:
        ms = jnp.mean(x2 * x2, axis=-1, keepdims=True)
        x2 = x2 * lax.rsqrt(ms + EPS) * gf_ref[...]
    o_ref[...] = x2


def _ffn(x, mod_b, g, w1, w2, g_final, gsz, r, final, name, layer):
    nb, rr, _ = x.shape
    const = lambda a: pl.BlockSpec(a.shape, lambda i, j: (0,) * a.ndim)
    lw = lambda a: _layer_spec(a, layer)
    if gsz == 1:
        grid = (nb, rr // r)
        x_spec = pl.BlockSpec((1, r, D), lambda i, j: (i, j, 0))
        mod_spec = pl.BlockSpec((1, 1, 3 * D), lambda i, j: (i, 0, 0))
    else:
        grid = (nb // gsz, 1)
        x_spec = pl.BlockSpec((gsz, rr, D), lambda i, j: (i, 0, 0))
        mod_spec = pl.BlockSpec((gsz, 1, 3 * D), lambda i, j: (i, 0, 0))
    return pl.pallas_call(
        functools.partial(_ffn_kernel, final=final),
        grid=grid,
        in_specs=[x_spec, mod_spec, const(g), lw(w1), lw(w2), const(g_final)],
        out_specs=x_spec,
        out_shape=jax.ShapeDtypeStruct(x.shape, F32),
        compiler_params=_cparams("arbitrary", "arbitrary"),
        name=name,
    )(x, mod_b, g, w1, w2, g_final)


TM_PROMPT = 512
G_SAMPLE = 16


def kernel(x_prompt, x_sample, c_prompt, c_sample, cache_a_k, cache_a_v, cache_a_kidx, cache_b_k, cache_b_v, cache_c_k, cache_c_v, cache_c_logf, page_table, t5_table, w_ada, b_ada, g_mix, g_ffn, w_in, b_forget, lam_q1, lam_k1, lam_q2, lam_k2, g_subln, w_gate, w_pa, w_pb, w_pc, w_out, w_ff1, w_ff2, g_final):
    depth = w_in.shape[0]
    nbp, seq, _ = x_prompt.shape
    nbs, dec_seq, _ = x_sample.shape
    n_pool, page = cache_a_k.shape[1], cache_a_k.shape[2]
    n_pages = page_table.shape[1]
    past_len = n_pages * page
    assert page == PAGE and seq % max(TM_PROMPT, _CUM_T, FAR_BLOCKS * TKC) == 0 and dec_seq == 8
    assert nbs % G_SAMPLE == 0 and nbs % G_DEC == 0
    tm = min(TM_PROMPT, seq)

    nc = nbp + nbs
    mod = _ada(jnp.concatenate([c_prompt, c_sample], axis=0), w_ada, b_ada)
    mod = mod.reshape(depth, nc, 1, 6 * D)
    bias_pt, bias_dec = _bias_tiles(t5_table, past_len, dec_seq)

    kv_t = lambda c: jnp.transpose(c, (0, 1, 3, 4, 2))
    a_kt, a_vt, c_kt, c_vt = kv_t(cache_a_k), kv_t(cache_a_v), kv_t(cache_c_k), kv_t(cache_c_v)
    a_kit = jnp.transpose(cache_a_kidx, (0, 1, 3, 2))
    b_kt = jnp.transpose(cache_b_k, (0, 1, 3, 4, 5, 2))
    b_v2 = cache_b_v.reshape(depth, n_pool, page * KVH_B, 2 * HD)
    c_lft = jnp.transpose(cache_c_logf, (0, 3, 1, 2))

    wr, wt = _prep_in_prompt(w_in)
    ws = _prep_in_sample(w_in)
    wg, wo = w_gate.astype(BF16), w_out.astype(BF16)
    wpa, wpb, wpc = w_pa.astype(BF16), w_pb.astype(BF16), w_pc.astype(BF16)
    wpa_s = _pad_proj_rows(w_pa, H_A, lambda h: h // (H_A // KVH_A))
    wpc_s = _pad_proj_rows(w_pc, H_C, lambda h: h // (H_C // KVH_C))
    w1, w2 = w_ff1.astype(BF16), w_ff2.astype(BF16)

    xp, xs = x_prompt, x_sample
    rows_s = []
    stacked = [jnp.zeros(s.shape, s.dtype) for s in _stacked_row_shapes(depth, nbp, seq)]
    g_final2 = g_final.reshape(1, D)
    for l in range(depth):
        lam_init = 0.8 - 0.6 * math.exp(-0.3 * l)
        bfb = _cf_block(b_forget[l].reshape(1, H_C))
        gm, gf = g_mix[l].reshape(1, D), g_ffn[l].reshape(1, D)
        lamvec = jnp.stack([lam_q1[l], lam_k1[l], lam_q2[l], lam_k2[l]])
        mod_pa, mod_pb = mod[l, :nbp, :, :3 * D], mod[l, :nbp, :, 3 * D:]
        mod_sa, mod_sb = mod[l, nbp:, :, :3 * D], mod[l, nbp:, :, 3 * D:]
        last = l == depth - 1

        stacked, (akb, akib, bkb, ckb, lfrep, awiT, aqT, aqiT, bqT, cqT, avT, bvT, cvT) = _inproj_prompt(
            xp, mod_pa, gm, wr, wt, bfb, tm, l, stacked)
        kcat = _cum_prompt(lfrep, ckb)
        oaT = _attn_a(aqT, aqiT, awiT, akb, akib, avT, bias_pt[:H_A])
        obT = _attn_b(bqT, bkb, bvT, bias_pt[H_A:], lamvec, g_subln[l].reshape(2 * HD, 1), lam_init)
        ocT = _attn_c(cqT, kcat, cvT)
        x1 = _merge(xp, mod_pa, gm, oaT, obT, ocT, wg, wpa, wpb, wpc, wo, 1, tm, True, l)
        xp = _ffn(x1, mod_pb, gf, w1, w2, g_final2, 1, tm, last, "ffn_prompt", l)

        (sak, sav, saki, sbk, sbv, sck, scv, slf, qa, qi, wi, qb, qc, slfrep) = _inproj_sample(
            xs, mod_sa, gm, ws, bfb, G_SAMPLE, l)
        rows_s.append((sak, sav, saki, sbk, sbv, sck, scv, slf))
        lft_new = jnp.swapaxes(slfrep.reshape(nbs, dec_seq, LANES)[:, :, 0:8], 1, 2)
        lft_new = jnp.concatenate([lft_new, jnp.zeros((nbs, 8, LANES - dec_seq), F32)], axis=2)
        lane_head = jnp.arange(8)[None, :, None] < H_C
        lft_new = jnp.where(lane_head, lft_new, 0.0)
        oa = _decode_a(page_table, l, qa, qi, wi, sak, sav, saki, a_kt, a_vt, a_kit, bias_dec, dec_seq)
        ob = _decode_b(page_table, l, qb, sbk, sbv, b_kt, b_v2, bias_dec, lamvec, g_subln[l].reshape(1, 2 * HD),
                       lam_init, dec_seq)
        oc = _decode_c(page_table, l, qc, sck, scv, lft_new, c_kt, c_vt, c_lft, dec_seq)
        x1s = _merge(xs, mod_sa, gm, oa, ob, oc, wg, wpa_s, wpb, wpc_s, wo, G_SAMPLE, dec_seq, False, l)
        xs = _ffn(x1s, mod_sb, gf, w1, w2, g_final2, G_SAMPLE, dec_seq, last, "ffn_sample", l)

    def stack(rows, i):
        return jnp.stack([r[i] for r in rows])

    def stack_sample(i, shape):
        return stack(rows_s, i).reshape((depth, nbs, dec_seq) + shape)

    def prompt_t(y, shape):
        n = len(shape)
        y = y.reshape((depth, nbp) + shape + (seq,))
        return jnp.transpose(y, (0, 1, n + 2) + tuple(range(2, n + 2)))

    bv, lf, akT, avT, akiT, bkT, ckT, cvT = stacked
    out_p = (prompt_t(akT, (KVH_A, HD)), prompt_t(avT, (KVH_A, HD)), prompt_t(akiT, (D_IDX,)),
             prompt_t(bkT, (KVH_B, 2, HD)), bv.reshape(depth, nbp, seq, KVH_B, 2 * HD),
             prompt_t(ckT, (KVH_C, HD)), prompt_t(cvT, (KVH_C, HD)), lf)
    out_s = (stack_sample(0, (KVH_A, HD)), stack_sample(1, (KVH_A, HD)), stack_sample(2, (D_IDX,)),
             stack_sample(3, (KVH_B, 2, HD)), stack_sample(4, (KVH_B, 2 * HD)), stack_sample(5, (KVH_C, HD)),
             stack_sample(6, (KVH_C, HD)), stack_sample(7, (H_C,)))
    return (xp, xs) + out_p + out_s
```

```python
import functools
import math

import numpy as np
import jax
import jax.numpy as jnp
from jax import lax
from jax.experimental import pallas as pl
from jax.experimental.pallas import tpu as pltpu

F32 = jnp.float32
BF16 = jnp.bfloat16
I32 = jnp.int32

D = 1024
HD = 64
H_A, KVH_A = 6, 2
H_IDX, D_IDX = 8, 64
K_TOP_MAX = 256
IDX_W_SCALE = (H_IDX ** -0.5) * (D_IDX ** -0.5)
H_B, KVH_B = 4, 2
H_C, KVH_C = 6, 2
N_BUCKETS, T5_MAX_EXACT, T5_MAX_DIST = 32, 16, 128
D_FF = 4 * D
EPS = 1e-6
N_HEADS_T5 = H_A + H_B
PAGE = 128

LANES = 128
TQ = 256
TKC = 128
Q_BLOCKS = TQ // TKC
NEAR_BLOCKS = Q_BLOCKS + 1
FAR_BLOCKS = 4
NEG = -1e30
INT_MIN = -2 ** 31
VMEM_LIMIT = 56 * 1024 * 1024

_W = dict(aq=H_A * HD, ak=KVH_A * HD, av=KVH_A * HD, aqi=H_IDX * D_IDX, aki=D_IDX, awi=H_IDX,
          bq=H_B * 2 * HD, bk=KVH_B * 2 * HD, bv=KVH_B * 2 * HD,
          cq=H_C * HD, ck=KVH_C * HD, cv=KVH_C * HD, cf=H_C)
_OFF = {}
_o = 0
for _k, _v in _W.items():
    _OFF[_k] = _o
    _o += _v

_R_AK, _R_AV, _R_AKI, _R_BK, _R_BV, _R_CK, _R_CV, _R_CF = 0, 128, 256, 384, 640, 896, 1024, 1152
N_ROWS = 1280
_P_AK, _P_AKI, _P_BK, _P_CK, _P_CF, _P_BV = 0, 128, 256, 512, 640, 768
N_PROWS = 1024
_T_AQ, _T_AQI, _T_AWI, _T_BQ, _T_CQ, _T_AV, _T_BV, _T_CV = 0, 384, 896, 912, 1424, 1808, 1936, 2192
_T_AK, _T_AKI, _T_BK, _T_CK = 2320, 2448, 2512, 2768
N_T = 2896
_S_AQ, _S_AQI, _S_AWI, _S_BQ, _S_CQ = 0, 768, 1792, 2816, 4864
N_SQ = 5632


def _cparams(*sem):
    return pltpu.CompilerParams(dimension_semantics=sem, vmem_limit_bytes=VMEM_LIMIT)


def _dot(a, b):
    return jnp.dot(a, b, preferred_element_type=F32)


def _dot_nt(a, b):
    return lax.dot_general(a, b, (((1,), (1,)), ((), ())), preferred_element_type=F32)


def _split3(x):
    hi = x.astype(BF16)
    r1 = x - hi.astype(F32)
    mid = r1.astype(BF16)
    lo = (r1 - mid.astype(F32)).astype(BF16)
    return hi, mid, lo


def _dot3_rhs(a_bf16, x):
    hi, mid, lo = _split3(x)
    return _dot(a_bf16, hi) + _dot(a_bf16, mid) + _dot(a_bf16, lo)


def _dot3_lhs(x, b_bf16):
    hi, mid, lo = _split3(x)
    return _dot(hi, b_bf16) + _dot(mid, b_bf16) + _dot(lo, b_bf16)


def _norm_mod(x, g, sc, sh):
    ms = jnp.mean(x * x, axis=-1, keepdims=True)
    return (x * lax.rsqrt(ms + EPS) * g) * (1.0 + sc) + sh


def _log_sigmoid(z):
    return jnp.minimum(z, 0.0) - jnp.log1p(jnp.exp(-jnp.abs(z)))


def _sigmoid(z):
    return 1.0 / (1.0 + jnp.exp(-z))


def _sortable_key(x):
    bits = lax.bitcast_convert_type(x, I32)
    return bits ^ ((bits >> 31) & 0x7FFFFFFF)


def _ada_kernel(c_ref, w_ref, b_ref, o_ref):
    c = c_ref[...]
    s = (c * _sigmoid(c)).astype(BF16)
    o_ref[0] = _dot(s, w_ref[0].astype(BF16)) + b_ref[0]


def _ada(c_all, w_ada, b_ada):
    depth = w_ada.shape[0]
    nc = c_all.shape[0]
    nt = 6
    return pl.pallas_call(
        _ada_kernel,
        grid=(depth, nt),
        in_specs=[pl.BlockSpec((nc, D), lambda l, j: (0, 0)),
                  pl.BlockSpec((1, D, D), lambda l, j: (l, 0, j)),
                  pl.BlockSpec((1, 1, D), lambda l, j: (l, 0, j))],
        out_specs=pl.BlockSpec((1, nc, D), lambda l, j: (l, 0, j)),
        out_shape=jax.ShapeDtypeStruct((depth, nc, 6 * D), F32),
        compiler_params=_cparams("arbitrary", "arbitrary"),
        name="ada",
    )(c_all, w_ada, b_ada.reshape(depth, 1, 6 * D))


def _t5_bucket_np(rel):
    n = np.maximum(rel, 0)
    nf = np.maximum(n, 1).astype(np.float32)
    large = T5_MAX_EXACT + (np.log(nf / np.float32(T5_MAX_EXACT)) / np.float32(math.log(T5_MAX_DIST / T5_MAX_EXACT))
                            * np.float32(N_BUCKETS - T5_MAX_EXACT)).astype(np.int32)
    return np.where(n < T5_MAX_EXACT, n, np.minimum(large, N_BUCKETS - 1)).astype(np.int32)


def _bias_bucket_tables(past_len, dec_seq):
    s = np.arange(TKC)[:, None]
    t = np.arange(TQ)[None, :]
    tiles = []
    for u in range(NEAR_BLOCKS):
        rel = t - s - (TQ - TKC) + u * TKC
        tiles.append(np.where(rel >= 0, _t5_bucket_np(rel), -1))
    prompt = np.stack(tiles).astype(np.int32)
    i = np.arange(dec_seq)[:, None]
    lane = np.arange(PAGE)[None, :]
    rel_last = (past_len + i) - (past_len - PAGE + lane)
    rel_new = i - lane
    dl = _t5_bucket_np(rel_last)
    dn = np.where((rel_new >= 0) & (lane < dec_seq), _t5_bucket_np(rel_new), -1)
    dec = np.stack([dl, dn]).astype(np.int32)
    return prompt, dec


def _bias_kernel(tab_ref, pb_ref, db_ref, pt_ref, dt_ref):
    pb = pb_ref[...]
    db = db_ref[...]
    for h in range(N_HEADS_T5):
        def lut(bk):
            acc = jnp.zeros(bk.shape, F32)
            for b in range(N_BUCKETS):
                acc = jnp.where(bk == b, tab_ref[b, h], acc)
            return jnp.where(bk < 0, NEG, acc - tab_ref[N_BUCKETS - 1, h])
        pt_ref[h] = lut(pb)
        dt_ref[h] = lut(db)


def _bias_tiles(t5_table, past_len, dec_seq):
    pb, db = _bias_bucket_tables(past_len, dec_seq)
    return pl.pallas_call(
        _bias_kernel,
        in_specs=[pl.BlockSpec(memory_space=pltpu.SMEM),
                  pl.BlockSpec(memory_space=pltpu.VMEM),
                  pl.BlockSpec(memory_space=pltpu.VMEM)],
        out_specs=[pl.BlockSpec(memory_space=pltpu.VMEM), pl.BlockSpec(memory_space=pltpu.VMEM)],
        out_shape=[jax.ShapeDtypeStruct((N_HEADS_T5,) + pb.shape, F32),
                   jax.ShapeDtypeStruct((N_HEADS_T5,) + db.shape, F32)],
        name="t5_bias_tiles",
    )(t5_table, jnp.asarray(pb), jnp.asarray(db))


def _cf_block(cf):
    n = cf.shape[0]
    z2 = jnp.zeros((n, 2), cf.dtype)
    return jnp.concatenate([cf, z2, cf, z2, cf, z2, jnp.zeros((n, LANES - 24), cf.dtype)], axis=1)


class _Plan:
    def __init__(self):
        self.runs = []

    def seg(self, name, lo=0, hi=None, scale=1.0):
        hi = _W[name] if hi is None else hi
        return self.raw(_OFF[name] + lo, _OFF[name] + hi, scale)

    def raw(self, lo, hi, scale=1.0, repeat=1):
        self.runs.append((lo, hi, scale, repeat))
        return self

    def zeros(self, n):
        if n > 0:
            self.runs.append((None, n, 0.0, 1))
        return self

    def cf_block(self):
        for _ in range(3):
            self.seg("cf").zeros(8 - H_C)
        return self.zeros(LANES - 24)

    def padded_heads(self, name, n_heads, slot_of, n_slots, scale=1.0):
        for h in range(n_heads):
            s = slot_of(h)
            self.zeros(s * HD).seg(name, h * HD, (h + 1) * HD, scale).zeros((n_slots - 1 - s) * HD)
        return self

    def rows_block(self):
        return (self.seg("ak").seg("av").seg("aki").zeros(64).seg("bk").seg("bv").seg("ck").seg("cv").cf_block())

    def arrange(self, w_rows):
        depth, _, width = w_rows.shape
        pieces = []
        for lo, hi, scale, repeat in self.runs:
            if lo is None:
                pieces.append(jnp.zeros((depth, hi, width), w_rows.dtype))
                continue
            piece = w_rows[:, lo:hi, :]
            if scale != 1.0:
                piece = piece * scale
            if repeat != 1:
                piece = jnp.repeat(piece, repeat, axis=1)
            pieces.append(piece)
        return jnp.concatenate(pieces, axis=1).astype(BF16)


_QSCALE = HD ** -0.5


def _prep_in_prompt(w_in_t):
    w_t = (_Plan().seg("aq", scale=_QSCALE).seg("aqi").seg("awi").zeros(8).seg("bq", scale=_QSCALE)
           .seg("cq", scale=_QSCALE).seg("av").seg("bv").seg("cv").seg("ak").seg("aki").seg("bk").seg("ck"))
    w_r = _Plan().seg("ak").seg("aki").zeros(64).seg("bk").seg("ck").cf_block().seg("bv")
    return jnp.swapaxes(w_r.arrange(w_in_t), 1, 2), w_t.arrange(w_in_t)


def _prep_in_sample(w_in_t):
    plan = _Plan()
    plan.padded_heads("aq", H_A, lambda h: h // (H_A // KVH_A), 2, _QSCALE)
    plan.padded_heads("aqi", H_IDX, lambda h: 0, 2)
    plan.raw(_OFF["awi"], _OFF["awi"] + H_IDX, repeat=LANES)
    plan.padded_heads("bq", 2 * H_B, lambda hc: (hc // 2 // (H_B // KVH_B)) * 2 + hc % 2, 4, _QSCALE)
    plan.padded_heads("cq", H_C, lambda h: h // (H_C // KVH_C), 2, _QSCALE)
    return jnp.swapaxes(plan.rows_block().arrange(w_in_t), 1, 2)


def _pad_proj_rows(wp, n_heads, kv_of):
    plan = _Plan()
    for h in range(n_heads):
        s = kv_of(h)
        plan.zeros(s * HD).raw(h * HD, (h + 1) * HD).zeros((1 - s) * HD)
    return plan.arrange(wp)


def _store_rows(pr, bf, ak_o, av_o, aki_o, bk_o, bv_o, ck_o, cv_o, lf_o, idx):
    ak_o[idx] = pr[:, _R_AK:_R_AK + 128]
    av_o[idx] = pr[:, _R_AV:_R_AV + 128]
    aki_o[idx] = pr[:, _R_AKI:_R_AKI + 64]
    bk_o[idx] = pr[:, _R_BK:_R_BK + 256]
    bv_o[idx] = pr[:, _R_BV:_R_BV + 256]
    ck_o[idx] = pr[:, _R_CK:_R_CK + 128]
    cv_o[idx] = pr[:, _R_CV:_R_CV + 128]
    lf = _log_sigmoid(pr[:, _R_CF:_R_CF + 128] + bf)
    lf_o[idx] = lf[:, 0:H_C]
    return lf


N_STACKED = 8


def _inproj_p_kernel(x_ref, mod_ref, g_ref, wr_ref, wt_ref, bf_ref, *refs):
    (bv_o, lf_o, akT_o, avT_o, akiT_o, bkT_o, ckT_o, cvT_o,
     akb_o, akib_o, bkb_o, ckb_o, lfrep_o, awiT_o,
     aqT_o, aqiT_o, bqT_o, cqT_o, avTb_o, bvTb_o, cvTb_o) = refs[N_STACKED:]
    x = x_ref[0]
    mod = mod_ref[0]
    h = _norm_mod(x, g_ref[...], mod[:, D:2 * D], mod[:, 0:D])
    pr = _dot(h.astype(BF16), wr_ref[...])
    bv_o[0] = pr[:, _P_BV:_P_BV + 256]
    lf = _log_sigmoid(pr[:, _P_CF:_P_CF + 128] + bf_ref[...])
    lf_o[0] = lf[:, 0:H_C]
    lfrep_o[0] = lf
    akb_o[0] = pr[:, _P_AK:_P_AK + 128].astype(BF16)
    akib_o[0] = pr[:, _P_AKI:_P_AKI + 64].astype(BF16)
    bkb_o[0] = pr[:, _P_BK:_P_BK + 256].astype(BF16)
    ckb_o[0] = pr[:, _P_CK:_P_CK + 128].astype(BF16)
    pt = _dot(wt_ref[...], h.T.astype(BF16))
    akT_o[0] = pt[_T_AK:_T_AK + 128]
    avT_o[0] = pt[_T_AV:_T_AV + 128]
    akiT_o[0] = pt[_T_AKI:_T_AKI + 64]
    bkT_o[0] = pt[_T_BK:_T_BK + 256]
    ckT_o[0] = pt[_T_CK:_T_CK + 128]
    cvT_o[0] = pt[_T_CV:_T_CV + 128]
    awiT_o[0] = pt[_T_AWI:_T_AWI + 16]
    aqT_o[0] = pt[_T_AQ:_T_AQ + 384].astype(BF16)
    aqiT_o[0] = pt[_T_AQI:_T_AQI + 512].astype(BF16)
    bqT_o[0] = pt[_T_BQ:_T_BQ + 512].astype(BF16)
    cqT_o[0] = pt[_T_CQ:_T_CQ + 384].astype(BF16)
    avTb_o[0] = pt[_T_AV:_T_AV + 128].astype(BF16)
    bvTb_o[0] = pt[_T_BV:_T_BV + 256].astype(BF16)
    cvTb_o[0] = pt[_T_CV:_T_CV + 128].astype(BF16)


def _layer_spec(a, layer):
    return pl.BlockSpec((None,) + a.shape[1:], lambda *_: (layer,) + (0,) * (a.ndim - 1))


def _stacked_row_shapes(depth, b, l):
    return ([jax.ShapeDtypeStruct((depth, b, l, w), F32) for w in (256, H_C)]
            + [jax.ShapeDtypeStruct((depth, b, r, l), F32) for r in (128, 128, 64, 256, 128, 128)])


def _inproj_prompt(x, mod_a, g, wr, wt, bfb, tm, layer, stacked):
    b, l, _ = x.shape
    row = lambda w, dt: jax.ShapeDtypeStruct((b, l, w), dt)
    tr = lambda r, dt: jax.ShapeDtypeStruct((b, r, l), dt)
    row_spec = lambda w: pl.BlockSpec((1, tm, w), lambda i, j: (i, j, 0))
    tr_spec = lambda r: pl.BlockSpec((1, r, tm), lambda i, j: (i, 0, j))
    st_specs = ([pl.BlockSpec((None, 1, tm, a.shape[3]), lambda i, j: (layer, i, j, 0)) for a in stacked[:2]]
                + [pl.BlockSpec((None, 1, a.shape[2], tm), lambda i, j: (layer, i, 0, j)) for a in stacked[2:]])
    rows = [(128, BF16), (64, BF16), (256, BF16), (128, BF16), (128, F32)]
    trs = [(16, F32), (384, BF16), (512, BF16), (512, BF16), (384, BF16), (128, BF16), (256, BF16), (128, BF16)]
    n_in = 6
    outs = pl.pallas_call(
        _inproj_p_kernel,
        grid=(b, l // tm),
        in_specs=[pl.BlockSpec((1, tm, D), lambda i, j: (i, j, 0)),
                  pl.BlockSpec((1, 1, 3 * D), lambda i, j: (i, 0, 0)),
                  pl.BlockSpec((1, D), lambda i, j: (0, 0)),
                  _layer_spec(wr, layer), _layer_spec(wt, layer),
                  pl.BlockSpec((1, LANES), lambda i, j: (0, 0))] + [_any_spec() for _ in stacked],
        out_specs=st_specs + [row_spec(w) for w, _ in rows] + [tr_spec(r) for r, _ in trs],
        out_shape=[jax.ShapeDtypeStruct(a.shape, a.dtype) for a in stacked]
        + [row(w, dt) for w, dt in rows] + [tr(r, dt) for r, dt in trs],
        input_output_aliases={n_in + k: k for k in range(N_STACKED)},
        compiler_params=_cparams("arbitrary", "arbitrary"),
        name="inproj_prompt",
    )(x, mod_a, g, wr, wt, bfb, *stacked)
    return outs[:N_STACKED], outs[N_STACKED:]


def _inproj_s_kernel(x_ref, mod_ref, g_ref, w_ref, bf_ref,
                     ak_o, av_o, aki_o, bk_o, bv_o, ck_o, cv_o, lf_o,
                     qa_o, qi_o, wi_o, qb_o, qc_o, lfrep_o):
    x = x_ref[...]
    gsz, r, _ = x.shape
    mod = mod_ref[...]
    h = _norm_mod(x, g_ref[...], mod[:, :, D:2 * D], mod[:, :, 0:D]).reshape(gsz * r, D)
    pr = _dot(h.astype(BF16), w_ref[...])
    lf = _store_rows(pr[:, N_SQ:], bf_ref[...], ak_o, av_o, aki_o, bk_o, bv_o, ck_o, cv_o, lf_o,
                     (slice(None), slice(None)))
    lfrep_o[...] = lf
    qa_o[...] = pr[:, _S_AQ:_S_AQ + 768].astype(BF16)
    qi_o[...] = pr[:, _S_AQI:_S_AQI + 1024].astype(BF16)
    wi_o[...] = pr[:, _S_AWI:_S_AWI + 1024]
    qb_o[...] = pr[:, _S_BQ:_S_BQ + 2048].astype(BF16)
    qc_o[...] = pr[:, _S_CQ:_S_CQ + 768].astype(BF16)


def _inproj_sample(x, mod_a, g, w, bfb, gsz, layer):
    nb, r, _ = x.shape
    n = nb * r
    tm = gsz * r
    row_w = [128, 128, 64, 256, 256, 128, 128, H_C]
    outs = [(w_, F32) for w_ in row_w] + [(768, BF16), (1024, BF16), (1024, F32), (2048, BF16), (768, BF16), (128, F32)]
    return pl.pallas_call(
        _inproj_s_kernel,
        grid=(nb // gsz,),
        in_specs=[pl.BlockSpec((gsz, r, D), lambda i: (i, 0, 0)),
                  pl.BlockSpec((gsz, 1, 3 * D), lambda i: (i, 0, 0)),
                  pl.BlockSpec((1, D), lambda i: (0, 0)),
                  _layer_spec(w, layer),
                  pl.BlockSpec((1, LANES), lambda i: (0, 0))],
        out_specs=[pl.BlockSpec((tm, w_), lambda i: (i, 0)) for w_, _ in outs],
        out_shape=[jax.ShapeDtypeStruct((n, w_), dt) for w_, dt in outs],
        compiler_params=_cparams("arbitrary"),
        name="inproj_sample",
    )(x, mod_a, g, w, bfb)


_CUM_T = 256


def _cum_kernel(lf_ref, k_ref, o_ref):
    l = lf_ref.shape[1]
    r = lax.broadcasted_iota(I32, (_CUM_T, _CUM_T), 0)
    c = lax.broadcasted_iota(I32, (_CUM_T, _CUM_T), 1)
    tri = jnp.where(c <= r, 1.0, 0.0).astype(BF16)
    lane = lax.broadcasted_iota(I32, (_CUM_T, LANES), 1)
    carry = jnp.zeros((1, LANES), F32)
    for i in range(l // _CUM_T):
        rows = slice(i * _CUM_T, (i + 1) * _CUM_T)
        cum = _dot3_rhs(tri, lf_ref[0, rows, :]) + carry
        carry = cum[_CUM_T - 1:_CUM_T, :]
        hi, mid, lo = _split3(cum)
        piece = jnp.where(lane < 8, hi, jnp.where(lane < 16, mid, lo))
        o_ref[0, rows, 0:LANES] = k_ref[0, rows, :]
        o_ref[0, rows, LANES:2 * LANES] = -piece


def _cum_prompt(lfrep, ckb):
    b, l, _ = lfrep.shape
    return pl.pallas_call(
        _cum_kernel,
        grid=(b,),
        in_specs=[pl.BlockSpec((1, l, LANES), lambda i: (i, 0, 0)),
                  pl.BlockSpec((1, l, LANES), lambda i: (i, 0, 0))],
        out_specs=pl.BlockSpec((1, l, 2 * LANES), lambda i: (i, 0, 0)),
        out_shape=jax.ShapeDtypeStruct((b, l, 2 * LANES), BF16),
        compiler_params=_cparams("arbitrary"),
        name="forget_cumsum",
    )(lfrep, ckb)


def _online_update(s, m_scr, acc_scr, vta):
    m_prev = m_scr[...]
    m_new = jnp.maximum(m_prev, jnp.max(s, axis=0, keepdims=True))
    alpha = jnp.exp(m_prev - m_new)
    p = jnp.exp(s - m_new).astype(BF16)
    m_scr[...] = m_new
    n_g = len(vta)
    w = s.shape[1] // n_g
    for g in range(n_g):
        pv = _dot(vta[g], p[:, g * w:(g + 1) * w])
        acc_scr[g] = acc_scr[g] * alpha[:, g * w:(g + 1) * w] + pv


def _key_rows(start, n):
    return pl.ds(pl.multiple_of(start * TKC, TKC), n * TKC)


def _v_aug(vt_ref, start, n, g, rows):
    v = vt_ref[0, g * rows:(g + 1) * rows, _key_rows(start, n)]
    return jnp.concatenate([v, jnp.ones((16, n * TKC), BF16)], axis=0)


def _add_near(s, n, tiles):
    parts = []
    first_near = max(n - len(tiles), 0)
    if first_near > 0:
        parts.append(s[0:first_near * TKC])
    for blk in range(first_near, n):
        tile = tiles[n - 1 - blk]
        rows = s[blk * TKC:(blk + 1) * TKC]
        parts.append(rows if tile is None else rows + tile)
    return parts[0] if len(parts) == 1 else jnp.concatenate(parts, axis=0)


def _chunk_loop(qi, chunk):
    n_tot = (qi + 1) * Q_BLOCKS
    n_big = jnp.maximum(n_tot - NEAR_BLOCKS, 0) // FAR_BLOCKS

    def far(c, carry):
        chunk(c * FAR_BLOCKS, FAR_BLOCKS, False)
        return carry
    lax.fori_loop(0, n_big, far, 0)
    start = n_big * FAR_BLOCKS
    n_tail = n_tot - start
    for v in range(Q_BLOCKS, FAR_BLOCKS + NEAR_BLOCKS, Q_BLOCKS):
        @pl.when(n_tail == v)
        def _(v=v):
            chunk(start, v, True)


def _attn_a_kernel(qT_ref, qiT_ref, wT_ref, k_ref, ki_ref, vT_ref, bias_ref, oT_ref,
                   qpad, qipad, key_scr, am_scr, j_scr, m_scr, acc_scr, *, k_top):
    b = pl.program_id(0)
    qi = pl.program_id(1)
    n_chunk = (qi + 1) * Q_BLOCKS
    grp = H_A // KVH_A

    @pl.when((b == 0) & (qi == 0))
    def _():
        qpad[...] = jnp.zeros_like(qpad)
    for h in range(H_A):
        g = h // grp
        qpad[g * HD:(g + 1) * HD, h * TQ:(h + 1) * TQ] = qT_ref[0, h * HD:(h + 1) * HD, :]
    for h in range(H_IDX):
        qipad[:, h * TQ:(h + 1) * TQ] = qiT_ref[0, h * D_IDX:(h + 1) * D_IDX, :]
    w = wT_ref[0, 0:H_IDX, :] * IDX_W_SCALE
    grows = FAR_BLOCKS * TKC
    n_grp = (n_chunk + FAR_BLOCKS - 1) // FAR_BLOCKS
    row = lax.broadcasted_iota(I32, (TKC, TQ), 0)
    col = lax.broadcasted_iota(I32, (TKC, TQ), 1) + qi * TQ
    grow = lax.broadcasted_iota(I32, (grows, TQ), 0)

    def score_group(gi, carry):
        for bi in range(FAR_BLOCKS):
            start = pl.multiple_of(gi * grows + bi * TKC, TKC)
            sl = pl.ds(start, TKC)
            s = _dot(ki_ref[0, sl, :], qipad[...])
            sc = jnp.zeros((TKC, TQ), F32)
            for h in range(H_IDX):
                sc = sc + jnp.maximum(s[:, h * TQ:(h + 1) * TQ], 0.0) * w[h:h + 1, :]
            sc = jnp.where(sc == 0.0, 0.0, sc)
            sc = jnp.where(row + start > col, -jnp.inf, sc)
            key_scr[sl, :] = _sortable_key(sc)
        return carry
    lax.fori_loop(0, n_grp, score_group, 0)

    def count(pred):
        def body(gi, cnt):
            start = pl.multiple_of(gi * grows, grows)
            hit = jnp.where(pred(key_scr[pl.ds(start, grows), :], start), 1, 0).astype(I32)
            return cnt + jnp.sum(hit.reshape(grows // 8, 8, TQ), axis=0)
        cnt8 = lax.fori_loop(0, n_grp, body, jnp.zeros((8, TQ), I32))
        return jnp.sum(cnt8, axis=0, keepdims=True)

    needs_search = (qi + 1) * TQ > k_top

    @pl.when(jnp.logical_not(needs_search))
    def _():
        def body(gi, carry):
            am_scr[pl.ds(pl.multiple_of(gi * grows, grows), grows), :] = jnp.zeros((grows, TQ), F32)
            return carry
        lax.fori_loop(0, n_grp, body, 0)

    @pl.when(needs_search)
    def _():
        def it_body(it, t):
            cand = t ^ lax.shift_left(jnp.int32(1), 31 - it)
            cnt = count(lambda k, start: k >= cand)
            return jnp.where(cnt >= k_top, cand, t)
        thr = lax.fori_loop(0, 32, it_body, jnp.full((1, TQ), INT_MIN, I32))

        def recode(gi, cnts):
            start = pl.multiple_of(gi * grows, grows)
            k = key_scr[pl.ds(start, grows), :]
            gt, eq = k > thr, k == thr
            key_scr[pl.ds(start, grows), :] = jnp.where(gt, -1, jnp.where(eq, grow + start, 2 ** 31 - 1))
            fold = lambda m: jnp.sum(jnp.where(m, 1, 0).astype(I32).reshape(grows // 8, 8, TQ), axis=0)
            return cnts[0] + fold(gt), cnts[1] + fold(eq)
        z8 = jnp.zeros((8, TQ), I32)
        gt8, eq8 = lax.fori_loop(0, n_grp, recode, (z8, z8))
        cnt_gt = jnp.sum(gt8, axis=0, keepdims=True)
        cnt_eq = jnp.sum(eq8, axis=0, keepdims=True)
        need = k_top - cnt_gt
        n_bits = max(1, (key_scr.shape[0] - 1).bit_length())
        j_scr[...] = jnp.full((1, TQ), key_scr.shape[0], I32)

        @pl.when(jnp.max(jnp.where(cnt_eq > need, 1, 0)) > 0)
        def _():
            def tie_body(it, jv):
                cand = jv | lax.shift_left(jnp.int32(1), n_bits - 1 - it)
                ties_before = count(lambda code, start: code < cand) - cnt_gt
                return jnp.where(ties_before < need, cand, jv)
            j_scr[...] = lax.fori_loop(0, n_bits, tie_body, jnp.zeros((1, TQ), I32))
        jv = j_scr[...]

        def body(gi, carry):
            start = pl.multiple_of(gi * grows, grows)
            am_scr[pl.ds(start, grows), :] = jnp.where(key_scr[pl.ds(start, grows), :] <= jv, 0.0, NEG)
            return carry
        lax.fori_loop(0, n_grp, body, 0)

    m_scr[...] = jnp.full(m_scr.shape, NEG, F32)
    acc_scr[...] = jnp.zeros_like(acc_scr)

    def chunk(start, n, near):
        sl = _key_rows(start, n)
        s = _dot(k_ref[0, sl, :], qpad[...])
        s = s + jnp.concatenate([am_scr[sl, :]] * H_A, axis=1)
        if near:
            s = _add_near(s, n, [jnp.concatenate([bias_ref[h, u] for h in range(H_A)], axis=1)
                                 for u in range(NEAR_BLOCKS)])
        _online_update(s, m_scr, acc_scr, [_v_aug(vT_ref, start, n, g, HD) for g in range(KVH_A)])

    _chunk_loop(qi, chunk)

    for h in range(H_A):
        g, hh = h // grp, h % grp
        a = acc_scr[g][:, hh * TQ:(hh + 1) * TQ]
        oT_ref[0, h * HD:(h + 1) * HD, :] = a[0:HD] / a[HD:HD + 1]


def _attn_a(aqT, aqiT, awiT, akb, akib, avT, bias_pt):
    b, _, l = aqT.shape
    k_top = min(K_TOP_MAX, l // 4)
    blk_t = lambda r: pl.BlockSpec((1, r, TQ), lambda i, j: (i, 0, j))
    full = lambda s1, s2: pl.BlockSpec((1, s1, s2), lambda i, j: (i, 0, 0))
    return pl.pallas_call(
        functools.partial(_attn_a_kernel, k_top=k_top),
        grid=(b, l // TQ),
        in_specs=[blk_t(384), blk_t(512), blk_t(16), full(l, 128), full(l, 64), full(128, l),
                  pl.BlockSpec((H_A, NEAR_BLOCKS, TKC, TQ), lambda i, j: (0, 0, 0, 0))],
        out_specs=blk_t(384),
        out_shape=jax.ShapeDtypeStruct((b, 384, l), F32),
        scratch_shapes=[pltpu.VMEM((128, H_A * TQ), BF16),
                        pltpu.VMEM((D_IDX, H_IDX * TQ), BF16),
                        pltpu.VMEM((l, TQ), I32),
                        pltpu.VMEM((l, TQ), F32),
                        pltpu.VMEM((1, TQ), I32),
                        pltpu.VMEM((1, H_A * TQ), F32),
                        pltpu.VMEM((KVH_A, HD + 16, (H_A // KVH_A) * TQ), F32)],
        compiler_params=_cparams("arbitrary", "arbitrary"),
        name="attn_dsa_prompt",
    )(aqT, aqiT, awiT, akb, akib, avT, bias_pt)


def _lambda_value(lam_ref, lam_init):
    lv = lam_ref[...]
    s1 = jnp.sum(lv[0:1] * lv[1:2], axis=1, keepdims=True)
    s2 = jnp.sum(lv[2:3] * lv[3:4], axis=1, keepdims=True)
    return jnp.exp(s1) - jnp.exp(s2) + lam_init


def _attn_b_kernel(qT_ref, k_ref, vT_ref, bias_ref, lam_ref, gsub_ref, oT_ref,
                   qpad, m_scr, acc_scr, *, lam_init):
    b = pl.program_id(0)
    qi = pl.program_id(1)
    grp = H_B // KVH_B

    @pl.when((b == 0) & (qi == 0))
    def _():
        qpad[...] = jnp.zeros_like(qpad)
    for h in range(H_B):
        for c in range(2):
            slot = (h // grp) * 2 + c
            hc = h * 2 + c
            qpad[slot * HD:(slot + 1) * HD, hc * TQ:(hc + 1) * TQ] = qT_ref[0, hc * HD:(hc + 1) * HD, :]

    m_scr[...] = jnp.full(m_scr.shape, NEG, F32)
    acc_scr[...] = jnp.zeros_like(acc_scr)

    def chunk(start, n, near):
        s = _dot(k_ref[0, _key_rows(start, n), :], qpad[...])
        if near:
            s = _add_near(s, n, [jnp.concatenate([bias_ref[hc // 2, u] for hc in range(2 * H_B)], axis=1)
                                 for u in range(NEAR_BLOCKS)])
        _online_update(s, m_scr, acc_scr, [_v_aug(vT_ref, start, n, g, 2 * HD) for g in range(KVH_B)])

    _chunk_loop(qi, chunk)

    lam = _lambda_value(lam_ref, lam_init)
    gs = gsub_ref[...]
    for h in range(H_B):
        g, hh = h // grp, h % grp
        a1 = acc_scr[g][:, (hh * 2) * TQ:(hh * 2 + 1) * TQ]
        a2 = acc_scr[g][:, (hh * 2 + 1) * TQ:(hh * 2 + 2) * TQ]
        o = a1[0:2 * HD] / a1[2 * HD:2 * HD + 1] - lam * (a2[0:2 * HD] / a2[2 * HD:2 * HD + 1])
        ms = jnp.mean(o * o, axis=0, keepdims=True)
        oT_ref[0, h * 2 * HD:(h + 1) * 2 * HD, :] = (o * lax.rsqrt(ms + EPS) * gs) * (1.0 - lam_init)


def _attn_b(bqT, bkb, bvT, bias_pt, lamvec, gsub_col, lam_init):
    b, _, l = bqT.shape
    blk_t = lambda r: pl.BlockSpec((1, r, TQ), lambda i, j: (i, 0, j))
    full = lambda s1, s2: pl.BlockSpec((1, s1, s2), lambda i, j: (i, 0, 0))
    return pl.pallas_call(
        functools.partial(_attn_b_kernel, lam_init=lam_init),
        grid=(b, l // TQ),
        in_specs=[blk_t(512), full(l, 256), full(256, l),
                  pl.BlockSpec((H_B, NEAR_BLOCKS, TKC, TQ), lambda i, j: (0, 0, 0, 0)),
                  pl.BlockSpec((4, HD), lambda i, j: (0, 0)),
                  pl.BlockSpec((2 * HD, 1), lambda i, j: (0, 0))],
        out_specs=blk_t(512),
        out_shape=jax.ShapeDtypeStruct((b, 512, l), F32),
        scratch_shapes=[pltpu.VMEM((256, 2 * H_B * TQ), BF16),
                        pltpu.VMEM((1, 2 * H_B * TQ), F32),
                        pltpu.VMEM((KVH_B, 2 * HD + 16, 2 * (H_B // KVH_B) * TQ), F32)],
        compiler_params=_cparams("arbitrary", "arbitrary"),
        name="attn_diff_prompt",
    )(bqT, bkb, bvT, bias_pt, lamvec, gsub_col)


def _attn_c_kernel(qT_ref, k_ref, vT_ref, oT_ref, qpad, m_scr, acc_scr):
    b = pl.program_id(0)
    qi = pl.program_id(1)
    grp = H_C // KVH_C

    @pl.when((b == 0) & (qi == 0))
    def _():
        r = lax.broadcasted_iota(I32, (256, H_C * TQ), 0) - 128
        cblk = lax.broadcasted_iota(I32, (256, H_C * TQ), 1) // TQ
        ones = (r >= 0) & (r < 24) & ((r % 8) == cblk)
        qpad[...] = jnp.where(ones, 1.0, 0.0).astype(BF16)
    for h in range(H_C):
        g = h // grp
        qpad[g * HD:(g + 1) * HD, h * TQ:(h + 1) * TQ] = qT_ref[0, h * HD:(h + 1) * HD, :]

    m_scr[...] = jnp.full(m_scr.shape, NEG, F32)
    acc_scr[...] = jnp.zeros_like(acc_scr)
    row = lax.broadcasted_iota(I32, (TKC, TQ), 0)
    col = lax.broadcasted_iota(I32, (TKC, TQ), 1)
    causal = [jnp.concatenate([jnp.where(row - col + (TQ - TKC) - u * TKC > 0, NEG, 0.0)] * H_C, axis=1)
              for u in range(Q_BLOCKS)] + [None]

    def chunk(start, n, near):
        s = _dot(k_ref[0, _key_rows(start, n), :], qpad[...])
        if near:
            s = _add_near(s, n, causal)
        _online_update(s, m_scr, acc_scr, [_v_aug(vT_ref, start, n, g, HD) for g in range(KVH_C)])

    _chunk_loop(qi, chunk)

    for h in range(H_C):
        g, hh = h // grp, h % grp
        a = acc_scr[g][:, hh * TQ:(hh + 1) * TQ]
        oT_ref[0, h * HD:(h + 1) * HD, :] = a[0:HD] / a[HD:HD + 1]


def _attn_c(cqT, kcat, cvT):
    b, _, l = cqT.shape
    blk_t = lambda r: pl.BlockSpec((1, r, TQ), lambda i, j: (i, 0, j))
    full = lambda s1, s2: pl.BlockSpec((1, s1, s2), lambda i, j: (i, 0, 0))
    return pl.pallas_call(
        _attn_c_kernel,
        grid=(b, l // TQ),
        in_specs=[blk_t(384), full(l, 256), full(128, l)],
        out_specs=blk_t(384),
        out_shape=jax.ShapeDtypeStruct((b, 384, l), F32),
        scratch_shapes=[pltpu.VMEM((256, H_C * TQ), BF16),
                        pltpu.VMEM((1, H_C * TQ), F32),
                        pltpu.VMEM((KVH_C, HD + 16, (H_C // KVH_C) * TQ), F32)],
        compiler_params=_cparams("arbitrary", "arbitrary"),
        name="attn_forget_prompt",
    )(cqT, kcat, cvT)


G_DEC = 4


def _stack_heads(ref, tok, n, width):
    return jnp.concatenate([ref[tok, h * width:(h + 1) * width] for h in range(n)], axis=0)


def _pad_new(x):
    return jnp.concatenate([x, jnp.zeros((PAGE - x.shape[0], x.shape[1]), x.dtype)], axis=0)


def _probabilities(s_pages):
    mx = s_pages[0]
    for s in s_pages[1:]:
        mx = jnp.maximum(mx, s)
    m = jnp.max(mx, axis=1, keepdims=True)
    p_pages = [jnp.exp(s - m) for s in s_pages]
    lsum = p_pages[0]
    for p in p_pages[1:]:
        lsum = lsum + p
    return p_pages, jnp.sum(lsum, axis=1, keepdims=True)


def _pv(p_pages, vt_pages, v_new):
    acc = _dot(p_pages[-1].astype(BF16), v_new)
    for p, vt in zip(p_pages[:-1], vt_pages):
        acc = acc + _dot_nt(p.astype(BF16), vt)
    return acc


class _PageFetch:
    def __init__(self, pt_ref, gsz, n_pages, hbms, indexers, bufs, sems):
        self.pt_ref, self.gsz, self.n_pages = pt_ref, gsz, n_pages
        self.hbms, self.indexers, self.bufs, self.sems = hbms, indexers, bufs, sems
        self.step = pl.program_id(0)
        self.last = pl.num_programs(0) - 1
        self.cur = lax.rem(self.step, 2)

    def _copy(self, c, page_id, buf, j):
        return pltpu.make_async_copy(self.hbms[c].at[self.indexers[c](page_id)], self.bufs[c].at[buf, j],
                                     self.sems.at[c, buf])

    def _start(self, step, buf, j):
        page_id = self.pt_ref[step * self.gsz + j // self.n_pages, j % self.n_pages]
        for c in range(len(self.hbms)):
            self._copy(c, page_id, buf, j).start()

    def _wait(self, buf):
        for c in range(len(self.hbms)):
            for j in range(self.gsz * self.n_pages):
                self._copy(c, 0, buf, j).wait()

    def begin(self):
        @pl.when(self.step == 0)
        def _():
            for j in range(self.gsz * self.n_pages):
                self._start(0, 0, j)
        self._wait(self.cur)

    def prefetch(self, j):
        self._start(jnp.minimum(self.step + 1, self.last), 1 - self.cur, j)

    def finish(self):
        @pl.when(self.step == self.last)
        def _():
            self._wait(1 - self.cur)

    def page(self, c, j):
        return self.bufs[c][self.cur, j]


def _any_spec():
    return pl.BlockSpec(memory_space=pl.ANY)


def _decode_a_kernel(pt_ref, qa_ref, qi_ref, wi_ref, ak_n, av_n, aki_n, bias_ref, kt_hbm, vt_hbm, kit_hbm,
                     oa_ref, kt_buf, vt_buf, kit_buf, sems, *, layer, gsz, n_pages, k_top):
    at_page = lambda pg: (layer, pg)
    fetch = _PageFetch(pt_ref, gsz, n_pages, (kt_hbm, vt_hbm, kit_hbm), (at_page,) * 3,
                       (kt_buf, vt_buf, kit_buf), sems)
    fetch.begin()
    r = qa_ref.shape[0] // gsz
    rows = gsz * r
    n_all = n_pages + 1
    lane = lax.broadcasted_iota(I32, (rows, PAGE), 1)
    qrow = lax.rem(lax.broadcasted_iota(I32, (rows, PAGE), 0), r)
    new_visible = lane <= qrow

    sc_pages = [[] for _ in range(n_all)]
    for g in range(gsz):
        tok = slice(g * r, (g + 1) * r)
        qi2 = _stack_heads(qi_ref, tok, H_IDX, 2 * D_IDX)[:, 0:D_IDX]
        wcol = _stack_heads(wi_ref, tok, H_IDX, LANES) * IDX_W_SCALE
        for p in range(n_all):
            if p < n_pages:
                fetch.prefetch(g * n_pages + p)
                z = _dot(qi2, fetch.page(2, g * n_pages + p).astype(BF16))
            else:
                z = _dot_nt(qi2, _pad_new(aki_n[tok, :]).astype(BF16))
            z = jnp.maximum(z, 0.0) * wcol
            sc = z[0:r]
            for h in range(1, H_IDX):
                sc = sc + z[h * r:(h + 1) * r]
            sc_pages[p].append(sc)
    key_pages = []
    for p in range(n_all):
        sc = jnp.concatenate(sc_pages[p], axis=0)
        sc = jnp.where(sc == 0.0, 0.0, sc)
        if p == n_pages:
            sc = jnp.where(new_visible, sc, -jnp.inf)
        key_pages.append(_sortable_key(sc))

    def count(pred):
        tot = None
        for p, k in enumerate(key_pages):
            hit = jnp.where(pred(k, p), 1.0, 0.0)
            tot = hit if tot is None else tot + hit
        return jnp.sum(tot, axis=1, keepdims=True)

    bit = lambda b: jnp.int32(-2 ** 31 if b == 31 else 1 << b)
    thr = jnp.full((rows, 1), INT_MIN, I32)
    for it in range(16):
        c_hi, c_lo = thr ^ bit(31 - 2 * it), thr ^ bit(30 - 2 * it)
        c_both = c_hi ^ bit(30 - 2 * it)
        n_hi, n_lo, n_both = (count(lambda k, p, c=c: k >= c) for c in (c_hi, c_lo, c_both))
        thr = jnp.where(n_both >= k_top, c_both, jnp.where(n_hi >= k_top, c_hi, jnp.where(n_lo >= k_top, c_lo, thr)))
    cnt_gt = count(lambda k, p: k > thr)
    cnt_eq = count(lambda k, p: k == thr)
    need = k_top - cnt_gt
    n_bits = max(1, (n_all * PAGE - 1).bit_length())

    def tie_search():
        jv = jnp.zeros((rows, 1), I32)
        for it in range(n_bits):
            cand = jv | jnp.int32(1 << (n_bits - 1 - it))
            cnt = count(lambda k, p: (k == thr) & ((lane + p * PAGE) < cand))
            jv = jnp.where(cnt < need, cand, jv)
        return jv

    any_excess = jnp.max(jnp.where(cnt_eq > need, 1, 0)) > 0
    jv = lax.cond(any_excess, tie_search, lambda: jnp.full((rows, 1), n_all * PAGE, I32))
    am_pages = []
    for p, k in enumerate(key_pages):
        sel = (k > thr) | ((k == thr) & ((lane + p * PAGE) <= jv))
        am_pages.append(jnp.where(sel, 0.0, NEG))

    bias_last = jnp.concatenate([bias_ref[h, 0] for h in range(H_A)], axis=0)
    bias_new = jnp.concatenate([bias_ref[h, 1] for h in range(H_A)], axis=0)
    for g in range(gsz):
        tok = slice(g * r, (g + 1) * r)
        qa2 = _stack_heads(qa_ref, tok, H_A, 2 * HD)
        s_pages = []
        for p in range(n_all):
            if p < n_pages:
                s = _dot(qa2, fetch.page(0, g * n_pages + p).reshape(2 * HD, PAGE).astype(BF16))
            else:
                s = _dot_nt(qa2, _pad_new(ak_n[tok, :]).astype(BF16))
            s = s + jnp.concatenate([am_pages[p][tok]] * H_A, axis=0)
            if p == n_pages - 1:
                s = s + bias_last
            if p == n_pages:
                s = s + bias_new
            s_pages.append(s)
        p_pages, lsum = _probabilities(s_pages)
        vts = [fetch.page(1, g * n_pages + p).reshape(2 * HD, PAGE).astype(BF16) for p in range(n_pages)]
        o = _pv(p_pages, vts, _pad_new(av_n[tok, :]).astype(BF16)) / lsum
        oa_ref[tok, :] = jnp.concatenate([o[h * r:(h + 1) * r] for h in range(H_A)], axis=1)
    fetch.finish()


def _decode_call(kernel_fn, name, page_table, tok_args, tok_widths, const_args, caches, page_shapes, out_w, r):
    nb, n_pages = page_table.shape
    gsz = G_DEC
    in_specs = [pl.BlockSpec((gsz * r, w), lambda i, pt: (i, 0)) for w in tok_widths]
    in_specs += [pl.BlockSpec(a.shape, functools.partial(lambda i, pt, n: (0,) * n, n=a.ndim)) for a in const_args]
    in_specs += [_any_spec() for _ in caches]
    grid_spec = pltpu.PrefetchScalarGridSpec(
        num_scalar_prefetch=1, grid=(nb // gsz,), in_specs=in_specs,
        out_specs=pl.BlockSpec((gsz * r, out_w), lambda i, pt: (i, 0)),
        scratch_shapes=[pltpu.VMEM((2, gsz * n_pages) + s, F32) for s in page_shapes]
        + [pltpu.SemaphoreType.DMA((len(caches), 2))])
    return pl.pallas_call(
        kernel_fn,
        grid_spec=grid_spec,
        out_shape=jax.ShapeDtypeStruct((nb * r, out_w), F32),
        compiler_params=_cparams("arbitrary"),
        name=name,
    )(page_table, *tok_args, *const_args, *caches)


def _decode_a(page_table, layer, qa, qi, wi, ak_n, av_n, aki_n, kt, vt, kit, bias_dec, r):
    n_pages = page_table.shape[1]
    k_top = min(K_TOP_MAX, (n_pages * PAGE + r) // 4)
    return _decode_call(
        functools.partial(_decode_a_kernel, layer=layer, gsz=G_DEC, n_pages=n_pages, k_top=k_top), "decode_dsa",
        page_table, (qa, qi, wi, ak_n, av_n, aki_n), (768, 1024, 1024, 128, 128, 64), (bias_dec[:H_A],),
        (kt, vt, kit), ((KVH_A, HD, PAGE), (KVH_A, HD, PAGE), (D_IDX, PAGE)), 768, r)


def _decode_b_kernel(pt_ref, qb_ref, bk_n, bv_n, bias_ref, lam_ref, gsub_ref, kt_hbm, v_hbm,
                     ob_ref, kt_buf, v_buf, sems, *, layer, gsz, n_pages, lam_init):
    at_page = lambda pg: (layer, pg)
    fetch = _PageFetch(pt_ref, gsz, n_pages, (kt_hbm, v_hbm), (at_page,) * 2, (kt_buf, v_buf), sems)
    fetch.begin()
    for j in range(gsz * n_pages):
        fetch.prefetch(j)
    r = qb_ref.shape[0] // gsz
    n_all = n_pages + 1
    grp = H_B // KVH_B
    half = 2 * grp * r
    bias_last = jnp.concatenate([bias_ref[h, 0] for h in range(H_B) for _ in range(2)], axis=0)
    bias_new = jnp.concatenate([bias_ref[h, 1] for h in range(H_B) for _ in range(2)], axis=0)
    lam = _lambda_value(lam_ref, lam_init)
    gs = gsub_ref[...]
    for g in range(gsz):
        tok = slice(g * r, (g + 1) * r)
        qb2 = _stack_heads(qb_ref, tok, 2 * H_B, 4 * HD)
        s_pages = []
        for p in range(n_all):
            if p < n_pages:
                s = _dot(qb2, fetch.page(0, g * n_pages + p).reshape(4 * HD, PAGE).astype(BF16))
            else:
                s = _dot_nt(qb2, _pad_new(bk_n[tok, :]).astype(BF16))
            if p == n_pages - 1:
                s = s + bias_last
            if p == n_pages:
                s = s + bias_new
            s_pages.append(s)
        p_pages, lsum = _probabilities(s_pages)
        v_new = _pad_new(bv_n[tok, :]).astype(BF16)
        outs = []
        for kv in range(KVH_B):
            rs = slice(kv * half, (kv + 1) * half)
            acc = _dot(p_pages[-1][rs].astype(BF16), v_new[:, kv * 2 * HD:(kv + 1) * 2 * HD])
            for p in range(n_pages):
                v = v_buf[fetch.cur, g * n_pages + p, pl.ds(kv, PAGE, stride=KVH_B), :].astype(BF16)
                acc = acc + _dot(p_pages[p][rs].astype(BF16), v)
            o = acc / lsum[rs]
            for hh in range(grp):
                od = o[(2 * hh) * r:(2 * hh + 1) * r] - lam * o[(2 * hh + 1) * r:(2 * hh + 2) * r]
                ms = jnp.mean(od * od, axis=1, keepdims=True)
                outs.append((od * lax.rsqrt(ms + EPS) * gs) * (1.0 - lam_init))
        ob_ref[tok, :] = jnp.concatenate(outs, axis=1)
    fetch.finish()


def _decode_b(page_table, layer, qb, bk_n, bv_n, kt, v2, bias_dec, lamvec, gsub_row, lam_init, r):
    n_pages = page_table.shape[1]
    return _decode_call(
        functools.partial(_decode_b_kernel, layer=layer, gsz=G_DEC, n_pages=n_pages, lam_init=lam_init),
        "decode_diff", page_table, (qb, bk_n, bv_n), (2048, 256, 256), (bias_dec[H_A:], lamvec, gsub_row),
        (kt, v2), ((KVH_B, 2, HD, PAGE), (KVH_B * PAGE, 2 * HD)), 512, r)


def _decode_c_kernel(pt_ref, qc_ref, ck_n, cv_n, lft_n, kt_hbm, vt_hbm, lf_hbm,
                     oc_ref, kt_buf, vt_buf, lf_buf, sems, *, layer, gsz, n_pages):
    at_page = lambda pg: (layer, pg)
    fetch = _PageFetch(pt_ref, gsz, n_pages, (kt_hbm, vt_hbm, lf_hbm),
                       (at_page, at_page, lambda pg: (layer, slice(None), pg)), (kt_buf, vt_buf, lf_buf), sems)
    fetch.begin()
    r = qc_ref.shape[0] // gsz
    n_all = n_pages + 1
    lane = lax.broadcasted_iota(I32, (r, PAGE), 1)
    qrow = lax.broadcasted_iota(I32, (r, PAGE), 0)
    causal_new = jnp.concatenate([jnp.where(lane <= qrow, 0.0, NEG)] * H_C, axis=0)
    ri = lax.broadcasted_iota(I32, (PAGE, PAGE), 0)
    ci = lax.broadcasted_iota(I32, (PAGE, PAGE), 1)
    upper = jnp.where(ri <= ci, 1.0, 0.0).astype(BF16)
    ones = jnp.ones((PAGE, PAGE), BF16)
    nr = 8 * n_all
    rr = lax.broadcasted_iota(I32, (nr, nr), 0)
    cc = lax.broadcasted_iota(I32, (nr, nr), 1)
    prev_pages = jnp.where(((rr % 8) == (cc % 8)) & ((cc // 8) < (rr // 8)), 1.0, 0.0).astype(BF16)
    zrow = jnp.zeros((8 - H_C, PAGE), F32)
    for g in range(gsz):
        tok = slice(g * r, (g + 1) * r)
        xs = []
        for p in range(n_pages):
            fetch.prefetch(g * n_pages + p)
            xs += [fetch.page(2, g * n_pages + p), zrow]
        x = jnp.concatenate(xs + [lft_n[g * 8:(g + 1) * 8, :]], axis=0)
        cum = _dot3_lhs(x, upper) + _dot3_rhs(prev_pages, _dot3_lhs(x, ones))
        qc2 = _stack_heads(qc_ref, tok, H_C, 2 * HD)
        s_pages = []
        for p in range(n_all):
            if p < n_pages:
                s = _dot(qc2, fetch.page(0, g * n_pages + p).reshape(2 * HD, PAGE).astype(BF16))
            else:
                s = _dot_nt(qc2, _pad_new(ck_n[tok, :]).astype(BF16)) + causal_new
            decay = jnp.concatenate(
                [jnp.broadcast_to(cum[p * 8 + h:p * 8 + h + 1, :], (r, PAGE)) for h in range(H_C)], axis=0)
            s_pages.append(s - decay)
        p_pages, lsum = _probabilities(s_pages)
        vts = [fetch.page(1, g * n_pages + p).reshape(2 * HD, PAGE).astype(BF16) for p in range(n_pages)]
        o = _pv(p_pages, vts, _pad_new(cv_n[tok, :]).astype(BF16)) / lsum
        oc_ref[tok, :] = jnp.concatenate([o[h * r:(h + 1) * r] for h in range(H_C)], axis=1)
    fetch.finish()


def _decode_c(page_table, layer, qc, ck_n, cv_n, lft_new, kt, vt, lft, r):
    n_pages = page_table.shape[1]
    assert r == 8
    return _decode_call(
        functools.partial(_decode_c_kernel, layer=layer, gsz=G_DEC, n_pages=n_pages), "decode_forget",
        page_table, (qc, ck_n, cv_n, lft_new.reshape(-1, LANES)), (768, 128, 128, LANES), (),
        (kt, vt, lft), ((KVH_C, HD, PAGE), (KVH_C, HD, PAGE), (H_C, PAGE)), 768, r)


def _merge_kernel(x_ref, mod_ref, g_ref, oa_ref, ob_ref, oc_ref, wg_ref, wpa_ref, wpb_ref, wpc_ref, wo_ref,
                  o_ref, *, transposed):
    x = x_ref[...]
    gsz, r, _ = x.shape
    mod = mod_ref[...]
    h = _norm_mod(x, g_ref[...], mod[:, :, D:2 * D], mod[:, :, 0:D]).reshape(gsz * r, D).astype(BF16)
    gates = _sigmoid(_dot(h, wg_ref[...]))
    if transposed:
        oa, ob, oc = oa_ref[0].T, ob_ref[0].T, oc_ref[0].T
    else:
        oa, ob, oc = oa_ref[...], ob_ref[...], oc_ref[...]
    merged = (gates[:, 0:D] * _dot(oa.astype(BF16), wpa_ref[...])
              + gates[:, D:2 * D] * _dot(ob.astype(BF16), wpb_ref[...])
              + gates[:, 2 * D:3 * D] * _dot(oc.astype(BF16), wpc_ref[...]))
    y = _dot(merged.astype(BF16), wo_ref[...]).reshape(gsz, r, D)
    o_ref[...] = x + mod[:, :, 2 * D:3 * D] * y


def _merge(x, mod_a, g, oa, ob, oc, wg, wpa, wpb, wpc, wo, gsz, r, transposed, layer):
    nb, rr, _ = x.shape
    const = lambda a: pl.BlockSpec(a.shape, lambda i, j: (0,) * a.ndim)
    lw = lambda a: _layer_spec(a, layer)
    if transposed:
        grid = (nb, rr // r)
        x_spec = pl.BlockSpec((1, r, D), lambda i, j: (i, j, 0))
        mod_spec = pl.BlockSpec((1, 1, 3 * D), lambda i, j: (i, 0, 0))
        o_spec = lambda a: pl.BlockSpec((1, a.shape[1], r), lambda i, j: (i, 0, j))
    else:
        grid = (nb // gsz, 1)
        x_spec = pl.BlockSpec((gsz, rr, D), lambda i, j: (i, 0, 0))
        mod_spec = pl.BlockSpec((gsz, 1, 3 * D), lambda i, j: (i, 0, 0))
        o_spec = lambda a: pl.BlockSpec((gsz * rr, a.shape[1]), lambda i, j: (i, 0))
    return pl.pallas_call(
        functools.partial(_merge_kernel, transposed=transposed),
        grid=grid,
        in_specs=[x_spec, mod_spec, const(g), o_spec(oa), o_spec(ob), o_spec(oc),
                  lw(wg), lw(wpa), lw(wpb), lw(wpc), lw(wo)],
        out_specs=x_spec,
        out_shape=jax.ShapeDtypeStruct(x.shape, F32),
        compiler_params=_cparams("arbitrary", "arbitrary"),
        name="merge_prompt" if transposed else "merge_sample",
    )(x, mod_a, g, oa, ob, oc, wg, wpa, wpb, wpc, wo)


def _ffn_kernel(x_ref, mod_ref, g_ref, w1_ref, w2_ref, gf_ref, o_ref, *, final):
    x = x_ref[...]
    gsz, r, _ = x.shape
    mod = mod_ref[...]
    h = _norm_mod(x, g_ref[...], mod[:, :, D:2 * D], mod[:, :, 0:D]).reshape(gsz * r, D).astype(BF16)
    u = jnp.maximum(_dot(h, w1_ref[...]), 0.0)
    y = _dot((u * u).astype(BF16), w2_ref[...]).reshape(gsz, r, D)
    x2 = x + mod[:, :, 2 * D:3 * D] * y
    if final:
        ms = jnp.mean(x2 * x2, axis=-1, keepdims=True)
        x2 = x2 * lax.rsqrt(ms + EPS) * gf_ref[...]
    o_ref[...] = x2


def _ffn(x, mod_b, g, w1, w2, g_final, gsz, r, final, name, layer):
    nb, rr, _ = x.shape
    const = lambda a: pl.BlockSpec(a.shape, lambda i, j: (0,) * a.ndim)
    lw = lambda a: _layer_spec(a, layer)
    if gsz == 1:
        grid = (nb, rr // r)
        x_spec = pl.BlockSpec((1, r, D), lambda i, j: (i, j, 0))
        mod_spec = pl.BlockSpec((1, 1, 3 * D), lambda i, j: (i, 0, 0))
    else:
        grid = (nb // gsz, 1)
        x_spec = pl.BlockSpec((gsz, rr, D), lambda i, j: (i, 0, 0))
        mod_spec = pl.BlockSpec((gsz, 1, 3 * D), lambda i, j: (i, 0, 0))
    return pl.pallas_call(
        functools.partial(_ffn_kernel, final=final),
        grid=grid,
        in_specs=[x_spec, mod_spec, const(g), lw(w1), lw(w2), const(g_final)],
        out_specs=x_spec,
        out_shape=jax.ShapeDtypeStruct(x.shape, F32),
        compiler_params=_cparams("arbitrary", "arbitrary"),
        name=name,
    )(x, mod_b, g, w1, w2, g_final)


TM_PROMPT = 512
G_SAMPLE = 16


def kernel(x_prompt, x_sample, c_prompt, c_sample, cache_a_k, cache_a_v, cache_a_kidx, cache_b_k, cache_b_v, cache_c_k, cache_c_v, cache_c_logf, page_table, t5_table, w_ada, b_ada, g_mix, g_ffn, w_in, b_forget, lam_q1, lam_k1, lam_q2, lam_k2, g_subln, w_gate, w_pa, w_pb, w_pc, w_out, w_ff1, w_ff2, g_final):
    depth = w_in.shape[0]
    nbp, seq, _ = x_prompt.shape
    nbs, dec_seq, _ = x_sample.shape
    n_pool, page = cache_a_k.shape[1], cache_a_k.shape[2]
    n_pages = page_table.shape[1]
    past_len = n_pages * page
    assert page == PAGE and seq % max(TM_PROMPT, _CUM_T, FAR_BLOCKS * TKC) == 0 and dec_seq == 8
    assert nbs % G_SAMPLE == 0 and nbs % G_DEC == 0
    tm = min(TM_PROMPT, seq)

    nc = nbp + nbs
    mod = _ada(jnp.concatenate([c_prompt, c_sample], axis=0), w_ada, b_ada)
    mod = mod.reshape(depth, nc, 1, 6 * D)
    bias_pt, bias_dec = _bias_tiles(t5_table, past_len, dec_seq)

    kv_t = lambda c: jnp.transpose(c, (0, 1, 3, 4, 2))
    a_kt, a_vt, c_kt, c_vt = kv_t(cache_a_k), kv_t(cache_a_v), kv_t(cache_c_k), kv_t(cache_c_v)
    a_kit = jnp.transpose(cache_a_kidx, (0, 1, 3, 2))
    b_kt = jnp.transpose(cache_b_k, (0, 1, 3, 4, 5, 2))
    b_v2 = cache_b_v.reshape(depth, n_pool, page * KVH_B, 2 * HD)
    c_lft = jnp.transpose(cache_c_logf, (0, 3, 1, 2))

    w_in_t = jnp.swapaxes(w_in, 1, 2)
    wr, wt = _prep_in_prompt(w_in_t)
    ws = _prep_in_sample(w_in_t)
    wg, wo = w_gate.astype(BF16), w_out.astype(BF16)
    wpa, wpb, wpc = w_pa.astype(BF16), w_pb.astype(BF16), w_pc.astype(BF16)
    wpa_s = _pad_proj_rows(w_pa, H_A, lambda h: h // (H_A // KVH_A))
    wpc_s = _pad_proj_rows(w_pc, H_C, lambda h: h // (H_C // KVH_C))
    w1, w2 = w_ff1.astype(BF16), w_ff2.astype(BF16)

    xp, xs = x_prompt, x_sample
    rows_s = []
    stacked = [jnp.zeros(s.shape, s.dtype) for s in _stacked_row_shapes(depth, nbp, seq)]
    g_final2 = g_final.reshape(1, D)
    for l in range(depth):
        lam_init = 0.8 - 0.6 * math.exp(-0.3 * l)
        bfb = _cf_block(b_forget[l].reshape(1, H_C))
        gm, gf = g_mix[l].reshape(1, D), g_ffn[l].reshape(1, D)
        lamvec = jnp.stack([lam_q1[l], lam_k1[l], lam_q2[l], lam_k2[l]])
        mod_pa, mod_pb = mod[l, :nbp, :, :3 * D], mod[l, :nbp, :, 3 * D:]
        mod_sa, mod_sb = mod[l, nbp:, :, :3 * D], mod[l, nbp:, :, 3 * D:]
        last = l == depth - 1

        stacked, (akb, akib, bkb, ckb, lfrep, awiT, aqT, aqiT, bqT, cqT, avT, bvT, cvT) = _inproj_prompt(
            xp, mod_pa, gm, wr, wt, bfb, tm, l, stacked)
        kcat = _cum_prompt(lfrep, ckb)
        oaT = _attn_a(aqT, aqiT, awiT, akb, akib, avT, bias_pt[:H_A])
        obT = _attn_b(bqT, bkb, bvT, bias_pt[H_A:], lamvec, g_subln[l].reshape(2 * HD, 1), lam_init)
        ocT = _attn_c(cqT, kcat, cvT)
        x1 = _merge(xp, mod_pa, gm, oaT, obT, ocT, wg, wpa, wpb, wpc, wo, 1, tm, True, l)
        xp = _ffn(x1, mod_pb, gf, w1, w2, g_final2, 1, tm, last, "ffn_prompt", l)

        (sak, sav, saki, sbk, sbv, sck, scv, slf, qa, qi, wi, qb, qc, slfrep) = _inproj_sample(
            xs, mod_sa, gm, ws, bfb, G_SAMPLE, l)
        rows_s.append((sak, sav, saki, sbk, sbv, sck, scv, slf))
        lft_new = jnp.swapaxes(slfrep.reshape(nbs, dec_seq, LANES)[:, :, 0:8], 1, 2)
        lft_new = jnp.concatenate([lft_new, jnp.zeros((nbs, 8, LANES - dec_seq), F32)], axis=2)
        lane_head = jnp.arange(8)[None, :, None] < H_C
        lft_new = jnp.where(lane_head, lft_new, 0.0)
        oa = _decode_a(page_table, l, qa, qi, wi, sak, sav, saki, a_kt, a_vt, a_kit, bias_dec, dec_seq)
        ob = _decode_b(page_table, l, qb, sbk, sbv, b_kt, b_v2, bias_dec, lamvec, g_subln[l].reshape(1, 2 * HD),
                       lam_init, dec_seq)
        oc = _decode_c(page_table, l, qc, sck, scv, lft_new, c_kt, c_vt, c_lft, dec_seq)
        x1s = _merge(xs, mod_sa, gm, oa, ob, oc, wg, wpa_s, wpb, wpc_s, wo, G_SAMPLE, dec_seq, False, l)
        xs = _ffn(x1s, mod_sb, gf, w1, w2, g_final2, G_SAMPLE, dec_seq, last, "ffn_sample", l)

    def stack(rows, i):
        return jnp.stack([r[i] for r in rows])

    def stack_sample(i, shape):
        return stack(rows_s, i).reshape((depth, nbs, dec_seq) + shape)

    def prompt_t(y, shape):
        n = len(shape)
        y = y.reshape((depth, nbp) + shape + (seq,))
        return jnp.transpose(y, (0, 1, n + 2) + tuple(range(2, n + 2)))

    bv, lf, akT, avT, akiT, bkT, ckT, cvT = stacked
    out_p = (prompt_t(akT, (KVH_A, HD)), prompt_t(avT, (KVH_A, HD)), prompt_t(akiT, (D_IDX,)),
             prompt_t(bkT, (KVH_B, 2, HD)), bv.reshape(depth, nbp, seq, KVH_B, 2 * HD),
             prompt_t(ckT, (KVH_C, HD)), prompt_t(cvT, (KVH_C, HD)), lf)
    out_s = (stack_sample(0, (KVH_A, HD)), stack_sample(1, (KVH_A, HD)), stack_sample(2, (D_IDX,)),
             stack_sample(3, (KVH_B, 2, HD)), stack_sample(4, (KVH_B, 2 * HD)), stack_sample(5, (KVH_C, HD)),
             stack_sample(6, (KVH_C, HD)), stack_sample(7, (H_C,)))
    return (xp, xs) + out_p + out_s
```

```python
import functools
import math

import numpy as np
import jax
import jax.numpy as jnp
from jax import lax
from jax.experimental import pallas as pl
from jax.experimental.pallas import tpu as pltpu

F32 = jnp.float32
BF16 = jnp.bfloat16
I32 = jnp.int32

D = 1024
HD = 64
H_A, KVH_A = 6, 2
H_IDX, D_IDX = 8, 64
K_TOP_MAX = 256
IDX_W_SCALE = (H_IDX ** -0.5) * (D_IDX ** -0.5)
H_B, KVH_B = 4, 2
H_C, KVH_C = 6, 2
N_BUCKETS, T5_MAX_EXACT, T5_MAX_DIST = 32, 16, 128
D_FF = 4 * D
EPS = 1e-6
N_HEADS_T5 = H_A + H_B
PAGE = 128

LANES = 128
TQ = 256
TKC = 128
Q_BLOCKS = TQ // TKC
NEAR_BLOCKS = Q_BLOCKS + 1
FAR_BLOCKS = 4
NEG = -1e30
INT_MIN = -2 ** 31
VMEM_LIMIT = 56 * 1024 * 1024

_W = dict(aq=H_A * HD, ak=KVH_A * HD, av=KVH_A * HD, aqi=H_IDX * D_IDX, aki=D_IDX, awi=H_IDX,
          bq=H_B * 2 * HD, bk=KVH_B * 2 * HD, bv=KVH_B * 2 * HD,
          cq=H_C * HD, ck=KVH_C * HD, cv=KVH_C * HD, cf=H_C)
_OFF = {}
_o = 0
for _k, _v in _W.items():
    _OFF[_k] = _o
    _o += _v

_R_AK, _R_AV, _R_AKI, _R_BK, _R_BV, _R_CK, _R_CV, _R_CF = 0, 128, 256, 384, 640, 896, 1024, 1152
N_ROWS = 1280
_P_AK, _P_AKI, _P_BK, _P_CK, _P_CF, _P_BV = 0, 128, 256, 512, 640, 768
N_PROWS = 1024
_T_AQ, _T_AQI, _T_AWI, _T_BQ, _T_CQ, _T_AV, _T_BV, _T_CV = 0, 384, 896, 912, 1424, 1808, 1936, 2192
_T_AK, _T_AKI, _T_BK, _T_CK = 2320, 2448, 2512, 2768
N_T = 2896
_S_AQ, _S_AQI, _S_AWI, _S_BQ, _S_CQ = 0, 768, 1792, 1920, 3968
N_SQ = 4736


def _cparams(*sem):
    return pltpu.CompilerParams(dimension_semantics=sem, vmem_limit_bytes=VMEM_LIMIT)


def _dot(a, b):
    return jnp.dot(a, b, preferred_element_type=F32)


def _dot_nt(a, b):
    return lax.dot_general(a, b, (((1,), (1,)), ((), ())), preferred_element_type=F32)


def _split3(x):
    hi = x.astype(BF16)
    r1 = x - hi.astype(F32)
    mid = r1.astype(BF16)
    lo = (r1 - mid.astype(F32)).astype(BF16)
    return hi, mid, lo


def _dot3_rhs(a_bf16, x):
    hi, mid, lo = _split3(x)
    return _dot(a_bf16, hi) + _dot(a_bf16, mid) + _dot(a_bf16, lo)


def _dot3_lhs(x, b_bf16):
    hi, mid, lo = _split3(x)
    return _dot(hi, b_bf16) + _dot(mid, b_bf16) + _dot(lo, b_bf16)


def _norm_mod(x, g, sc, sh):
    ms = jnp.mean(x * x, axis=-1, keepdims=True)
    return (x * lax.rsqrt(ms + EPS) * g) * (1.0 + sc) + sh


def _log_sigmoid(z):
    return jnp.minimum(z, 0.0) - jnp.log1p(jnp.exp(-jnp.abs(z)))


def _sigmoid(z):
    return 1.0 / (1.0 + jnp.exp(-z))


def _key_to_float(s):
    bits = s ^ ((s >> 31) & 0x7FFFFFFF)
    return lax.bitcast_convert_type(bits, F32)


def _threshold_value(s):
    t = _key_to_float(s)
    return jnp.where(t != t, -jnp.inf, t)


def _ada_kernel(c_ref, w_ref, b_ref, o_ref):
    c = c_ref[...]
    s = (c * _sigmoid(c)).astype(BF16)
    o_ref[0] = _dot(s, w_ref[0].astype(BF16)) + b_ref[0]


def _ada(c_all, w_ada, b_ada):
    depth = w_ada.shape[0]
    nc = c_all.shape[0]
    nt = 6
    return pl.pallas_call(
        _ada_kernel,
        grid=(depth, nt),
        in_specs=[pl.BlockSpec((nc, D), lambda l, j: (0, 0)),
                  pl.BlockSpec((1, D, D), lambda l, j: (l, 0, j)),
                  pl.BlockSpec((1, 1, D), lambda l, j: (l, 0, j))],
        out_specs=pl.BlockSpec((1, nc, D), lambda l, j: (l, 0, j)),
        out_shape=jax.ShapeDtypeStruct((depth, nc, 6 * D), F32),
        compiler_params=_cparams("arbitrary", "arbitrary"),
        name="ada",
    )(c_all, w_ada, b_ada.reshape(depth, 1, 6 * D))


def _t5_bucket_np(rel):
    n = np.maximum(rel, 0)
    nf = np.maximum(n, 1).astype(np.float32)
    large = T5_MAX_EXACT + (np.log(nf / np.float32(T5_MAX_EXACT)) / np.float32(math.log(T5_MAX_DIST / T5_MAX_EXACT))
                            * np.float32(N_BUCKETS - T5_MAX_EXACT)).astype(np.int32)
    return np.where(n < T5_MAX_EXACT, n, np.minimum(large, N_BUCKETS - 1)).astype(np.int32)


def _bias_bucket_tables(past_len, dec_seq):
    s = np.arange(TKC)[:, None]
    t = np.arange(TQ)[None, :]
    tiles = []
    for u in range(NEAR_BLOCKS):
        rel = t - s - (TQ - TKC) + u * TKC
        tiles.append(np.where(rel >= 0, _t5_bucket_np(rel), -1))
    prompt = np.stack(tiles).astype(np.int32)
    i = np.arange(dec_seq)[:, None]
    lane = np.arange(PAGE)[None, :]
    rel_last = (past_len + i) - (past_len - PAGE + lane)
    rel_new = i - lane
    dl = _t5_bucket_np(rel_last)
    dn = np.where((rel_new >= 0) & (lane < dec_seq), _t5_bucket_np(rel_new), -1)
    dec = np.stack([dl, dn]).astype(np.int32)
    return prompt, dec


def _bias_kernel(tab_ref, pb_ref, db_ref, pt_ref, dt_ref):
    pb = pb_ref[...]
    db = db_ref[...]
    for h in range(N_HEADS_T5):
        def lut(bk):
            acc = jnp.zeros(bk.shape, F32)
            for b in range(N_BUCKETS):
                acc = jnp.where(bk == b, tab_ref[b, h], acc)
            return jnp.where(bk < 0, NEG, acc - tab_ref[N_BUCKETS - 1, h])
        pt_ref[h] = lut(pb)
        dt_ref[h] = lut(db)


def _bias_tiles(t5_table, past_len, dec_seq):
    pb, db = _bias_bucket_tables(past_len, dec_seq)
    return pl.pallas_call(
        _bias_kernel,
        in_specs=[pl.BlockSpec(memory_space=pltpu.SMEM),
                  pl.BlockSpec(memory_space=pltpu.VMEM),
                  pl.BlockSpec(memory_space=pltpu.VMEM)],
        out_specs=[pl.BlockSpec(memory_space=pltpu.VMEM), pl.BlockSpec(memory_space=pltpu.VMEM)],
        out_shape=[jax.ShapeDtypeStruct((N_HEADS_T5,) + pb.shape, F32),
                   jax.ShapeDtypeStruct((N_HEADS_T5,) + db.shape, F32)],
        name="t5_bias_tiles",
    )(t5_table, jnp.asarray(pb), jnp.asarray(db))


def _cf_block(cf):
    n = cf.shape[0]
    z2 = jnp.zeros((n, 2), cf.dtype)
    return jnp.concatenate([cf, z2, cf, z2, cf, z2, jnp.zeros((n, LANES - 24), cf.dtype)], axis=1)


class _Plan:
    def __init__(self):
        self.runs = []

    def seg(self, name, lo=0, hi=None, scale=1.0):
        hi = _W[name] if hi is None else hi
        return self.raw(_OFF[name] + lo, _OFF[name] + hi, scale)

    def raw(self, lo, hi, scale=1.0):
        self.runs.append((lo, hi, scale))
        return self

    def zeros(self, n):
        if n > 0:
            self.runs.append((None, n, 0.0))
        return self

    def cf_block(self):
        for _ in range(3):
            self.seg("cf").zeros(8 - H_C)
        return self.zeros(LANES - 24)

    def padded_heads(self, name, n_heads, slot_of, n_slots, scale=1.0):
        for h in range(n_heads):
            s = slot_of(h)
            self.zeros(s * HD).seg(name, h * HD, (h + 1) * HD, scale).zeros((n_slots - 1 - s) * HD)
        return self

    def rows_block(self):
        return (self.seg("ak").seg("av").seg("aki").zeros(64).seg("bk").seg("bv").seg("ck").seg("cv").cf_block())

    def arrange(self, w_rows):
        depth, _, width = w_rows.shape
        pieces = []
        for lo, hi, scale in self.runs:
            if lo is None:
                pieces.append(jnp.zeros((depth, hi, width), w_rows.dtype))
                continue
            piece = w_rows[:, lo:hi, :]
            pieces.append(piece if scale == 1.0 else piece * scale)
        return jnp.concatenate(pieces, axis=1).astype(BF16)


_QSCALE = HD ** -0.5


def _prep_in_prompt(w_in_t):
    w_t = (_Plan().seg("aq", scale=_QSCALE).seg("aqi").seg("awi").zeros(8).seg("bq", scale=_QSCALE)
           .seg("cq", scale=_QSCALE).seg("av").seg("bv").seg("cv").seg("ak").seg("aki").seg("bk").seg("ck"))
    w_r = _Plan().seg("ak").seg("aki").zeros(64).seg("bk").seg("ck").cf_block().seg("bv")
    return jnp.swapaxes(w_r.arrange(w_in_t), 1, 2), w_t.arrange(w_in_t)


def _prep_in_sample(w_in_t):
    plan = _Plan()
    plan.padded_heads("aq", H_A, lambda h: h // (H_A // KVH_A), 2, _QSCALE)
    plan.padded_heads("aqi", H_IDX, lambda h: 0, 2)
    plan.seg("awi").zeros(LANES - H_IDX)
    plan.padded_heads("bq", 2 * H_B, lambda hc: (hc // 2 // (H_B // KVH_B)) * 2 + hc % 2, 4, _QSCALE)
    plan.padded_heads("cq", H_C, lambda h: h // (H_C // KVH_C), 2, _QSCALE)
    return jnp.swapaxes(plan.rows_block().arrange(w_in_t), 1, 2)


def _pad_proj_rows(wp, n_heads, kv_of):
    plan = _Plan()
    for h in range(n_heads):
        s = kv_of(h)
        plan.zeros(s * HD).raw(h * HD, (h + 1) * HD).zeros((1 - s) * HD)
    return plan.arrange(wp)


def _store_rows(pr, bf, ak_o, av_o, aki_o, bk_o, bv_o, ck_o, cv_o, lf_o, idx):
    ak_o[idx] = pr[:, _R_AK:_R_AK + 128]
    av_o[idx] = pr[:, _R_AV:_R_AV + 128]
    aki_o[idx] = pr[:, _R_AKI:_R_AKI + 64]
    bk_o[idx] = pr[:, _R_BK:_R_BK + 256]
    bv_o[idx] = pr[:, _R_BV:_R_BV + 256]
    ck_o[idx] = pr[:, _R_CK:_R_CK + 128]
    cv_o[idx] = pr[:, _R_CV:_R_CV + 128]
    lf = _log_sigmoid(pr[:, _R_CF:_R_CF + 128] + bf)
    lf_o[idx] = lf[:, 0:H_C]
    return lf


N_STACKED = 8


def _inproj_p_kernel(x_ref, mod_ref, g_ref, wr_ref, wt_ref, bf_ref, *refs):
    (bv_o, lf_o, akT_o, avT_o, akiT_o, bkT_o, ckT_o, cvT_o,
     akb_o, akib_o, bkb_o, ckb_o, lfrep_o, awiT_o,
     aqT_o, aqiT_o, bqT_o, cqT_o, avTb_o, bvTb_o, cvTb_o) = refs[N_STACKED:]
    x = x_ref[0]
    mod = mod_ref[0]
    h = _norm_mod(x, g_ref[...], mod[:, D:2 * D], mod[:, 0:D])
    pr = _dot(h.astype(BF16), wr_ref[...])
    for kv in range(KVH_B):
        bv_o[0, pl.ds(kv, x.shape[0], stride=KVH_B), :] = pr[:, _P_BV + kv * 2 * HD:_P_BV + (kv + 1) * 2 * HD]
    lf = _log_sigmoid(pr[:, _P_CF:_P_CF + 128] + bf_ref[...])
    lf_o[0] = lf[:, 0:H_C]
    lfrep_o[0] = lf
    akb_o[0] = pr[:, _P_AK:_P_AK + 128].astype(BF16)
    akib_o[0] = pr[:, _P_AKI:_P_AKI + 64].astype(BF16)
    bkb_o[0] = pr[:, _P_BK:_P_BK + 256].astype(BF16)
    ckb_o[0] = pr[:, _P_CK:_P_CK + 128].astype(BF16)
    pt = _dot(wt_ref[...], h.T.astype(BF16))
    akT_o[0] = pt[_T_AK:_T_AK + 128]
    avT_o[0] = pt[_T_AV:_T_AV + 128]
    akiT_o[0] = pt[_T_AKI:_T_AKI + 64]
    bkT_o[0] = pt[_T_BK:_T_BK + 256]
    ckT_o[0] = pt[_T_CK:_T_CK + 128]
    cvT_o[0] = pt[_T_CV:_T_CV + 128]
    awiT_o[0] = pt[_T_AWI:_T_AWI + 16]
    aqT_o[0] = pt[_T_AQ:_T_AQ + 384].astype(BF16)
    aqiT_o[0] = pt[_T_AQI:_T_AQI + 512].astype(BF16)
    bqT_o[0] = pt[_T_BQ:_T_BQ + 512].astype(BF16)
    cqT_o[0] = pt[_T_CQ:_T_CQ + 384].astype(BF16)
    avTb_o[0] = pt[_T_AV:_T_AV + 128].astype(BF16)
    bvTb_o[0] = pt[_T_BV:_T_BV + 256].astype(BF16)
    cvTb_o[0] = pt[_T_CV:_T_CV + 128].astype(BF16)


def _layer_spec(a, layer):
    return pl.BlockSpec((None,) + a.shape[1:], lambda *_: (layer,) + (0,) * (a.ndim - 1))


def _stacked_row_shapes(depth, b, l):
    return ([jax.ShapeDtypeStruct((depth, b, l * KVH_B, 2 * HD), F32), jax.ShapeDtypeStruct((depth, b, l, H_C), F32)]
            + [jax.ShapeDtypeStruct((depth, b, r, l), F32) for r in (128, 128, 64, 256, 128, 128)])


def _inproj_prompt(x, mod_a, g, wr, wt, bfb, tm, layer, stacked):
    b, l, _ = x.shape
    row = lambda w, dt: jax.ShapeDtypeStruct((b, l, w), dt)
    tr = lambda r, dt: jax.ShapeDtypeStruct((b, r, l), dt)
    row_spec = lambda w: pl.BlockSpec((1, tm, w), lambda i, j: (i, j, 0))
    tr_spec = lambda r: pl.BlockSpec((1, r, tm), lambda i, j: (i, 0, j))
    st_specs = ([pl.BlockSpec((None, 1, tm * KVH_B, 2 * HD), lambda i, j: (layer, i, j, 0)),
                 pl.BlockSpec((None, 1, tm, H_C), lambda i, j: (layer, i, j, 0))]
                + [pl.BlockSpec((None, 1, a.shape[2], tm), lambda i, j: (layer, i, 0, j)) for a in stacked[2:]])
    rows = [(128, BF16), (64, BF16), (256, BF16), (128, BF16), (128, F32)]
    trs = [(16, F32), (384, BF16), (512, BF16), (512, BF16), (384, BF16), (128, BF16), (256, BF16), (128, BF16)]
    n_in = 6
    outs = pl.pallas_call(
        _inproj_p_kernel,
        grid=(b, l // tm),
        in_specs=[pl.BlockSpec((1, tm, D), lambda i, j: (i, j, 0)),
                  pl.BlockSpec((1, 1, 3 * D), lambda i, j: (i, 0, 0)),
                  pl.BlockSpec((1, D), lambda i, j: (0, 0)),
                  _layer_spec(wr, layer), _layer_spec(wt, layer),
                  pl.BlockSpec((1, LANES), lambda i, j: (0, 0))] + [_any_spec() for _ in stacked],
        out_specs=st_specs + [row_spec(w) for w, _ in rows] + [tr_spec(r) for r, _ in trs],
        out_shape=[jax.ShapeDtypeStruct(a.shape, a.dtype) for a in stacked]
        + [row(w, dt) for w, dt in rows] + [tr(r, dt) for r, dt in trs],
        input_output_aliases={n_in + k: k for k in range(N_STACKED)},
        compiler_params=_cparams("arbitrary", "arbitrary"),
        name="inproj_prompt",
    )(x, mod_a, g, wr, wt, bfb, *stacked)
    return outs[:N_STACKED], outs[N_STACKED:]


def _inproj_s_kernel(x_ref, mod_ref, g_ref, w_ref, bf_ref,
                     ak_o, av_o, aki_o, bk_o, bv_o, ck_o, cv_o, lf_o,
                     qa_o, qi_o, wi_o, qb_o, qc_o, lfrep_o):
    x = x_ref[...]
    gsz, r, _ = x.shape
    mod = mod_ref[...]
    h = _norm_mod(x, g_ref[...], mod[:, :, D:2 * D], mod[:, :, 0:D]).reshape(gsz * r, D)
    pr = _dot(h.astype(BF16), w_ref[...])
    lf = _store_rows(pr[:, N_SQ:], bf_ref[...], ak_o, av_o, aki_o, bk_o, bv_o, ck_o, cv_o, lf_o,
                     (slice(None), slice(None)))
    lfrep_o[...] = lf
    qa_o[...] = pr[:, _S_AQ:_S_AQ + 768].astype(BF16)
    qi_o[...] = pr[:, _S_AQI:_S_AQI + 1024].astype(BF16)
    wi_o[...] = jnp.concatenate(
        [jnp.broadcast_to(pr[:, _S_AWI + h:_S_AWI + h + 1], (gsz * r, LANES)) for h in range(H_IDX)], axis=1)
    qb_o[...] = pr[:, _S_BQ:_S_BQ + 2048].astype(BF16)
    qc_o[...] = pr[:, _S_CQ:_S_CQ + 768].astype(BF16)


def _inproj_sample(x, mod_a, g, w, bfb, gsz, layer):
    nb, r, _ = x.shape
    n = nb * r
    tm = gsz * r
    row_w = [128, 128, 64, 256, 256, 128, 128, H_C]
    outs = [(w_, F32) for w_ in row_w] + [(768, BF16), (1024, BF16), (1024, F32), (2048, BF16), (768, BF16), (128, F32)]
    return pl.pallas_call(
        _inproj_s_kernel,
        grid=(nb // gsz,),
        in_specs=[pl.BlockSpec((gsz, r, D), lambda i: (i, 0, 0)),
                  pl.BlockSpec((gsz, 1, 3 * D), lambda i: (i, 0, 0)),
                  pl.BlockSpec((1, D), lambda i: (0, 0)),
                  _layer_spec(w, layer),
                  pl.BlockSpec((1, LANES), lambda i: (0, 0))],
        out_specs=[pl.BlockSpec((tm, w_), lambda i: (i, 0)) for w_, _ in outs],
        out_shape=[jax.ShapeDtypeStruct((n, w_), dt) for w_, dt in outs],
        compiler_params=_cparams("arbitrary"),
        name="inproj_sample",
    )(x, mod_a, g, w, bfb)


_CUM_T = 256


def _cum_kernel(lf_ref, k_ref, o_ref):
    l = lf_ref.shape[1]
    r = lax.broadcasted_iota(I32, (_CUM_T, _CUM_T), 0)
    c = lax.broadcasted_iota(I32, (_CUM_T, _CUM_T), 1)
    tri = jnp.where(c <= r, 1.0, 0.0).astype(BF16)
    lane = lax.broadcasted_iota(I32, (_CUM_T, LANES), 1)
    carry = jnp.zeros((1, LANES), F32)
    for i in range(l // _CUM_T):
        rows = slice(i * _CUM_T, (i + 1) * _CUM_T)
        cum = _dot3_rhs(tri, lf_ref[0, rows, :]) + carry
        carry = cum[_CUM_T - 1:_CUM_T, :]
        hi, mid, lo = _split3(cum)
        piece = jnp.where(lane < 8, hi, jnp.where(lane < 16, mid, lo))
        o_ref[0, rows, 0:LANES] = k_ref[0, rows, :]
        o_ref[0, rows, LANES:2 * LANES] = -piece


def _cum_prompt(lfrep, ckb):
    b, l, _ = lfrep.shape
    return pl.pallas_call(
        _cum_kernel,
        grid=(b,),
        in_specs=[pl.BlockSpec((1, l, LANES), lambda i: (i, 0, 0)),
                  pl.BlockSpec((1, l, LANES), lambda i: (i, 0, 0))],
        out_specs=pl.BlockSpec((1, l, 2 * LANES), lambda i: (i, 0, 0)),
        out_shape=jax.ShapeDtypeStruct((b, l, 2 * LANES), BF16),
        compiler_params=_cparams("arbitrary"),
        name="forget_cumsum",
    )(lfrep, ckb)


def _online_update(s, m_scr, acc_scr, vta):
    m_prev = m_scr[...]
    m_new = jnp.maximum(m_prev, jnp.max(s, axis=0, keepdims=True))
    alpha = jnp.exp(m_prev - m_new)
    p = jnp.exp(s - m_new).astype(BF16)
    m_scr[...] = m_new
    n_g = len(vta)
    w = s.shape[1] // n_g
    for g in range(n_g):
        pv = _dot(vta[g], p[:, g * w:(g + 1) * w])
        acc_scr[g] = acc_scr[g] * alpha[:, g * w:(g + 1) * w] + pv


def _key_rows(start, n):
    return pl.ds(pl.multiple_of(start * TKC, TKC), n * TKC)


def _v_aug(vt_ref, start, n, g, rows):
    v = vt_ref[0, g * rows:(g + 1) * rows, _key_rows(start, n)]
    return jnp.concatenate([v, jnp.ones((16, n * TKC), BF16)], axis=0)


def _add_near(s, n, tiles):
    parts = []
    first_near = max(n - len(tiles), 0)
    if first_near > 0:
        parts.append(s[0:first_near * TKC])
    for blk in range(first_near, n):
        tile = tiles[n - 1 - blk]
        rows = s[blk * TKC:(blk + 1) * TKC]
        parts.append(rows if tile is None else rows + tile)
    return parts[0] if len(parts) == 1 else jnp.concatenate(parts, axis=0)


def _chunk_loop(qi, chunk):
    n_tot = (qi + 1) * Q_BLOCKS
    n_big = jnp.maximum(n_tot - NEAR_BLOCKS, 0) // FAR_BLOCKS

    def far(c, carry):
        chunk(c * FAR_BLOCKS, FAR_BLOCKS, False)
        return carry
    lax.fori_loop(0, n_big, far, 0)
    start = n_big * FAR_BLOCKS
    n_tail = n_tot - start
    for v in range(Q_BLOCKS, FAR_BLOCKS + NEAR_BLOCKS, Q_BLOCKS):
        @pl.when(n_tail == v)
        def _(v=v):
            chunk(start, v, True)


def _attn_a_kernel(qT_ref, qiT_ref, wT_ref, k_ref, ki_ref, vT_ref, bias_ref, oT_ref,
                   qpad, qipad, key_scr, am_scr, j_scr, m_scr, acc_scr, *, k_top):
    b = pl.program_id(0)
    qi = pl.program_id(1)
    n_chunk = (qi + 1) * Q_BLOCKS
    grp = H_A // KVH_A

    @pl.when((b == 0) & (qi == 0))
    def _():
        qpad[...] = jnp.zeros_like(qpad)
    for h in range(H_A):
        g = h // grp
        qpad[g * HD:(g + 1) * HD, h * TQ:(h + 1) * TQ] = qT_ref[0, h * HD:(h + 1) * HD, :]
    for h in range(H_IDX):
        qipad[:, h * TQ:(h + 1) * TQ] = qiT_ref[0, h * D_IDX:(h + 1) * D_IDX, :]
    w = wT_ref[0, 0:H_IDX, :] * IDX_W_SCALE
    grows = FAR_BLOCKS * TKC
    n_grp = (n_chunk + FAR_BLOCKS - 1) // FAR_BLOCKS
    row = lax.broadcasted_iota(I32, (TKC, TQ), 0)
    col = lax.broadcasted_iota(I32, (TKC, TQ), 1) + qi * TQ
    grow = lax.broadcasted_iota(I32, (grows, TQ), 0)

    def score_group(gi, carry):
        for bi in range(FAR_BLOCKS):
            start = pl.multiple_of(gi * grows + bi * TKC, TKC)
            sl = pl.ds(start, TKC)
            s = _dot(ki_ref[0, sl, :], qipad[...])
            sc = jnp.zeros((TKC, TQ), F32)
            for h in range(H_IDX):
                sc = sc + jnp.maximum(s[:, h * TQ:(h + 1) * TQ], 0.0) * w[h:h + 1, :]
            am_scr[sl, :] = jnp.where(row + start > col, -jnp.inf, sc)
        return carry
    lax.fori_loop(0, n_grp, score_group, 0)

    def count(ref, pred):
        def body(gi, cnt):
            start = pl.multiple_of(gi * grows, grows)
            hit = jnp.where(pred(ref[pl.ds(start, grows), :]), 1, 0).astype(I32)
            return cnt + jnp.sum(hit.reshape(grows // 8, 8, TQ), axis=0)
        cnt8 = lax.fori_loop(0, n_grp, body, jnp.zeros((8, TQ), I32))
        return jnp.sum(cnt8, axis=0, keepdims=True)

    needs_search = (qi + 1) * TQ > k_top

    @pl.when(jnp.logical_not(needs_search))
    def _():
        def body(gi, carry):
            am_scr[pl.ds(pl.multiple_of(gi * grows, grows), grows), :] = jnp.zeros((grows, TQ), F32)
            return carry
        lax.fori_loop(0, n_grp, body, 0)

    @pl.when(needs_search)
    def _():
        def it_body(it, t):
            cand = t ^ lax.shift_left(jnp.int32(1), 31 - it)
            cand_f = _key_to_float(cand)
            cnt = count(am_scr, lambda x: x >= cand_f)
            return jnp.where(cnt >= k_top, cand, t)
        thr = _threshold_value(lax.fori_loop(0, 32, it_body, jnp.full((1, TQ), INT_MIN, I32)))

        def recode(gi, cnts):
            start = pl.multiple_of(gi * grows, grows)
            x = am_scr[pl.ds(start, grows), :]
            gt, eq = x > thr, x == thr
            key_scr[pl.ds(start, grows), :] = jnp.where(gt, -1, jnp.where(eq, grow + start, 2 ** 31 - 1))
            fold = lambda m: jnp.sum(jnp.where(m, 1, 0).astype(I32).reshape(grows // 8, 8, TQ), axis=0)
            return cnts[0] + fold(gt), cnts[1] + fold(eq)
        z8 = jnp.zeros((8, TQ), I32)
        gt8, eq8 = lax.fori_loop(0, n_grp, recode, (z8, z8))
        cnt_gt = jnp.sum(gt8, axis=0, keepdims=True)
        cnt_eq = jnp.sum(eq8, axis=0, keepdims=True)
        need = k_top - cnt_gt
        n_bits = max(1, (key_scr.shape[0] - 1).bit_length())
        j_scr[...] = jnp.full((1, TQ), key_scr.shape[0], I32)

        @pl.when(jnp.max(jnp.where(cnt_eq > need, 1, 0)) > 0)
        def _():
            def tie_body(it, jv):
                cand = jv | lax.shift_left(jnp.int32(1), n_bits - 1 - it)
                ties_before = count(key_scr, lambda code: code < cand) - cnt_gt
                return jnp.where(ties_before < need, cand, jv)
            j_scr[...] = lax.fori_loop(0, n_bits, tie_body, jnp.zeros((1, TQ), I32))
        jv = j_scr[...]

        def body(gi, carry):
            start = pl.multiple_of(gi * grows, grows)
            am_scr[pl.ds(start, grows), :] = jnp.where(key_scr[pl.ds(start, grows), :] <= jv, 0.0, NEG)
            return carry
        lax.fori_loop(0, n_grp, body, 0)

    m_scr[...] = jnp.full(m_scr.shape, NEG, F32)
    acc_scr[...] = jnp.zeros_like(acc_scr)

    def chunk(start, n, near):
        sl = _key_rows(start, n)
        s = _dot(k_ref[0, sl, :], qpad[...])
        s = s + jnp.concatenate([am_scr[sl, :]] * H_A, axis=1)
        if near:
            s = _add_near(s, n, [jnp.concatenate([bias_ref[h, u] for h in range(H_A)], axis=1)
                                 for u in range(NEAR_BLOCKS)])
        _online_update(s, m_scr, acc_scr, [_v_aug(vT_ref, start, n, g, HD) for g in range(KVH_A)])

    _chunk_loop(qi, chunk)

    for h in range(H_A):
        g, hh = h // grp, h % grp
        a = acc_scr[g][:, hh * TQ:(hh + 1) * TQ]
        oT_ref[0, h * HD:(h + 1) * HD, :] = a[0:HD] / a[HD:HD + 1]


def _attn_a(aqT, aqiT, awiT, akb, akib, avT, bias_pt):
    b, _, l = aqT.shape
    k_top = min(K_TOP_MAX, l // 4)
    blk_t = lambda r: pl.BlockSpec((1, r, TQ), lambda i, j: (i, 0, j))
    full = lambda s1, s2: pl.BlockSpec((1, s1, s2), lambda i, j: (i, 0, 0))
    return pl.pallas_call(
        functools.partial(_attn_a_kernel, k_top=k_top),
        grid=(b, l // TQ),
        in_specs=[blk_t(384), blk_t(512), blk_t(16), full(l, 128), full(l, 64), full(128, l),
                  pl.BlockSpec((H_A, NEAR_BLOCKS, TKC, TQ), lambda i, j: (0, 0, 0, 0))],
        out_specs=blk_t(384),
        out_shape=jax.ShapeDtypeStruct((b, 384, l), F32),
        scratch_shapes=[pltpu.VMEM((128, H_A * TQ), BF16),
                        pltpu.VMEM((D_IDX, H_IDX * TQ), BF16),
                        pltpu.VMEM((l, TQ), I32),
                        pltpu.VMEM((l, TQ), F32),
                        pltpu.VMEM((1, TQ), I32),
                        pltpu.VMEM((1, H_A * TQ), F32),
                        pltpu.VMEM((KVH_A, HD + 16, (H_A // KVH_A) * TQ), F32)],
        compiler_params=_cparams("arbitrary", "arbitrary"),
        name="attn_dsa_prompt",
    )(aqT, aqiT, awiT, akb, akib, avT, bias_pt)


def _lambda_value(lam_ref, lam_init):
    lv = lam_ref[...]
    s1 = jnp.sum(lv[0:1] * lv[1:2], axis=1, keepdims=True)
    s2 = jnp.sum(lv[2:3] * lv[3:4], axis=1, keepdims=True)
    return jnp.exp(s1) - jnp.exp(s2) + lam_init


def _attn_b_kernel(qT_ref, k_ref, vT_ref, bias_ref, lam_ref, gsub_ref, oT_ref,
                   qpad, m_scr, acc_scr, *, lam_init):
    b = pl.program_id(0)
    qi = pl.program_id(1)
    grp = H_B // KVH_B

    @pl.when((b == 0) & (qi == 0))
    def _():
        qpad[...] = jnp.zeros_like(qpad)
    for h in range(H_B):
        for c in range(2):
            slot = (h // grp) * 2 + c
            hc = h * 2 + c
            qpad[slot * HD:(slot + 1) * HD, hc * TQ:(hc + 1) * TQ] = qT_ref[0, hc * HD:(hc + 1) * HD, :]

    m_scr[...] = jnp.full(m_scr.shape, NEG, F32)
    acc_scr[...] = jnp.zeros_like(acc_scr)

    def chunk(start, n, near):
        s = _dot(k_ref[0, _key_rows(start, n), :], qpad[...])
        if near:
            s = _add_near(s, n, [jnp.concatenate([bias_ref[hc // 2, u] for hc in range(2 * H_B)], axis=1)
                                 for u in range(NEAR_BLOCKS)])
        _online_update(s, m_scr, acc_scr, [_v_aug(vT_ref, start, n, g, 2 * HD) for g in range(KVH_B)])

    _chunk_loop(qi, chunk)

    lam = _lambda_value(lam_ref, lam_init)
    gs = gsub_ref[...]
    for h in range(H_B):
        g, hh = h // grp, h % grp
        a1 = acc_scr[g][:, (hh * 2) * TQ:(hh * 2 + 1) * TQ]
        a2 = acc_scr[g][:, (hh * 2 + 1) * TQ:(hh * 2 + 2) * TQ]
        o = a1[0:2 * HD] / a1[2 * HD:2 * HD + 1] - lam * (a2[0:2 * HD] / a2[2 * HD:2 * HD + 1])
        ms = jnp.mean(o * o, axis=0, keepdims=True)
        oT_ref[0, h * 2 * HD:(h + 1) * 2 * HD, :] = (o * lax.rsqrt(ms + EPS) * gs) * (1.0 - lam_init)


def _attn_b(bqT, bkb, bvT, bias_pt, lamvec, gsub_col, lam_init):
    b, _, l = bqT.shape
    blk_t = lambda r: pl.BlockSpec((1, r, TQ), lambda i, j: (i, 0, j))
    full = lambda s1, s2: pl.BlockSpec((1, s1, s2), lambda i, j: (i, 0, 0))
    return pl.pallas_call(
        functools.partial(_attn_b_kernel, lam_init=lam_init),
        grid=(b, l // TQ),
        in_specs=[blk_t(512), full(l, 256), full(256, l),
                  pl.BlockSpec((H_B, NEAR_BLOCKS, TKC, TQ), lambda i, j: (0, 0, 0, 0)),
                  pl.BlockSpec((4, HD), lambda i, j: (0, 0)),
                  pl.BlockSpec((2 * HD, 1), lambda i, j: (0, 0))],
        out_specs=blk_t(512),
        out_shape=jax.ShapeDtypeStruct((b, 512, l), F32),
        scratch_shapes=[pltpu.VMEM((256, 2 * H_B * TQ), BF16),
                        pltpu.VMEM((1, 2 * H_B * TQ), F32),
                        pltpu.VMEM((KVH_B, 2 * HD + 16, 2 * (H_B // KVH_B) * TQ), F32)],
        compiler_params=_cparams("arbitrary", "arbitrary"),
        name="attn_diff_prompt",
    )(bqT, bkb, bvT, bias_pt, lamvec, gsub_col)


def _attn_c_kernel(qT_ref, k_ref, vT_ref, oT_ref, qpad, m_scr, acc_scr):
    b = pl.program_id(0)
    qi = pl.program_id(1)
    grp = H_C // KVH_C

    @pl.when((b == 0) & (qi == 0))
    def _():
        r = lax.broadcasted_iota(I32, (256, H_C * TQ), 0) - 128
        cblk = lax.broadcasted_iota(I32, (256, H_C * TQ), 1) // TQ
        ones = (r >= 0) & (r < 24) & ((r % 8) == cblk)
        qpad[...] = jnp.where(ones, 1.0, 0.0).astype(BF16)
    for h in range(H_C):
        g = h // grp
        qpad[g * HD:(g + 1) * HD, h * TQ:(h + 1) * TQ] = qT_ref[0, h * HD:(h + 1) * HD, :]

    m_scr[...] = jnp.full(m_scr.shape, NEG, F32)
    acc_scr[...] = jnp.zeros_like(acc_scr)
    row = lax.broadcasted_iota(I32, (TKC, TQ), 0)
    col = lax.broadcasted_iota(I32, (TKC, TQ), 1)
    causal = [jnp.concatenate([jnp.where(row - col + (TQ - TKC) - u * TKC > 0, NEG, 0.0)] * H_C, axis=1)
              for u in range(Q_BLOCKS)] + [None]

    def chunk(start, n, near):
        s = _dot(k_ref[0, _key_rows(start, n), :], qpad[...])
        if near:
            s = _add_near(s, n, causal)
        _online_update(s, m_scr, acc_scr, [_v_aug(vT_ref, start, n, g, HD) for g in range(KVH_C)])

    _chunk_loop(qi, chunk)

    for h in range(H_C):
        g, hh = h // grp, h % grp
        a = acc_scr[g][:, hh * TQ:(hh + 1) * TQ]
        oT_ref[0, h * HD:(h + 1) * HD, :] = a[0:HD] / a[HD:HD + 1]


def _attn_c(cqT, kcat, cvT):
    b, _, l = cqT.shape
    blk_t = lambda r: pl.BlockSpec((1, r, TQ), lambda i, j: (i, 0, j))
    full = lambda s1, s2: pl.BlockSpec((1, s1, s2), lambda i, j: (i, 0, 0))
    return pl.pallas_call(
        _attn_c_kernel,
        grid=(b, l // TQ),
        in_specs=[blk_t(384), full(l, 256), full(128, l)],
        out_specs=blk_t(384),
        out_shape=jax.ShapeDtypeStruct((b, 384, l), F32),
        scratch_shapes=[pltpu.VMEM((256, H_C * TQ), BF16),
                        pltpu.VMEM((1, H_C * TQ), F32),
                        pltpu.VMEM((KVH_C, HD + 16, (H_C // KVH_C) * TQ), F32)],
        compiler_params=_cparams("arbitrary", "arbitrary"),
        name="attn_forget_prompt",
    )(cqT, kcat, cvT)


G_DEC = 4


def _stack_heads(ref, tok, n, width):
    return jnp.concatenate([ref[tok, h * width:(h + 1) * width] for h in range(n)], axis=0)


def _pad_new(x):
    return jnp.concatenate([x, jnp.zeros((PAGE - x.shape[0], x.shape[1]), x.dtype)], axis=0)


def _probabilities(s_pages):
    mx = s_pages[0]
    for s in s_pages[1:]:
        mx = jnp.maximum(mx, s)
    m = jnp.max(mx, axis=1, keepdims=True)
    p_pages = [jnp.exp(s - m) for s in s_pages]
    lsum = p_pages[0]
    for p in p_pages[1:]:
        lsum = lsum + p
    return p_pages, jnp.sum(lsum, axis=1, keepdims=True)


def _pv(p_pages, vt_pages, v_new):
    acc = _dot(p_pages[-1].astype(BF16), v_new)
    for p, vt in zip(p_pages[:-1], vt_pages):
        acc = acc + _dot_nt(p.astype(BF16), vt)
    return acc


class _PageFetch:
    def __init__(self, pt_ref, gsz, n_pages, hbms, indexers, bufs, sems):
        self.pt_ref, self.gsz, self.n_pages = pt_ref, gsz, n_pages
        self.hbms, self.indexers, self.bufs, self.sems = hbms, indexers, bufs, sems
        self.step = pl.program_id(0)
        self.last = pl.num_programs(0) - 1
        self.cur = lax.rem(self.step, 2)

    def _copy(self, c, page_id, buf, j):
        return pltpu.make_async_copy(self.hbms[c].at[self.indexers[c](page_id)], self.bufs[c].at[buf, j],
                                     self.sems.at[c, buf])

    def _start(self, step, buf, j):
        page_id = self.pt_ref[step * self.gsz + j // self.n_pages, j % self.n_pages]
        for c in range(len(self.hbms)):
            self._copy(c, page_id, buf, j).start()

    def _wait(self, buf):
        for c in range(len(self.hbms)):
            for j in range(self.gsz * self.n_pages):
                self._copy(c, 0, buf, j).wait()

    def begin(self):
        @pl.when(self.step == 0)
        def _():
            for j in range(self.gsz * self.n_pages):
                self._start(0, 0, j)
        self._wait(self.cur)

    def prefetch(self, j):
        self._start(jnp.minimum(self.step + 1, self.last), 1 - self.cur, j)

    def finish(self):
        @pl.when(self.step == self.last)
        def _():
            self._wait(1 - self.cur)

    def page(self, c, j):
        return self.bufs[c][self.cur, j]


def _any_spec():
    return pl.BlockSpec(memory_space=pl.ANY)


def _decode_a_kernel(pt_ref, qa_ref, qi_ref, wi_ref, ak_n, av_n, aki_n, bias_ref, kt_hbm, vt_hbm, kit_hbm,
                     oa_ref, kt_buf, vt_buf, kit_buf, sems, *, layer, gsz, n_pages, k_top):
    at_page = lambda pg: (layer, pg)
    fetch = _PageFetch(pt_ref, gsz, n_pages, (kt_hbm, vt_hbm, kit_hbm), (at_page,) * 3,
                       (kt_buf, vt_buf, kit_buf), sems)
    fetch.begin()
    r = qa_ref.shape[0] // gsz
    rows = gsz * r
    n_all = n_pages + 1
    lane = lax.broadcasted_iota(I32, (rows, PAGE), 1)
    qrow = lax.rem(lax.broadcasted_iota(I32, (rows, PAGE), 0), r)
    new_visible = lane <= qrow

    sc_pages = [[] for _ in range(n_all)]
    for g in range(gsz):
        tok = slice(g * r, (g + 1) * r)
        qi2 = _stack_heads(qi_ref, tok, H_IDX, 2 * D_IDX)[:, 0:D_IDX]
        wcol = _stack_heads(wi_ref, tok, H_IDX, LANES) * IDX_W_SCALE
        for p in range(n_all):
            if p < n_pages:
                fetch.prefetch(g * n_pages + p)
                z = _dot(qi2, fetch.page(2, g * n_pages + p).astype(BF16))
            else:
                z = _dot_nt(qi2, _pad_new(aki_n[tok, :]).astype(BF16))
            z = jnp.maximum(z, 0.0) * wcol
            sc = z[0:r]
            for h in range(1, H_IDX):
                sc = sc + z[h * r:(h + 1) * r]
            sc_pages[p].append(sc)
    key_pages = []
    for p in range(n_all):
        sc = jnp.concatenate(sc_pages[p], axis=0)
        key_pages.append(jnp.where(new_visible, sc, -jnp.inf) if p == n_pages else sc)

    def count(pred):
        tot = None
        for p, k in enumerate(key_pages):
            hit = jnp.where(pred(k, p), 1.0, 0.0)
            tot = hit if tot is None else tot + hit
        return jnp.sum(tot, axis=1, keepdims=True)

    bit = lambda b: jnp.int32(-2 ** 31 if b == 31 else 1 << b)
    rank = jnp.full((rows, 1), INT_MIN, I32)
    for it in range(16):
        c_hi, c_lo = rank ^ bit(31 - 2 * it), rank ^ bit(30 - 2 * it)
        c_both = c_hi ^ bit(30 - 2 * it)
        n_hi, n_lo, n_both = (count(lambda k, p, c=_key_to_float(c): k >= c) for c in (c_hi, c_lo, c_both))
        rank = jnp.where(n_both >= k_top, c_both,
                         jnp.where(n_hi >= k_top, c_hi, jnp.where(n_lo >= k_top, c_lo, rank)))
    thr = _threshold_value(rank)
    cnt_gt = count(lambda k, p: k > thr)
    cnt_eq = count(lambda k, p: k == thr)
    need = k_top - cnt_gt
    n_bits = max(1, (n_all * PAGE - 1).bit_length())

    def tie_search():
        jv = jnp.zeros((rows, 1), I32)
        for it in range(n_bits):
            cand = jv | jnp.int32(1 << (n_bits - 1 - it))
            cnt = count(lambda k, p: (k == thr) & ((lane + p * PAGE) < cand))
            jv = jnp.where(cnt < need, cand, jv)
        return jv

    any_excess = jnp.max(jnp.where(cnt_eq > need, 1, 0)) > 0
    jv = lax.cond(any_excess, tie_search, lambda: jnp.full((rows, 1), n_all * PAGE, I32))
    am_pages = []
    for p, k in enumerate(key_pages):
        sel = (k > thr) | ((k == thr) & ((lane + p * PAGE) <= jv))
        am_pages.append(jnp.where(sel, 0.0, NEG))

    bias_last = jnp.concatenate([bias_ref[h, 0] for h in range(H_A)], axis=0)
    bias_new = jnp.concatenate([bias_ref[h, 1] for h in range(H_A)], axis=0)
    for g in range(gsz):
        tok = slice(g * r, (g + 1) * r)
        qa2 = _stack_heads(qa_ref, tok, H_A, 2 * HD)
        s_pages = []
        for p in range(n_all):
            if p < n_pages:
                s = _dot(qa2, fetch.page(0, g * n_pages + p).reshape(2 * HD, PAGE).astype(BF16))
            else:
                s = _dot_nt(qa2, _pad_new(ak_n[tok, :]).astype(BF16))
            s = s + jnp.concatenate([am_pages[p][tok]] * H_A, axis=0)
            if p == n_pages - 1:
                s = s + bias_last
            if p == n_pages:
                s = s + bias_new
            s_pages.append(s)
        p_pages, lsum = _probabilities(s_pages)
        vts = [fetch.page(1, g * n_pages + p).reshape(2 * HD, PAGE).astype(BF16) for p in range(n_pages)]
        o = _pv(p_pages, vts, _pad_new(av_n[tok, :]).astype(BF16)) / lsum
        oa_ref[tok, :] = jnp.concatenate([o[h * r:(h + 1) * r] for h in range(H_A)], axis=1)
    fetch.finish()


def _decode_call(kernel_fn, name, page_table, tok_args, tok_widths, const_args, caches, page_shapes, out_w, r):
    nb, n_pages = page_table.shape
    gsz = G_DEC
    in_specs = [pl.BlockSpec((gsz * r, w), lambda i, pt: (i, 0)) for w in tok_widths]
    in_specs += [pl.BlockSpec(a.shape, functools.partial(lambda i, pt, n: (0,) * n, n=a.ndim)) for a in const_args]
    in_specs += [_any_spec() for _ in caches]
    grid_spec = pltpu.PrefetchScalarGridSpec(
        num_scalar_prefetch=1, grid=(nb // gsz,), in_specs=in_specs,
        out_specs=pl.BlockSpec((gsz * r, out_w), lambda i, pt: (i, 0)),
        scratch_shapes=[pltpu.VMEM((2, gsz * n_pages) + s, F32) for s in page_shapes]
        + [pltpu.SemaphoreType.DMA((len(caches), 2))])
    return pl.pallas_call(
        kernel_fn,
        grid_spec=grid_spec,
        out_shape=jax.ShapeDtypeStruct((nb * r, out_w), F32),
        compiler_params=_cparams("arbitrary"),
        name=name,
    )(page_table, *tok_args, *const_args, *caches)


def _decode_a(page_table, layer, qa, qi, wi, ak_n, av_n, aki_n, kt, vt, kit, bias_dec, r):
    n_pages = page_table.shape[1]
    k_top = min(K_TOP_MAX, (n_pages * PAGE + r) // 4)
    return _decode_call(
        functools.partial(_decode_a_kernel, layer=layer, gsz=G_DEC, n_pages=n_pages, k_top=k_top), "decode_dsa",
        page_table, (qa, qi, wi, ak_n, av_n, aki_n), (768, 1024, 1024, 128, 128, 64), (bias_dec[:H_A],),
        (kt, vt, kit), ((KVH_A, HD, PAGE), (KVH_A, HD, PAGE), (D_IDX, PAGE)), 768, r)


def _decode_b_kernel(pt_ref, qb_ref, bk_n, bv_n, bias_ref, lam_ref, gsub_ref, kt_hbm, v_hbm,
                     ob_ref, kt_buf, v_buf, sems, *, layer, gsz, n_pages, lam_init):
    at_page = lambda pg: (layer, pg)
    fetch = _PageFetch(pt_ref, gsz, n_pages, (kt_hbm, v_hbm), (at_page,) * 2, (kt_buf, v_buf), sems)
    fetch.begin()
    for j in range(gsz * n_pages):
        fetch.prefetch(j)
    r = qb_ref.shape[0] // gsz
    n_all = n_pages + 1
    grp = H_B // KVH_B
    half = 2 * grp * r
    bias_last = jnp.concatenate([bias_ref[h, 0] for h in range(H_B) for _ in range(2)], axis=0)
    bias_new = jnp.concatenate([bias_ref[h, 1] for h in range(H_B) for _ in range(2)], axis=0)
    lam = _lambda_value(lam_ref, lam_init)
    gs = gsub_ref[...]
    for g in range(gsz):
        tok = slice(g * r, (g + 1) * r)
        qb2 = _stack_heads(qb_ref, tok, 2 * H_B, 4 * HD)
        s_pages = []
        for p in range(n_all):
            if p < n_pages:
                s = _dot(qb2, fetch.page(0, g * n_pages + p).reshape(4 * HD, PAGE).astype(BF16))
            else:
                s = _dot_nt(qb2, _pad_new(bk_n[tok, :]).astype(BF16))
            if p == n_pages - 1:
                s = s + bias_last
            if p == n_pages:
                s = s + bias_new
            s_pages.append(s)
        p_pages, lsum = _probabilities(s_pages)
        v_new = _pad_new(bv_n[tok, :]).astype(BF16)
        outs = []
        for kv in range(KVH_B):
            rs = slice(kv * half, (kv + 1) * half)
            acc = _dot(p_pages[-1][rs].astype(BF16), v_new[:, kv * 2 * HD:(kv + 1) * 2 * HD])
            for p in range(n_pages):
                v = v_buf[fetch.cur, g * n_pages + p, pl.ds(kv, PAGE, stride=KVH_B), :].astype(BF16)
                acc = acc + _dot(p_pages[p][rs].astype(BF16), v)
            o = acc / lsum[rs]
            for hh in range(grp):
                od = o[(2 * hh) * r:(2 * hh + 1) * r] - lam * o[(2 * hh + 1) * r:(2 * hh + 2) * r]
                ms = jnp.mean(od * od, axis=1, keepdims=True)
                outs.append((od * lax.rsqrt(ms + EPS) * gs) * (1.0 - lam_init))
        ob_ref[tok, :] = jnp.concatenate(outs, axis=1)
    fetch.finish()


def _decode_b(page_table, layer, qb, bk_n, bv_n, kt, v2, bias_dec, lamvec, gsub_row, lam_init, r):
    n_pages = page_table.shape[1]
    return _decode_call(
        functools.partial(_decode_b_kernel, layer=layer, gsz=G_DEC, n_pages=n_pages, lam_init=lam_init),
        "decode_diff", page_table, (qb, bk_n, bv_n), (2048, 256, 256), (bias_dec[H_A:], lamvec, gsub_row),
        (kt, v2), ((KVH_B, 2, HD, PAGE), (KVH_B * PAGE, 2 * HD)), 512, r)


def _decode_c_kernel(pt_ref, qc_ref, ck_n, cv_n, lft_n, kt_hbm, vt_hbm, lf_hbm,
                     oc_ref, kt_buf, vt_buf, lf_buf, sems, *, layer, gsz, n_pages):
    at_page = lambda pg: (layer, pg)
    fetch = _PageFetch(pt_ref, gsz, n_pages, (kt_hbm, vt_hbm, lf_hbm),
                       (at_page, at_page, lambda pg: (layer, slice(None), pg)), (kt_buf, vt_buf, lf_buf), sems)
    fetch.begin()
    r = qc_ref.shape[0] // gsz
    n_all = n_pages + 1
    lane = lax.broadcasted_iota(I32, (r, PAGE), 1)
    qrow = lax.broadcasted_iota(I32, (r, PAGE), 0)
    causal_new = jnp.concatenate([jnp.where(lane <= qrow, 0.0, NEG)] * H_C, axis=0)
    ri = lax.broadcasted_iota(I32, (PAGE, PAGE), 0)
    ci = lax.broadcasted_iota(I32, (PAGE, PAGE), 1)
    upper = jnp.where(ri <= ci, 1.0, 0.0).astype(BF16)
    ones = jnp.ones((PAGE, PAGE), BF16)
    nr = 8 * n_all
    rr = lax.broadcasted_iota(I32, (nr, nr), 0)
    cc = lax.broadcasted_iota(I32, (nr, nr), 1)
    prev_pages = jnp.where(((rr % 8) == (cc % 8)) & ((cc // 8) < (rr // 8)), 1.0, 0.0).astype(BF16)
    zrow = jnp.zeros((8 - H_C, PAGE), F32)
    for g in range(gsz):
        tok = slice(g * r, (g + 1) * r)
        xs = []
        for p in range(n_pages):
            fetch.prefetch(g * n_pages + p)
            xs += [fetch.page(2, g * n_pages + p), zrow]
        x = jnp.concatenate(xs + [lft_n[g * 8:(g + 1) * 8, :]], axis=0)
        cum = _dot3_lhs(x, upper) + _dot3_rhs(prev_pages, _dot3_lhs(x, ones))
        qc2 = _stack_heads(qc_ref, tok, H_C, 2 * HD)
        s_pages = []
        for p in range(n_all):
            if p < n_pages:
                s = _dot(qc2, fetch.page(0, g * n_pages + p).reshape(2 * HD, PAGE).astype(BF16))
            else:
                s = _dot_nt(qc2, _pad_new(ck_n[tok, :]).astype(BF16)) + causal_new
            decay = jnp.concatenate(
                [jnp.broadcast_to(cum[p * 8 + h:p * 8 + h + 1, :], (r, PAGE)) for h in range(H_C)], axis=0)
            s_pages.append(s - decay)
        p_pages, lsum = _probabilities(s_pages)
        vts = [fetch.page(1, g * n_pages + p).reshape(2 * HD, PAGE).astype(BF16) for p in range(n_pages)]
        o = _pv(p_pages, vts, _pad_new(cv_n[tok, :]).astype(BF16)) / lsum
        oc_ref[tok, :] = jnp.concatenate([o[h * r:(h + 1) * r] for h in range(H_C)], axis=1)
    fetch.finish()


def _decode_c(page_table, layer, qc, ck_n, cv_n, lft_new, kt, vt, lft, r):
    n_pages = page_table.shape[1]
    assert r == 8
    return _decode_call(
        functools.partial(_decode_c_kernel, layer=layer, gsz=G_DEC, n_pages=n_pages), "decode_forget",
        page_table, (qc, ck_n, cv_n, lft_new.reshape(-1, LANES)), (768, 128, 128, LANES), (),
        (kt, vt, lft), ((KVH_C, HD, PAGE), (KVH_C, HD, PAGE), (H_C, PAGE)), 768, r)


def _merge_kernel(x_ref, mod_ref, g_ref, oa_ref, ob_ref, oc_ref, wg_ref, wpa_ref, wpb_ref, wpc_ref, wo_ref,
                  o_ref, *, transposed):
    x = x_ref[...]
    gsz, r, _ = x.shape
    mod = mod_ref[...]
    h = _norm_mod(x, g_ref[...], mod[:, :, D:2 * D], mod[:, :, 0:D]).reshape(gsz * r, D).astype(BF16)
    gates = _sigmoid(_dot(h, wg_ref[...]))
    if transposed:
        oa, ob, oc = oa_ref[0].T, ob_ref[0].T, oc_ref[0].T
    else:
        oa, ob, oc = oa_ref[...], ob_ref[...], oc_ref[...]
    merged = (gates[:, 0:D] * _dot(oa.astype(BF16), wpa_ref[...])
              + gates[:, D:2 * D] * _dot(ob.astype(BF16), wpb_ref[...])
              + gates[:, 2 * D:3 * D] * _dot(oc.astype(BF16), wpc_ref[...]))
    y = _dot(merged.astype(BF16), wo_ref[...]).reshape(gsz, r, D)
    o_ref[...] = x + mod[:, :, 2 * D:3 * D] * y


def _merge(x, mod_a, g, oa, ob, oc, wg, wpa, wpb, wpc, wo, gsz, r, transposed, layer):
    nb, rr, _ = x.shape
    const = lambda a: pl.BlockSpec(a.shape, lambda i, j: (0,) * a.ndim)
    lw = lambda a: _layer_spec(a, layer)
    if transposed:
        grid = (nb, rr // r)
        x_spec = pl.BlockSpec((1, r, D), lambda i, j: (i, j, 0))
        mod_spec = pl.BlockSpec((1, 1, 3 * D), lambda i, j: (i, 0, 0))
        o_spec = lambda a: pl.BlockSpec((1, a.shape[1], r), lambda i, j: (i, 0, j))
    else:
        grid = (nb // gsz, 1)
        x_spec = pl.BlockSpec((gsz, rr, D), lambda i, j: (i, 0, 0))
        mod_spec = pl.BlockSpec((gsz, 1, 3 * D), lambda i, j: (i, 0, 0))
        o_spec = lambda a: pl.BlockSpec((gsz * rr, a.shape[1]), lambda i, j: (i, 0))
    return pl.pallas_call(
        functools.partial(_merge_kernel, transposed=transposed),
        grid=grid,
        in_specs=[x_spec, mod_spec, const(g), o_spec(oa), o_spec(ob), o_spec(oc),
                  lw(wg), lw(wpa), lw(wpb), lw(wpc), lw(wo)],
        out_specs=x_spec,
        out_shape=jax.ShapeDtypeStruct(x.shape, F32),
        compiler_params=_cparams("arbitrary", "arbitrary"),
        name="merge_prompt" if transposed else "merge_sample",
    )(x, mod_a, g, oa, ob, oc, wg, wpa, wpb, wpc, wo)


def _ffn_kernel(x_ref, mod_ref, g_ref, w1_ref, w2_ref, gf_ref, o_ref, *, final):
    x = x_ref[...]
    gsz, r, _ = x.shape
    mod = mod_ref[...]
    h = _norm_mod(x, g_ref[...], mod[:, :, D:2 * D], mod[:, :, 0:D]).reshape(gsz * r, D).astype(BF16)
    u = jnp.maximum(_dot(h, w1_ref[...]), 0.0)
    y = _dot((u * u).astype(BF16), w2_ref[...]).reshape(gsz, r, D)
    x2 = x + mod[:, :, 2 * D:3 * D] * y
    if final:
        ms = jnp.mean(x2 * x2, axis=-1, keepdims=True)
        x2 = x2 * lax.rsqrt(ms + EPS) * gf_ref[...]
    o_ref[...] = x2


def _ffn(x, mod_b, g, w1, w2, g_final, gsz, r, final, name, layer):
    nb, rr, _ = x.shape
    const = lambda a: pl.BlockSpec(a.shape, lambda i, j: (0,) * a.ndim)
    lw = lambda a: _layer_spec(a, layer)
    if gsz == 1:
        grid = (nb, rr // r)
        x_spec = pl.BlockSpec((1, r, D), lambda i, j: (i, j, 0))
        mod_spec = pl.BlockSpec((1, 1, 3 * D), lambda i, j: (i, 0, 0))
    else:
        grid = (nb // gsz, 1)
        x_spec = pl.BlockSpec((gsz, rr, D), lambda i, j: (i, 0, 0))
        mod_spec = pl.BlockSpec((gsz, 1, 3 * D), lambda i, j: (i, 0, 0))
    return pl.pallas_call(
        functools.partial(_ffn_kernel, final=final),
        grid=grid,
        in_specs=[x_spec, mod_spec, const(g), lw(w1), lw(w2), const(g_final)],
        out_specs=x_spec,
        out_shape=jax.ShapeDtypeStruct(x.shape, F32),
        compiler_params=_cparams("arbitrary", "arbitrary"),
        name=name,
    )(x, mod_b, g, w1, w2, g_final)


TM_PROMPT = 512
G_SAMPLE = 16


def kernel(x_prompt, x_sample, c_prompt, c_sample, cache_a_k, cache_a_v, cache_a_kidx, cache_b_k, cache_b_v, cache_c_k, cache_c_v, cache_c_logf, page_table, t5_table, w_ada, b_ada, g_mix, g_ffn, w_in, b_forget, lam_q1, lam_k1, lam_q2, lam_k2, g_subln, w_gate, w_pa, w_pb, w_pc, w_out, w_ff1, w_ff2, g_final):
    depth = w_in.shape[0]
    nbp, seq, _ = x_prompt.shape
    nbs, dec_seq, _ = x_sample.shape
    n_pool, page = cache_a_k.shape[1], cache_a_k.shape[2]
    n_pages = page_table.shape[1]
    past_len = n_pages * page
    assert page == PAGE and seq % max(TM_PROMPT, _CUM_T, FAR_BLOCKS * TKC) == 0 and dec_seq == 8
    assert nbs % G_SAMPLE == 0 and nbs % G_DEC == 0
    tm = min(TM_PROMPT, seq)

    nc = nbp + nbs
    mod = _ada(jnp.concatenate([c_prompt, c_sample], axis=0), w_ada, b_ada)
    mod = mod.reshape(depth, nc, 1, 6 * D)
    bias_pt, bias_dec = _bias_tiles(t5_table, past_len, dec_seq)

    kv_t = lambda c: jnp.transpose(c, (0, 1, 3, 4, 2))
    a_kt, a_vt, c_kt, c_vt = kv_t(cache_a_k), kv_t(cache_a_v), kv_t(cache_c_k), kv_t(cache_c_v)
    a_kit = jnp.transpose(cache_a_kidx, (0, 1, 3, 2))
    b_kt = jnp.transpose(cache_b_k, (0, 1, 3, 4, 5, 2))
    b_v2 = cache_b_v.reshape(depth, n_pool, page * KVH_B, 2 * HD)
    c_lft = jnp.transpose(cache_c_logf, (0, 3, 1, 2))

    w_in_t = jnp.swapaxes(w_in, 1, 2)
    wr, wt = _prep_in_prompt(w_in_t)
    ws = _prep_in_sample(w_in_t)
    wg, wo = w_gate.astype(BF16), w_out.astype(BF16)
    wpa, wpb, wpc = w_pa.astype(BF16), w_pb.astype(BF16), w_pc.astype(BF16)
    wpa_s = _pad_proj_rows(w_pa, H_A, lambda h: h // (H_A // KVH_A))
    wpc_s = _pad_proj_rows(w_pc, H_C, lambda h: h // (H_C // KVH_C))
    w1, w2 = w_ff1.astype(BF16), w_ff2.astype(BF16)

    xp, xs = x_prompt, x_sample
    rows_s = []
    stacked = [jnp.zeros(s.shape, s.dtype) for s in _stacked_row_shapes(depth, nbp, seq)]
    g_final2 = g_final.reshape(1, D)
    for l in range(depth):
        lam_init = 0.8 - 0.6 * math.exp(-0.3 * l)
        bfb = _cf_block(b_forget[l].reshape(1, H_C))
        gm, gf = g_mix[l].reshape(1, D), g_ffn[l].reshape(1, D)
        lamvec = jnp.stack([lam_q1[l], lam_k1[l], lam_q2[l], lam_k2[l]])
        mod_pa, mod_pb = mod[l, :nbp, :, :3 * D], mod[l, :nbp, :, 3 * D:]
        mod_sa, mod_sb = mod[l, nbp:, :, :3 * D], mod[l, nbp:, :, 3 * D:]
        last = l == depth - 1

        stacked, (akb, akib, bkb, ckb, lfrep, awiT, aqT, aqiT, bqT, cqT, avT, bvT, cvT) = _inproj_prompt(
            xp, mod_pa, gm, wr, wt, bfb, tm, l, stacked)
        kcat = _cum_prompt(lfrep, ckb)
        oaT = _attn_a(aqT, aqiT, awiT, akb, akib, avT, bias_pt[:H_A])
        obT = _attn_b(bqT, bkb, bvT, bias_pt[H_A:], lamvec, g_subln[l].reshape(2 * HD, 1), lam_init)
        ocT = _attn_c(cqT, kcat, cvT)
        x1 = _merge(xp, mod_pa, gm, oaT, obT, ocT, wg, wpa, wpb, wpc, wo, 1, tm, True, l)
        xp = _ffn(x1, mod_pb, gf, w1, w2, g_final2, 1, tm, last, "ffn_prompt", l)

        (sak, sav, saki, sbk, sbv, sck, scv, slf, qa, qi, wi, qb, qc, slfrep) = _inproj_sample(
            xs, mod_sa, gm, ws, bfb, G_SAMPLE, l)
        rows_s.append((sak, sav, saki, sbk, sbv, sck, scv, slf))
        lft_new = jnp.swapaxes(slfrep.reshape(nbs, dec_seq, LANES)[:, :, 0:8], 1, 2)
        lft_new = jnp.concatenate([lft_new, jnp.zeros((nbs, 8, LANES - dec_seq), F32)], axis=2)
        lane_head = jnp.arange(8)[None, :, None] < H_C
        lft_new = jnp.where(lane_head, lft_new, 0.0)
        oa = _decode_a(page_table, l, qa, qi, wi, sak, sav, saki, a_kt, a_vt, a_kit, bias_dec, dec_seq)
        ob = _decode_b(page_table, l, qb, sbk, sbv, b_kt, b_v2, bias_dec, lamvec, g_subln[l].reshape(1, 2 * HD),
                       lam_init, dec_seq)
        oc = _decode_c(page_table, l, qc, sck, scv, lft_new, c_kt, c_vt, c_lft, dec_seq)
        x1s = _merge(xs, mod_sa, gm, oa, ob, oc, wg, wpa_s, wpb, wpc_s, wo, G_SAMPLE, dec_seq, False, l)
        xs = _ffn(x1s, mod_sb, gf, w1, w2, g_final2, G_SAMPLE, dec_seq, last, "ffn_sample", l)

    def stack(rows, i):
        return jnp.stack([r[i] for r in rows])

    def stack_sample(i, shape):
        return stack(rows_s, i).reshape((depth, nbs, dec_seq) + shape)

    def prompt_t(y, shape):
        n = len(shape)
        y = y.reshape((depth, nbp) + shape + (seq,))
        return jnp.transpose(y, (0, 1, n + 2) + tuple(range(2, n + 2)))

    bv, lf, akT, avT, akiT, bkT, ckT, cvT = stacked
    out_p = (prompt_t(akT, (KVH_A, HD)), prompt_t(avT, (KVH_A, HD)), prompt_t(akiT, (D_IDX,)),
             prompt_t(bkT, (KVH_B, 2, HD)), bv.reshape(depth, nbp, seq, KVH_B, 2 * HD),
             prompt_t(ckT, (KVH_C, HD)), prompt_t(cvT, (KVH_C, HD)), lf)
    out_s = (stack_sample(0, (KVH_A, HD)), stack_sample(1, (KVH_A, HD)), stack_sample(2, (D_IDX,)),
             stack_sample(3, (KVH_B, 2, HD)), stack_sample(4, (KVH_B, 2 * HD)), stack_sample(5, (KVH_C, HD)),
             stack_sample(6, (KVH_C, HD)), stack_sample(7, (H_C,)))
    return (xp, xs) + out_p + out_s
```

```python
import functools
import math

import numpy as np
import jax
import jax.numpy as jnp
from jax import lax
from jax.experimental import pallas as pl
from jax.experimental.pallas import tpu as pltpu

F32 = jnp.float32
BF16 = jnp.bfloat16
I32 = jnp.int32

D = 1024
HD = 64
H_A, KVH_A = 6, 2
H_IDX, D_IDX = 8, 64
K_TOP_MAX = 256
IDX_W_SCALE = (H_IDX ** -0.5) * (D_IDX ** -0.5)
H_B, KVH_B = 4, 2
H_C, KVH_C = 6, 2
N_BUCKETS, T5_MAX_EXACT, T5_MAX_DIST = 32, 16, 128
D_FF = 4 * D
EPS = 1e-6
N_HEADS_T5 = H_A + H_B
PAGE = 128

LANES = 128
TQ = 256
TKC = 128
Q_BLOCKS = TQ // TKC
NEAR_BLOCKS = Q_BLOCKS + 1
FAR_BLOCKS = 4
NEG = -1e30
INT_MIN = -2 ** 31
VMEM_LIMIT = 56 * 1024 * 1024

_W = dict(aq=H_A * HD, ak=KVH_A * HD, av=KVH_A * HD, aqi=H_IDX * D_IDX, aki=D_IDX, awi=H_IDX,
          bq=H_B * 2 * HD, bk=KVH_B * 2 * HD, bv=KVH_B * 2 * HD,
          cq=H_C * HD, ck=KVH_C * HD, cv=KVH_C * HD, cf=H_C)
_OFF = {}
_o = 0
for _k, _v in _W.items():
    _OFF[_k] = _o
    _o += _v

_R_AK, _R_AV, _R_AKI, _R_BK, _R_BV, _R_CK, _R_CV, _R_CF = 0, 128, 256, 384, 640, 896, 1024, 1152
N_ROWS = 1280
_P_AK, _P_AKI, _P_BK, _P_CK, _P_CF, _P_BV = 0, 128, 256, 512, 640, 768
N_PROWS = 1024
_T_AQ, _T_AQI, _T_AWI, _T_BQ, _T_CQ, _T_AV, _T_BV, _T_CV = 0, 384, 896, 912, 1424, 1808, 1936, 2192
_T_AK, _T_AKI, _T_BK, _T_CK = 2320, 2448, 2512, 2768
N_T = 2896
_S_AQ, _S_AQI, _S_AWI, _S_BQ, _S_CQ = 0, 768, 1792, 1920, 3968
N_SQ = 4736


def _cparams(*sem):
    return pltpu.CompilerParams(dimension_semantics=sem, vmem_limit_bytes=VMEM_LIMIT)


def _dot(a, b):
    return jnp.dot(a, b, preferred_element_type=F32)


def _dot_nt(a, b):
    return lax.dot_general(a, b, (((1,), (1,)), ((), ())), preferred_element_type=F32)


def _split3(x):
    hi = x.astype(BF16)
    r1 = x - hi.astype(F32)
    mid = r1.astype(BF16)
    lo = (r1 - mid.astype(F32)).astype(BF16)
    return hi, mid, lo


def _dot3_rhs(a_bf16, x):
    hi, mid, lo = _split3(x)
    return _dot(a_bf16, hi) + _dot(a_bf16, mid) + _dot(a_bf16, lo)


def _dot3_lhs(x, b_bf16):
    hi, mid, lo = _split3(x)
    return _dot(hi, b_bf16) + _dot(mid, b_bf16) + _dot(lo, b_bf16)


def _norm_mod(x, g, sc, sh):
    ms = jnp.mean(x * x, axis=-1, keepdims=True)
    return (x * lax.rsqrt(ms + EPS) * g) * (1.0 + sc) + sh


def _log_sigmoid(z):
    return jnp.minimum(z, 0.0) - jnp.log1p(jnp.exp(-jnp.abs(z)))


def _sigmoid(z):
    return 1.0 / (1.0 + jnp.exp(-z))


def _key_to_float(s):
    bits = s ^ ((s >> 31) & 0x7FFFFFFF)
    return lax.bitcast_convert_type(bits, F32)


def _threshold_value(s):
    t = _key_to_float(s)
    return jnp.where(t != t, -jnp.inf, t)


def _ada_kernel(c_ref, w_ref, b_ref, o_ref):
    c = c_ref[...]
    s = (c * _sigmoid(c)).astype(BF16)
    o_ref[0] = _dot(s, w_ref[0].astype(BF16)) + b_ref[0]


def _ada(c_all, w_ada, b_ada):
    depth = w_ada.shape[0]
    nc = c_all.shape[0]
    nt = 6
    return pl.pallas_call(
        _ada_kernel,
        grid=(depth, nt),
        in_specs=[pl.BlockSpec((nc, D), lambda l, j: (0, 0)),
                  pl.BlockSpec((1, D, D), lambda l, j: (l, 0, j)),
                  pl.BlockSpec((1, 1, D), lambda l, j: (l, 0, j))],
        out_specs=pl.BlockSpec((1, nc, D), lambda l, j: (l, 0, j)),
        out_shape=jax.ShapeDtypeStruct((depth, nc, 6 * D), F32),
        compiler_params=_cparams("arbitrary", "arbitrary"),
        name="ada",
    )(c_all, w_ada, b_ada.reshape(depth, 1, 6 * D))


def _t5_bucket_np(rel):
    n = np.maximum(rel, 0)
    nf = np.maximum(n, 1).astype(np.float32)
    large = T5_MAX_EXACT + (np.log(nf / np.float32(T5_MAX_EXACT)) / np.float32(math.log(T5_MAX_DIST / T5_MAX_EXACT))
                            * np.float32(N_BUCKETS - T5_MAX_EXACT)).astype(np.int32)
    return np.where(n < T5_MAX_EXACT, n, np.minimum(large, N_BUCKETS - 1)).astype(np.int32)


def _bias_bucket_tables(past_len, dec_seq):
    s = np.arange(TKC)[:, None]
    t = np.arange(TQ)[None, :]
    tiles = []
    for u in range(NEAR_BLOCKS):
        rel = t - s - (TQ - TKC) + u * TKC
        tiles.append(np.where(rel >= 0, _t5_bucket_np(rel), -1))
    prompt = np.stack(tiles).astype(np.int32)
    i = np.arange(dec_seq)[:, None]
    lane = np.arange(PAGE)[None, :]
    rel_last = (past_len + i) - (past_len - PAGE + lane)
    rel_new = i - lane
    dl = _t5_bucket_np(rel_last)
    dn = np.where((rel_new >= 0) & (lane < dec_seq), _t5_bucket_np(rel_new), -1)
    dec = np.stack([dl, dn]).astype(np.int32)
    return prompt, dec


def _bias_kernel(tab_ref, pb_ref, db_ref, pt_ref, dt_ref):
    pb = pb_ref[...]
    db = db_ref[...]
    for h in range(N_HEADS_T5):
        def lut(bk):
            acc = jnp.zeros(bk.shape, F32)
            for b in range(N_BUCKETS):
                acc = jnp.where(bk == b, tab_ref[b, h], acc)
            return jnp.where(bk < 0, NEG, acc - tab_ref[N_BUCKETS - 1, h])
        pt_ref[h] = lut(pb)
        dt_ref[h] = lut(db)


def _bias_tiles(t5_table, past_len, dec_seq):
    pb, db = _bias_bucket_tables(past_len, dec_seq)
    return pl.pallas_call(
        _bias_kernel,
        in_specs=[pl.BlockSpec(memory_space=pltpu.SMEM),
                  pl.BlockSpec(memory_space=pltpu.VMEM),
                  pl.BlockSpec(memory_space=pltpu.VMEM)],
        out_specs=[pl.BlockSpec(memory_space=pltpu.VMEM), pl.BlockSpec(memory_space=pltpu.VMEM)],
        out_shape=[jax.ShapeDtypeStruct((N_HEADS_T5,) + pb.shape, F32),
                   jax.ShapeDtypeStruct((N_HEADS_T5,) + db.shape, F32)],
        name="t5_bias_tiles",
    )(t5_table, jnp.asarray(pb), jnp.asarray(db))


def _cf_block(cf):
    n = cf.shape[0]
    z2 = jnp.zeros((n, 2), cf.dtype)
    return jnp.concatenate([cf, z2, cf, z2, cf, z2, jnp.zeros((n, LANES - 24), cf.dtype)], axis=1)


class _Plan:
    def __init__(self):
        self.runs = []

    def seg(self, name, lo=0, hi=None, scale=1.0):
        hi = _W[name] if hi is None else hi
        return self.raw(_OFF[name] + lo, _OFF[name] + hi, scale)

    def raw(self, lo, hi, scale=1.0):
        self.runs.append((lo, hi, scale))
        return self

    def zeros(self, n):
        if n > 0:
            self.runs.append((None, n, 0.0))
        return self

    def cf_block(self):
        for _ in range(3):
            self.seg("cf").zeros(8 - H_C)
        return self.zeros(LANES - 24)

    def padded_heads(self, name, n_heads, slot_of, n_slots, scale=1.0):
        for h in range(n_heads):
            s = slot_of(h)
            self.zeros(s * HD).seg(name, h * HD, (h + 1) * HD, scale).zeros((n_slots - 1 - s) * HD)
        return self

    def rows_block(self):
        return (self.seg("ak").seg("av").seg("aki").zeros(64).seg("bk").seg("bv").seg("ck").seg("cv").cf_block())

    def arrange(self, w_rows):
        depth, _, width = w_rows.shape
        pieces = []
        for lo, hi, scale in self.runs:
            if lo is None:
                pieces.append(jnp.zeros((depth, hi, width), w_rows.dtype))
                continue
            piece = w_rows[:, lo:hi, :]
            pieces.append(piece if scale == 1.0 else piece * scale)
        return jnp.concatenate(pieces, axis=1).astype(BF16)


_QSCALE = HD ** -0.5


def _prep_in_prompt(w_in_t):
    w_t = (_Plan().seg("aq", scale=_QSCALE).seg("aqi").seg("awi").zeros(8).seg("bq", scale=_QSCALE)
           .seg("cq", scale=_QSCALE).seg("av").seg("bv").seg("cv").seg("ak").seg("aki").seg("bk").seg("ck"))
    w_r = _Plan().seg("ak").seg("aki").zeros(64).seg("bk").seg("ck").cf_block().seg("bv")
    return jnp.swapaxes(w_r.arrange(w_in_t), 1, 2), w_t.arrange(w_in_t)


def _prep_in_sample(w_in_t):
    plan = _Plan()
    plan.padded_heads("aq", H_A, lambda h: h // (H_A // KVH_A), 2, _QSCALE)
    plan.padded_heads("aqi", H_IDX, lambda h: 0, 2)
    plan.seg("awi").zeros(LANES - H_IDX)
    plan.padded_heads("bq", 2 * H_B, lambda hc: (hc // 2 // (H_B // KVH_B)) * 2 + hc % 2, 4, _QSCALE)
    plan.padded_heads("cq", H_C, lambda h: h // (H_C // KVH_C), 2, _QSCALE)
    return jnp.swapaxes(plan.rows_block().arrange(w_in_t), 1, 2)


def _pad_proj_rows(wp, n_heads, kv_of):
    plan = _Plan()
    for h in range(n_heads):
        s = kv_of(h)
        plan.zeros(s * HD).raw(h * HD, (h + 1) * HD).zeros((1 - s) * HD)
    return plan.arrange(wp)


def _store_rows(pr, bf, ak_o, av_o, aki_o, bk_o, bv_o, ck_o, cv_o, lf_o, idx):
    ak_o[idx] = pr[:, _R_AK:_R_AK + 128]
    av_o[idx] = pr[:, _R_AV:_R_AV + 128]
    aki_o[idx] = pr[:, _R_AKI:_R_AKI + 64]
    bk_o[idx] = pr[:, _R_BK:_R_BK + 256]
    bv_o[idx] = pr[:, _R_BV:_R_BV + 256]
    ck_o[idx] = pr[:, _R_CK:_R_CK + 128]
    cv_o[idx] = pr[:, _R_CV:_R_CV + 128]
    lf = _log_sigmoid(pr[:, _R_CF:_R_CF + 128] + bf)
    lf_o[idx] = lf[:, 0:H_C]
    return lf


N_STACKED = 8


def _inproj_p_kernel(x_ref, mod_ref, g_ref, wr_ref, wt_ref, bf_ref, *refs):
    (bv_o, lf_o, akT_o, avT_o, akiT_o, bkT_o, ckT_o, cvT_o,
     akb_o, akib_o, bkb_o, ckb_o, lfrep_o, awiT_o,
     aqT_o, aqiT_o, bqT_o, cqT_o, avTb_o, bvTb_o, cvTb_o) = refs[N_STACKED:]
    x = x_ref[0]
    mod = mod_ref[0]
    h = _norm_mod(x, g_ref[...], mod[:, D:2 * D], mod[:, 0:D])
    pr = _dot(h.astype(BF16), wr_ref[...])
    for kv in range(KVH_B):
        bv_o[0, pl.ds(kv, x.shape[0], stride=KVH_B), :] = pr[:, _P_BV + kv * 2 * HD:_P_BV + (kv + 1) * 2 * HD]
    lf = _log_sigmoid(pr[:, _P_CF:_P_CF + 128] + bf_ref[...])
    lf_o[0] = lf[:, 0:H_C]
    lfrep_o[0] = lf
    akb_o[0] = pr[:, _P_AK:_P_AK + 128].astype(BF16)
    akib_o[0] = pr[:, _P_AKI:_P_AKI + 64].astype(BF16)
    bkb_o[0] = pr[:, _P_BK:_P_BK + 256].astype(BF16)
    ckb_o[0] = pr[:, _P_CK:_P_CK + 128].astype(BF16)
    pt = _dot(wt_ref[...], h.T.astype(BF16))
    akT_o[0] = pt[_T_AK:_T_AK + 128]
    avT_o[0] = pt[_T_AV:_T_AV + 128]
    akiT_o[0] = pt[_T_AKI:_T_AKI + 64]
    bkT_o[0] = pt[_T_BK:_T_BK + 256]
    ckT_o[0] = pt[_T_CK:_T_CK + 128]
    cvT_o[0] = pt[_T_CV:_T_CV + 128]
    awiT_o[0] = pt[_T_AWI:_T_AWI + 16]
    aqT_o[0] = pt[_T_AQ:_T_AQ + 384].astype(BF16)
    aqiT_o[0] = pt[_T_AQI:_T_AQI + 512].astype(BF16)
    bqT_o[0] = pt[_T_BQ:_T_BQ + 512].astype(BF16)
    cqT_o[0] = pt[_T_CQ:_T_CQ + 384].astype(BF16)
    avTb_o[0] = pt[_T_AV:_T_AV + 128].astype(BF16)
    bvTb_o[0] = pt[_T_BV:_T_BV + 256].astype(BF16)
    cvTb_o[0] = pt[_T_CV:_T_CV + 128].astype(BF16)


def _layer_spec(a, layer):
    return pl.BlockSpec((None,) + a.shape[1:], lambda *_: (layer,) + (0,) * (a.ndim - 1))


def _stacked_row_shapes(depth, b, l):
    return ([jax.ShapeDtypeStruct((depth, b, l * KVH_B, 2 * HD), F32), jax.ShapeDtypeStruct((depth, b, l, H_C), F32)]
            + [jax.ShapeDtypeStruct((depth, b, r, l), F32) for r in (128, 128, 64, 256, 128, 128)])


def _inproj_prompt(x, mod_a, g, wr, wt, bfb, tm, layer, stacked):
    b, l, _ = x.shape
    row = lambda w, dt: jax.ShapeDtypeStruct((b, l, w), dt)
    tr = lambda r, dt: jax.ShapeDtypeStruct((b, r, l), dt)
    row_spec = lambda w: pl.BlockSpec((1, tm, w), lambda i, j: (i, j, 0))
    tr_spec = lambda r: pl.BlockSpec((1, r, tm), lambda i, j: (i, 0, j))
    st_specs = ([pl.BlockSpec((None, 1, tm * KVH_B, 2 * HD), lambda i, j: (layer, i, j, 0)),
                 pl.BlockSpec((None, 1, tm, H_C), lambda i, j: (layer, i, j, 0))]
                + [pl.BlockSpec((None, 1, a.shape[2], tm), lambda i, j: (layer, i, 0, j)) for a in stacked[2:]])
    rows = [(128, BF16), (64, BF16), (256, BF16), (128, BF16), (128, F32)]
    trs = [(16, F32), (384, BF16), (512, BF16), (512, BF16), (384, BF16), (128, BF16), (256, BF16), (128, BF16)]
    n_in = 6
    outs = pl.pallas_call(
        _inproj_p_kernel,
        grid=(b, l // tm),
        in_specs=[pl.BlockSpec((1, tm, D), lambda i, j: (i, j, 0)),
                  pl.BlockSpec((1, 1, 3 * D), lambda i, j: (i, 0, 0)),
                  pl.BlockSpec((1, D), lambda i, j: (0, 0)),
                  _layer_spec(wr, layer), _layer_spec(wt, layer),
                  pl.BlockSpec((1, LANES), lambda i, j: (0, 0))] + [_any_spec() for _ in stacked],
        out_specs=st_specs + [row_spec(w) for w, _ in rows] + [tr_spec(r) for r, _ in trs],
        out_shape=[jax.ShapeDtypeStruct(a.shape, a.dtype) for a in stacked]
        + [row(w, dt) for w, dt in rows] + [tr(r, dt) for r, dt in trs],
        input_output_aliases={n_in + k: k for k in range(N_STACKED)},
        compiler_params=_cparams("arbitrary", "arbitrary"),
        name="inproj_prompt",
    )(x, mod_a, g, wr, wt, bfb, *stacked)
    return outs[:N_STACKED], outs[N_STACKED:]


def _inproj_s_kernel(x_ref, mod_ref, g_ref, w_ref, bf_ref,
                     ak_o, av_o, aki_o, bk_o, bv_o, ck_o, cv_o, lf_o,
                     qa_o, qi_o, wi_o, qb_o, qc_o, lfrep_o):
    x = x_ref[...]
    gsz, r, _ = x.shape
    mod = mod_ref[...]
    h = _norm_mod(x, g_ref[...], mod[:, :, D:2 * D], mod[:, :, 0:D]).reshape(gsz * r, D)
    pr = _dot(h.astype(BF16), w_ref[...])
    lf = _store_rows(pr[:, N_SQ:], bf_ref[...], ak_o, av_o, aki_o, bk_o, bv_o, ck_o, cv_o, lf_o,
                     (slice(None), slice(None)))
    lfrep_o[...] = lf
    qa_o[...] = pr[:, _S_AQ:_S_AQ + 768].astype(BF16)
    qi_o[...] = pr[:, _S_AQI:_S_AQI + 1024].astype(BF16)
    wi_o[...] = jnp.concatenate(
        [jnp.broadcast_to(pr[:, _S_AWI + h:_S_AWI + h + 1], (gsz * r, LANES)) for h in range(H_IDX)], axis=1)
    qb_o[...] = pr[:, _S_BQ:_S_BQ + 2048].astype(BF16)
    qc_o[...] = pr[:, _S_CQ:_S_CQ + 768].astype(BF16)


def _inproj_sample(x, mod_a, g, w, bfb, gsz, layer):
    nb, r, _ = x.shape
    n = nb * r
    tm = gsz * r
    row_w = [128, 128, 64, 256, 256, 128, 128, H_C]
    outs = [(w_, F32) for w_ in row_w] + [(768, BF16), (1024, BF16), (1024, F32), (2048, BF16), (768, BF16), (128, F32)]
    return pl.pallas_call(
        _inproj_s_kernel,
        grid=(nb // gsz,),
        in_specs=[pl.BlockSpec((gsz, r, D), lambda i: (i, 0, 0)),
                  pl.BlockSpec((gsz, 1, 3 * D), lambda i: (i, 0, 0)),
                  pl.BlockSpec((1, D), lambda i: (0, 0)),
                  _layer_spec(w, layer),
                  pl.BlockSpec((1, LANES), lambda i: (0, 0))],
        out_specs=[pl.BlockSpec((tm, w_), lambda i: (i, 0)) for w_, _ in outs],
        out_shape=[jax.ShapeDtypeStruct((n, w_), dt) for w_, dt in outs],
        compiler_params=_cparams("arbitrary"),
        name="inproj_sample",
    )(x, mod_a, g, w, bfb)


_CUM_T = 256


def _cum_kernel(lf_ref, k_ref, o_ref):
    l = lf_ref.shape[1]
    r = lax.broadcasted_iota(I32, (_CUM_T, _CUM_T), 0)
    c = lax.broadcasted_iota(I32, (_CUM_T, _CUM_T), 1)
    tri = jnp.where(c <= r, 1.0, 0.0).astype(BF16)
    lane = lax.broadcasted_iota(I32, (_CUM_T, LANES), 1)
    carry = jnp.zeros((1, LANES), F32)
    for i in range(l // _CUM_T):
        rows = slice(i * _CUM_T, (i + 1) * _CUM_T)
        cum = _dot3_rhs(tri, lf_ref[0, rows, :]) + carry
        carry = cum[_CUM_T - 1:_CUM_T, :]
        hi, mid, lo = _split3(cum)
        piece = jnp.where(lane < 8, hi, jnp.where(lane < 16, mid, lo))
        o_ref[0, rows, 0:LANES] = k_ref[0, rows, :]
        o_ref[0, rows, LANES:2 * LANES] = -piece


def _cum_prompt(lfrep, ckb):
    b, l, _ = lfrep.shape
    return pl.pallas_call(
        _cum_kernel,
        grid=(b,),
        in_specs=[pl.BlockSpec((1, l, LANES), lambda i: (i, 0, 0)),
                  pl.BlockSpec((1, l, LANES), lambda i: (i, 0, 0))],
        out_specs=pl.BlockSpec((1, l, 2 * LANES), lambda i: (i, 0, 0)),
        out_shape=jax.ShapeDtypeStruct((b, l, 2 * LANES), BF16),
        compiler_params=_cparams("arbitrary"),
        name="forget_cumsum",
    )(lfrep, ckb)


def _online_update(s, m_scr, acc_scr, vta):
    m_prev = m_scr[...]
    m_new = jnp.maximum(m_prev, jnp.max(s, axis=0, keepdims=True))
    alpha = jnp.exp(m_prev - m_new)
    p = jnp.exp(s - m_new).astype(BF16)
    m_scr[...] = m_new
    n_g = len(vta)
    w = s.shape[1] // n_g
    for g in range(n_g):
        pv = _dot(vta[g], p[:, g * w:(g + 1) * w])
        acc_scr[g] = acc_scr[g] * alpha[:, g * w:(g + 1) * w] + pv


def _key_rows(start, n):
    return pl.ds(pl.multiple_of(start * TKC, TKC), n * TKC)


def _v_aug(vt_ref, start, n, g, rows):
    v = vt_ref[0, g * rows:(g + 1) * rows, _key_rows(start, n)]
    return jnp.concatenate([v, jnp.ones((16, n * TKC), BF16)], axis=0)


def _add_near(s, n, tiles):
    parts = []
    first_near = max(n - len(tiles), 0)
    if first_near > 0:
        parts.append(s[0:first_near * TKC])
    for blk in range(first_near, n):
        tile = tiles[n - 1 - blk]
        rows = s[blk * TKC:(blk + 1) * TKC]
        parts.append(rows if tile is None else rows + tile)
    return parts[0] if len(parts) == 1 else jnp.concatenate(parts, axis=0)


def _chunk_loop(qi, chunk):
    n_tot = (qi + 1) * Q_BLOCKS
    n_big = jnp.maximum(n_tot - NEAR_BLOCKS, 0) // FAR_BLOCKS

    def far(c, carry):
        chunk(c * FAR_BLOCKS, FAR_BLOCKS, False)
        return carry
    lax.fori_loop(0, n_big, far, 0)
    start = n_big * FAR_BLOCKS
    n_tail = n_tot - start
    for v in range(Q_BLOCKS, FAR_BLOCKS + NEAR_BLOCKS, Q_BLOCKS):
        @pl.when(n_tail == v)
        def _(v=v):
            chunk(start, v, True)


def _attn_a_kernel(qT_ref, qiT_ref, wT_ref, k_ref, ki_ref, vT_ref, bias_ref, oT_ref,
                   qpad, qipad, key_scr, am_scr, j_scr, m_scr, acc_scr, *, k_top):
    b = pl.program_id(0)
    qi = pl.program_id(1)
    n_chunk = (qi + 1) * Q_BLOCKS
    grp = H_A // KVH_A

    @pl.when((b == 0) & (qi == 0))
    def _():
        qpad[...] = jnp.zeros_like(qpad)
    for h in range(H_A):
        g = h // grp
        qpad[g * HD:(g + 1) * HD, h * TQ:(h + 1) * TQ] = qT_ref[0, h * HD:(h + 1) * HD, :]
    for h in range(H_IDX):
        qipad[:, h * TQ:(h + 1) * TQ] = qiT_ref[0, h * D_IDX:(h + 1) * D_IDX, :]
    w = wT_ref[0, 0:H_IDX, :] * IDX_W_SCALE
    grows = FAR_BLOCKS * TKC
    n_grp = (n_chunk + FAR_BLOCKS - 1) // FAR_BLOCKS
    row = lax.broadcasted_iota(I32, (TKC, TQ), 0)
    col = lax.broadcasted_iota(I32, (TKC, TQ), 1) + qi * TQ
    grow = lax.broadcasted_iota(I32, (grows, TQ), 0)

    def score_group(gi, carry):
        for bi in range(FAR_BLOCKS):
            start = pl.multiple_of(gi * grows + bi * TKC, TKC)
            sl = pl.ds(start, TKC)
            s = _dot(ki_ref[0, sl, :], qipad[...])
            sc = jnp.zeros((TKC, TQ), F32)
            for h in range(H_IDX):
                sc = sc + jnp.maximum(s[:, h * TQ:(h + 1) * TQ], 0.0) * w[h:h + 1, :]
            am_scr[sl, :] = jnp.where(row + start > col, -jnp.inf, sc)
        return carry
    lax.fori_loop(0, n_grp, score_group, 0)

    def count(ref, pred):
        def body(gi, cnt):
            start = pl.multiple_of(gi * grows, grows)
            hit = jnp.where(pred(ref[pl.ds(start, grows), :]), 1, 0).astype(I32)
            return cnt + jnp.sum(hit.reshape(grows // 8, 8, TQ), axis=0)
        cnt8 = lax.fori_loop(0, n_grp, body, jnp.zeros((8, TQ), I32))
        return jnp.sum(cnt8, axis=0, keepdims=True)

    needs_search = (qi + 1) * TQ > k_top

    @pl.when(jnp.logical_not(needs_search))
    def _():
        def body(gi, carry):
            am_scr[pl.ds(pl.multiple_of(gi * grows, grows), grows), :] = jnp.zeros((grows, TQ), F32)
            return carry
        lax.fori_loop(0, n_grp, body, 0)

    @pl.when(needs_search)
    def _():
        def it_body(it, t):
            cand = t ^ lax.shift_left(jnp.int32(1), 31 - it)
            cand_f = _key_to_float(cand)
            cnt = count(am_scr, lambda x: x >= cand_f)
            return jnp.where(cnt >= k_top, cand, t)
        thr = _threshold_value(lax.fori_loop(0, 32, it_body, jnp.full((1, TQ), INT_MIN, I32)))

        def recode(gi, cnts):
            start = pl.multiple_of(gi * grows, grows)
            x = am_scr[pl.ds(start, grows), :]
            gt, eq = x > thr, x == thr
            key_scr[pl.ds(start, grows), :] = jnp.where(gt, -1, jnp.where(eq, grow + start, 2 ** 31 - 1))
            fold = lambda m: jnp.sum(jnp.where(m, 1, 0).astype(I32).reshape(grows // 8, 8, TQ), axis=0)
            return cnts[0] + fold(gt), cnts[1] + fold(eq)
        z8 = jnp.zeros((8, TQ), I32)
        gt8, eq8 = lax.fori_loop(0, n_grp, recode, (z8, z8))
        cnt_gt = jnp.sum(gt8, axis=0, keepdims=True)
        cnt_eq = jnp.sum(eq8, axis=0, keepdims=True)
        need = k_top - cnt_gt
        n_bits = max(1, (key_scr.shape[0] - 1).bit_length())
        j_scr[...] = jnp.full((1, TQ), key_scr.shape[0], I32)

        @pl.when(jnp.max(jnp.where(cnt_eq > need, 1, 0)) > 0)
        def _():
            def tie_body(it, jv):
                cand = jv | lax.shift_left(jnp.int32(1), n_bits - 1 - it)
                ties_before = count(key_scr, lambda code: code < cand) - cnt_gt
                return jnp.where(ties_before < need, cand, jv)
            j_scr[...] = lax.fori_loop(0, n_bits, tie_body, jnp.zeros((1, TQ), I32))
        jv = j_scr[...]

        def body(gi, carry):
            start = pl.multiple_of(gi * grows, grows)
            am_scr[pl.ds(start, grows), :] = jnp.where(key_scr[pl.ds(start, grows), :] <= jv, 0.0, NEG)
            return carry
        lax.fori_loop(0, n_grp, body, 0)

    m_scr[...] = jnp.full(m_scr.shape, NEG, F32)
    acc_scr[...] = jnp.zeros_like(acc_scr)

    def chunk(start, n, near):
        sl = _key_rows(start, n)
        s = _dot(k_ref[0, sl, :], qpad[...])
        s = s + jnp.concatenate([am_scr[sl, :]] * H_A, axis=1)
        if near:
            s = _add_near(s, n, [jnp.concatenate([bias_ref[h, u] for h in range(H_A)], axis=1)
                                 for u in range(NEAR_BLOCKS)])
        _online_update(s, m_scr, acc_scr, [_v_aug(vT_ref, start, n, g, HD) for g in range(KVH_A)])

    _chunk_loop(qi, chunk)

    for h in range(H_A):
        g, hh = h // grp, h % grp
        a = acc_scr[g][:, hh * TQ:(hh + 1) * TQ]
        oT_ref[0, h * HD:(h + 1) * HD, :] = a[0:HD] / a[HD:HD + 1]


def _attn_a(aqT, aqiT, awiT, akb, akib, avT, bias_pt):
    b, _, l = aqT.shape
    k_top = min(K_TOP_MAX, l // 4)
    blk_t = lambda r: pl.BlockSpec((1, r, TQ), lambda i, j: (i, 0, j))
    full = lambda s1, s2: pl.BlockSpec((1, s1, s2), lambda i, j: (i, 0, 0))
    return pl.pallas_call(
        functools.partial(_attn_a_kernel, k_top=k_top),
        grid=(b, l // TQ),
        in_specs=[blk_t(384), blk_t(512), blk_t(16), full(l, 128), full(l, 64), full(128, l),
                  pl.BlockSpec((H_A, NEAR_BLOCKS, TKC, TQ), lambda i, j: (0, 0, 0, 0))],
        out_specs=blk_t(384),
        out_shape=jax.ShapeDtypeStruct((b, 384, l), F32),
        scratch_shapes=[pltpu.VMEM((128, H_A * TQ), BF16),
                        pltpu.VMEM((D_IDX, H_IDX * TQ), BF16),
                        pltpu.VMEM((l, TQ), I32),
                        pltpu.VMEM((l, TQ), F32),
                        pltpu.VMEM((1, TQ), I32),
                        pltpu.VMEM((1, H_A * TQ), F32),
                        pltpu.VMEM((KVH_A, HD + 16, (H_A // KVH_A) * TQ), F32)],
        compiler_params=_cparams("arbitrary", "arbitrary"),
        name="attn_dsa_prompt",
    )(aqT, aqiT, awiT, akb, akib, avT, bias_pt)


def _lambda_value(lam_ref, lam_init):
    lv = lam_ref[...]
    s1 = jnp.sum(lv[0:1] * lv[1:2], axis=1, keepdims=True)
    s2 = jnp.sum(lv[2:3] * lv[3:4], axis=1, keepdims=True)
    return jnp.exp(s1) - jnp.exp(s2) + lam_init


def _attn_b_kernel(qT_ref, k_ref, vT_ref, bias_ref, lam_ref, gsub_ref, oT_ref,
                   qpad, m_scr, acc_scr, *, lam_init):
    b = pl.program_id(0)
    qi = pl.program_id(1)
    grp = H_B // KVH_B

    @pl.when((b == 0) & (qi == 0))
    def _():
        qpad[...] = jnp.zeros_like(qpad)
    for h in range(H_B):
        for c in range(2):
            slot = (h // grp) * 2 + c
            hc = h * 2 + c
            qpad[slot * HD:(slot + 1) * HD, hc * TQ:(hc + 1) * TQ] = qT_ref[0, hc * HD:(hc + 1) * HD, :]

    m_scr[...] = jnp.full(m_scr.shape, NEG, F32)
    acc_scr[...] = jnp.zeros_like(acc_scr)

    def chunk(start, n, near):
        s = _dot(k_ref[0, _key_rows(start, n), :], qpad[...])
        if near:
            s = _add_near(s, n, [jnp.concatenate([bias_ref[hc // 2, u] for hc in range(2 * H_B)], axis=1)
                                 for u in range(NEAR_BLOCKS)])
        _online_update(s, m_scr, acc_scr, [_v_aug(vT_ref, start, n, g, 2 * HD) for g in range(KVH_B)])

    _chunk_loop(qi, chunk)

    lam = _lambda_value(lam_ref, lam_init)
    gs = gsub_ref[...]
    for h in range(H_B):
        g, hh = h // grp, h % grp
        a1 = acc_scr[g][:, (hh * 2) * TQ:(hh * 2 + 1) * TQ]
        a2 = acc_scr[g][:, (hh * 2 + 1) * TQ:(hh * 2 + 2) * TQ]
        o = a1[0:2 * HD] / a1[2 * HD:2 * HD + 1] - lam * (a2[0:2 * HD] / a2[2 * HD:2 * HD + 1])
        ms = jnp.mean(o * o, axis=0, keepdims=True)
        oT_ref[0, h * 2 * HD:(h + 1) * 2 * HD, :] = (o * lax.rsqrt(ms + EPS) * gs) * (1.0 - lam_init)


def _attn_b(bqT, bkb, bvT, bias_pt, lamvec, gsub_col, lam_init):
    b, _, l = bqT.shape
    blk_t = lambda r: pl.BlockSpec((1, r, TQ), lambda i, j: (i, 0, j))
    full = lambda s1, s2: pl.BlockSpec((1, s1, s2), lambda i, j: (i, 0, 0))
    return pl.pallas_call(
        functools.partial(_attn_b_kernel, lam_init=lam_init),
        grid=(b, l // TQ),
        in_specs=[blk_t(512), full(l, 256), full(256, l),
                  pl.BlockSpec((H_B, NEAR_BLOCKS, TKC, TQ), lambda i, j: (0, 0, 0, 0)),
                  pl.BlockSpec((4, HD), lambda i, j: (0, 0)),
                  pl.BlockSpec((2 * HD, 1), lambda i, j: (0, 0))],
        out_specs=blk_t(512),
        out_shape=jax.ShapeDtypeStruct((b, 512, l), F32),
        scratch_shapes=[pltpu.VMEM((256, 2 * H_B * TQ), BF16),
                        pltpu.VMEM((1, 2 * H_B * TQ), F32),
                        pltpu.VMEM((KVH_B, 2 * HD + 16, 2 * (H_B // KVH_B) * TQ), F32)],
        compiler_params=_cparams("arbitrary", "arbitrary"),
        name="attn_diff_prompt",
    )(bqT, bkb, bvT, bias_pt, lamvec, gsub_col)


def _attn_c_kernel(qT_ref, k_ref, vT_ref, oT_ref, qpad, m_scr, acc_scr):
    b = pl.program_id(0)
    qi = pl.program_id(1)
    grp = H_C // KVH_C

    @pl.when((b == 0) & (qi == 0))
    def _():
        r = lax.broadcasted_iota(I32, (256, H_C * TQ), 0) - 128
        cblk = lax.broadcasted_iota(I32, (256, H_C * TQ), 1) // TQ
        ones = (r >= 0) & (r < 24) & ((r % 8) == cblk)
        qpad[...] = jnp.where(ones, 1.0, 0.0).astype(BF16)
    for h in range(H_C):
        g = h // grp
        qpad[g * HD:(g + 1) * HD, h * TQ:(h + 1) * TQ] = qT_ref[0, h * HD:(h + 1) * HD, :]

    m_scr[...] = jnp.full(m_scr.shape, NEG, F32)
    acc_scr[...] = jnp.zeros_like(acc_scr)
    row = lax.broadcasted_iota(I32, (TKC, TQ), 0)
    col = lax.broadcasted_iota(I32, (TKC, TQ), 1)
    causal = [jnp.concatenate([jnp.where(row - col + (TQ - TKC) - u * TKC > 0, NEG, 0.0)] * H_C, axis=1)
              for u in range(Q_BLOCKS)] + [None]

    def chunk(start, n, near):
        s = _dot(k_ref[0, _key_rows(start, n), :], qpad[...])
        if near:
            s = _add_near(s, n, causal)
        _online_update(s, m_scr, acc_scr, [_v_aug(vT_ref, start, n, g, HD) for g in range(KVH_C)])

    _chunk_loop(qi, chunk)

    for h in range(H_C):
        g, hh = h // grp, h % grp
        a = acc_scr[g][:, hh * TQ:(hh + 1) * TQ]
        oT_ref[0, h * HD:(h + 1) * HD, :] = a[0:HD] / a[HD:HD + 1]


def _attn_c(cqT, kcat, cvT):
    b, _, l = cqT.shape
    blk_t = lambda r: pl.BlockSpec((1, r, TQ), lambda i, j: (i, 0, j))
    full = lambda s1, s2: pl.BlockSpec((1, s1, s2), lambda i, j: (i, 0, 0))
    return pl.pallas_call(
        _attn_c_kernel,
        grid=(b, l // TQ),
        in_specs=[blk_t(384), full(l, 256), full(128, l)],
        out_specs=blk_t(384),
        out_shape=jax.ShapeDtypeStruct((b, 384, l), F32),
        scratch_shapes=[pltpu.VMEM((256, H_C * TQ), BF16),
                        pltpu.VMEM((1, H_C * TQ), F32),
                        pltpu.VMEM((KVH_C, HD + 16, (H_C // KVH_C) * TQ), F32)],
        compiler_params=_cparams("arbitrary", "arbitrary"),
        name="attn_forget_prompt",
    )(cqT, kcat, cvT)


G_DEC = 4


def _stack_heads(ref, tok, n, width):
    return jnp.concatenate([ref[tok, h * width:(h + 1) * width] for h in range(n)], axis=0)


def _pad_new(x):
    return jnp.concatenate([x, jnp.zeros((PAGE - x.shape[0], x.shape[1]), x.dtype)], axis=0)


def _probabilities(s_pages):
    mx = s_pages[0]
    for s in s_pages[1:]:
        mx = jnp.maximum(mx, s)
    m = jnp.max(mx, axis=1, keepdims=True)
    p_pages = [jnp.exp(s - m) for s in s_pages]
    lsum = p_pages[0]
    for p in p_pages[1:]:
        lsum = lsum + p
    return p_pages, jnp.sum(lsum, axis=1, keepdims=True)


def _pv(p_pages, vt_pages, v_new):
    acc = _dot(p_pages[-1].astype(BF16), v_new)
    for p, vt in zip(p_pages[:-1], vt_pages):
        acc = acc + _dot_nt(p.astype(BF16), vt)
    return acc


class _PageFetch:
    def __init__(self, pt_ref, gsz, n_pages, hbms, indexers, bufs, sems):
        self.pt_ref, self.gsz, self.n_pages = pt_ref, gsz, n_pages
        self.hbms, self.indexers, self.bufs, self.sems = hbms, indexers, bufs, sems
        self.step = pl.program_id(0)
        self.last = pl.num_programs(0) - 1
        self.cur = lax.rem(self.step, 2)

    def _copy(self, c, page_id, buf, j):
        return pltpu.make_async_copy(self.hbms[c].at[self.indexers[c](page_id)], self.bufs[c].at[buf, j],
                                     self.sems.at[c, buf])

    def _start(self, step, buf, j):
        page_id = self.pt_ref[step * self.gsz + j // self.n_pages, j % self.n_pages]
        for c in range(len(self.hbms)):
            self._copy(c, page_id, buf, j).start()

    def _wait(self, buf):
        for c in range(len(self.hbms)):
            for j in range(self.gsz * self.n_pages):
                self._copy(c, 0, buf, j).wait()

    def begin(self):
        @pl.when(self.step == 0)
        def _():
            for j in range(self.gsz * self.n_pages):
                self._start(0, 0, j)
        self._wait(self.cur)

    def prefetch(self, j):
        self._start(jnp.minimum(self.step + 1, self.last), 1 - self.cur, j)

    def finish(self):
        @pl.when(self.step == self.last)
        def _():
            self._wait(1 - self.cur)

    def page(self, c, j):
        return self.bufs[c][self.cur, j]


def _any_spec():
    return pl.BlockSpec(memory_space=pl.ANY)


def _decode_a_kernel(pt_ref, qa_ref, qi_ref, wi_ref, ak_n, av_n, aki_n, bias_ref, kt_hbm, vt_hbm, kit_hbm,
                     oa_ref, kt_buf, vt_buf, kit_buf, sems, *, layer, gsz, n_pages, k_top):
    at_page = lambda pg: (layer, pg)
    fetch = _PageFetch(pt_ref, gsz, n_pages, (kt_hbm, vt_hbm, kit_hbm), (at_page,) * 3,
                       (kt_buf, vt_buf, kit_buf), sems)
    fetch.begin()
    r = qa_ref.shape[0] // gsz
    rows = gsz * r
    n_all = n_pages + 1
    lane = lax.broadcasted_iota(I32, (rows, PAGE), 1)
    qrow = lax.rem(lax.broadcasted_iota(I32, (rows, PAGE), 0), r)
    new_visible = lane <= qrow

    sc_pages = [[] for _ in range(n_all)]
    for g in range(gsz):
        tok = slice(g * r, (g + 1) * r)
        qi2 = _stack_heads(qi_ref, tok, H_IDX, 2 * D_IDX)[:, 0:D_IDX]
        wcol = _stack_heads(wi_ref, tok, H_IDX, LANES) * IDX_W_SCALE
        for p in range(n_all):
            if p < n_pages:
                fetch.prefetch(g * n_pages + p)
                z = _dot(qi2, fetch.page(2, g * n_pages + p).astype(BF16))
            else:
                z = _dot_nt(qi2, _pad_new(aki_n[tok, :]).astype(BF16))
            z = jnp.maximum(z, 0.0) * wcol
            sc = z[0:r]
            for h in range(1, H_IDX):
                sc = sc + z[h * r:(h + 1) * r]
            sc_pages[p].append(sc)
    key_pages = []
    for p in range(n_all):
        sc = jnp.concatenate(sc_pages[p], axis=0)
        key_pages.append(jnp.where(new_visible, sc, -jnp.inf) if p == n_pages else sc)

    def count(pred):
        tot = None
        for p, k in enumerate(key_pages):
            hit = jnp.where(pred(k, p), 1.0, 0.0)
            tot = hit if tot is None else tot + hit
        return jnp.sum(tot, axis=1, keepdims=True)

    bit = lambda b: jnp.int32(-2 ** 31 if b == 31 else 1 << b)
    rank = jnp.full((rows, 1), INT_MIN, I32)
    for it in range(16):
        c_hi, c_lo = rank ^ bit(31 - 2 * it), rank ^ bit(30 - 2 * it)
        c_both = c_hi ^ bit(30 - 2 * it)
        n_hi, n_lo, n_both = (count(lambda k, p, c=_key_to_float(c): k >= c) for c in (c_hi, c_lo, c_both))
        rank = jnp.where(n_both >= k_top, c_both,
                         jnp.where(n_hi >= k_top, c_hi, jnp.where(n_lo >= k_top, c_lo, rank)))
    thr = _threshold_value(rank)
    cnt_gt = count(lambda k, p: k > thr)
    cnt_eq = count(lambda k, p: k == thr)
    need = k_top - cnt_gt
    n_bits = max(1, (n_all * PAGE - 1).bit_length())

    def tie_search():
        jv = jnp.zeros((rows, 1), I32)
        for it in range(n_bits):
            cand = jv | jnp.int32(1 << (n_bits - 1 - it))
            cnt = count(lambda k, p: (k == thr) & ((lane + p * PAGE) < cand))
            jv = jnp.where(cnt < need, cand, jv)
        return jv

    any_excess = jnp.max(jnp.where(cnt_eq > need, 1, 0)) > 0
    jv = lax.cond(any_excess, tie_search, lambda: jnp.full((rows, 1), n_all * PAGE, I32))
    am_pages = []
    for p, k in enumerate(key_pages):
        sel = (k > thr) | ((k == thr) & ((lane + p * PAGE) <= jv))
        am_pages.append(jnp.where(sel, 0.0, NEG))

    bias_last = jnp.concatenate([bias_ref[h, 0] for h in range(H_A)], axis=0)
    bias_new = jnp.concatenate([bias_ref[h, 1] for h in range(H_A)], axis=0)
    for g in range(gsz):
        tok = slice(g * r, (g + 1) * r)
        qa2 = _stack_heads(qa_ref, tok, H_A, 2 * HD)
        s_pages = []
        for p in range(n_all):
            if p < n_pages:
                s = _dot(qa2, fetch.page(0, g * n_pages + p).reshape(2 * HD, PAGE).astype(BF16))
            else:
                s = _dot_nt(qa2, _pad_new(ak_n[tok, :]).astype(BF16))
            s = s + jnp.concatenate([am_pages[p][tok]] * H_A, axis=0)
            if p == n_pages - 1:
                s = s + bias_last
            if p == n_pages:
                s = s + bias_new
            s_pages.append(s)
        p_pages, lsum = _probabilities(s_pages)
        vts = [fetch.page(1, g * n_pages + p).reshape(2 * HD, PAGE).astype(BF16) for p in range(n_pages)]
        o = _pv(p_pages, vts, _pad_new(av_n[tok, :]).astype(BF16)) / lsum
        oa_ref[tok, :] = jnp.concatenate([o[h * r:(h + 1) * r] for h in range(H_A)], axis=1)
    fetch.finish()


def _decode_call(kernel_fn, name, page_table, tok_args, tok_widths, const_args, caches, page_shapes, out_w, r):
    nb, n_pages = page_table.shape
    gsz = G_DEC
    in_specs = [pl.BlockSpec((gsz * r, w), lambda i, pt: (i, 0)) for w in tok_widths]
    in_specs += [pl.BlockSpec(a.shape, functools.partial(lambda i, pt, n: (0,) * n, n=a.ndim)) for a in const_args]
    in_specs += [_any_spec() for _ in caches]
    grid_spec = pltpu.PrefetchScalarGridSpec(
        num_scalar_prefetch=1, grid=(nb // gsz,), in_specs=in_specs,
        out_specs=pl.BlockSpec((gsz * r, out_w), lambda i, pt: (i, 0)),
        scratch_shapes=[pltpu.VMEM((2, gsz * n_pages) + s, F32) for s in page_shapes]
        + [pltpu.SemaphoreType.DMA((len(caches), 2))])
    return pl.pallas_call(
        kernel_fn,
        grid_spec=grid_spec,
        out_shape=jax.ShapeDtypeStruct((nb * r, out_w), F32),
        compiler_params=_cparams("arbitrary"),
        name=name,
    )(page_table, *tok_args, *const_args, *caches)


def _decode_a(page_table, layer, qa, qi, wi, ak_n, av_n, aki_n, kt, vt, kit, bias_dec, r):
    n_pages = page_table.shape[1]
    k_top = min(K_TOP_MAX, (n_pages * PAGE + r) // 4)
    return _decode_call(
        functools.partial(_decode_a_kernel, layer=layer, gsz=G_DEC, n_pages=n_pages, k_top=k_top), "decode_dsa",
        page_table, (qa, qi, wi, ak_n, av_n, aki_n), (768, 1024, 1024, 128, 128, 64), (bias_dec[:H_A],),
        (kt, vt, kit), ((KVH_A, HD, PAGE), (KVH_A, HD, PAGE), (D_IDX, PAGE)), 768, r)


def _decode_b_kernel(pt_ref, qb_ref, bk_n, bv_n, bias_ref, lam_ref, gsub_ref, kt_hbm, v_hbm,
                     ob_ref, kt_buf, v_buf, sems, *, layer, gsz, n_pages, lam_init):
    at_page = lambda pg: (layer, pg)
    fetch = _PageFetch(pt_ref, gsz, n_pages, (kt_hbm, v_hbm), (at_page,) * 2, (kt_buf, v_buf), sems)
    fetch.begin()
    for j in range(gsz * n_pages):
        fetch.prefetch(j)
    r = qb_ref.shape[0] // gsz
    n_all = n_pages + 1
    grp = H_B // KVH_B
    half = 2 * grp * r
    bias_last = jnp.concatenate([bias_ref[h, 0] for h in range(H_B) for _ in range(2)], axis=0)
    bias_new = jnp.concatenate([bias_ref[h, 1] for h in range(H_B) for _ in range(2)], axis=0)
    lam = _lambda_value(lam_ref, lam_init)
    gs = gsub_ref[...]
    for g in range(gsz):
        tok = slice(g * r, (g + 1) * r)
        qb2 = _stack_heads(qb_ref, tok, 2 * H_B, 4 * HD)
        s_pages = []
        for p in range(n_all):
            if p < n_pages:
                s = _dot(qb2, fetch.page(0, g * n_pages + p).reshape(4 * HD, PAGE).astype(BF16))
            else:
                s = _dot_nt(qb2, _pad_new(bk_n[tok, :]).astype(BF16))
            if p == n_pages - 1:
                s = s + bias_last
            if p == n_pages:
                s = s + bias_new
            s_pages.append(s)
        p_pages, lsum = _probabilities(s_pages)
        v_new = _pad_new(bv_n[tok, :]).astype(BF16)
        outs = []
        for kv in range(KVH_B):
            rs = slice(kv * half, (kv + 1) * half)
            acc = _dot(p_pages[-1][rs].astype(BF16), v_new[:, kv * 2 * HD:(kv + 1) * 2 * HD])
            for p in range(n_pages):
                v = v_buf[fetch.cur, g * n_pages + p, pl.ds(kv, PAGE, stride=KVH_B), :].astype(BF16)
                acc = acc + _dot(p_pages[p][rs].astype(BF16), v)
            o = acc / lsum[rs]
            for hh in range(grp):
                od = o[(2 * hh) * r:(2 * hh + 1) * r] - lam * o[(2 * hh + 1) * r:(2 * hh + 2) * r]
                ms = jnp.mean(od * od, axis=1, keepdims=True)
                outs.append((od * lax.rsqrt(ms + EPS) * gs) * (1.0 - lam_init))
        ob_ref[tok, :] = jnp.concatenate(outs, axis=1)
    fetch.finish()


def _decode_b(page_table, layer, qb, bk_n, bv_n, kt, v2, bias_dec, lamvec, gsub_row, lam_init, r):
    n_pages = page_table.shape[1]
    return _decode_call(
        functools.partial(_decode_b_kernel, layer=layer, gsz=G_DEC, n_pages=n_pages, lam_init=lam_init),
        "decode_diff", page_table, (qb, bk_n, bv_n), (2048, 256, 256), (bias_dec[H_A:], lamvec, gsub_row),
        (kt, v2), ((KVH_B, 2, HD, PAGE), (KVH_B * PAGE, 2 * HD)), 512, r)


def _decode_c_kernel(pt_ref, qc_ref, ck_n, cv_n, lft_n, kt_hbm, vt_hbm, lf_hbm,
                     oc_ref, kt_buf, vt_buf, lf_buf, sems, *, layer, gsz, n_pages):
    at_page = lambda pg: (layer, pg)
    fetch = _PageFetch(pt_ref, gsz, n_pages, (kt_hbm, vt_hbm, lf_hbm),
                       (at_page, at_page, lambda pg: (layer, slice(None), pg)), (kt_buf, vt_buf, lf_buf), sems)
    fetch.begin()
    r = qc_ref.shape[0] // gsz
    n_all = n_pages + 1
    lane = lax.broadcasted_iota(I32, (r, PAGE), 1)
    qrow = lax.broadcasted_iota(I32, (r, PAGE), 0)
    causal_new = jnp.concatenate([jnp.where(lane <= qrow, 0.0, NEG)] * H_C, axis=0)
    ri = lax.broadcasted_iota(I32, (PAGE, PAGE), 0)
    ci = lax.broadcasted_iota(I32, (PAGE, PAGE), 1)
    upper = jnp.where(ri <= ci, 1.0, 0.0).astype(BF16)
    ones = jnp.ones((PAGE, PAGE), BF16)
    nr = 8 * n_all
    rr = lax.broadcasted_iota(I32, (nr, nr), 0)
    cc = lax.broadcasted_iota(I32, (nr, nr), 1)
    prev_pages = jnp.where(((rr % 8) == (cc % 8)) & ((cc // 8) < (rr // 8)), 1.0, 0.0).astype(BF16)
    zrow = jnp.zeros((8 - H_C, PAGE), F32)
    for g in range(gsz):
        tok = slice(g * r, (g + 1) * r)
        xs = []
        for p in range(n_pages):
            fetch.prefetch(g * n_pages + p)
            xs += [fetch.page(2, g * n_pages + p), zrow]
        x = jnp.concatenate(xs + [lft_n[g * 8:(g + 1) * 8, :]], axis=0)
        cum = _dot3_lhs(x, upper) + _dot3_rhs(prev_pages, _dot3_lhs(x, ones))
        qc2 = _stack_heads(qc_ref, tok, H_C, 2 * HD)
        s_pages = []
        for p in range(n_all):
            if p < n_pages:
                s = _dot(qc2, fetch.page(0, g * n_pages + p).reshape(2 * HD, PAGE).astype(BF16))
            else:
                s = _dot_nt(qc2, _pad_new(ck_n[tok, :]).astype(BF16)) + causal_new
            decay = jnp.concatenate(
                [jnp.broadcast_to(cum[p * 8 + h:p * 8 + h + 1, :], (r, PAGE)) for h in range(H_C)], axis=0)
            s_pages.append(s - decay)
        p_pages, lsum = _probabilities(s_pages)
        vts = [fetch.page(1, g * n_pages + p).reshape(2 * HD, PAGE).astype(BF16) for p in range(n_pages)]
        o = _pv(p_pages, vts, _pad_new(cv_n[tok, :]).astype(BF16)) / lsum
        oc_ref[tok, :] = jnp.concatenate([o[h * r:(h + 1) * r] for h in range(H_C)], axis=1)
    fetch.finish()


def _decode_c(page_table, layer, qc, ck_n, cv_n, lft_new, kt, vt, lft, r):
    n_pages = page_table.shape[1]
    assert r == 8
    return _decode_call(
        functools.partial(_decode_c_kernel, layer=layer, gsz=G_DEC, n_pages=n_pages), "decode_forget",
        page_table, (qc, ck_n, cv_n, lft_new.reshape(-1, LANES)), (768, 128, 128, LANES), (),
        (kt, vt, lft), ((KVH_C, HD, PAGE), (KVH_C, HD, PAGE), (H_C, PAGE)), 768, r)


def _merge_kernel(x_ref, mod_ref, g_ref, oa_ref, ob_ref, oc_ref, wg_ref, wpa_ref, wpb_ref, wpc_ref, wo_ref,
                  o_ref, *, transposed):
    x = x_ref[...]
    gsz, r, _ = x.shape
    mod = mod_ref[...]
    h = _norm_mod(x, g_ref[...], mod[:, :, D:2 * D], mod[:, :, 0:D]).reshape(gsz * r, D).astype(BF16)
    gates = _sigmoid(_dot(h, wg_ref[...]))
    if transposed:
        oa, ob, oc = oa_ref[0].T, ob_ref[0].T, oc_ref[0].T
    else:
        oa, ob, oc = oa_ref[...], ob_ref[...], oc_ref[...]
    merged = (gates[:, 0:D] * _dot(oa.astype(BF16), wpa_ref[...])
              + gates[:, D:2 * D] * _dot(ob.astype(BF16), wpb_ref[...])
              + gates[:, 2 * D:3 * D] * _dot(oc.astype(BF16), wpc_ref[...]))
    y = _dot(merged.astype(BF16), wo_ref[...]).reshape(gsz, r, D)
    o_ref[...] = x + mod[:, :, 2 * D:3 * D] * y


def _merge(x, mod_a, g, oa, ob, oc, wg, wpa, wpb, wpc, wo, gsz, r, transposed, layer):
    nb, rr, _ = x.shape
    const = lambda a: pl.BlockSpec(a.shape, lambda i, j: (0,) * a.ndim)
    lw = lambda a: _layer_spec(a, layer)
    if transposed:
        grid = (nb, rr // r)
        x_spec = pl.BlockSpec((1, r, D), lambda i, j: (i, j, 0))
        mod_spec = pl.BlockSpec((1, 1, 3 * D), lambda i, j: (i, 0, 0))
        o_spec = lambda a: pl.BlockSpec((1, a.shape[1], r), lambda i, j: (i, 0, j))
    else:
        grid = (nb // gsz, 1)
        x_spec = pl.BlockSpec((gsz, rr, D), lambda i, j: (i, 0, 0))
        mod_spec = pl.BlockSpec((gsz, 1, 3 * D), lambda i, j: (i, 0, 0))
        o_spec = lambda a: pl.BlockSpec((gsz * rr, a.shape[1]), lambda i, j: (i, 0))
    return pl.pallas_call(
        functools.partial(_merge_kernel, transposed=transposed),
        grid=grid,
        in_specs=[x_spec, mod_spec, const(g), o_spec(oa), o_spec(ob), o_spec(oc),
                  lw(wg), lw(wpa), lw(wpb), lw(wpc), lw(wo)],
        out_specs=x_spec,
        out_shape=jax.ShapeDtypeStruct(x.shape, F32),
        compiler_params=_cparams("arbitrary", "arbitrary"),
        name="merge_prompt" if transposed else "merge_sample",
    )(x, mod_a, g, oa, ob, oc, wg, wpa, wpb, wpc, wo)


def _ffn_kernel(x_ref, mod_ref, g_ref, w1_ref, w2_ref, gf_ref, o_ref, *, final):
    x = x_ref[...]
    gsz, r, _ = x.shape
    mod = mod_ref[...]
    h = _norm_mod(x, g_ref[...], mod[:, :, D:2 * D], mod[:, :, 0:D]).reshape(gsz * r, D).astype(BF16)
    u = jnp.maximum(_dot(h, w1_ref[...]), 0.0)
    y = _dot((u * u).astype(BF16), w2_ref[...]).reshape(gsz, r, D)
    x2 = x + mod[:, :, 2 * D:3 * D] * y
    if final:
        ms = jnp.mean(x2 * x2, axis=-1, keepdims=True)
        x2 = x2 * lax.rsqrt(ms + EPS) * gf_ref[...]
    o_ref[...] = x2


def _ffn(x, mod_b, g, w1, w2, g_final, gsz, r, final, name, layer):
    nb, rr, _ = x.shape
    const = lambda a: pl.BlockSpec(a.shape, lambda i, j: (0,) * a.ndim)
    lw = lambda a: _layer_spec(a, layer)
    if gsz == 1:
        grid = (nb, rr // r)
        x_spec = pl.BlockSpec((1, r, D), lambda i, j: (i, j, 0))
        mod_spec = pl.BlockSpec((1, 1, 3 * D), lambda i, j: (i, 0, 0))
    else:
        grid = (nb // gsz, 1)
        x_spec = pl.BlockSpec((gsz, rr, D), lambda i, j: (i, 0, 0))
        mod_spec = pl.BlockSpec((gsz, 1, 3 * D), lambda i, j: (i, 0, 0))
    return pl.pallas_call(
        functools.partial(_ffn_kernel, final=final),
        grid=grid,
        in_specs=[x_spec, mod_spec, const(g), lw(w1), lw(w2), const(g_final)],
        out_specs=x_spec,
        out_shape=jax.ShapeDtypeStruct(x.shape, F32),
        compiler_params=_cparams("arbitrary", "arbitrary"),
        name=name,
    )(x, mod_b, g, w1, w2, g_final)


TM_PROMPT = 512
G_SAMPLE = 32


def kernel(x_prompt, x_sample, c_prompt, c_sample, cache_a_k, cache_a_v, cache_a_kidx, cache_b_k, cache_b_v, cache_c_k, cache_c_v, cache_c_logf, page_table, t5_table, w_ada, b_ada, g_mix, g_ffn, w_in, b_forget, lam_q1, lam_k1, lam_q2, lam_k2, g_subln, w_gate, w_pa, w_pb, w_pc, w_out, w_ff1, w_ff2, g_final):
    depth = w_in.shape[0]
    nbp, seq, _ = x_prompt.shape
    nbs, dec_seq, _ = x_sample.shape
    n_pool, page = cache_a_k.shape[1], cache_a_k.shape[2]
    n_pages = page_table.shape[1]
    past_len = n_pages * page
    assert page == PAGE and seq % max(TM_PROMPT, _CUM_T, FAR_BLOCKS * TKC) == 0 and dec_seq == 8
    assert nbs % G_SAMPLE == 0 and nbs % G_DEC == 0
    tm = min(TM_PROMPT, seq)

    nc = nbp + nbs
    mod = _ada(jnp.concatenate([c_prompt, c_sample], axis=0), w_ada, b_ada)
    mod = mod.reshape(depth, nc, 1, 6 * D)
    bias_pt, bias_dec = _bias_tiles(t5_table, past_len, dec_seq)

    kv_t = lambda c: jnp.transpose(c, (0, 1, 3, 4, 2))
    a_kt, a_vt, c_kt, c_vt = kv_t(cache_a_k), kv_t(cache_a_v), kv_t(cache_c_k), kv_t(cache_c_v)
    a_kit = jnp.transpose(cache_a_kidx, (0, 1, 3, 2))
    b_kt = jnp.transpose(cache_b_k, (0, 1, 3, 4, 5, 2))
    b_v2 = cache_b_v.reshape(depth, n_pool, page * KVH_B, 2 * HD)
    c_lft = jnp.transpose(cache_c_logf, (0, 3, 1, 2))

    w_in_t = jnp.swapaxes(w_in, 1, 2)
    wr, wt = _prep_in_prompt(w_in_t)
    ws = _prep_in_sample(w_in_t)
    wg, wo = w_gate.astype(BF16), w_out.astype(BF16)
    wpa, wpb, wpc = w_pa.astype(BF16), w_pb.astype(BF16), w_pc.astype(BF16)
    wpa_s = _pad_proj_rows(w_pa, H_A, lambda h: h // (H_A // KVH_A))
    wpc_s = _pad_proj_rows(w_pc, H_C, lambda h: h // (H_C // KVH_C))
    w1, w2 = w_ff1.astype(BF16), w_ff2.astype(BF16)

    xp, xs = x_prompt, x_sample
    rows_s = []
    stacked = [jnp.zeros(s.shape, s.dtype) for s in _stacked_row_shapes(depth, nbp, seq)]
    g_final2 = g_final.reshape(1, D)
    for l in range(depth):
        lam_init = 0.8 - 0.6 * math.exp(-0.3 * l)
        bfb = _cf_block(b_forget[l].reshape(1, H_C))
        gm, gf = g_mix[l].reshape(1, D), g_ffn[l].reshape(1, D)
        lamvec = jnp.stack([lam_q1[l], lam_k1[l], lam_q2[l], lam_k2[l]])
        mod_pa, mod_pb = mod[l, :nbp, :, :3 * D], mod[l, :nbp, :, 3 * D:]
        mod_sa, mod_sb = mod[l, nbp:, :, :3 * D], mod[l, nbp:, :, 3 * D:]
        last = l == depth - 1

        stacked, (akb, akib, bkb, ckb, lfrep, awiT, aqT, aqiT, bqT, cqT, avT, bvT, cvT) = _inproj_prompt(
            xp, mod_pa, gm, wr, wt, bfb, tm, l, stacked)
        kcat = _cum_prompt(lfrep, ckb)
        oaT = _attn_a(aqT, aqiT, awiT, akb, akib, avT, bias_pt[:H_A])
        obT = _attn_b(bqT, bkb, bvT, bias_pt[H_A:], lamvec, g_subln[l].reshape(2 * HD, 1), lam_init)
        ocT = _attn_c(cqT, kcat, cvT)
        x1 = _merge(xp, mod_pa, gm, oaT, obT, ocT, wg, wpa, wpb, wpc, wo, 1, tm, True, l)
        xp = _ffn(x1, mod_pb, gf, w1, w2, g_final2, 1, tm, last, "ffn_prompt", l)

        (sak, sav, saki, sbk, sbv, sck, scv, slf, qa, qi, wi, qb, qc, slfrep) = _inproj_sample(
            xs, mod_sa, gm, ws, bfb, G_SAMPLE, l)
        rows_s.append((sak, sav, saki, sbk, sbv, sck, scv, slf))
        lft_new = jnp.swapaxes(slfrep.reshape(nbs, dec_seq, LANES)[:, :, 0:8], 1, 2)
        lft_new = jnp.concatenate([lft_new, jnp.zeros((nbs, 8, LANES - dec_seq), F32)], axis=2)
        lane_head = jnp.arange(8)[None, :, None] < H_C
        lft_new = jnp.where(lane_head, lft_new, 0.0)
        oa = _decode_a(page_table, l, qa, qi, wi, sak, sav, saki, a_kt, a_vt, a_kit, bias_dec, dec_seq)
        ob = _decode_b(page_table, l, qb, sbk, sbv, b_kt, b_v2, bias_dec, lamvec, g_subln[l].reshape(1, 2 * HD),
                       lam_init, dec_seq)
        oc = _decode_c(page_table, l, qc, sck, scv, lft_new, c_kt, c_vt, c_lft, dec_seq)
        x1s = _merge(xs, mod_sa, gm, oa, ob, oc, wg, wpa_s, wpb, wpc_s, wo, G_SAMPLE, dec_seq, False, l)
        xs = _ffn(x1s, mod_sb, gf, w1, w2, g_final2, G_SAMPLE, dec_seq, last, "ffn_sample", l)

    def stack(rows, i):
        return jnp.stack([r[i] for r in rows])

    def stack_sample(i, shape):
        return stack(rows_s, i).reshape((depth, nbs, dec_seq) + shape)

    def prompt_t(y, shape):
        n = len(shape)
        y = y.reshape((depth, nbp) + shape + (seq,))
        return jnp.transpose(y, (0, 1, n + 2) + tuple(range(2, n + 2)))

    bv, lf, akT, avT, akiT, bkT, ckT, cvT = stacked
    out_p = (prompt_t(akT, (KVH_A, HD)), prompt_t(avT, (KVH_A, HD)), prompt_t(akiT, (D_IDX,)),
             prompt_t(bkT, (KVH_B, 2, HD)), bv.reshape(depth, nbp, seq, KVH_B, 2 * HD),
             prompt_t(ckT, (KVH_C, HD)), prompt_t(cvT, (KVH_C, HD)), lf)
    out_s = (stack_sample(0, (KVH_A, HD)), stack_sample(1, (KVH_A, HD)), stack_sample(2, (D_IDX,)),
             stack_sample(3, (KVH_B, 2, HD)), stack_sample(4, (KVH_B, 2 * HD)), stack_sample(5, (KVH_C, HD)),
             stack_sample(6, (KVH_C, HD)), stack_sample(7, (H_C,)))
    return (xp, xs) + out_p + out_s
```
